```python
import math
import jax, jax.numpy as jnp
from jax import lax
import numpy as np

D_MODEL = 2048
BATCH = 4
SEQ = 2048
DEPTH = 2

GRID_W = 64
CTX_LEN = 256
NORM_EPS = 1e-6

N_DIFF_HEADS = 8
HEAD_DIM = 64
V_HEAD_DIM = 2 * HEAD_DIM
QK_W = N_DIFF_HEADS * 2 * HEAD_DIM
ATTN_W = N_DIFF_HEADS * V_HEAD_DIM
Q_BLOCK = 128
ROPE_BASE = 10000.0
ROPE_AXIS_DIM = HEAD_DIM // 2
ROPE_FREQS = ROPE_AXIS_DIM // 2
DIFF_SUBLN_EPS = 1e-5

SC_W = D_MODEL // 4
SC_CONV_WIDTH = 3

CF_W = D_MODEL // 4
CF_CONV_WIDTH = 31
CF_LN_EPS = 1e-5

N_BRANCHES = 3

OFF_Q = 0
OFF_K = OFF_Q + QK_W
OFF_V = OFF_K + QK_W
OFF_SC = OFF_V + ATTN_W
OFF_CF = OFF_SC + 3 * SC_W
OFF_GATE = OFF_CF + 2 * CF_W
C_TOT = OFF_GATE + N_BRANCHES * D_MODEL

N_GROUPS = 4
EXPERTS_PER_GROUP = 8
N_EXPERTS = N_GROUPS * EXPERTS_PER_GROUP
TOP_K = 2
EXPERT_HIDDEN = D_MODEL // 2
MOE_BLOCK = 128

kernel_name = "hybrid_diffattn_conv_hmoe_dit_block"


def rmsnorm(x, g, eps=NORM_EPS):
    xf = x.astype(jnp.float32)
    y = xf * lax.rsqrt(jnp.mean(xf * xf, axis=-1, keepdims=True) + eps)
    return (y * g.astype(jnp.float32)).astype(x.dtype)


def layernorm(x, g, b, eps=CF_LN_EPS):
    xf = x.astype(jnp.float32)
    mu = jnp.mean(xf, axis=-1, keepdims=True)
    var = jnp.mean(jnp.square(xf - mu), axis=-1, keepdims=True)
    y = (xf - mu) * lax.rsqrt(var + eps)
    return (y * g.astype(jnp.float32) + b.astype(jnp.float32)).astype(x.dtype)


def modulate(x, shift, scale):
    return x * (1 + scale) + shift


def depthwise_conv_centred(x, w):
    k = w.shape[0]
    return lax.conv_general_dilated(
        x, w[:, None, :].astype(x.dtype), window_strides=(1,),
        padding=[(k // 2, k // 2)], dimension_numbers=("NWC", "WIO", "NWC"),
        feature_group_count=x.shape[-1])


def axial_rope_tables(n_tokens):
    rows = n_tokens // GRID_W
    row = jnp.repeat(jnp.arange(rows, dtype=jnp.int32), GRID_W)
    col = jnp.tile(jnp.arange(GRID_W, dtype=jnp.int32), rows)
    pos = jnp.stack([row, col], axis=-1).astype(jnp.float32)
    inv_freq = ROPE_BASE ** (-jnp.arange(ROPE_FREQS, dtype=jnp.float32) / ROPE_FREQS)
    ang = pos[:, :, None] * inv_freq
    return jnp.cos(ang), jnp.sin(ang)


def apply_axial_rope(x, cos, sin):
    xr = x.reshape(x.shape[:-1] + (2, 2, ROPE_FREQS))
    x1, x2 = xr[..., 0, :], xr[..., 1, :]
    cos = cos.astype(x.dtype)
    sin = sin.astype(x.dtype)
    out = jnp.stack([x1 * cos - x2 * sin, x1 * sin + x2 * cos], axis=-2)
    return out.reshape(x.shape)


def split_qk(p):
    b, l, _ = p.shape
    return p.reshape(b, l, N_DIFF_HEADS, 2, HEAD_DIM).transpose(0, 2, 3, 1, 4)


def split_v(p):
    b, l, _ = p.shape
    return p.reshape(b, l, N_DIFF_HEADS, V_HEAD_DIM).transpose(0, 2, 1, 3)


def diff_attention(q, k, v, lam):
    b, h, _, lq, d = q.shape
    nb = lq // Q_BLOCK
    q_blocks = jnp.moveaxis(q.reshape(b, h, 2, nb, Q_BLOCK, d), 3, 0)
    scale = HEAD_DIM ** -0.5

    def one_block(qb):
        s = jnp.einsum("bhmqd,bhmkd->bhmqk", qb, k).astype(jnp.float32) * scale
        p = jax.nn.softmax(s, axis=-1)
        a = p[:, :, 0] - lam * p[:, :, 1]
        return jnp.einsum("bhqk,bhkv->bhqv", a.astype(v.dtype), v)

    o = lax.map(one_block, q_blocks)
    return jnp.moveaxis(o, 0, 2).reshape(b, h, lq, V_HEAD_DIM)


def diff_heads_out(o, subln_g, lam_init):
    o = rmsnorm(o, subln_g, DIFF_SUBLN_EPS) * (1.0 - lam_init)
    b, h, l, dv = o.shape
    return o.transpose(0, 2, 1, 3).reshape(b, l, h * dv)


def short_conv_mixer(p, conv_w):
    b_gate, c_gate, x_in = jnp.split(p, 3, axis=-1)
    return b_gate * depthwise_conv_centred(c_gate * x_in, conv_w)


def conformer_conv(p, dw_w, dw_b, ln_g, ln_b):
    a, g = jnp.split(p, 2, axis=-1)
    z = depthwise_conv_centred(a * jax.nn.sigmoid(g), dw_w) + dw_b.astype(p.dtype)
    return jax.nn.silu(layernorm(z, ln_g, ln_b))


def merge_branches(p, attn_flat, w_attn_out, sc_conv_w, w_sc_out, cf_dw_w, cf_dw_b,
                   cf_ln_g, cf_ln_b, w_cf_out, b_gate, w_mix):
    y_a = attn_flat @ w_attn_out
    y_b = short_conv_mixer(p[..., OFF_SC:OFF_CF], sc_conv_w) @ w_sc_out
    y_c = conformer_conv(p[..., OFF_CF:OFF_GATE], cf_dw_w, cf_dw_b, cf_ln_g, cf_ln_b) @ w_cf_out
    g_a, g_b, g_c = jnp.split(jax.nn.sigmoid(p[..., OFF_GATE:] + b_gate), N_BRANCHES, axis=-1)
    return (g_a * y_a + g_b * y_b + g_c * y_c) @ w_mix


def hier_moe(tokens, layer, rg_w, rg_b, re_w, re_b, exp_w_gu, exp_w_down):
    n_tok = tokens.shape[0]
    lg = (tokens @ rg_w).astype(jnp.float32) + rg_b.astype(jnp.float32)
    grp = jnp.argmax(lg, axis=-1).astype(jnp.int32)
    p_grp = jnp.take_along_axis(jax.nn.softmax(lg, axis=-1), grp[:, None], axis=-1)
    le = ((tokens @ re_w).astype(jnp.float32) + re_b.astype(jnp.float32)).reshape(
        n_tok, N_GROUPS, EXPERTS_PER_GROUP)
    le = jnp.take_along_axis(le, grp[:, None, None], axis=1)[:, 0]
    top_p, top_i = lax.top_k(jax.nn.softmax(le, axis=-1), TOP_K)
    weights = p_grp * top_p / jnp.sum(top_p, axis=-1, keepdims=True)
    expert_id = grp[:, None] * EXPERTS_PER_GROUP + top_i.astype(jnp.int32)

    n_slot = n_tok * TOP_K
    e_flat = expert_id.reshape(n_slot)
    w_flat = weights.reshape(n_slot)
    tok_flat = jnp.arange(n_slot, dtype=jnp.int32) // TOP_K
    order = jnp.argsort(e_flat)
    e_sorted = e_flat[order]
    counts = jnp.zeros((N_EXPERTS,), jnp.int32).at[e_flat].add(1)
    padded = (counts + MOE_BLOCK - 1) // MOE_BLOCK * MOE_BLOCK
    start = jnp.cumsum(counts) - counts
    pad_end = jnp.cumsum(padded)
    pad_start = pad_end - padded
    dest = pad_start[e_sorted] + jnp.arange(n_slot, dtype=jnp.int32) - start[e_sorted]
    n_rows = -(-n_slot // MOE_BLOCK) * MOE_BLOCK + N_EXPERTS * MOE_BLOCK
    n_blocks = n_rows // MOE_BLOCK
    row_tok = jnp.zeros((n_rows,), jnp.int32).at[dest].set(tok_flat[order])
    row_w = jnp.zeros((n_rows,), jnp.float32).at[dest].set(w_flat[order])
    blk_start = jnp.arange(n_blocks, dtype=jnp.int32) * MOE_BLOCK
    blk_exp = jnp.minimum(jnp.searchsorted(pad_end, blk_start, side="right"), N_EXPERTS - 1)
    x_blocks = tokens[row_tok].reshape(n_blocks, MOE_BLOCK, tokens.shape[-1])

    def expert_block(args):
        xb, e = args
        g, u = jnp.split(xb @ exp_w_gu[layer, e], 2, axis=-1)
        return (jax.nn.silu(g) * u) @ exp_w_down[layer, e]

    y = lax.map(expert_block, (x_blocks, blk_exp)).reshape(n_rows, tokens.shape[-1])
    y = y * row_w[:, None].astype(y.dtype)
    return jnp.zeros_like(tokens).at[row_tok].add(y)


def setup_inputs(seed: int = 0) -> dict:
    key = jax.random.key(seed)
    keys = jax.random.split(key, 32)
    counter = [0]
    f32 = jnp.float32
    D, L = D_MODEL, DEPTH

    def nrm(shape, scale):
        k = keys[counter[0]]
        counter[0] += 1
        return jax.random.normal(k, shape, f32) * scale

    def gain(shape):
        return 1.0 + nrm(shape, 0.05)

    return {
        "x": nrm((BATCH, SEQ, D), 1.0),
        "c": nrm((BATCH, D), 1.0),
        "ctx": nrm((BATCH, CTX_LEN, D), 1.0),
        "c_ctx": nrm((D,), 1.0),
        "ada_w": nrm((L, D, 6 * D), 0.5 * D ** -0.5),
        "ada_b": nrm((L, 6 * D), 0.02),
        "norm1_g": gain((L, D)),
        "w_in": nrm((L, D, C_TOT), D ** -0.5),
        "b_gate": nrm((L, N_BRANCHES * D), 0.02),
        "diff_lambda": nrm((L, 4, HEAD_DIM), 0.1),
        "subln_g": gain((L, V_HEAD_DIM)),
        "w_attn_out": nrm((L, ATTN_W, D), ATTN_W ** -0.5),
        "sc_conv_w": nrm((L, SC_CONV_WIDTH, SC_W), SC_CONV_WIDTH ** -0.5),
        "w_sc_out": nrm((L, SC_W, D), SC_W ** -0.5),
        "cf_dw_w": nrm((L, CF_CONV_WIDTH, CF_W), CF_CONV_WIDTH ** -0.5),
        "cf_dw_b": nrm((L, CF_W), 0.02),
        "cf_ln_g": gain((L, CF_W)),
        "cf_ln_b": nrm((L, CF_W), 0.02),
        "w_cf_out": nrm((L, CF_W, D), CF_W ** -0.5),
        "w_mix": nrm((L, D, D), D ** -0.5),
        "norm2_g": gain((L, D)),
        "router_g_w": nrm((L, D, N_GROUPS), D ** -0.5),
        "router_g_b": nrm((L, N_GROUPS), 0.01),
        "router_e_w": nrm((L, D, N_EXPERTS), D ** -0.5),
        "router_e_b": nrm((L, N_EXPERTS), 0.01),
        "exp_w_gu": nrm((L, N_EXPERTS, D, 2 * EXPERT_HIDDEN), D ** -0.5),
        "exp_w_down": nrm((L, N_EXPERTS, EXPERT_HIDDEN, D), EXPERT_HIDDEN ** -0.5),
        "final_g": gain((D,)),
    }


def reference(x, c, ctx, c_ctx, ada_w, ada_b, norm1_g, w_in, b_gate, diff_lambda, subln_g,
              w_attn_out, sc_conv_w, w_sc_out, cf_dw_w, cf_dw_b, cf_ln_g, cf_ln_b, w_cf_out,
              w_mix, norm2_g, router_g_w, router_g_b, router_e_w, router_e_b, exp_w_gu,
              exp_w_down, final_g):
    bsz, n_lat, d = x.shape
    n_ctx = ctx.shape[1]
    cos, sin = axial_rope_tables(n_lat)
    h_lat, h_ctx = x, ctx
    silu_c = jax.nn.silu(c)
    silu_cc = jax.nn.silu(c_ctx)

    for layer in range(DEPTH):
        last = layer == DEPTH - 1
        lam_init = 0.8 - 0.6 * math.exp(-0.3 * layer)
        mod_l = silu_c @ ada_w[layer] + ada_b[layer]
        mod_c = silu_cc @ ada_w[layer] + ada_b[layer]
        sh1_l, sc1_l, g1_l, sh2_l, sc2_l, g2_l = jnp.split(mod_l[:, None, :], 6, axis=-1)
        sh1_c, sc1_c, g1_c, sh2_c, sc2_c, g2_c = jnp.split(mod_c, 6, axis=-1)
        lv = diff_lambda[layer].astype(jnp.float32)
        lam = jnp.exp(jnp.sum(lv[0] * lv[1])) - jnp.exp(jnp.sum(lv[2] * lv[3])) + lam_init
        w_in_l = w_in[layer]
        branch_params = (w_attn_out[layer], sc_conv_w[layer], w_sc_out[layer], cf_dw_w[layer],
                         cf_dw_b[layer], cf_ln_g[layer], cf_ln_b[layer], w_cf_out[layer],
                         b_gate[layer], w_mix[layer])

        u_lat = modulate(rmsnorm(h_lat, norm1_g[layer]), sh1_l, sc1_l)
        u_ctx = modulate(rmsnorm(h_ctx, norm1_g[layer]), sh1_c, sc1_c)
        p_lat = u_lat @ w_in_l
        if last:
            kv_ctx = u_ctx @ w_in_l[:, OFF_K:OFF_SC]
            k_ctx = split_qk(kv_ctx[..., :QK_W])
            v_ctx = split_v(kv_ctx[..., QK_W:])
        else:
            p_ctx = u_ctx @ w_in_l
            k_ctx = split_qk(p_ctx[..., OFF_K:OFF_V])
            v_ctx = split_v(p_ctx[..., OFF_V:OFF_SC])

        q_lat = apply_axial_rope(split_qk(p_lat[..., OFF_Q:OFF_K]), cos, sin)
        k_lat = apply_axial_rope(split_qk(p_lat[..., OFF_K:OFF_V]), cos, sin)
        v_lat = split_v(p_lat[..., OFF_V:OFF_SC])
        k_all = jnp.concatenate([k_ctx, k_lat], axis=3)
        v_all = jnp.concatenate([v_ctx, v_lat], axis=2)
        a_lat = diff_heads_out(diff_attention(q_lat, k_all, v_all, lam), subln_g[layer], lam_init)
        mix_lat = merge_branches(p_lat, a_lat, *branch_params)
        if not last:
            q_ctx = split_qk(p_ctx[..., OFF_Q:OFF_K])
            a_ctx = diff_heads_out(diff_attention(q_ctx, k_ctx, v_ctx, lam), subln_g[layer], lam_init)
            h_ctx = h_ctx + g1_c * merge_branches(p_ctx, a_ctx, *branch_params)
        h_lat = h_lat + g1_l * mix_lat

        f_lat = modulate(rmsnorm(h_lat, norm2_g[layer]), sh2_l, sc2_l)
        moe_args = (layer, router_g_w[layer], router_g_b[layer], router_e_w[layer],
                    router_e_b[layer], exp_w_gu, exp_w_down)
        if last:
            f = hier_moe(f_lat.reshape(-1, d), *moe_args).reshape(bsz, n_lat, d)
            h_lat = h_lat + g2_l * f
        else:
            f_ctx = modulate(rmsnorm(h_ctx, norm2_g[layer]), sh2_c, sc2_c)
            toks = jnp.concatenate([f_ctx, f_lat], axis=1).reshape(-1, d)
            f = hier_moe(toks, *moe_args).reshape(bsz, n_ctx + n_lat, d)
            h_ctx = h_ctx + g2_c * f[:, :n_ctx]
            h_lat = h_lat + g2_l * f[:, n_ctx:]

    return rmsnorm(h_lat, final_g)
```

```python
import functools
import math

import jax
import jax.numpy as jnp
from jax import lax
from jax.experimental import pallas as pl
from jax.experimental.pallas import tpu as pltpu

F32 = jnp.float32
BF16 = jnp.bfloat16
I32 = jnp.int32

D_MODEL = 2048
GRID_W = 64
NORM_EPS = 1e-6
N_HEADS = 8
HEAD_DIM = 64
HEAD_W = 2 * HEAD_DIM
QK_W = N_HEADS * HEAD_W
ATTN_W = N_HEADS * HEAD_W
ROPE_BASE = 10000.0
ROPE_FREQS = HEAD_DIM // 4
SUBLN_EPS = 1e-5
SC_W = D_MODEL // 4
CF_W = D_MODEL // 4
SC_TAPS = 3
CF_TAPS = 31
CF_LN_EPS = 1e-5
OFF_Q = 0
OFF_K = OFF_Q + QK_W
OFF_V = OFF_K + QK_W
OFF_SC = OFF_V + ATTN_W
OFF_CF = OFF_SC + 3 * SC_W
OFF_GATE = OFF_CF + 2 * CF_W
C_TOT = OFF_GATE + 3 * D_MODEL
N_GROUPS = 4
EXPERTS_PER_GROUP = 8
N_EXPERTS = N_GROUPS * EXPERTS_PER_GROUP
EXPERT_HIDDEN = D_MODEL // 2

LANES = 128
MOD_ROWS = 8
CONV_HALO = 16
CONV_ROWS = 256
CONV_CHUNK = 32
MOE_ITEM_ROWS = 1024
MOE_ROW_BLOCK = 256
MOE_HIDDEN_BLOCK = 256
VMEM_LIMIT = 56 * 1024 * 1024


def _params(sem, vmem=VMEM_LIMIT):
    return pltpu.CompilerParams(dimension_semantics=sem, vmem_limit_bytes=vmem)


def _sigmoid(x):
    return 1.0 / (1.0 + jnp.exp(-x))


def _adaln_kernel(s_ref, w_ref, b_ref, o_ref):
    s = s_ref[...]
    s = s * _sigmoid(s)
    s_hi = s.astype(BF16)
    s_lo = (s - s_hi.astype(F32)).astype(BF16)
    w = w_ref[...]
    w_hi = w.astype(BF16)
    w_lo = (w - w_hi.astype(F32)).astype(BF16)
    lhs = jnp.concatenate([s_hi.astype(F32), s - s_hi.astype(F32)], axis=0).astype(BF16)
    r = jnp.dot(lhs, w_hi, preferred_element_type=F32)
    r2 = jnp.dot(s_hi, w_lo, preferred_element_type=F32)
    o_ref[...] = r[:MOD_ROWS] + r[MOD_ROWS:] + r2 + b_ref[...]


def _adaln(cond, ada_w, ada_b):
    n_layers, d, n = ada_w.shape
    tn = 512
    return pl.pallas_call(
        _adaln_kernel,
        out_shape=jax.ShapeDtypeStruct((n_layers, MOD_ROWS, n), F32),
        grid=(n_layers, n // tn),
        in_specs=[
            pl.BlockSpec((MOD_ROWS, d), lambda l, j: (0, 0)),
            pl.BlockSpec((None, d, tn), lambda l, j: (l, 0, j)),
            pl.BlockSpec((None, 1, tn), lambda l, j: (l, 0, j)),
        ],
        out_specs=pl.BlockSpec((None, MOD_ROWS, tn), lambda l, j: (l, 0, j)),
        compiler_params=_params(("arbitrary", "arbitrary")),
        name="adaln",
    )(cond, ada_w, ada_b.reshape(n_layers, 1, n))


def _mod_row(row0, n_lat, seq):
    return jnp.where(row0 < n_lat, row0 // seq, MOD_ROWS // 2)


def _in_proj_kernel(x_ref, g_ref, sh_ref, sc_ref, w_ref, o_ref, u_ref):
    @pl.when(pl.program_id(1) == 0)
    def _():
        x = x_ref[...]
        ms = jnp.mean(x * x, axis=-1, keepdims=True)
        y = x * lax.rsqrt(ms + NORM_EPS) * g_ref[...]
        u_ref[...] = (y * (1.0 + sc_ref[...]) + sh_ref[...]).astype(BF16)

    o_ref[...] = jnp.dot(u_ref[...], w_ref[...], preferred_element_type=F32).astype(o_ref.dtype)


def _in_proj(h, gain, mods, w_bf16, *, row0, n_rows, col0, n_cols, n_lat, seq, sh_idx, sc_idx, tm, tn=512):
    d = h.shape[1]
    rb0, cb0 = row0 // tm, col0 // tn

    def mod_map(idx):
        return lambda i, j: (_mod_row((i + rb0) * tm, n_lat, seq), 0, idx)

    return pl.pallas_call(
        _in_proj_kernel,
        out_shape=jax.ShapeDtypeStruct((n_rows, n_cols), BF16),
        grid=(n_rows // tm, n_cols // tn),
        in_specs=[
            pl.BlockSpec((tm, d), lambda i, j: (i + rb0, 0)),
            pl.BlockSpec((1, d), lambda i, j: (0, 0)),
            pl.BlockSpec((None, 1, d), mod_map(sh_idx)),
            pl.BlockSpec((None, 1, d), mod_map(sc_idx)),
            pl.BlockSpec((d, tn), lambda i, j: (0, j + cb0)),
        ],
        out_specs=pl.BlockSpec((tm, tn), lambda i, j: (i, j)),
        scratch_shapes=[pltpu.VMEM((tm, d), BF16)],
        compiler_params=_params(("arbitrary", "arbitrary")),
        name="in_proj",
    )(h, gain.reshape(1, d), mods, mods, w_bf16)


def _rope_tables(n_tokens):
    t = jnp.arange(n_tokens, dtype=I32)
    pos = jnp.stack([t // GRID_W, t % GRID_W], axis=-1).astype(F32)
    inv_freq = ROPE_BASE ** (-jnp.arange(ROPE_FREQS, dtype=F32) / ROPE_FREQS)
    ang = pos[:, :, None] * inv_freq
    cos, sin = jnp.cos(ang), jnp.sin(ang)
    c = jnp.stack([cos, cos], axis=2).reshape(n_tokens, HEAD_DIM)
    s = jnp.stack([-sin, sin], axis=2).reshape(n_tokens, HEAD_DIM)
    return jnp.tile(c, (1, 2)), jnp.tile(s, (1, 2))


def _rope(x, c, s):
    lane = lax.broadcasted_iota(I32, x.shape, 1)
    first_half = (lane % (2 * ROPE_FREQS)) < ROPE_FREQS
    partner = jnp.where(first_half, pltpu.roll(x, LANES - ROPE_FREQS, 1), pltpu.roll(x, ROPE_FREQS, 1))
    return x * c + partner * s


def _diff_lambda(lam_ref, lam_init):
    lv = lam_ref[...]
    a = jnp.sum(lv[0:1] * lv[1:2], axis=-1, keepdims=True)
    b = jnp.sum(lv[2:3] * lv[3:4], axis=-1, keepdims=True)
    return jnp.exp(a) - jnp.exp(b) + lam_init


def _attend(q, k_all, v_all, lam, subg, lam_init):
    tq = q.shape[0]
    lane = lax.broadcasted_iota(I32, q.shape, 1)
    q0 = jnp.where(lane < HEAD_DIM, q, 0.0).astype(BF16)
    q1 = jnp.where(lane >= HEAD_DIM, q, 0.0).astype(BF16)
    qq = jnp.concatenate([q0, q1], axis=0)
    s = lax.dot_general(qq, k_all, (((1,), (1,)), ((), ())), preferred_element_type=F32)
    m = jnp.max(s, axis=-1, keepdims=True)
    e = jnp.exp(s - m)
    r = 1.0 / jnp.sum(e, axis=-1, keepdims=True)
    a = e[:tq] * r[:tq] - e[tq:] * (lam * r[tq:])
    o = jnp.dot(a.astype(BF16), v_all, preferred_element_type=F32)
    ms = jnp.mean(o * o, axis=-1, keepdims=True)
    return o * lax.rsqrt(ms + SUBLN_EPS) * subg * (1.0 - lam_init)


def _attn_lat_kernel(q_ref, kl_ref, vl_ref, kc_ref, vc_ref, cq_ref, sq_ref, ck_ref, sk_ref, lam_ref, g_ref,
                     o_ref, k_all, v_all, *, n_ctx, lam_init):
    @pl.when(pl.program_id(2) == 0)
    def _():
        k_all[0:n_ctx, :] = kc_ref[...]
        v_all[0:n_ctx, :] = vc_ref[...]
        k_all[n_ctx:, :] = _rope(kl_ref[...].astype(F32), ck_ref[...], sk_ref[...]).astype(BF16)
        v_all[n_ctx:, :] = vl_ref[...]

    q = _rope(q_ref[...].astype(F32), cq_ref[...], sq_ref[...]) * (HEAD_DIM ** -0.5)
    lam = _diff_lambda(lam_ref, lam_init)
    o_ref[...] = _attend(q, k_all[...], v_all[...], lam, g_ref[...], lam_init).astype(o_ref.dtype)


def _attn_ctx_kernel(q_ref, k_ref, v_ref, lam_ref, g_ref, o_ref, *, lam_init):
    q = q_ref[...].astype(F32) * (HEAD_DIM ** -0.5)
    lam = _diff_lambda(lam_ref, lam_init)
    o_ref[...] = _attend(q, k_ref[...], v_ref[...], lam, g_ref[...], lam_init).astype(o_ref.dtype)


def _attn_lat(p, kv_ctx, kc_blk, vc_blk, ctx_rb0, cos, sin, lam4, subg, *, batch, seq, n_ctx, lam_init, tq=256):
    nq = seq // tq
    hb = lambda off: off // HEAD_W
    kernel = functools.partial(_attn_lat_kernel, n_ctx=n_ctx, lam_init=lam_init)
    return pl.pallas_call(
        kernel,
        out_shape=jax.ShapeDtypeStruct((batch * seq, ATTN_W), BF16),
        grid=(batch, N_HEADS, nq),
        in_specs=[
            pl.BlockSpec((tq, HEAD_W), lambda b, h, i: (b * nq + i, hb(OFF_Q) + h)),
            pl.BlockSpec((seq, HEAD_W), lambda b, h, i: (b, hb(OFF_K) + h)),
            pl.BlockSpec((seq, HEAD_W), lambda b, h, i: (b, hb(OFF_V) + h)),
            pl.BlockSpec((n_ctx, HEAD_W), lambda b, h, i: (ctx_rb0 + b, kc_blk + h)),
            pl.BlockSpec((n_ctx, HEAD_W), lambda b, h, i: (ctx_rb0 + b, vc_blk + h)),
            pl.BlockSpec((tq, HEAD_W), lambda b, h, i: (i, 0)),
            pl.BlockSpec((tq, HEAD_W), lambda b, h, i: (i, 0)),
            pl.BlockSpec((seq, HEAD_W), lambda b, h, i: (0, 0)),
            pl.BlockSpec((seq, HEAD_W), lambda b, h, i: (0, 0)),
            pl.BlockSpec((4, HEAD_DIM), lambda b, h, i: (0, 0)),
            pl.BlockSpec((1, HEAD_W), lambda b, h, i: (0, 0)),
        ],
        out_specs=pl.BlockSpec((tq, HEAD_W), lambda b, h, i: (b * nq + i, h)),
        scratch_shapes=[pltpu.VMEM((n_ctx + seq, HEAD_W), BF16), pltpu.VMEM((n_ctx + seq, HEAD_W), BF16)],
        compiler_params=_params(("arbitrary", "arbitrary", "arbitrary")),
        name="attn_lat",
    )(p, p, p, kv_ctx, kv_ctx, cos, sin, cos, sin, lam4, subg.reshape(1, HEAD_W))


def _attn_ctx(p, ctx_rb0, lam4, subg, *, batch, n_ctx, lam_init):
    hb = lambda off: off // HEAD_W
    kernel = functools.partial(_attn_ctx_kernel, lam_init=lam_init)
    return pl.pallas_call(
        kernel,
        out_shape=jax.ShapeDtypeStruct((batch * n_ctx, ATTN_W), BF16),
        grid=(batch, N_HEADS),
        in_specs=[
            pl.BlockSpec((n_ctx, HEAD_W), lambda b, h: (ctx_rb0 + b, hb(OFF_Q) + h)),
            pl.BlockSpec((n_ctx, HEAD_W), lambda b, h: (ctx_rb0 + b, hb(OFF_K) + h)),
            pl.BlockSpec((n_ctx, HEAD_W), lambda b, h: (ctx_rb0 + b, hb(OFF_V) + h)),
            pl.BlockSpec((4, HEAD_DIM), lambda b, h: (0, 0)),
            pl.BlockSpec((1, HEAD_W), lambda b, h: (0, 0)),
        ],
        out_specs=pl.BlockSpec((n_ctx, HEAD_W), lambda b, h: (b, h)),
        compiler_params=_params(("arbitrary", "arbitrary")),
        name="attn_ctx",
    )(p, p, p, lam4, subg.reshape(1, HEAD_W))


def _conv_kernel(bg_ref, cg_ref, xi_ref, a_ref, g_ref,
                 cg_p, xi_p, a_p, g_p, cg_n, xi_n, a_n, g_n,
                 scw_ref, cfw_ref, cfb_ref, lng_ref, lnb_ref,
                 sco_ref, cfo_ref, pad_ref, *, lat_blocks, blocks_per_seq):
    i = pl.program_id(0)
    in_lat = i < lat_blocks
    pos = i % blocks_per_seq
    has_prev = jnp.logical_and(in_lat, pos != 0)
    has_next = jnp.logical_and(in_lat, pos != blocks_per_seq - 1)
    keep_prev = jnp.where(has_prev, 1.0, 0.0)
    keep_next = jnp.where(has_next, 1.0, 0.0)
    lo, hi = CONV_HALO, CONV_HALO + CONV_ROWS

    def fill(main, prev, nxt):
        pad_ref[0:lo, :] = prev * keep_prev
        pad_ref[lo:hi, :] = main
        pad_ref[hi:hi + CONV_HALO, :] = nxt * keep_next

    def f(ref):
        return ref[...].astype(F32)

    def glu(a, g):
        return a * _sigmoid(g)

    fill(f(cg_ref) * f(xi_ref), f(cg_p) * f(xi_p), f(cg_n) * f(xi_n))
    for c in range(CONV_ROWS // CONV_CHUNK):
        r0 = lo + c * CONV_CHUNK - SC_TAPS // 2
        acc = scw_ref[0:1, :] * pad_ref[r0:r0 + CONV_CHUNK, :]
        for k in range(1, SC_TAPS):
            acc = acc + scw_ref[k:k + 1, :] * pad_ref[r0 + k:r0 + k + CONV_CHUNK, :]
        rows = slice(c * CONV_CHUNK, (c + 1) * CONV_CHUNK)
        sco_ref[rows, :] = (bg_ref[rows, :].astype(F32) * acc).astype(sco_ref.dtype)

    fill(glu(f(a_ref), f(g_ref)), glu(f(a_p), f(g_p)), glu(f(a_n), f(g_n)))
    for c in range(CONV_ROWS // CONV_CHUNK):
        r0 = lo + c * CONV_CHUNK - CF_TAPS // 2
        acc = cfw_ref[0:1, :] * pad_ref[r0:r0 + CONV_CHUNK, :]
        for k in range(1, CF_TAPS):
            acc = acc + cfw_ref[k:k + 1, :] * pad_ref[r0 + k:r0 + k + CONV_CHUNK, :]
        z = acc + cfb_ref[...]
        mu = jnp.mean(z, axis=-1, keepdims=True)
        zc = z - mu
        var = jnp.mean(zc * zc, axis=-1, keepdims=True)
        y = zc * lax.rsqrt(var + CF_LN_EPS) * lng_ref[...] + lnb_ref[...]
        rows = slice(c * CONV_CHUNK, (c + 1) * CONV_CHUNK)
        cfo_ref[rows, :] = (y * _sigmoid(y)).astype(cfo_ref.dtype)


def _convs(p, sc_w, cf_w, cf_b, ln_g, ln_b, *, n_rows, n_lat, seq):
    nb = n_rows // CONV_ROWS
    halo_per_block = CONV_ROWS // CONV_HALO
    last_halo = n_rows // CONV_HALO - 1
    cb = lambda off: off // SC_W

    def main(off):
        return pl.BlockSpec((CONV_ROWS, SC_W), lambda i: (i, cb(off)))

    def prev(off):
        return pl.BlockSpec((CONV_HALO, SC_W), lambda i: (jnp.maximum(i * halo_per_block - 1, 0), cb(off)))

    def nxt(off):
        return pl.BlockSpec((CONV_HALO, SC_W), lambda i: (jnp.minimum((i + 1) * halo_per_block, last_halo), cb(off)))

    def vec(rows):
        return pl.BlockSpec((rows, SC_W), lambda i: (0, 0))

    o_bg, o_cg, o_xi, o_a, o_g = OFF_SC, OFF_SC + SC_W, OFF_SC + 2 * SC_W, OFF_CF, OFF_CF + CF_W
    kernel = functools.partial(_conv_kernel, lat_blocks=n_lat // CONV_ROWS, blocks_per_seq=seq // CONV_ROWS)
    return pl.pallas_call(
        kernel,
        out_shape=(jax.ShapeDtypeStruct((n_rows, SC_W), BF16), jax.ShapeDtypeStruct((n_rows, CF_W), BF16)),
        grid=(nb,),
        in_specs=[main(o_bg), main(o_cg), main(o_xi), main(o_a), main(o_g),
                  prev(o_cg), prev(o_xi), prev(o_a), prev(o_g),
                  nxt(o_cg), nxt(o_xi), nxt(o_a), nxt(o_g),
                  vec(SC_TAPS), vec(CF_TAPS), vec(1), vec(1), vec(1)],
        out_specs=(pl.BlockSpec((CONV_ROWS, SC_W), lambda i: (i, 0)),
                   pl.BlockSpec((CONV_ROWS, CF_W), lambda i: (i, 0))),
        scratch_shapes=[pltpu.VMEM((CONV_ROWS + 2 * CONV_HALO, SC_W), F32)],
        compiler_params=_params(("arbitrary",)),
        name="convs",
    )(p, p, p, p, p, p, p, p, p, p, p, p, p,
      sc_w, cf_w, cf_b.reshape(1, CF_W), ln_g.reshape(1, CF_W), ln_b.reshape(1, CF_W))


def _merge_kernel(at_ref, sc_ref, cf_ref, ga_ref, gb_ref, gc_ref, ba_ref, bb_ref, bc_ref,
                  wa_ref, wb_ref, wc_ref, wm_ref, h_ref, g1_ref, o_ref, acc_ref):
    k = pl.program_id(1)
    ya = jnp.dot(at_ref[...], wa_ref[...], preferred_element_type=F32)
    yb = jnp.dot(sc_ref[...], wb_ref[...], preferred_element_type=F32)
    yc = jnp.dot(cf_ref[...], wc_ref[...], preferred_element_type=F32)
    ga = _sigmoid(ga_ref[...].astype(F32) + ba_ref[...])
    gb = _sigmoid(gb_ref[...].astype(F32) + bb_ref[...])
    gc = _sigmoid(gc_ref[...].astype(F32) + bc_ref[...])
    m = (ga * ya + gb * yb + gc * yc).astype(BF16)
    part = jnp.dot(m, wm_ref[...], preferred_element_type=F32)

    @pl.when(k == 0)
    def _():
        acc_ref[...] = part

    @pl.when(k != 0)
    def _():
        acc_ref[...] += part

    @pl.when(k == pl.num_programs(1) - 1)
    def _():
        o_ref[...] = h_ref[...] + g1_ref[...] * acc_ref[...]


def _merge(attn, sc, cf, p, b_gate, wa, wb, wc, wm, h, mods, *, n_rows, n_lat, seq, gate_idx, tm=512, tk=512):
    d = D_MODEL
    nk = d // tk
    gb0 = OFF_GATE // tk

    def gate(branch):
        return pl.BlockSpec((tm, tk), lambda i, k: (i, gb0 + branch * nk + k))

    def bias(branch):
        return pl.BlockSpec((1, tk), lambda i, k: (0, branch * nk + k))

    return pl.pallas_call(
        _merge_kernel,
        out_shape=jax.ShapeDtypeStruct((n_rows, d), F32),
        grid=(n_rows // tm, nk),
        in_specs=[
            pl.BlockSpec((tm, ATTN_W), lambda i, k: (i, 0)),
            pl.BlockSpec((tm, SC_W), lambda i, k: (i, 0)),
            pl.BlockSpec((tm, CF_W), lambda i, k: (i, 0)),
            gate(0), gate(1), gate(2), bias(0), bias(1), bias(2),
            pl.BlockSpec((ATTN_W, tk), lambda i, k: (0, k)),
            pl.BlockSpec((SC_W, tk), lambda i, k: (0, k)),
            pl.BlockSpec((CF_W, tk), lambda i, k: (0, k)),
            pl.BlockSpec((tk, d), lambda i, k: (k, 0)),
            pl.BlockSpec((tm, d), lambda i, k: (i, 0)),
            pl.BlockSpec((None, 1, d), lambda i, k: (_mod_row(i * tm, n_lat, seq), 0, gate_idx)),
        ],
        out_specs=pl.BlockSpec((tm, d), lambda i, k: (i, 0)),
        scratch_shapes=[pltpu.VMEM((tm, d), F32)],
        compiler_params=_params(("arbitrary", "arbitrary")),
        name="merge",
    )(attn, sc, cf, p, p, p, b_gate, b_gate, b_gate, wa, wb, wc, wm, h, mods)


def _split3(x):
    hi = x.astype(BF16)
    r = x - hi.astype(F32)
    mid = r.astype(BF16)
    lo = (r - mid.astype(F32)).astype(BF16)
    return hi, mid, lo


def _router_kernel(h_ref, g_ref, sh_ref, sc_ref, rw_ref, rb_ref, f_ref, id_ref, wt_ref):
    x = h_ref[...]
    ms = jnp.mean(x * x, axis=-1, keepdims=True)
    f = x * lax.rsqrt(ms + NORM_EPS) * g_ref[...] * (1.0 + sc_ref[...]) + sh_ref[...]
    f_ref[...] = f

    f_hi, f_mid, f_lo = _split3(f)
    w_hi, w_mid, w_lo = _split3(rw_ref[...])
    dot = lambda a, b: jnp.dot(a, b, preferred_element_type=F32)
    logits = (dot(f_lo, w_hi) + dot(f_mid, w_mid) + dot(f_hi, w_lo)
              + dot(f_mid, w_hi) + dot(f_hi, w_mid) + dot(f_hi, w_hi)) + rb_ref[...]

    lane_i = lax.broadcasted_iota(I32, logits.shape, 1)
    lane = lane_i.astype(F32)
    neg = -jnp.inf
    big = float(LANES)
    lg = jnp.where(lane_i < N_GROUPS, logits, neg)
    mg = jnp.max(lg, axis=-1, keepdims=True)
    grp = jnp.min(jnp.where(lg == mg, lane, big), axis=-1, keepdims=True)
    p_grp = 1.0 / jnp.sum(jnp.exp(lg - mg), axis=-1, keepdims=True)

    e_lane = lane - N_GROUPS
    in_grp = jnp.logical_and(e_lane >= grp * EXPERTS_PER_GROUP, e_lane < (grp + 1.0) * EXPERTS_PER_GROUP)
    le = jnp.where(in_grp, logits, neg)
    m1 = jnp.max(le, axis=-1, keepdims=True)
    i1 = jnp.min(jnp.where(le == m1, lane, big), axis=-1, keepdims=True)
    le2 = jnp.where(lane == i1, neg, le)
    m2 = jnp.max(le2, axis=-1, keepdims=True)
    i2 = jnp.min(jnp.where(le2 == m2, lane, big), axis=-1, keepdims=True)
    e2 = jnp.exp(m2 - m1)
    w1 = p_grp / (1.0 + e2)
    w2 = p_grp * e2 / (1.0 + e2)
    ids = jnp.where(lane_i == 0, i1 - N_GROUPS, jnp.where(lane_i == 1, i2 - N_GROUPS, 0.0))
    id_ref[...] = ids.astype(I32)
    wt_ref[...] = jnp.where(lane_i == 0, w1, jnp.where(lane_i == 1, w2, 0.0))


def _router(h, gain, mods, rw, rb, *, n_rows, n_lat, seq, sh_idx, sc_idx, tm=512):
    d = D_MODEL

    def mod_map(idx):
        return lambda i: (_mod_row(i * tm, n_lat, seq), 0, idx)

    return pl.pallas_call(
        _router_kernel,
        out_shape=(jax.ShapeDtypeStruct((n_rows, d), F32),
                   jax.ShapeDtypeStruct((n_rows, LANES), I32),
                   jax.ShapeDtypeStruct((n_rows, LANES), F32)),
        grid=(n_rows // tm,),
        in_specs=[
            pl.BlockSpec((tm, d), lambda i: (i, 0)),
            pl.BlockSpec((1, d), lambda i: (0, 0)),
            pl.BlockSpec((None, 1, d), mod_map(sh_idx)),
            pl.BlockSpec((None, 1, d), mod_map(sc_idx)),
            pl.BlockSpec((d, LANES), lambda i: (0, 0)),
            pl.BlockSpec((1, LANES), lambda i: (0, 0)),
        ],
        out_specs=(pl.BlockSpec((tm, d), lambda i: (i, 0)),
                   pl.BlockSpec((tm, LANES), lambda i: (i, 0)),
                   pl.BlockSpec((tm, LANES), lambda i: (i, 0))),
        compiler_params=_params(("arbitrary",)),
        name="router",
    )(h, gain.reshape(1, d), mods, mods, rw, rb)


def _moe_plan(ids, n_tok):
    n_items = N_EXPERTS + (2 * n_tok) // MOE_ITEM_ROWS
    e_flat = ids[:, :2].reshape(-1)
    order = jnp.argsort(e_flat, stable=True).astype(I32)
    counts = jnp.sum((e_flat[:, None] == jnp.arange(N_EXPERTS, dtype=I32)[None, :]).astype(I32), axis=0)
    starts = jnp.cumsum(counts) - counts
    n_sb = (counts + MOE_ITEM_ROWS - 1) // MOE_ITEM_ROWS
    cum = jnp.cumsum(n_sb)
    total = cum[-1]
    it = jnp.arange(n_items, dtype=I32)
    e_i = jnp.minimum(jnp.searchsorted(cum, it, side="right").astype(I32), N_EXPERTS - 1)
    sb = it - (cum[e_i] - n_sb[e_i])
    valid = it < total
    last_e = e_i[jnp.maximum(total - 1, 0)]
    item_e = jnp.where(valid, e_i, last_e).astype(I32)
    item_start = jnp.where(valid, starts[e_i] + sb * MOE_ITEM_ROWS, 0).astype(I32)
    item_n = jnp.where(valid, jnp.minimum(MOE_ITEM_ROWS, counts[e_i] - sb * MOE_ITEM_ROWS), 0).astype(I32)
    return item_e, item_start, item_n, (order // 2).astype(I32), order


def _moe_kernel(ie_ref, ist_ref, in_ref, tok_ref, slot_ref,
                f_hbm, wg_ref, wu_ref, wd_ref, out_hbm,
                xf_ref, xb_ref, y_ref, wgu_ref, wdb_ref, gsem, ssem):
    i, j = pl.program_id(0), pl.program_id(1)
    nj = pl.num_programs(1)
    n = in_ref[i]
    start = ist_ref[i]
    hk = MOE_HIDDEN_BLOCK

    def row_in(r, tok):
        return pltpu.make_async_copy(f_hbm.at[pl.ds(tok, 1), :], xf_ref.at[pl.ds(r, 1), :], gsem)

    def row_out(r, slot):
        return pltpu.make_async_copy(y_ref.at[pl.ds(r, 1), :], out_hbm.at[pl.ds(slot, 1), :], ssem)

    @pl.when(jnp.logical_and(i == 0, j == 0))
    def _():
        xf_ref[...] = jnp.zeros_like(xf_ref)

    @pl.when(n > 0)
    def _():
        @pl.when(j == 0)
        def _():
            def issue(r, c):
                row_in(r, tok_ref[start + r]).start()
                return c

            lax.fori_loop(0, n, issue, 0)

            def drain(r, c):
                row_in(0, 0).wait()
                return c

            lax.fori_loop(0, n, drain, 0)
            xb_ref[...] = xf_ref[...].astype(BF16)

        wgu_ref[:, 0:hk] = wg_ref[...].astype(BF16)
        wgu_ref[:, hk:2 * hk] = wu_ref[...].astype(BF16)
        wdb_ref[...] = wd_ref[...].astype(BF16)

        def block(rb, c):
            rows = pl.ds(pl.multiple_of(rb * MOE_ROW_BLOCK, MOE_ROW_BLOCK), MOE_ROW_BLOCK)
            gu = jnp.dot(xb_ref[rows, :], wgu_ref[...], preferred_element_type=F32)
            g, u = gu[:, 0:hk], gu[:, hk:2 * hk]
            hmid = (g * _sigmoid(g) * u).astype(BF16)
            y = jnp.dot(hmid, wdb_ref[...], preferred_element_type=F32)

            @pl.when(j == 0)
            def _():
                y_ref[rows, :] = y

            @pl.when(j != 0)
            def _():
                y_ref[rows, :] += y

            return c

        lax.fori_loop(0, (n + MOE_ROW_BLOCK - 1) // MOE_ROW_BLOCK, block, 0)

        @pl.when(j == nj - 1)
        def _():
            def issue(r, c):
                row_out(r, slot_ref[start + r]).start()
                return c

            lax.fori_loop(0, n, issue, 0)

            def drain(r, c):
                row_out(0, 0).wait()
                return c

            lax.fori_loop(0, n, drain, 0)


def _moe(f, plan, w_gu, w_down, layer, *, n_tok):
    item_e, item_start, item_n, tok_sorted, slot_sorted = plan
    d, hid, hk = D_MODEL, EXPERT_HIDDEN, MOE_HIDDEN_BLOCK
    nj = hid // hk
    n_items = item_e.shape[0]

    def chunk(j, nn, i):
        return jnp.where(nn[i] > 0, j, nj - 1)

    grid_spec = pltpu.PrefetchScalarGridSpec(
        num_scalar_prefetch=5,
        grid=(n_items, nj),
        in_specs=[
            pl.BlockSpec(memory_space=pl.ANY),
            pl.BlockSpec((None, None, d, hk), lambda i, j, ie, ist, nn, tk, sl: (layer, ie[i], 0, chunk(j, nn, i))),
            pl.BlockSpec((None, None, d, hk),
                         lambda i, j, ie, ist, nn, tk, sl: (layer, ie[i], 0, nj + chunk(j, nn, i))),
            pl.BlockSpec((None, None, hk, d), lambda i, j, ie, ist, nn, tk, sl: (layer, ie[i], chunk(j, nn, i), 0)),
        ],
        out_specs=pl.BlockSpec(memory_space=pl.ANY),
        scratch_shapes=[
            pltpu.VMEM((MOE_ITEM_ROWS, d), F32),
            pltpu.VMEM((MOE_ITEM_ROWS, d), BF16),
            pltpu.VMEM((MOE_ITEM_ROWS, d), F32),
            pltpu.VMEM((d, 2 * hk), BF16),
            pltpu.VMEM((hk, d), BF16),
            pltpu.SemaphoreType.DMA,
            pltpu.SemaphoreType.DMA,
        ],
    )
    return pl.pallas_call(
        _moe_kernel,
        out_shape=jax.ShapeDtypeStruct((2 * n_tok, d), F32),
        grid_spec=grid_spec,
        compiler_params=_params(("arbitrary", "arbitrary")),
        name="moe_experts",
    )(item_e, item_start, item_n, tok_sorted, slot_sorted, f, w_gu, w_gu, w_down)


def _combine_kernel(h_ref, y_ref, wt_ref, g2_ref, fg_ref, o_ref, *, final):
    d = D_MODEL
    wt = wt_ref[...]
    y = y_ref[...]
    moe = wt[:, 0:1] * y[:, 0:d] + wt[:, 1:2] * y[:, d:2 * d]
    h = h_ref[...] + g2_ref[...] * moe
    if final:
        ms = jnp.mean(h * h, axis=-1, keepdims=True)
        h = h * lax.rsqrt(ms + NORM_EPS) * fg_ref[...]
    o_ref[...] = h


def _combine(h, slots, wts, mods, final_g, *, n_rows, n_lat, seq, gate_idx, final, tm=256):
    d = D_MODEL
    kernel = functools.partial(_combine_kernel, final=final)
    return pl.pallas_call(
        kernel,
        out_shape=jax.ShapeDtypeStruct((n_rows, d), F32),
        grid=(n_rows // tm,),
        in_specs=[
            pl.BlockSpec((tm, d), lambda i: (i, 0)),
            pl.BlockSpec((tm, 2 * d), lambda i: (i, 0)),
            pl.BlockSpec((tm, LANES), lambda i: (i, 0)),
            pl.BlockSpec((None, 1, d), lambda i: (_mod_row(i * tm, n_lat, seq), 0, gate_idx)),
            pl.BlockSpec((1, d), lambda i: (0, 0)),
        ],
        out_specs=pl.BlockSpec((tm, d), lambda i: (i, 0)),
        compiler_params=_params(("arbitrary",)),
        name="combine",
    )(h, slots.reshape(n_rows, 2 * d), wts, mods, final_g.reshape(1, d))


def kernel(x, c, ctx, c_ctx, ada_w, ada_b, norm1_g, w_in, b_gate, diff_lambda, subln_g, w_attn_out, sc_conv_w,
           w_sc_out, cf_dw_w, cf_dw_b, cf_ln_g, cf_ln_b, w_cf_out, w_mix, norm2_g, router_g_w, router_g_b,
           router_e_w, router_e_b, exp_w_gu, exp_w_down, final_g):
    batch, seq, d = x.shape
    n_ctx = ctx.shape[1]
    depth = ada_w.shape[0]
    n_lat = batch * seq
    n_all = n_lat + batch * n_ctx
    assert d == D_MODEL and batch == MOD_ROWS // 2 and w_in.shape[2] == C_TOT

    cond = jnp.concatenate([c, c_ctx[None, :], jnp.zeros((MOD_ROWS - batch - 1, d), F32)], axis=0)
    mods_all = _adaln(cond, ada_w, ada_b)
    cos, sin = _rope_tables(seq)
    h = jnp.concatenate([x.reshape(n_lat, d), ctx.reshape(batch * n_ctx, d)], axis=0)
    geo = dict(n_lat=n_lat, seq=seq)

    for layer in range(depth):
        last = layer == depth - 1
        lam_init = 0.8 - 0.6 * math.exp(-0.3 * layer)
        mods = mods_all[layer].reshape(MOD_ROWS, 1, 6 * d)
        w_in_b = w_in[layer].astype(BF16)
        n_rows = n_lat if last else n_all

        p = _in_proj(h, norm1_g[layer], mods, w_in_b, row0=0, n_rows=n_rows, col0=0, n_cols=C_TOT,
                     sh_idx=0, sc_idx=1, tm=1024, **geo)
        if last:
            kv_ctx = _in_proj(h, norm1_g[layer], mods, w_in_b, row0=n_lat, n_rows=batch * n_ctx, col0=OFF_K,
                              n_cols=OFF_SC - OFF_K, sh_idx=0, sc_idx=1, tm=n_ctx, **geo)
            kc_blk, vc_blk, ctx_rb0 = 0, QK_W // HEAD_W, 0
        else:
            kv_ctx = p
            kc_blk, vc_blk, ctx_rb0 = OFF_K // HEAD_W, OFF_V // HEAD_W, n_lat // n_ctx
        attn = _attn_lat(p, kv_ctx, kc_blk, vc_blk, ctx_rb0, cos, sin, diff_lambda[layer], subln_g[layer],
                         batch=batch, seq=seq, n_ctx=n_ctx, lam_init=lam_init)
        if not last:
            attn_c = _attn_ctx(p, n_lat // n_ctx, diff_lambda[layer], subln_g[layer],
                               batch=batch, n_ctx=n_ctx, lam_init=lam_init)
            attn = jnp.concatenate([attn, attn_c], axis=0)
        sc, cf = _convs(p, sc_conv_w[layer], cf_dw_w[layer], cf_dw_b[layer], cf_ln_g[layer], cf_ln_b[layer],
                        n_rows=n_rows, **geo)
        h_mix = _merge(attn, sc, cf, p, b_gate[layer].reshape(1, 3 * d),
                       w_attn_out[layer].astype(BF16), w_sc_out[layer].astype(BF16),
                       w_cf_out[layer].astype(BF16), w_mix[layer].astype(BF16), h, mods,
                       n_rows=n_rows, gate_idx=2, **geo)

        rw = jnp.concatenate([router_g_w[layer], router_e_w[layer],
                              jnp.zeros((d, LANES - N_GROUPS - N_EXPERTS), F32)], axis=1)
        rb = jnp.concatenate([router_g_b[layer], router_e_b[layer],
                              jnp.zeros((LANES - N_GROUPS - N_EXPERTS,), F32)]).reshape(1, LANES)
        f, ids, wts = _router(h_mix, norm2_g[layer], mods, rw, rb, n_rows=n_rows, sh_idx=3, sc_idx=4, **geo)
        slots = _moe(f, _moe_plan(ids, n_rows), exp_w_gu, exp_w_down, layer, n_tok=n_rows)
        h = _combine(h_mix, slots, wts, mods, final_g, n_rows=n_rows, gate_idx=5, final=last, **geo)

    return h.reshape(batch, seq, d)
```

```python
import functools
import math

import jax
import jax.numpy as jnp
from jax import lax
from jax.experimental import pallas as pl
from jax.experimental.pallas import tpu as pltpu

F32 = jnp.float32
BF16 = jnp.bfloat16
I32 = jnp.int32

D_MODEL = 2048
GRID_W = 64
NORM_EPS = 1e-6
N_HEADS = 8
HEAD_DIM = 64
HEAD_W = 2 * HEAD_DIM
QK_W = N_HEADS * HEAD_W
ATTN_W = N_HEADS * HEAD_W
ROPE_BASE = 10000.0
ROPE_FREQS = HEAD_DIM // 4
SUBLN_EPS = 1e-5
QK_SCALE = HEAD_DIM ** -0.5 * math.log2(math.e)
SC_W = D_MODEL // 4
CF_W = D_MODEL // 4
SC_TAPS = 3
CF_TAPS = 31
CF_LN_EPS = 1e-5
OFF_Q = 0
OFF_K = OFF_Q + QK_W
OFF_V = OFF_K + QK_W
OFF_SC = OFF_V + ATTN_W
OFF_CF = OFF_SC + 3 * SC_W
OFF_GATE = OFF_CF + 2 * CF_W
C_TOT = OFF_GATE + 3 * D_MODEL
N_GROUPS = 4
EXPERTS_PER_GROUP = 8
N_EXPERTS = N_GROUPS * EXPERTS_PER_GROUP
EXPERT_HIDDEN = D_MODEL // 2

LANES = 128
MOD_ROWS = 8
ATTN_SUB_ROWS = 256
CONV_HALO = 16
CONV_ROWS = 256
CONV_CHUNK = 32
MOE_ITEM_ROWS = 1024
MOE_ROW_PAD = 128
MOE_ISSUE_UNROLL = 8
MOE_HIDDEN_BLOCK = 256
VMEM_LIMIT = 56 * 1024 * 1024


def _params(sem, vmem=VMEM_LIMIT):
    return pltpu.CompilerParams(dimension_semantics=sem, vmem_limit_bytes=vmem)


def _sigmoid(x):
    return 1.0 / (1.0 + jnp.exp(-x))


def _adaln_kernel(s_ref, w_ref, b_ref, o_ref):
    s = s_ref[...]
    s = s * _sigmoid(s)
    s_hi = s.astype(BF16)
    s_lo = (s - s_hi.astype(F32)).astype(BF16)
    w = w_ref[...]
    w_hi = w.astype(BF16)
    w_lo = (w - w_hi.astype(F32)).astype(BF16)
    lhs = jnp.concatenate([s_hi.astype(F32), s - s_hi.astype(F32)], axis=0).astype(BF16)
    r = jnp.dot(lhs, w_hi, preferred_element_type=F32)
    r2 = jnp.dot(s_hi, w_lo, preferred_element_type=F32)
    o_ref[...] = r[:MOD_ROWS] + r[MOD_ROWS:] + r2 + b_ref[...]


def _adaln(cond, ada_w, ada_b):
    n_layers, d, n = ada_w.shape
    tn = 512
    return pl.pallas_call(
        _adaln_kernel,
        out_shape=jax.ShapeDtypeStruct((n_layers, MOD_ROWS, n), F32),
        grid=(n_layers, n // tn),
        in_specs=[
            pl.BlockSpec((MOD_ROWS, d), lambda l, j: (0, 0)),
            pl.BlockSpec((None, d, tn), lambda l, j: (l, 0, j)),
            pl.BlockSpec((None, 1, tn), lambda l, j: (l, 0, j)),
        ],
        out_specs=pl.BlockSpec((None, MOD_ROWS, tn), lambda l, j: (l, 0, j)),
        compiler_params=_params(("arbitrary", "arbitrary")),
        name="adaln",
    )(cond, ada_w, ada_b.reshape(n_layers, 1, n))


def _mod_row(row0, n_lat, seq):
    return jnp.where(row0 < n_lat, row0 // seq, MOD_ROWS // 2)


def _in_proj_kernel(x_ref, g_ref, sh_ref, sc_ref, w_ref, o_ref, u_ref):
    @pl.when(pl.program_id(1) == 0)
    def _():
        x = x_ref[...]
        ms = jnp.mean(x * x, axis=-1, keepdims=True)
        y = x * lax.rsqrt(ms + NORM_EPS) * g_ref[...]
        u_ref[...] = (y * (1.0 + sc_ref[...]) + sh_ref[...]).astype(BF16)

    o_ref[...] = jnp.dot(u_ref[...], w_ref[...].astype(BF16), preferred_element_type=F32).astype(o_ref.dtype)


def _in_proj(h, gain, mods, w_in, layer, *, row0, n_rows, col0, n_cols, n_lat, seq, sh_idx, sc_idx, tm, tn=512):
    d = h.shape[1]
    rb0, cb0 = row0 // tm, col0 // tn

    def mod_map(idx):
        return lambda i, j: (_mod_row((i + rb0) * tm, n_lat, seq), 0, idx)

    return pl.pallas_call(
        _in_proj_kernel,
        out_shape=jax.ShapeDtypeStruct((n_rows, n_cols), BF16),
        grid=(n_rows // tm, n_cols // tn),
        in_specs=[
            pl.BlockSpec((tm, d), lambda i, j: (i + rb0, 0)),
            pl.BlockSpec((1, d), lambda i, j: (0, 0)),
            pl.BlockSpec((None, 1, d), mod_map(sh_idx)),
            pl.BlockSpec((None, 1, d), mod_map(sc_idx)),
            pl.BlockSpec((None, d, tn), lambda i, j: (layer, 0, j + cb0)),
        ],
        out_specs=pl.BlockSpec((tm, tn), lambda i, j: (i, j)),
        scratch_shapes=[pltpu.VMEM((tm, d), BF16)],
        compiler_params=_params(("arbitrary", "arbitrary")),
        name="in_proj",
    )(h, gain.reshape(1, d), mods, mods, w_in)


def _rope_tables(n_tokens):
    t = jnp.arange(n_tokens, dtype=I32)
    pos = jnp.stack([t // GRID_W, t % GRID_W], axis=-1).astype(F32)
    inv_freq = ROPE_BASE ** (-jnp.arange(ROPE_FREQS, dtype=F32) / ROPE_FREQS)
    ang = pos[:, :, None] * inv_freq
    cos, sin = jnp.cos(ang), jnp.sin(ang)
    c = jnp.stack([cos, cos], axis=2).reshape(n_tokens, HEAD_DIM)
    s = jnp.stack([-sin, sin], axis=2).reshape(n_tokens, HEAD_DIM)
    return jnp.tile(c, (1, 2)), jnp.tile(s, (1, 2))


def _rope(x, c, s):
    lane = lax.broadcasted_iota(I32, x.shape, 1)
    first_half = (lane % (2 * ROPE_FREQS)) < ROPE_FREQS
    partner = jnp.where(first_half, pltpu.roll(x, LANES - ROPE_FREQS, 1), pltpu.roll(x, ROPE_FREQS, 1))
    return x * c + partner * s


def _diff_lambda(lam_ref, lam_init):
    lv = lam_ref[...]
    a = jnp.sum(lv[0:1] * lv[1:2], axis=-1, keepdims=True)
    b = jnp.sum(lv[2:3] * lv[3:4], axis=-1, keepdims=True)
    return jnp.exp(a) - jnp.exp(b) + lam_init


def _attend(q, k_all, v_all, lam, subg, lam_init):
    tq = q.shape[0]
    lane = lax.broadcasted_iota(I32, q.shape, 1)
    q0 = jnp.where(lane < HEAD_DIM, q, 0.0).astype(BF16)
    q1 = jnp.where(lane >= HEAD_DIM, q, 0.0).astype(BF16)
    qq = jnp.concatenate([q0, q1], axis=0)
    s = lax.dot_general(qq, k_all, (((1,), (1,)), ((), ())), preferred_element_type=F32)
    m = jnp.max(s, axis=-1, keepdims=True)
    e = jnp.exp2(s - m)
    l = jnp.sum(e, axis=-1, keepdims=True)
    a = e[:tq] - e[tq:] * (lam * l[:tq] / l[tq:])
    o = jnp.dot(a.astype(BF16), v_all, preferred_element_type=F32) / l[:tq]
    ms = jnp.mean(o * o, axis=-1, keepdims=True)
    return o * lax.rsqrt(ms + SUBLN_EPS) * subg * (1.0 - lam_init)


def _attn_lat_kernel(q_ref, kl_ref, vl_ref, kc_ref, vc_ref, cq_ref, sq_ref, ck_ref, sk_ref, lam_ref, g_ref,
                     o_ref, k_all, v_all, *, n_ctx, lam_init):
    @pl.when(pl.program_id(2) == 0)
    def _():
        k_all[0:n_ctx, :] = kc_ref[...]
        v_all[0:n_ctx, :] = vc_ref[...]
        k_all[n_ctx:, :] = _rope(kl_ref[...].astype(F32), ck_ref[...], sk_ref[...]).astype(BF16)
        v_all[n_ctx:, :] = vl_ref[...]

    lam = _diff_lambda(lam_ref, lam_init)
    for r in range(0, q_ref.shape[0], ATTN_SUB_ROWS):
        rows = slice(r, r + ATTN_SUB_ROWS)
        q = _rope(q_ref[rows, :].astype(F32), cq_ref[rows, :], sq_ref[rows, :]) * QK_SCALE
        o_ref[rows, :] = _attend(q, k_all[...], v_all[...], lam, g_ref[...], lam_init).astype(o_ref.dtype)


def _attn_ctx_kernel(q_ref, k_ref, v_ref, lam_ref, g_ref, o_ref, *, lam_init):
    q = q_ref[...].astype(F32) * QK_SCALE
    lam = _diff_lambda(lam_ref, lam_init)
    o_ref[...] = _attend(q, k_ref[...], v_ref[...], lam, g_ref[...], lam_init).astype(o_ref.dtype)


def _attn_lat(p, kv_ctx, kc_blk, vc_blk, ctx_rb0, cos, sin, lam4, subg, *, batch, seq, n_ctx, lam_init, tq=256):
    nq = seq // tq
    hb = lambda off: off // HEAD_W
    kernel = functools.partial(_attn_lat_kernel, n_ctx=n_ctx, lam_init=lam_init)
    return pl.pallas_call(
        kernel,
        out_shape=jax.ShapeDtypeStruct((batch * seq, ATTN_W), BF16),
        grid=(batch, N_HEADS, nq),
        in_specs=[
            pl.BlockSpec((tq, HEAD_W), lambda b, h, i: (b * nq + i, hb(OFF_Q) + h)),
            pl.BlockSpec((seq, HEAD_W), lambda b, h, i: (b, hb(OFF_K) + h)),
            pl.BlockSpec((seq, HEAD_W), lambda b, h, i: (b, hb(OFF_V) + h)),
            pl.BlockSpec((n_ctx, HEAD_W), lambda b, h, i: (ctx_rb0 + b, kc_blk + h)),
            pl.BlockSpec((n_ctx, HEAD_W), lambda b, h, i: (ctx_rb0 + b, vc_blk + h)),
            pl.BlockSpec((tq, HEAD_W), lambda b, h, i: (i, 0)),
            pl.BlockSpec((tq, HEAD_W), lambda b, h, i: (i, 0)),
            pl.BlockSpec((seq, HEAD_W), lambda b, h, i: (0, 0)),
            pl.BlockSpec((seq, HEAD_W), lambda b, h, i: (0, 0)),
            pl.BlockSpec((4, HEAD_DIM), lambda b, h, i: (0, 0)),
            pl.BlockSpec((1, HEAD_W), lambda b, h, i: (0, 0)),
        ],
        out_specs=pl.BlockSpec((tq, HEAD_W), lambda b, h, i: (b * nq + i, h)),
        scratch_shapes=[pltpu.VMEM((n_ctx + seq, HEAD_W), BF16), pltpu.VMEM((n_ctx + seq, HEAD_W), BF16)],
        compiler_params=_params(("arbitrary", "arbitrary", "arbitrary")),
        name="attn_lat",
    )(p, p, p, kv_ctx, kv_ctx, cos, sin, cos, sin, lam4, subg.reshape(1, HEAD_W))


def _attn_ctx(p, ctx_rb0, lam4, subg, *, batch, n_ctx, lam_init):
    hb = lambda off: off // HEAD_W
    kernel = functools.partial(_attn_ctx_kernel, lam_init=lam_init)
    return pl.pallas_call(
        kernel,
        out_shape=jax.ShapeDtypeStruct((batch * n_ctx, ATTN_W), BF16),
        grid=(batch, N_HEADS),
        in_specs=[
            pl.BlockSpec((n_ctx, HEAD_W), lambda b, h: (ctx_rb0 + b, hb(OFF_Q) + h)),
            pl.BlockSpec((n_ctx, HEAD_W), lambda b, h: (ctx_rb0 + b, hb(OFF_K) + h)),
            pl.BlockSpec((n_ctx, HEAD_W), lambda b, h: (ctx_rb0 + b, hb(OFF_V) + h)),
            pl.BlockSpec((4, HEAD_DIM), lambda b, h: (0, 0)),
            pl.BlockSpec((1, HEAD_W), lambda b, h: (0, 0)),
        ],
        out_specs=pl.BlockSpec((n_ctx, HEAD_W), lambda b, h: (b, h)),
        compiler_params=_params(("arbitrary", "arbitrary")),
        name="attn_ctx",
    )(p, p, p, lam4, subg.reshape(1, HEAD_W))


def _conv_kernel(bg_ref, cg_ref, xi_ref, a_ref, g_ref,
                 cg_p, xi_p, a_p, g_p, cg_n, xi_n, a_n, g_n,
                 scw_ref, cfw_ref, cfb_ref, lng_ref, lnb_ref,
                 sco_ref, cfo_ref, pad_ref, *, lat_blocks, blocks_per_seq):
    i = pl.program_id(0)
    in_lat = i < lat_blocks
    pos = i % blocks_per_seq
    has_prev = jnp.logical_and(in_lat, pos != 0)
    has_next = jnp.logical_and(in_lat, pos != blocks_per_seq - 1)
    keep_prev = jnp.where(has_prev, 1.0, 0.0)
    keep_next = jnp.where(has_next, 1.0, 0.0)
    lo, hi = CONV_HALO, CONV_HALO + CONV_ROWS

    def fill(main, prev, nxt):
        pad_ref[0:lo, :] = prev * keep_prev
        pad_ref[lo:hi, :] = main
        pad_ref[hi:hi + CONV_HALO, :] = nxt * keep_next

    def f(ref):
        return ref[...].astype(F32)

    def glu(a, g):
        return a * _sigmoid(g)

    fill(f(cg_ref) * f(xi_ref), f(cg_p) * f(xi_p), f(cg_n) * f(xi_n))
    for c in range(CONV_ROWS // CONV_CHUNK):
        r0 = lo + c * CONV_CHUNK - SC_TAPS // 2
        acc = scw_ref[0:1, :] * pad_ref[r0:r0 + CONV_CHUNK, :]
        for k in range(1, SC_TAPS):
            acc = acc + scw_ref[k:k + 1, :] * pad_ref[r0 + k:r0 + k + CONV_CHUNK, :]
        rows = slice(c * CONV_CHUNK, (c + 1) * CONV_CHUNK)
        sco_ref[rows, :] = (bg_ref[rows, :].astype(F32) * acc).astype(sco_ref.dtype)

    fill(glu(f(a_ref), f(g_ref)), glu(f(a_p), f(g_p)), glu(f(a_n), f(g_n)))
    for c in range(CONV_ROWS // CONV_CHUNK):
        r0 = lo + c * CONV_CHUNK - CF_TAPS // 2
        acc = cfw_ref[0:1, :] * pad_ref[r0:r0 + CONV_CHUNK, :]
        for k in range(1, CF_TAPS):
            acc = acc + cfw_ref[k:k + 1, :] * pad_ref[r0 + k:r0 + k + CONV_CHUNK, :]
        z = acc + cfb_ref[...]
        mu = jnp.mean(z, axis=-1, keepdims=True)
        zc = z - mu
        var = jnp.mean(zc * zc, axis=-1, keepdims=True)
        y = zc * lax.rsqrt(var + CF_LN_EPS) * lng_ref[...] + lnb_ref[...]
        rows = slice(c * CONV_CHUNK, (c + 1) * CONV_CHUNK)
        cfo_ref[rows, :] = (y * _sigmoid(y)).astype(cfo_ref.dtype)


def _convs(p, sc_w, cf_w, cf_b, ln_g, ln_b, *, n_rows, n_lat, seq):
    nb = n_rows // CONV_ROWS
    halo_per_block = CONV_ROWS // CONV_HALO
    last_halo = n_rows // CONV_HALO - 1
    cb = lambda off: off // SC_W

    def main(off):
        return pl.BlockSpec((CONV_ROWS, SC_W), lambda i: (i, cb(off)))

    def prev(off):
        return pl.BlockSpec((CONV_HALO, SC_W), lambda i: (jnp.maximum(i * halo_per_block - 1, 0), cb(off)))

    def nxt(off):
        return pl.BlockSpec((CONV_HALO, SC_W), lambda i: (jnp.minimum((i + 1) * halo_per_block, last_halo), cb(off)))

    def vec(rows):
        return pl.BlockSpec((rows, SC_W), lambda i: (0, 0))

    o_bg, o_cg, o_xi, o_a, o_g = OFF_SC, OFF_SC + SC_W, OFF_SC + 2 * SC_W, OFF_CF, OFF_CF + CF_W
    kernel = functools.partial(_conv_kernel, lat_blocks=n_lat // CONV_ROWS, blocks_per_seq=seq // CONV_ROWS)
    return pl.pallas_call(
        kernel,
        out_shape=(jax.ShapeDtypeStruct((n_rows, SC_W), BF16), jax.ShapeDtypeStruct((n_rows, CF_W), BF16)),
        grid=(nb,),
        in_specs=[main(o_bg), main(o_cg), main(o_xi), main(o_a), main(o_g),
                  prev(o_cg), prev(o_xi), prev(o_a), prev(o_g),
                  nxt(o_cg), nxt(o_xi), nxt(o_a), nxt(o_g),
                  vec(SC_TAPS), vec(CF_TAPS), vec(1), vec(1), vec(1)],
        out_specs=(pl.BlockSpec((CONV_ROWS, SC_W), lambda i: (i, 0)),
                   pl.BlockSpec((CONV_ROWS, CF_W), lambda i: (i, 0))),
        scratch_shapes=[pltpu.VMEM((CONV_ROWS + 2 * CONV_HALO, SC_W), F32)],
        compiler_params=_params(("arbitrary",)),
        name="convs",
    )(p, p, p, p, p, p, p, p, p, p, p, p, p,
      sc_w, cf_w, cf_b.reshape(1, CF_W), ln_g.reshape(1, CF_W), ln_b.reshape(1, CF_W))


def _merge_kernel(at_ref, sc_ref, cf_ref, ga_ref, gb_ref, gc_ref, ba_ref, bb_ref, bc_ref,
                  wa_ref, wb_ref, wc_ref, wm_ref, h_ref, g1_ref, o_ref, acc_ref):
    k = pl.program_id(1)
    ya = jnp.dot(at_ref[...], wa_ref[...].astype(BF16), preferred_element_type=F32)
    yb = jnp.dot(sc_ref[...], wb_ref[...].astype(BF16), preferred_element_type=F32)
    yc = jnp.dot(cf_ref[...], wc_ref[...].astype(BF16), preferred_element_type=F32)
    ga = _sigmoid(ga_ref[...].astype(F32) + ba_ref[...])
    gb = _sigmoid(gb_ref[...].astype(F32) + bb_ref[...])
    gc = _sigmoid(gc_ref[...].astype(F32) + bc_ref[...])
    m = (ga * ya + gb * yb + gc * yc).astype(BF16)
    part = jnp.dot(m, wm_ref[...].astype(BF16), preferred_element_type=F32)

    @pl.when(k == 0)
    def _():
        acc_ref[...] = part

    @pl.when(k != 0)
    def _():
        acc_ref[...] += part

    @pl.when(k == pl.num_programs(1) - 1)
    def _():
        o_ref[...] = h_ref[...] + g1_ref[...] * acc_ref[...]


def _merge(attn, sc, cf, p, b_gate, wa, wb, wc, wm, layer, h, mods, *, n_rows, n_lat, seq, gate_idx, tm=512,
           tk=512):
    d = D_MODEL
    nk = d // tk
    gb0 = OFF_GATE // tk

    def gate(branch):
        return pl.BlockSpec((tm, tk), lambda i, k: (i, gb0 + branch * nk + k))

    def bias(branch):
        return pl.BlockSpec((1, tk), lambda i, k: (0, branch * nk + k))

    return pl.pallas_call(
        _merge_kernel,
        out_shape=jax.ShapeDtypeStruct((n_rows, d), F32),
        grid=(n_rows // tm, nk),
        in_specs=[
            pl.BlockSpec((tm, ATTN_W), lambda i, k: (i, 0)),
            pl.BlockSpec((tm, SC_W), lambda i, k: (i, 0)),
            pl.BlockSpec((tm, CF_W), lambda i, k: (i, 0)),
            gate(0), gate(1), gate(2), bias(0), bias(1), bias(2),
            pl.BlockSpec((None, ATTN_W, tk), lambda i, k: (layer, 0, k)),
            pl.BlockSpec((None, SC_W, tk), lambda i, k: (layer, 0, k)),
            pl.BlockSpec((None, CF_W, tk), lambda i, k: (layer, 0, k)),
            pl.BlockSpec((None, tk, d), lambda i, k: (layer, k, 0)),
            pl.BlockSpec((tm, d), lambda i, k: (i, 0)),
            pl.BlockSpec((None, 1, d), lambda i, k: (_mod_row(i * tm, n_lat, seq), 0, gate_idx)),
        ],
        out_specs=pl.BlockSpec((tm, d), lambda i, k: (i, 0)),
        scratch_shapes=[pltpu.VMEM((tm, d), F32)],
        compiler_params=_params(("arbitrary", "arbitrary")),
        name="merge",
    )(attn, sc, cf, p, p, p, b_gate, b_gate, b_gate, wa, wb, wc, wm, h, mods)


def _split3(x):
    hi = x.astype(BF16)
    r = x - hi.astype(F32)
    mid = r.astype(BF16)
    lo = (r - mid.astype(F32)).astype(BF16)
    return hi, mid, lo


def _router_kernel(h_ref, g_ref, sh_ref, sc_ref, rw_ref, rb_ref, f_ref, id_ref, wt_ref):
    x = h_ref[...]
    ms = jnp.mean(x * x, axis=-1, keepdims=True)
    f = x * lax.rsqrt(ms + NORM_EPS) * g_ref[...] * (1.0 + sc_ref[...]) + sh_ref[...]
    f_ref[...] = f

    f_hi, f_mid, f_lo = _split3(f)
    w_hi, w_mid, w_lo = _split3(rw_ref[...])
    dot = lambda a, b: jnp.dot(a, b, preferred_element_type=F32)
    logits = (dot(f_lo, w_hi) + dot(f_mid, w_mid) + dot(f_hi, w_lo)
              + dot(f_mid, w_hi) + dot(f_hi, w_mid) + dot(f_hi, w_hi)) + rb_ref[...]

    lane_i = lax.broadcasted_iota(I32, logits.shape, 1)
    lane = lane_i.astype(F32)
    neg = -jnp.inf
    big = float(LANES)
    lg = jnp.where(lane_i < N_GROUPS, logits, neg)
    mg = jnp.max(lg, axis=-1, keepdims=True)
    grp = jnp.min(jnp.where(lg == mg, lane, big), axis=-1, keepdims=True)
    p_grp = 1.0 / jnp.sum(jnp.exp(lg - mg), axis=-1, keepdims=True)

    e_lane = lane - N_GROUPS
    in_grp = jnp.logical_and(e_lane >= grp * EXPERTS_PER_GROUP, e_lane < (grp + 1.0) * EXPERTS_PER_GROUP)
    le = jnp.where(in_grp, logits, neg)
    m1 = jnp.max(le, axis=-1, keepdims=True)
    i1 = jnp.min(jnp.where(le == m1, lane, big), axis=-1, keepdims=True)
    le2 = jnp.where(lane == i1, neg, le)
    m2 = jnp.max(le2, axis=-1, keepdims=True)
    i2 = jnp.min(jnp.where(le2 == m2, lane, big), axis=-1, keepdims=True)
    e2 = jnp.exp(m2 - m1)
    w1 = p_grp / (1.0 + e2)
    w2 = p_grp * e2 / (1.0 + e2)
    ids = jnp.where(lane_i == 0, i1 - N_GROUPS, jnp.where(lane_i == 1, i2 - N_GROUPS, 0.0))
    id_ref[...] = ids.astype(I32)
    wt_ref[...] = jnp.where(lane_i == 0, w1, jnp.where(lane_i == 1, w2, 0.0))


def _router(h, gain, mods, rw, rb, *, n_rows, n_lat, seq, sh_idx, sc_idx, tm=512):
    d = D_MODEL

    def mod_map(idx):
        return lambda i: (_mod_row(i * tm, n_lat, seq), 0, idx)

    return pl.pallas_call(
        _router_kernel,
        out_shape=(jax.ShapeDtypeStruct((n_rows, d), F32),
                   jax.ShapeDtypeStruct((n_rows, LANES), I32),
                   jax.ShapeDtypeStruct((n_rows, LANES), F32)),
        grid=(n_rows // tm,),
        in_specs=[
            pl.BlockSpec((tm, d), lambda i: (i, 0)),
            pl.BlockSpec((1, d), lambda i: (0, 0)),
            pl.BlockSpec((None, 1, d), mod_map(sh_idx)),
            pl.BlockSpec((None, 1, d), mod_map(sc_idx)),
            pl.BlockSpec((d, LANES), lambda i: (0, 0)),
            pl.BlockSpec((1, LANES), lambda i: (0, 0)),
        ],
        out_specs=(pl.BlockSpec((tm, d), lambda i: (i, 0)),
                   pl.BlockSpec((tm, LANES), lambda i: (i, 0)),
                   pl.BlockSpec((tm, LANES), lambda i: (i, 0))),
        compiler_params=_params(("arbitrary",)),
        name="router",
    )(h, gain.reshape(1, d), mods, mods, rw, rb)


def _moe_plan(ids, n_tok):
    n_items = N_EXPERTS + (2 * n_tok) // MOE_ITEM_ROWS
    e_flat = ids[:, :2].reshape(-1)
    order = jnp.argsort(e_flat, stable=True).astype(I32)
    counts = jnp.sum((e_flat[:, None] == jnp.arange(N_EXPERTS, dtype=I32)[None, :]).astype(I32), axis=0)
    starts = jnp.cumsum(counts) - counts
    n_sb = (counts + MOE_ITEM_ROWS - 1) // MOE_ITEM_ROWS
    cum = jnp.cumsum(n_sb)
    total = cum[-1]
    it = jnp.arange(n_items, dtype=I32)
    e_i = jnp.minimum(jnp.searchsorted(cum, it, side="right").astype(I32), N_EXPERTS - 1)
    sb = it - (cum[e_i] - n_sb[e_i])
    valid = it < total
    last_e = e_i[jnp.maximum(total - 1, 0)]
    item_e = jnp.where(valid, e_i, last_e).astype(I32)
    item_start = jnp.where(valid, starts[e_i] + sb * MOE_ITEM_ROWS, 0).astype(I32)
    item_n = jnp.where(valid, jnp.minimum(MOE_ITEM_ROWS, counts[e_i] - sb * MOE_ITEM_ROWS), 0).astype(I32)
    return item_e, item_start, item_n, (order // 2).astype(I32), order


def _moe_kernel(ie_ref, ist_ref, in_ref, tok_ref, slot_ref,
                f_hbm, wg_ref, wu_ref, wd_ref, out_hbm,
                xf_ref, xb_ref, y_ref, gsem, ssem):
    i, j = pl.program_id(0), pl.program_id(1)
    nj = pl.num_programs(1)
    n = in_ref[i]
    start = ist_ref[i]

    def row_in(r, tok):
        return pltpu.make_async_copy(f_hbm.at[pl.ds(tok, 1), :], xf_ref.at[pl.ds(r, 1), :], gsem)

    def row_out(r, slot):
        return pltpu.make_async_copy(y_ref.at[pl.ds(r, 1), :], out_hbm.at[slot & 1, pl.ds(slot >> 1, 1), :], ssem)

    def rows_in(size):
        return pltpu.make_async_copy(f_hbm.at[pl.ds(0, size), :], xf_ref.at[pl.ds(0, size), :], gsem)

    def rows_out(size):
        return pltpu.make_async_copy(y_ref.at[pl.ds(0, size), :], out_hbm.at[0, pl.ds(0, size), :], ssem)

    def start_rows(make, idx_ref):
        def body8(t, c):
            r0 = pl.multiple_of(t * MOE_ISSUE_UNROLL, MOE_ISSUE_UNROLL)
            for k in range(MOE_ISSUE_UNROLL):
                make(r0 + k, idx_ref[start + r0 + k]).start()
            return c

        lax.fori_loop(0, n // MOE_ISSUE_UNROLL, body8, 0)

        def body1(r, c):
            make(r, idx_ref[start + r]).start()
            return c

        lax.fori_loop((n // MOE_ISSUE_UNROLL) * MOE_ISSUE_UNROLL, n, body1, 0)

    def wait_rows(make_block):
        for b in range(MOE_ITEM_ROWS.bit_length()):
            @pl.when(((n >> b) & 1) == 1)
            def _():
                make_block(1 << b).wait()

    @pl.when(jnp.logical_and(i == 0, j == 0))
    def _():
        xf_ref[...] = jnp.zeros_like(xf_ref)

    def compute(m, first):
        x = xb_ref[0:m, :]
        g = jnp.dot(x, wg_ref[...].astype(BF16), preferred_element_type=F32)
        u = jnp.dot(x, wu_ref[...].astype(BF16), preferred_element_type=F32)
        hmid = (g * _sigmoid(g) * u).astype(BF16)
        y = jnp.dot(hmid, wd_ref[...].astype(BF16), preferred_element_type=F32)
        if first:
            y_ref[0:m, :] = y
        else:
            y_ref[0:m, :] += y

    @pl.when(n > 0)
    def _():
        @pl.when(j == 0)
        def _():
            start_rows(row_in, tok_ref)
            wait_rows(rows_in)
            xb_ref[...] = xf_ref[...].astype(BF16)

        n_pad = (n + MOE_ROW_PAD - 1) // MOE_ROW_PAD
        for k in range(1, MOE_ITEM_ROWS // MOE_ROW_PAD + 1):
            @pl.when(jnp.logical_and(n_pad == k, j == 0))
            def _():
                compute(k * MOE_ROW_PAD, True)

            @pl.when(jnp.logical_and(n_pad == k, j != 0))
            def _():
                compute(k * MOE_ROW_PAD, False)

        @pl.when(j == nj - 1)
        def _():
            start_rows(row_out, slot_ref)
            wait_rows(rows_out)


def _moe(f, plan, w_gu, w_down, layer, *, n_tok):
    item_e, item_start, item_n, tok_sorted, slot_sorted = plan
    d, hid, hk = D_MODEL, EXPERT_HIDDEN, MOE_HIDDEN_BLOCK
    nj = hid // hk
    n_items = item_e.shape[0]

    def chunk(j, nn, i):
        return jnp.where(nn[i] > 0, j, nj - 1)

    grid_spec = pltpu.PrefetchScalarGridSpec(
        num_scalar_prefetch=5,
        grid=(n_items, nj),
        in_specs=[
            pl.BlockSpec(memory_space=pl.ANY),
            pl.BlockSpec((None, None, d, hk), lambda i, j, ie, ist, nn, tk, sl: (layer, ie[i], 0, chunk(j, nn, i))),
            pl.BlockSpec((None, None, d, hk),
                         lambda i, j, ie, ist, nn, tk, sl: (layer, ie[i], 0, nj + chunk(j, nn, i))),
            pl.BlockSpec((None, None, hk, d), lambda i, j, ie, ist, nn, tk, sl: (layer, ie[i], chunk(j, nn, i), 0)),
        ],
        out_specs=pl.BlockSpec(memory_space=pl.ANY),
        scratch_shapes=[
            pltpu.VMEM((MOE_ITEM_ROWS, d), F32),
            pltpu.VMEM((MOE_ITEM_ROWS, d), BF16),
            pltpu.VMEM((MOE_ITEM_ROWS, d), F32),
            pltpu.SemaphoreType.DMA,
            pltpu.SemaphoreType.DMA,
        ],
    )
    return pl.pallas_call(
        _moe_kernel,
        out_shape=jax.ShapeDtypeStruct((2, n_tok, d), F32),
        grid_spec=grid_spec,
        compiler_params=_params(("arbitrary", "arbitrary")),
        name="moe_experts",
    )(item_e, item_start, item_n, tok_sorted, slot_sorted, f, w_gu, w_gu, w_down)


def _combine_kernel(h_ref, y0_ref, y1_ref, wt_ref, g2_ref, fg_ref, o_ref, *, final):
    wt = wt_ref[...]
    moe = wt[:, 0:1] * y0_ref[...] + wt[:, 1:2] * y1_ref[...]
    h = h_ref[...] + g2_ref[...] * moe
    if final:
        ms = jnp.mean(h * h, axis=-1, keepdims=True)
        h = h * lax.rsqrt(ms + NORM_EPS) * fg_ref[...]
    o_ref[...] = h


def _combine(h, slots, wts, mods, final_g, *, n_rows, n_lat, seq, gate_idx, final, tm=256):
    d = D_MODEL
    kernel = functools.partial(_combine_kernel, final=final)
    return pl.pallas_call(
        kernel,
        out_shape=jax.ShapeDtypeStruct((n_rows, d), F32),
        grid=(n_rows // tm,),
        in_specs=[
            pl.BlockSpec((tm, d), lambda i: (i, 0)),
            pl.BlockSpec((None, tm, d), lambda i: (0, i, 0)),
            pl.BlockSpec((None, tm, d), lambda i: (1, i, 0)),
            pl.BlockSpec((tm, LANES), lambda i: (i, 0)),
            pl.BlockSpec((None, 1, d), lambda i: (_mod_row(i * tm, n_lat, seq), 0, gate_idx)),
            pl.BlockSpec((1, d), lambda i: (0, 0)),
        ],
        out_specs=pl.BlockSpec((tm, d), lambda i: (i, 0)),
        compiler_params=_params(("arbitrary",)),
        name="combine",
    )(h, slots, slots, wts, mods, final_g.reshape(1, d))


def kernel(x, c, ctx, c_ctx, ada_w, ada_b, norm1_g, w_in, b_gate, diff_lambda, subln_g, w_attn_out, sc_conv_w,
           w_sc_out, cf_dw_w, cf_dw_b, cf_ln_g, cf_ln_b, w_cf_out, w_mix, norm2_g, router_g_w, router_g_b,
           router_e_w, router_e_b, exp_w_gu, exp_w_down, final_g):
    batch, seq, d = x.shape
    n_ctx = ctx.shape[1]
    depth = ada_w.shape[0]
    n_lat = batch * seq
    n_all = n_lat + batch * n_ctx
    assert d == D_MODEL and batch == MOD_ROWS // 2 and w_in.shape[2] == C_TOT

    cond = jnp.concatenate([c, c_ctx[None, :], jnp.zeros((MOD_ROWS - batch - 1, d), F32)], axis=0)
    mods_all = _adaln(cond, ada_w, ada_b)
    cos, sin = _rope_tables(seq)
    h = jnp.concatenate([x.reshape(n_lat, d), ctx.reshape(batch * n_ctx, d)], axis=0)
    geo = dict(n_lat=n_lat, seq=seq)

    for layer in range(depth):
        last = layer == depth - 1
        lam_init = 0.8 - 0.6 * math.exp(-0.3 * layer)
        mods = mods_all[layer].reshape(MOD_ROWS, 1, 6 * d)
        n_rows = n_lat if last else n_all

        p = _in_proj(h, norm1_g[layer], mods, w_in, layer, row0=0, n_rows=n_rows, col0=0, n_cols=C_TOT,
                     sh_idx=0, sc_idx=1, tm=1024, **geo)
        if last:
            kv_ctx = _in_proj(h, norm1_g[layer], mods, w_in, layer, row0=n_lat, n_rows=batch * n_ctx, col0=OFF_K,
                              n_cols=OFF_SC - OFF_K, sh_idx=0, sc_idx=1, tm=n_ctx, **geo)
            kc_blk, vc_blk, ctx_rb0 = 0, QK_W // HEAD_W, 0
        else:
            kv_ctx = p
            kc_blk, vc_blk, ctx_rb0 = OFF_K // HEAD_W, OFF_V // HEAD_W, n_lat // n_ctx
        attn = _attn_lat(p, kv_ctx, kc_blk, vc_blk, ctx_rb0, cos, sin, diff_lambda[layer], subln_g[layer],
                         batch=batch, seq=seq, n_ctx=n_ctx, lam_init=lam_init)
        if not last:
            attn_c = _attn_ctx(p, n_lat // n_ctx, diff_lambda[layer], subln_g[layer],
                               batch=batch, n_ctx=n_ctx, lam_init=lam_init)
            attn = jnp.concatenate([attn, attn_c], axis=0)
        sc, cf = _convs(p, sc_conv_w[layer], cf_dw_w[layer], cf_dw_b[layer], cf_ln_g[layer], cf_ln_b[layer],
                        n_rows=n_rows, **geo)
        h_mix = _merge(attn, sc, cf, p, b_gate[layer].reshape(1, 3 * d),
                       w_attn_out, w_sc_out, w_cf_out, w_mix, layer, h, mods,
                       n_rows=n_rows, gate_idx=2, **geo)

        rw = jnp.concatenate([router_g_w[layer], router_e_w[layer],
                              jnp.zeros((d, LANES - N_GROUPS - N_EXPERTS), F32)], axis=1)
        rb = jnp.concatenate([router_g_b[layer], router_e_b[layer],
                              jnp.zeros((LANES - N_GROUPS - N_EXPERTS,), F32)]).reshape(1, LANES)
        f, ids, wts = _router(h_mix, norm2_g[layer], mods, rw, rb, n_rows=n_rows, sh_idx=3, sc_idx=4, **geo)
        slots = _moe(f, _moe_plan(ids, n_rows), exp_w_gu, exp_w_down, layer, n_tok=n_rows)
        h = _combine(h_mix, slots, wts, mods, final_g, n_rows=n_rows, gate_idx=5, final=last, **geo)

    return h.reshape(batch, seq, d)
```

```python
import functools
import math

import jax
import jax.numpy as jnp
from jax import lax
from jax.experimental import pallas as pl
from jax.experimental.pallas import tpu as pltpu

F32 = jnp.float32
BF16 = jnp.bfloat16
I32 = jnp.int32

D_MODEL = 2048
GRID_W = 64
NORM_EPS = 1e-6
N_HEADS = 8
HEAD_DIM = 64
HEAD_W = 2 * HEAD_DIM
QK_W = N_HEADS * HEAD_W
ATTN_W = N_HEADS * HEAD_W
ROPE_BASE = 10000.0
ROPE_FREQS = HEAD_DIM // 4
SUBLN_EPS = 1e-5
QK_SCALE = HEAD_DIM ** -0.5 * math.log2(math.e)
SC_W = D_MODEL // 4
CF_W = D_MODEL // 4
SC_TAPS = 3
CF_TAPS = 31
CF_LN_EPS = 1e-5
OFF_Q = 0
OFF_K = OFF_Q + QK_W
OFF_V = OFF_K + QK_W
OFF_SC = OFF_V + ATTN_W
OFF_CF = OFF_SC + 3 * SC_W
OFF_GATE = OFF_CF + 2 * CF_W
C_TOT = OFF_GATE + 3 * D_MODEL
N_GROUPS = 4
EXPERTS_PER_GROUP = 8
N_EXPERTS = N_GROUPS * EXPERTS_PER_GROUP
EXPERT_HIDDEN = D_MODEL // 2

LANES = 128
SUBLANES = 8
MOD_ROWS = 8
ATTN_SUB_ROWS = 256
CONV_HALO = 16
CONV_ROWS = 256
CONV_CHUNK = 32
MOE_ITEM_ROWS = 1024
MOE_ROW_PAD = 128
MOE_HIDDEN_BLOCK = 256
MOE_ITEM_SHIFT = 1
MOE_ISSUE_UNROLL = 8
VMEM_LIMIT = 56 * 1024 * 1024


def _params(sem, vmem=VMEM_LIMIT):
    return pltpu.CompilerParams(dimension_semantics=sem, vmem_limit_bytes=vmem)


def _sigmoid(x):
    return 1.0 / (1.0 + jnp.exp(-x))


def _adaln_kernel(s_ref, w_ref, b_ref, o_ref):
    s = s_ref[...]
    s = s * _sigmoid(s)
    s_hi = s.astype(BF16)
    s_lo = (s - s_hi.astype(F32)).astype(BF16)
    w = w_ref[...]
    w_hi = w.astype(BF16)
    w_lo = (w - w_hi.astype(F32)).astype(BF16)
    lhs = jnp.concatenate([s_hi.astype(F32), s - s_hi.astype(F32)], axis=0).astype(BF16)
    r = jnp.dot(lhs, w_hi, preferred_element_type=F32)
    r2 = jnp.dot(s_hi, w_lo, preferred_element_type=F32)
    o_ref[...] = r[:MOD_ROWS] + r[MOD_ROWS:] + r2 + b_ref[...]


def _adaln(cond, ada_w, ada_b):
    n_layers, d, n = ada_w.shape
    tn = 512
    return pl.pallas_call(
        _adaln_kernel,
        out_shape=jax.ShapeDtypeStruct((n_layers, MOD_ROWS, n), F32),
        grid=(n_layers, n // tn),
        in_specs=[
            pl.BlockSpec((MOD_ROWS, d), lambda l, j: (0, 0)),
            pl.BlockSpec((None, d, tn), lambda l, j: (l, 0, j)),
            pl.BlockSpec((None, 1, tn), lambda l, j: (l, 0, j)),
        ],
        out_specs=pl.BlockSpec((None, MOD_ROWS, tn), lambda l, j: (l, 0, j)),
        compiler_params=_params(("arbitrary", "arbitrary")),
        name="adaln",
    )(cond, ada_w, ada_b.reshape(n_layers, 1, n))


def _mod_row(row0, n_lat, seq):
    return jnp.where(row0 < n_lat, row0 // seq, MOD_ROWS // 2)


def _in_proj_kernel(x_ref, g_ref, sh_ref, sc_ref, w_ref, o_ref, u_ref):
    @pl.when(pl.program_id(1) == 0)
    def _():
        x = x_ref[...]
        ms = jnp.mean(x * x, axis=-1, keepdims=True)
        y = x * lax.rsqrt(ms + NORM_EPS) * g_ref[...]
        u_ref[...] = (y * (1.0 + sc_ref[...]) + sh_ref[...]).astype(BF16)

    o_ref[...] = jnp.dot(u_ref[...], w_ref[...].astype(BF16), preferred_element_type=F32).astype(o_ref.dtype)


def _in_proj(h, gain, mods, w_in, layer, *, row0, n_rows, col0, n_cols, n_lat, seq, sh_idx, sc_idx, tm, tn=512):
    d = h.shape[1]
    rb0, cb0 = row0 // tm, col0 // tn

    def mod_map(idx):
        return lambda i, j: (_mod_row((i + rb0) * tm, n_lat, seq), 0, idx)

    return pl.pallas_call(
        _in_proj_kernel,
        out_shape=jax.ShapeDtypeStruct((n_rows, n_cols), BF16),
        grid=(n_rows // tm, n_cols // tn),
        in_specs=[
            pl.BlockSpec((tm, d), lambda i, j: (i + rb0, 0)),
            pl.BlockSpec((1, d), lambda i, j: (0, 0)),
            pl.BlockSpec((None, 1, d), mod_map(sh_idx)),
            pl.BlockSpec((None, 1, d), mod_map(sc_idx)),
            pl.BlockSpec((None, d, tn), lambda i, j: (layer, 0, j + cb0)),
        ],
        out_specs=pl.BlockSpec((tm, tn), lambda i, j: (i, j)),
        scratch_shapes=[pltpu.VMEM((tm, d), BF16)],
        compiler_params=_params(("arbitrary", "arbitrary")),
        name="in_proj",
    )(h, gain.reshape(1, d), mods, mods, w_in)


def _rope_tables(n_tokens):
    t = jnp.arange(n_tokens, dtype=I32)
    pos = jnp.stack([t // GRID_W, t % GRID_W], axis=-1).astype(F32)
    inv_freq = ROPE_BASE ** (-jnp.arange(ROPE_FREQS, dtype=F32) / ROPE_FREQS)
    ang = pos[:, :, None] * inv_freq
    cos, sin = jnp.cos(ang), jnp.sin(ang)
    c = jnp.stack([cos, cos], axis=2).reshape(n_tokens, HEAD_DIM)
    s = jnp.stack([-sin, sin], axis=2).reshape(n_tokens, HEAD_DIM)
    return jnp.tile(c, (1, 2)), jnp.tile(s, (1, 2))


def _rope(x, c, s):
    lane = lax.broadcasted_iota(I32, x.shape, 1)
    first_half = (lane % (2 * ROPE_FREQS)) < ROPE_FREQS
    partner = jnp.where(first_half, pltpu.roll(x, LANES - ROPE_FREQS, 1), pltpu.roll(x, ROPE_FREQS, 1))
    return x * c + partner * s


def _diff_lambda(lam_ref, lam_init):
    lv = lam_ref[...]
    a = jnp.sum(lv[0:1] * lv[1:2], axis=-1, keepdims=True)
    b = jnp.sum(lv[2:3] * lv[3:4], axis=-1, keepdims=True)
    return jnp.exp(a) - jnp.exp(b) + lam_init


def _attend(q, k_all, v_all, lam, subg, lam_init):
    tq = q.shape[0]
    lane = lax.broadcasted_iota(I32, q.shape, 1)
    q0 = jnp.where(lane < HEAD_DIM, q, 0.0).astype(BF16)
    q1 = jnp.where(lane >= HEAD_DIM, q, 0.0).astype(BF16)
    qq = jnp.concatenate([q0, q1], axis=0)
    s = lax.dot_general(qq, k_all, (((1,), (1,)), ((), ())), preferred_element_type=F32)
    m = jnp.max(s, axis=-1, keepdims=True)
    e = jnp.exp2(s - m)
    l = jnp.sum(e, axis=-1, keepdims=True)
    a = e[:tq] - e[tq:] * (lam * l[:tq] / l[tq:])
    o = jnp.dot(a.astype(BF16), v_all, preferred_element_type=F32) / l[:tq]
    ms = jnp.mean(o * o, axis=-1, keepdims=True)
    return o * lax.rsqrt(ms + SUBLN_EPS) * subg * (1.0 - lam_init)


def _attn_lat_kernel(q_ref, kl_ref, vl_ref, kc_ref, vc_ref, cq_ref, sq_ref, ck_ref, sk_ref, lam_ref, g_ref,
                     o_ref, k_all, v_aug, s0_ref, m0_ref, s1_ref, m1_ref, *, n_ctx, lam_init):
    b, h, i = pl.program_id(0), pl.program_id(1), pl.program_id(2)
    tq = q_ref.shape[0]

    @pl.when(jnp.logical_and(jnp.logical_and(b == 0, h == 0), i == 0))
    def _():
        s1_ref[...] = jnp.zeros_like(s1_ref)
        m1_ref[...] = jnp.zeros_like(m1_ref)

    @pl.when(i == 0)
    def _():
        k_all[0:n_ctx, :] = kc_ref[...]
        k_all[n_ctx:, :] = _rope(kl_ref[...].astype(F32), ck_ref[...], sk_ref[...]).astype(BF16)
        v_aug[0:n_ctx, 0:HEAD_W] = vc_ref[...]
        v_aug[n_ctx:, 0:HEAD_W] = vl_ref[...]
        v_aug[:, HEAD_W:2 * HEAD_W] = jnp.ones((k_all.shape[0], HEAD_W), BF16)

    lam = _diff_lambda(lam_ref, lam_init)

    def step(sa_ref, ma_ref, sb_ref, mb_ref):
        q = _rope(q_ref[...].astype(F32), cq_ref[...], sq_ref[...]) * QK_SCALE
        lane = lax.broadcasted_iota(I32, q.shape, 1)
        q0 = jnp.where(lane < HEAD_DIM, q, 0.0).astype(BF16)
        q1 = jnp.where(lane >= HEAD_DIM, q, 0.0).astype(BF16)
        qq = jnp.concatenate([q0, q1], axis=0)
        s = lax.dot_general(qq, k_all[...], (((1,), (1,)), ((), ())), preferred_element_type=F32)
        sa_ref[...] = s
        ma_ref[...] = jnp.broadcast_to(jnp.max(s, axis=-1, keepdims=True), ma_ref.shape)
        mb = jnp.concatenate([mb_ref[...]] * (sb_ref.shape[1] // LANES), axis=1)
        e = jnp.exp2(sb_ref[...] - mb).astype(BF16)
        oa = jnp.dot(e, v_aug[...], preferred_element_type=F32)
        o = oa[:tq, 0:HEAD_W] / oa[:tq, HEAD_W:] - oa[tq:, 0:HEAD_W] * (lam / oa[tq:, HEAD_W:])
        ms = jnp.mean(o * o, axis=-1, keepdims=True)
        o_ref[...] = (o * lax.rsqrt(ms + SUBLN_EPS) * g_ref[...] * (1.0 - lam_init)).astype(o_ref.dtype)

    @pl.when(i % 2 == 0)
    def _():
        step(s0_ref, m0_ref, s1_ref, m1_ref)

    @pl.when(i % 2 == 1)
    def _():
        step(s1_ref, m1_ref, s0_ref, m0_ref)


def _attn_ctx_kernel(q_ref, k_ref, v_ref, lam_ref, g_ref, o_ref, *, lam_init):
    q = q_ref[...].astype(F32) * QK_SCALE
    lam = _diff_lambda(lam_ref, lam_init)
    o_ref[...] = _attend(q, k_ref[...], v_ref[...], lam, g_ref[...], lam_init).astype(o_ref.dtype)


def _attn_lat(p, kv_ctx, kc_blk, vc_blk, ctx_rb0, cos, sin, lam4, subg, *, batch, seq, n_ctx, lam_init, tq=256):
    nq = seq // tq
    n_keys = n_ctx + seq
    hb = lambda off: off // HEAD_W
    qblk = lambda i: jnp.minimum(i, nq - 1)
    kernel = functools.partial(_attn_lat_kernel, n_ctx=n_ctx, lam_init=lam_init)
    return pl.pallas_call(
        kernel,
        out_shape=jax.ShapeDtypeStruct((batch * seq, ATTN_W), BF16),
        grid=(batch, N_HEADS, nq + 1),
        in_specs=[
            pl.BlockSpec((tq, HEAD_W), lambda b, h, i: (b * nq + qblk(i), hb(OFF_Q) + h)),
            pl.BlockSpec((seq, HEAD_W), lambda b, h, i: (b, hb(OFF_K) + h)),
            pl.BlockSpec((seq, HEAD_W), lambda b, h, i: (b, hb(OFF_V) + h)),
            pl.BlockSpec((n_ctx, HEAD_W), lambda b, h, i: (ctx_rb0 + b, kc_blk + h)),
            pl.BlockSpec((n_ctx, HEAD_W), lambda b, h, i: (ctx_rb0 + b, vc_blk + h)),
            pl.BlockSpec((tq, HEAD_W), lambda b, h, i: (qblk(i), 0)),
            pl.BlockSpec((tq, HEAD_W), lambda b, h, i: (qblk(i), 0)),
            pl.BlockSpec((seq, HEAD_W), lambda b, h, i: (0, 0)),
            pl.BlockSpec((seq, HEAD_W), lambda b, h, i: (0, 0)),
            pl.BlockSpec((4, HEAD_DIM), lambda b, h, i: (0, 0)),
            pl.BlockSpec((1, HEAD_W), lambda b, h, i: (0, 0)),
        ],
        out_specs=pl.BlockSpec((tq, HEAD_W), lambda b, h, i: (b * nq + jnp.maximum(i - 1, 0), h)),
        scratch_shapes=[pltpu.VMEM((n_keys, HEAD_W), BF16), pltpu.VMEM((n_keys, 2 * HEAD_W), BF16),
                        pltpu.VMEM((2 * tq, n_keys), F32), pltpu.VMEM((2 * tq, LANES), F32),
                        pltpu.VMEM((2 * tq, n_keys), F32), pltpu.VMEM((2 * tq, LANES), F32)],
        compiler_params=_params(("arbitrary", "arbitrary", "arbitrary")),
        name="attn_lat",
    )(p, p, p, kv_ctx, kv_ctx, cos, sin, cos, sin, lam4, subg.reshape(1, HEAD_W))


def _attn_ctx(p, ctx_rb0, lam4, subg, *, batch, n_ctx, lam_init):
    hb = lambda off: off // HEAD_W
    kernel = functools.partial(_attn_ctx_kernel, lam_init=lam_init)
    return pl.pallas_call(
        kernel,
        out_shape=jax.ShapeDtypeStruct((batch * n_ctx, ATTN_W), BF16),
        grid=(batch, N_HEADS),
        in_specs=[
            pl.BlockSpec((n_ctx, HEAD_W), lambda b, h: (ctx_rb0 + b, hb(OFF_Q) + h)),
            pl.BlockSpec((n_ctx, HEAD_W), lambda b, h: (ctx_rb0 + b, hb(OFF_K) + h)),
            pl.BlockSpec((n_ctx, HEAD_W), lambda b, h: (ctx_rb0 + b, hb(OFF_V) + h)),
            pl.BlockSpec((4, HEAD_DIM), lambda b, h: (0, 0)),
            pl.BlockSpec((1, HEAD_W), lambda b, h: (0, 0)),
        ],
        out_specs=pl.BlockSpec((n_ctx, HEAD_W), lambda b, h: (b, h)),
        compiler_params=_params(("arbitrary", "arbitrary")),
        name="attn_ctx",
    )(p, p, p, lam4, subg.reshape(1, HEAD_W))


def _conv_kernel(bg_ref, cg_ref, xi_ref, a_ref, g_ref,
                 cg_p, xi_p, a_p, g_p, cg_n, xi_n, a_n, g_n,
                 scw_ref, cfw_ref, cfb_ref, lng_ref, lnb_ref,
                 sco_ref, cfo_ref, pad_ref, *, lat_blocks, blocks_per_seq):
    i = pl.program_id(0)
    in_lat = i < lat_blocks
    pos = i % blocks_per_seq
    has_prev = jnp.logical_and(in_lat, pos != 0)
    has_next = jnp.logical_and(in_lat, pos != blocks_per_seq - 1)
    keep_prev = jnp.where(has_prev, 1.0, 0.0)
    keep_next = jnp.where(has_next, 1.0, 0.0)
    lo, hi = CONV_HALO, CONV_HALO + CONV_ROWS

    def fill(main, prev, nxt):
        pad_ref[0:lo, :] = prev * keep_prev
        pad_ref[lo:hi, :] = main
        pad_ref[hi:hi + CONV_HALO, :] = nxt * keep_next

    def f(ref):
        return ref[...].astype(F32)

    def glu(a, g):
        return a * _sigmoid(g)

    fill(f(cg_ref) * f(xi_ref), f(cg_p) * f(xi_p), f(cg_n) * f(xi_n))
    for c in range(CONV_ROWS // CONV_CHUNK):
        r0 = lo + c * CONV_CHUNK - SC_TAPS // 2
        acc = scw_ref[0:1, :] * pad_ref[r0:r0 + CONV_CHUNK, :]
        for k in range(1, SC_TAPS):
            acc = acc + scw_ref[k:k + 1, :] * pad_ref[r0 + k:r0 + k + CONV_CHUNK, :]
        rows = slice(c * CONV_CHUNK, (c + 1) * CONV_CHUNK)
        sco_ref[rows, :] = (bg_ref[rows, :].astype(F32) * acc).astype(sco_ref.dtype)

    fill(glu(f(a_ref), f(g_ref)), glu(f(a_p), f(g_p)), glu(f(a_n), f(g_n)))
    off = lo - CF_TAPS // 2
    win = CONV_CHUNK + SUBLANES
    for c in range(CONV_ROWS // CONV_CHUNK):
        base = c * CONV_CHUNK
        acc = None
        for b in range(SUBLANES):
            q = None
            for k in range(b, CF_TAPS, SUBLANES):
                term = cfw_ref[k:k + 1, :] * pad_ref[base + k - b:base + k - b + win, :]
                q = term if q is None else q + term
            part = q[off + b:off + b + CONV_CHUNK, :]
            acc = part if acc is None else acc + part
        z = acc + cfb_ref[...]
        mu = jnp.mean(z, axis=-1, keepdims=True)
        zc = z - mu
        var = jnp.mean(zc * zc, axis=-1, keepdims=True)
        y = zc * lax.rsqrt(var + CF_LN_EPS) * lng_ref[...] + lnb_ref[...]
        rows = slice(c * CONV_CHUNK, (c + 1) * CONV_CHUNK)
        cfo_ref[rows, :] = (y * _sigmoid(y)).astype(cfo_ref.dtype)


def _convs(p, sc_w, cf_w, cf_b, ln_g, ln_b, *, n_rows, n_lat, seq):
    nb = n_rows // CONV_ROWS
    halo_per_block = CONV_ROWS // CONV_HALO
    last_halo = n_rows // CONV_HALO - 1
    cb = lambda off: off // SC_W

    def main(off):
        return pl.BlockSpec((CONV_ROWS, SC_W), lambda i: (i, cb(off)))

    def prev(off):
        return pl.BlockSpec((CONV_HALO, SC_W), lambda i: (jnp.maximum(i * halo_per_block - 1, 0), cb(off)))

    def nxt(off):
        return pl.BlockSpec((CONV_HALO, SC_W), lambda i: (jnp.minimum((i + 1) * halo_per_block, last_halo), cb(off)))

    def vec(rows):
        return pl.BlockSpec((rows, SC_W), lambda i: (0, 0))

    o_bg, o_cg, o_xi, o_a, o_g = OFF_SC, OFF_SC + SC_W, OFF_SC + 2 * SC_W, OFF_CF, OFF_CF + CF_W
    kernel = functools.partial(_conv_kernel, lat_blocks=n_lat // CONV_ROWS, blocks_per_seq=seq // CONV_ROWS)
    return pl.pallas_call(
        kernel,
        out_shape=(jax.ShapeDtypeStruct((n_rows, SC_W), BF16), jax.ShapeDtypeStruct((n_rows, CF_W), BF16)),
        grid=(nb,),
        in_specs=[main(o_bg), main(o_cg), main(o_xi), main(o_a), main(o_g),
                  prev(o_cg), prev(o_xi), prev(o_a), prev(o_g),
                  nxt(o_cg), nxt(o_xi), nxt(o_a), nxt(o_g),
                  vec(SC_TAPS), vec(CF_TAPS), vec(1), vec(1), vec(1)],
        out_specs=(pl.BlockSpec((CONV_ROWS, SC_W), lambda i: (i, 0)),
                   pl.BlockSpec((CONV_ROWS, CF_W), lambda i: (i, 0))),
        scratch_shapes=[pltpu.VMEM((CONV_ROWS + 2 * CONV_HALO, SC_W), F32)],
        compiler_params=_params(("arbitrary",)),
        name="convs",
    )(p, p, p, p, p, p, p, p, p, p, p, p, p,
      sc_w, cf_w, cf_b.reshape(1, CF_W), ln_g.reshape(1, CF_W), ln_b.reshape(1, CF_W))


def _merge_kernel(at_ref, sc_ref, cf_ref, ga_ref, gb_ref, gc_ref, ba_ref, bb_ref, bc_ref,
                  wa_ref, wb_ref, wc_ref, wm_ref, h_ref, g1_ref, o_ref, acc_ref):
    k = pl.program_id(1)
    ya = jnp.dot(at_ref[...], wa_ref[...].astype(BF16), preferred_element_type=F32)
    yb = jnp.dot(sc_ref[...], wb_ref[...].astype(BF16), preferred_element_type=F32)
    yc = jnp.dot(cf_ref[...], wc_ref[...].astype(BF16), preferred_element_type=F32)
    ga = _sigmoid(ga_ref[...].astype(F32) + ba_ref[...])
    gb = _sigmoid(gb_ref[...].astype(F32) + bb_ref[...])
    gc = _sigmoid(gc_ref[...].astype(F32) + bc_ref[...])
    m = (ga * ya + gb * yb + gc * yc).astype(BF16)
    part = jnp.dot(m, wm_ref[...].astype(BF16), preferred_element_type=F32)

    @pl.when(k == 0)
    def _():
        acc_ref[...] = part

    @pl.when(k != 0)
    def _():
        acc_ref[...] += part

    @pl.when(k == pl.num_programs(1) - 1)
    def _():
        o_ref[...] = h_ref[...] + g1_ref[...] * acc_ref[...]


def _merge(attn, sc, cf, p, b_gate, wa, wb, wc, wm, layer, h, mods, *, n_rows, n_lat, seq, gate_idx, tm=512,
           tk=512):
    d = D_MODEL
    nk = d // tk
    gb0 = OFF_GATE // tk

    def gate(branch):
        return pl.BlockSpec((tm, tk), lambda i, k: (i, gb0 + branch * nk + k))

    def bias(branch):
        return pl.BlockSpec((1, tk), lambda i, k: (0, branch * nk + k))

    return pl.pallas_call(
        _merge_kernel,
        out_shape=jax.ShapeDtypeStruct((n_rows, d), F32),
        grid=(n_rows // tm, nk),
        in_specs=[
            pl.BlockSpec((tm, ATTN_W), lambda i, k: (i, 0)),
            pl.BlockSpec((tm, SC_W), lambda i, k: (i, 0)),
            pl.BlockSpec((tm, CF_W), lambda i, k: (i, 0)),
            gate(0), gate(1), gate(2), bias(0), bias(1), bias(2),
            pl.BlockSpec((None, ATTN_W, tk), lambda i, k: (layer, 0, k)),
            pl.BlockSpec((None, SC_W, tk), lambda i, k: (layer, 0, k)),
            pl.BlockSpec((None, CF_W, tk), lambda i, k: (layer, 0, k)),
            pl.BlockSpec((None, tk, d), lambda i, k: (layer, k, 0)),
            pl.BlockSpec((tm, d), lambda i, k: (i, 0)),
            pl.BlockSpec((None, 1, d), lambda i, k: (_mod_row(i * tm, n_lat, seq), 0, gate_idx)),
        ],
        out_specs=pl.BlockSpec((tm, d), lambda i, k: (i, 0)),
        scratch_shapes=[pltpu.VMEM((tm, d), F32)],
        compiler_params=_params(("arbitrary", "arbitrary")),
        name="merge",
    )(attn, sc, cf, p, p, p, b_gate, b_gate, b_gate, wa, wb, wc, wm, h, mods)


def _split2(x):
    hi = x.astype(BF16)
    return hi, (x - hi.astype(F32)).astype(BF16)


def _router_kernel(h_ref, g_ref, sh_ref, sc_ref, rw_ref, rb_ref, f_ref, id_ref, wt_ref):
    x = h_ref[...]
    ms = jnp.mean(x * x, axis=-1, keepdims=True)
    f = x * lax.rsqrt(ms + NORM_EPS) * g_ref[...] * (1.0 + sc_ref[...]) + sh_ref[...]
    f_ref[...] = f

    f_hi, f_lo = _split2(f)
    w_hi, w_lo = _split2(rw_ref[...])
    dot = lambda a, b: jnp.dot(a, b, preferred_element_type=F32)
    logits = (dot(f_lo, w_hi) + dot(f_hi, w_lo) + dot(f_hi, w_hi)) + rb_ref[...]

    lane_i = lax.broadcasted_iota(I32, logits.shape, 1)
    lane = lane_i.astype(F32)
    neg = -jnp.inf
    big = float(LANES)
    lg = jnp.where(lane_i < N_GROUPS, logits, neg)
    mg = jnp.max(lg, axis=-1, keepdims=True)
    grp = jnp.min(jnp.where(lg == mg, lane, big), axis=-1, keepdims=True)
    p_grp = 1.0 / jnp.sum(jnp.exp(lg - mg), axis=-1, keepdims=True)

    e_lane = lane - N_GROUPS
    in_grp = jnp.logical_and(e_lane >= grp * EXPERTS_PER_GROUP, e_lane < (grp + 1.0) * EXPERTS_PER_GROUP)
    le = jnp.where(in_grp, logits, neg)
    m1 = jnp.max(le, axis=-1, keepdims=True)
    i1 = jnp.min(jnp.where(le == m1, lane, big), axis=-1, keepdims=True)
    le2 = jnp.where(lane == i1, neg, le)
    m2 = jnp.max(le2, axis=-1, keepdims=True)
    i2 = jnp.min(jnp.where(le2 == m2, lane, big), axis=-1, keepdims=True)
    e2 = jnp.exp(m2 - m1)
    w1 = p_grp / (1.0 + e2)
    w2 = p_grp * e2 / (1.0 + e2)
    ids = jnp.where(lane_i == 0, i1 - N_GROUPS, jnp.where(lane_i == 1, i2 - N_GROUPS, 0.0))
    id_ref[...] = ids.astype(I32)
    wt_ref[...] = jnp.where(lane_i == 0, w1, jnp.where(lane_i == 1, w2, 0.0))


def _router(h, gain, mods, rw, rb, *, n_rows, n_lat, seq, sh_idx, sc_idx, tm=512):
    d = D_MODEL

    def mod_map(idx):
        return lambda i: (_mod_row(i * tm, n_lat, seq), 0, idx)

    return pl.pallas_call(
        _router_kernel,
        out_shape=(jax.ShapeDtypeStruct((n_rows, d), F32),
                   jax.ShapeDtypeStruct((n_rows, LANES), I32),
                   jax.ShapeDtypeStruct((n_rows, LANES), F32)),
        grid=(n_rows // tm,),
        in_specs=[
            pl.BlockSpec((tm, d), lambda i: (i, 0)),
            pl.BlockSpec((1, d), lambda i: (0, 0)),
            pl.BlockSpec((None, 1, d), mod_map(sh_idx)),
            pl.BlockSpec((None, 1, d), mod_map(sc_idx)),
            pl.BlockSpec((d, LANES), lambda i: (0, 0)),
            pl.BlockSpec((1, LANES), lambda i: (0, 0)),
        ],
        out_specs=(pl.BlockSpec((tm, d), lambda i: (i, 0)),
                   pl.BlockSpec((tm, LANES), lambda i: (i, 0)),
                   pl.BlockSpec((tm, LANES), lambda i: (i, 0))),
        compiler_params=_params(("arbitrary",)),
        name="router",
    )(h, gain.reshape(1, d), mods, mods, rw, rb)


def _moe_plan(ids, n_tok):
    n_items = N_EXPERTS + (2 * n_tok) // MOE_ITEM_ROWS
    e_flat = ids[:, :2].reshape(-1)
    order = jnp.argsort(e_flat, stable=True).astype(I32)
    counts = jnp.sum((e_flat[:, None] == jnp.arange(N_EXPERTS, dtype=I32)[None, :]).astype(I32), axis=0)
    starts = jnp.cumsum(counts) - counts
    n_sb = (counts + MOE_ITEM_ROWS - 1) // MOE_ITEM_ROWS
    cum = jnp.cumsum(n_sb)
    total = cum[-1]
    it = jnp.arange(n_items, dtype=I32)
    e_i = jnp.minimum(jnp.searchsorted(cum, it, side="right").astype(I32), N_EXPERTS - 1)
    sb = it - (cum[e_i] - n_sb[e_i])
    valid = it < total
    last_e = e_i[jnp.maximum(total - 1, 0)]
    item_e = jnp.where(valid, e_i, last_e).astype(I32)
    item_start = jnp.where(valid, starts[e_i] + sb * MOE_ITEM_ROWS, 0).astype(I32)
    item_n = jnp.where(valid, jnp.minimum(MOE_ITEM_ROWS, counts[e_i] - sb * MOE_ITEM_ROWS), 0).astype(I32)
    front, back = jnp.zeros((MOE_ITEM_SHIFT,), I32), jnp.zeros((1,), I32)
    pad_items = lambda a: jnp.concatenate([front, a, back])
    return pad_items(item_e), pad_items(item_start), pad_items(item_n), (order // 2).astype(I32), order


def _moe_kernel(ie_ref, ist_ref, in_ref, tok_ref, slot_ref,
                f_hbm, wg_ref, wu_ref, wd_ref, out_hbm,
                xf_ref, xb_ref, yacc_ref, yout_ref, gsem, ssem):
    i, j = pl.program_id(0), pl.program_id(1)
    sh = MOE_ITEM_SHIFT
    n_prev, n, n_next = in_ref[i + sh - 1], in_ref[i + sh], in_ref[i + sh + 1]
    start, start_next = ist_ref[i + sh], ist_ref[i + sh + 1]

    def row_in(r, tok):
        return pltpu.make_async_copy(f_hbm.at[pl.ds(tok, 1), :], xf_ref.at[pl.ds(r, 1), :], gsem)

    def row_out(r, slot):
        return pltpu.make_async_copy(yout_ref.at[pl.ds(r, 1), :], out_hbm.at[slot & 1, pl.ds(slot >> 1, 1), :], ssem)

    def rows_in(size):
        return pltpu.make_async_copy(f_hbm.at[pl.ds(0, size), :], xf_ref.at[pl.ds(0, size), :], gsem)

    def rows_out(size):
        return pltpu.make_async_copy(yout_ref.at[pl.ds(0, size), :], out_hbm.at[0, pl.ds(0, size), :], ssem)

    def start_rows(make, idx_ref, base, count):
        def body8(t, c):
            r0 = pl.multiple_of(t * MOE_ISSUE_UNROLL, MOE_ISSUE_UNROLL)
            for k in range(MOE_ISSUE_UNROLL):
                make(r0 + k, idx_ref[base + r0 + k]).start()
            return c

        lax.fori_loop(0, count // MOE_ISSUE_UNROLL, body8, 0)

        def body1(r, c):
            make(r, idx_ref[base + r]).start()
            return c

        lax.fori_loop((count // MOE_ISSUE_UNROLL) * MOE_ISSUE_UNROLL, count, body1, 0)

    def gather_loop(base, count):
        start_rows(row_in, tok_ref, base, count)

    def scatter_loop(base, count):
        start_rows(row_out, slot_ref, base, count)

    def wait_rows(count, make_block):
        for b in range(MOE_ITEM_ROWS.bit_length()):
            @pl.when(((count >> b) & 1) == 1)
            def _():
                make_block(1 << b).wait()

    @pl.when(jnp.logical_and(i == 0, j == 0))
    def _():
        xf_ref[...] = jnp.zeros_like(xf_ref)
        yacc_ref[...] = jnp.zeros_like(yacc_ref)

    def publish(count):
        def body(t, c):
            rows = pl.ds(pl.multiple_of(t * MOE_ROW_PAD, MOE_ROW_PAD), MOE_ROW_PAD)
            yout_ref[rows, :] = yacc_ref[rows, :]
            return c

        lax.fori_loop(0, (count + MOE_ROW_PAD - 1) // MOE_ROW_PAD, body, 0)

    @pl.when(j == 0)
    def _():
        @pl.when(i == 0)
        def _():
            gather_loop(start, n)

        @pl.when(n > 0)
        def _():
            wait_rows(n, rows_in)
            xb_ref[...] = xf_ref[...].astype(BF16)
            gather_loop(start_next, n_next)

    def compute(m):
        x = xb_ref[0:m, :]
        g = jnp.dot(x, wg_ref[...].astype(BF16), preferred_element_type=F32)
        u = jnp.dot(x, wu_ref[...].astype(BF16), preferred_element_type=F32)
        hmid = (g * _sigmoid(g) * u).astype(BF16)
        y = jnp.dot(hmid, wd_ref[...].astype(BF16), preferred_element_type=F32)
        yacc_ref[0:m, :] = y + jnp.where(j == 0, 0.0, yacc_ref[0:m, :])

    n_pad = (n + MOE_ROW_PAD - 1) // MOE_ROW_PAD
    for k in range(1, MOE_ITEM_ROWS // MOE_ROW_PAD + 1):
        @pl.when(n_pad == k)
        def _():
            compute(k * MOE_ROW_PAD)

    @pl.when(j == pl.num_programs(1) - 1)
    def _():
        wait_rows(n_prev, rows_out)

        @pl.when(n > 0)
        def _():
            publish(n)
            scatter_loop(start, n)

        @pl.when(i == pl.num_programs(0) - 1)
        def _():
            wait_rows(n, rows_out)


def _moe(f, plan, w_gu, w_down, layer, *, n_tok):
    item_e, item_start, item_n, tok_sorted, slot_sorted = plan
    d, hid, hk = D_MODEL, EXPERT_HIDDEN, MOE_HIDDEN_BLOCK
    nj = hid // hk
    sh = MOE_ITEM_SHIFT
    n_items = item_e.shape[0] - sh - 1

    def chunk(j, nn, i):
        return jnp.where(nn[i + sh] > 0, j, nj - 1)

    grid_spec = pltpu.PrefetchScalarGridSpec(
        num_scalar_prefetch=5,
        grid=(n_items, nj),
        in_specs=[
            pl.BlockSpec(memory_space=pl.ANY),
            pl.BlockSpec((None, None, d, hk), lambda i, j, ie, ist, nn, tk, sl: (layer, ie[i + sh],0, chunk(j, nn, i))),
            pl.BlockSpec((None, None, d, hk),
                         lambda i, j, ie, ist, nn, tk, sl: (layer, ie[i + sh],0, nj + chunk(j, nn, i))),
            pl.BlockSpec((None, None, hk, d), lambda i, j, ie, ist, nn, tk, sl: (layer, ie[i + sh],chunk(j, nn, i), 0)),
        ],
        out_specs=pl.BlockSpec(memory_space=pl.ANY),
        scratch_shapes=[
            pltpu.VMEM((MOE_ITEM_ROWS, d), F32),
            pltpu.VMEM((MOE_ITEM_ROWS, d), BF16),
            pltpu.VMEM((MOE_ITEM_ROWS, d), F32),
            pltpu.VMEM((MOE_ITEM_ROWS, d), F32),
            pltpu.SemaphoreType.DMA,
            pltpu.SemaphoreType.DMA,
        ],
    )
    return pl.pallas_call(
        _moe_kernel,
        out_shape=jax.ShapeDtypeStruct((2, n_tok, d), F32),
        grid_spec=grid_spec,
        compiler_params=_params(("arbitrary", "arbitrary")),
        name="moe_experts",
    )(item_e, item_start, item_n, tok_sorted, slot_sorted, f, w_gu, w_gu, w_down)


def _combine_kernel(h_ref, y0_ref, y1_ref, wt_ref, g2_ref, fg_ref, o_ref, *, final):
    wt = wt_ref[...]
    moe = wt[:, 0:1] * y0_ref[...] + wt[:, 1:2] * y1_ref[...]
    h = h_ref[...] + g2_ref[...] * moe
    if final:
        ms = jnp.mean(h * h, axis=-1, keepdims=True)
        h = h * lax.rsqrt(ms + NORM_EPS) * fg_ref[...]
    o_ref[...] = h


def _combine(h, slots, wts, mods, final_g, *, n_rows, n_lat, seq, gate_idx, final, tm=256):
    d = D_MODEL
    kernel = functools.partial(_combine_kernel, final=final)
    return pl.pallas_call(
        kernel,
        out_shape=jax.ShapeDtypeStruct((n_rows, d), F32),
        grid=(n_rows // tm,),
        in_specs=[
            pl.BlockSpec((tm, d), lambda i: (i, 0)),
            pl.BlockSpec((None, tm, d), lambda i: (0, i, 0)),
            pl.BlockSpec((None, tm, d), lambda i: (1, i, 0)),
            pl.BlockSpec((tm, LANES), lambda i: (i, 0)),
            pl.BlockSpec((None, 1, d), lambda i: (_mod_row(i * tm, n_lat, seq), 0, gate_idx)),
            pl.BlockSpec((1, d), lambda i: (0, 0)),
        ],
        out_specs=pl.BlockSpec((tm, d), lambda i: (i, 0)),
        compiler_params=_params(("arbitrary",)),
        name="combine",
    )(h, slots, slots, wts, mods, final_g.reshape(1, d))


def kernel(x, c, ctx, c_ctx, ada_w, ada_b, norm1_g, w_in, b_gate, diff_lambda, subln_g, w_attn_out, sc_conv_w,
           w_sc_out, cf_dw_w, cf_dw_b, cf_ln_g, cf_ln_b, w_cf_out, w_mix, norm2_g, router_g_w, router_g_b,
           router_e_w, router_e_b, exp_w_gu, exp_w_down, final_g):
    batch, seq, d = x.shape
    n_ctx = ctx.shape[1]
    depth = ada_w.shape[0]
    n_lat = batch * seq
    n_all = n_lat + batch * n_ctx
    assert d == D_MODEL and batch == MOD_ROWS // 2 and w_in.shape[2] == C_TOT

    cond = jnp.concatenate([c, c_ctx[None, :], jnp.zeros((MOD_ROWS - batch - 1, d), F32)], axis=0)
    mods_all = _adaln(cond, ada_w, ada_b)
    cos, sin = _rope_tables(seq)
    h = jnp.concatenate([x.reshape(n_lat, d), ctx.reshape(batch * n_ctx, d)], axis=0)
    geo = dict(n_lat=n_lat, seq=seq)

    for layer in range(depth):
        last = layer == depth - 1
        lam_init = 0.8 - 0.6 * math.exp(-0.3 * layer)
        mods = mods_all[layer].reshape(MOD_ROWS, 1, 6 * d)
        n_rows = n_lat if last else n_all

        p = _in_proj(h, norm1_g[layer], mods, w_in, layer, row0=0, n_rows=n_rows, col0=0, n_cols=C_TOT,
                     sh_idx=0, sc_idx=1, tm=1024, **geo)
        if last:
            kv_ctx = _in_proj(h, norm1_g[layer], mods, w_in, layer, row0=n_lat, n_rows=batch * n_ctx, col0=OFF_K,
                              n_cols=OFF_SC - OFF_K, sh_idx=0, sc_idx=1, tm=n_ctx, **geo)
            kc_blk, vc_blk, ctx_rb0 = 0, QK_W // HEAD_W, 0
        else:
            kv_ctx = p
            kc_blk, vc_blk, ctx_rb0 = OFF_K // HEAD_W, OFF_V // HEAD_W, n_lat // n_ctx
        attn = _attn_lat(p, kv_ctx, kc_blk, vc_blk, ctx_rb0, cos, sin, diff_lambda[layer], subln_g[layer],
                         batch=batch, seq=seq, n_ctx=n_ctx, lam_init=lam_init)
        if not last:
            attn_c = _attn_ctx(p, n_lat // n_ctx, diff_lambda[layer], subln_g[layer],
                               batch=batch, n_ctx=n_ctx, lam_init=lam_init)
            attn = jnp.concatenate([attn, attn_c], axis=0)
        sc, cf = _convs(p, sc_conv_w[layer], cf_dw_w[layer], cf_dw_b[layer], cf_ln_g[layer], cf_ln_b[layer],
                        n_rows=n_rows, **geo)
        h_mix = _merge(attn, sc, cf, p, b_gate[layer].reshape(1, 3 * d),
                       w_attn_out, w_sc_out, w_cf_out, w_mix, layer, h, mods,
                       n_rows=n_rows, gate_idx=2, **geo)

        rw = jnp.concatenate([router_g_w[layer], router_e_w[layer],
                              jnp.zeros((d, LANES - N_GROUPS - N_EXPERTS), F32)], axis=1)
        rb = jnp.concatenate([router_g_b[layer], router_e_b[layer],
                              jnp.zeros((LANES - N_GROUPS - N_EXPERTS,), F32)]).reshape(1, LANES)
        f, ids, wts = _router(h_mix, norm2_g[layer], mods, rw, rb, n_rows=n_rows, sh_idx=3, sc_idx=4, **geo)
        slots = _moe(f, _moe_plan(ids, n_rows), exp_w_gu, exp_w_down, layer, n_tok=n_rows)
        h = _combine(h_mix, slots, wts, mods, final_g, n_rows=n_rows, gate_idx=5, final=last, **geo)

    return h.reshape(batch, seq, d)
```

```python
import functools
import math

import jax
import jax.numpy as jnp
from jax import lax
from jax.experimental import pallas as pl
from jax.experimental.pallas import tpu as pltpu

F32 = jnp.float32
BF16 = jnp.bfloat16
I32 = jnp.int32

D_MODEL = 2048
GRID_W = 64
NORM_EPS = 1e-6
N_HEADS = 8
HEAD_DIM = 64
HEAD_W = 2 * HEAD_DIM
QK_W = N_HEADS * HEAD_W
ATTN_W = N_HEADS * HEAD_W
ROPE_BASE = 10000.0
ROPE_FREQS = HEAD_DIM // 4
SUBLN_EPS = 1e-5
QK_SCALE = HEAD_DIM ** -0.5 * math.log2(math.e)
SC_W = D_MODEL // 4
CF_W = D_MODEL // 4
SC_TAPS = 3
CF_TAPS = 31
CF_LN_EPS = 1e-5
OFF_Q = 0
OFF_K = OFF_Q + QK_W
OFF_V = OFF_K + QK_W
OFF_SC = OFF_V + ATTN_W
OFF_CF = OFF_SC + 3 * SC_W
OFF_GATE = OFF_CF + 2 * CF_W
C_TOT = OFF_GATE + 3 * D_MODEL
N_GROUPS = 4
EXPERTS_PER_GROUP = 8
N_EXPERTS = N_GROUPS * EXPERTS_PER_GROUP
EXPERT_HIDDEN = D_MODEL // 2

LANES = 128
SUBLANES = 8
MOD_ROWS = 8
ATTN_SUB_ROWS = 256
MERGE_GATE_TILES = 4
CONV_HALO = 16
CONV_ROWS = 256
CONV_CHUNK = 32
MOE_ITEM_ROWS = 1024
MOE_ROW_PAD = 128
MOE_HIDDEN_BLOCK = 256
MOE_ITEM_SHIFT = 1
MOE_ISSUE_UNROLL = 8
VMEM_LIMIT = 56 * 1024 * 1024


def _params(sem, vmem=VMEM_LIMIT):
    return pltpu.CompilerParams(dimension_semantics=sem, vmem_limit_bytes=vmem)


def _sigmoid(x):
    return 1.0 / (1.0 + jnp.exp(-x))


def _adaln_kernel(s_ref, w_ref, b_ref, o_ref):
    s = s_ref[...]
    s = s * _sigmoid(s)
    s_hi = s.astype(BF16)
    s_lo = (s - s_hi.astype(F32)).astype(BF16)
    w = w_ref[...]
    w_hi = w.astype(BF16)
    w_lo = (w - w_hi.astype(F32)).astype(BF16)
    lhs = jnp.concatenate([s_hi.astype(F32), s - s_hi.astype(F32)], axis=0).astype(BF16)
    r = jnp.dot(lhs, w_hi, preferred_element_type=F32)
    r2 = jnp.dot(s_hi, w_lo, preferred_element_type=F32)
    o_ref[...] = r[:MOD_ROWS] + r[MOD_ROWS:] + r2 + b_ref[...]


def _adaln(cond, ada_w, ada_b):
    n_layers, d, n = ada_w.shape
    tn = 512
    return pl.pallas_call(
        _adaln_kernel,
        out_shape=jax.ShapeDtypeStruct((n_layers, MOD_ROWS, n), F32),
        grid=(n_layers, n // tn),
        in_specs=[
            pl.BlockSpec((MOD_ROWS, d), lambda l, j: (0, 0)),
            pl.BlockSpec((None, d, tn), lambda l, j: (l, 0, j)),
            pl.BlockSpec((None, 1, tn), lambda l, j: (l, 0, j)),
        ],
        out_specs=pl.BlockSpec((None, MOD_ROWS, tn), lambda l, j: (l, 0, j)),
        compiler_params=_params(("arbitrary", "arbitrary")),
        name="adaln",
    )(cond, ada_w, ada_b.reshape(n_layers, 1, n))


def _mod_row(row0, n_lat, seq):
    return jnp.where(row0 < n_lat, row0 // seq, MOD_ROWS // 2)


def _in_proj_kernel(xa_ref, xb_ref, g_ref, sh_ref, sc_ref, w_ref, o_ref, u_ref, *, a_blocks):
    first_col = pl.program_id(1) == 0
    from_a = pl.program_id(0) < a_blocks

    def normed(x_ref):
        x = x_ref[...]
        ms = jnp.mean(x * x, axis=-1, keepdims=True)
        y = x * lax.rsqrt(ms + NORM_EPS) * g_ref[...]
        u_ref[...] = (y * (1.0 + sc_ref[...]) + sh_ref[...]).astype(BF16)

    @pl.when(jnp.logical_and(first_col, from_a))
    def _():
        normed(xa_ref)

    @pl.when(jnp.logical_and(first_col, jnp.logical_not(from_a)))
    def _():
        normed(xb_ref)

    o_ref[...] = jnp.dot(u_ref[...], w_ref[...].astype(BF16), preferred_element_type=F32).astype(o_ref.dtype)


def _in_proj(rows, gain, mods, w_in, layer, *, row0, n_rows, col0, n_cols, n_lat, seq, sh_idx, sc_idx, tm, tn=512):
    xa, xb = rows
    d = xa.shape[1]
    rb0, cb0 = row0 // tm, col0 // tn
    a_blocks = xa.shape[0] // tm - rb0

    def mod_map(idx):
        return lambda i, j: (_mod_row((i + rb0) * tm, n_lat, seq), 0, idx)

    return pl.pallas_call(
        functools.partial(_in_proj_kernel, a_blocks=a_blocks),
        out_shape=jax.ShapeDtypeStruct((n_rows, n_cols), BF16),
        grid=(n_rows // tm, n_cols // tn),
        in_specs=[
            pl.BlockSpec((tm, d), lambda i, j: (jnp.minimum(i, a_blocks - 1) + rb0, 0)),
            pl.BlockSpec((tm, d), lambda i, j: (jnp.maximum(i - a_blocks, 0), 0)),
            pl.BlockSpec((1, d), lambda i, j: (0, 0)),
            pl.BlockSpec((None, 1, d), mod_map(sh_idx)),
            pl.BlockSpec((None, 1, d), mod_map(sc_idx)),
            pl.BlockSpec((None, d, tn), lambda i, j: (layer, 0, j + cb0)),
        ],
        out_specs=pl.BlockSpec((tm, tn), lambda i, j: (i, j)),
        scratch_shapes=[pltpu.VMEM((tm, d), BF16)],
        compiler_params=_params(("arbitrary", "arbitrary")),
        name="in_proj",
    )(xa, xb, gain.reshape(1, d), mods, mods, w_in)


def _rope_tables(n_tokens):
    t = jnp.arange(n_tokens, dtype=I32)
    pos = jnp.stack([t // GRID_W, t % GRID_W], axis=-1).astype(F32)
    inv_freq = ROPE_BASE ** (-jnp.arange(ROPE_FREQS, dtype=F32) / ROPE_FREQS)
    ang = pos[:, :, None] * inv_freq
    cos, sin = jnp.cos(ang), jnp.sin(ang)
    c = jnp.stack([cos, cos], axis=2).reshape(n_tokens, HEAD_DIM)
    s = jnp.stack([-sin, sin], axis=2).reshape(n_tokens, HEAD_DIM)
    return jnp.tile(c, (1, 2)), jnp.tile(s, (1, 2))


def _rope(x, c, s):
    lane = lax.broadcasted_iota(I32, x.shape, 1)
    first_half = (lane % (2 * ROPE_FREQS)) < ROPE_FREQS
    partner = jnp.where(first_half, pltpu.roll(x, LANES - ROPE_FREQS, 1), pltpu.roll(x, ROPE_FREQS, 1))
    return x * c + partner * s


def _diff_lambda(lam_ref, lam_init):
    lv = lam_ref[...]
    a = jnp.sum(lv[0:1] * lv[1:2], axis=-1, keepdims=True)
    b = jnp.sum(lv[2:3] * lv[3:4], axis=-1, keepdims=True)
    return jnp.exp(a) - jnp.exp(b) + lam_init


def _attend(q, k_all, v_all, lam, subg, lam_init):
    tq = q.shape[0]
    lane = lax.broadcasted_iota(I32, q.shape, 1)
    q0 = jnp.where(lane < HEAD_DIM, q, 0.0).astype(BF16)
    q1 = jnp.where(lane >= HEAD_DIM, q, 0.0).astype(BF16)
    qq = jnp.concatenate([q0, q1], axis=0)
    s = lax.dot_general(qq, k_all, (((1,), (1,)), ((), ())), preferred_element_type=F32)
    m = jnp.max(s, axis=-1, keepdims=True)
    e = jnp.exp2(s - m)
    l = jnp.sum(e, axis=-1, keepdims=True)
    a = e[:tq] - e[tq:] * (lam * l[:tq] / l[tq:])
    o = jnp.dot(a.astype(BF16), v_all, preferred_element_type=F32) / l[:tq]
    ms = jnp.mean(o * o, axis=-1, keepdims=True)
    return o * lax.rsqrt(ms + SUBLN_EPS) * subg * (1.0 - lam_init)


def _attn_lat_kernel(q_ref, kl_ref, vl_ref, kc_ref, vc_ref, cq_ref, sq_ref, ck_ref, sk_ref, lam_ref, g_ref,
                     o_ref, k_all, v_aug, s0_ref, m0_ref, s1_ref, m1_ref, *, n_ctx, lam_init, n_blocks):
    i = pl.program_id(2)
    tq = q_ref.shape[0]

    @pl.when(i == 0)
    def _():
        k_all[0:n_ctx, :] = kc_ref[...]
        k_all[n_ctx:, :] = _rope(kl_ref[...].astype(F32), ck_ref[...], sk_ref[...]).astype(BF16)
        v_aug[0:n_ctx, 0:HEAD_W] = vc_ref[...]
        v_aug[n_ctx:, 0:HEAD_W] = vl_ref[...]
        v_aug[:, HEAD_W:2 * HEAD_W] = jnp.ones((k_all.shape[0], HEAD_W), BF16)

    lam = _diff_lambda(lam_ref, lam_init)

    def stage_a(sa_ref, ma_ref):
        q = _rope(q_ref[...].astype(F32), cq_ref[...], sq_ref[...]) * QK_SCALE
        lane = lax.broadcasted_iota(I32, q.shape, 1)
        q0 = jnp.where(lane < HEAD_DIM, q, 0.0).astype(BF16)
        q1 = jnp.where(lane >= HEAD_DIM, q, 0.0).astype(BF16)
        qq = jnp.concatenate([q0, q1], axis=0)
        s = lax.dot_general(qq, k_all[...], (((1,), (1,)), ((), ())), preferred_element_type=F32)
        sa_ref[...] = s
        ma_ref[...] = jnp.broadcast_to(jnp.max(s, axis=-1, keepdims=True), ma_ref.shape)

    def stage_b(sb_ref, mb_ref):
        mb = jnp.concatenate([mb_ref[...]] * (sb_ref.shape[1] // LANES), axis=1)
        e = jnp.exp2(sb_ref[...] - mb).astype(BF16)
        oa = jnp.dot(e, v_aug[...], preferred_element_type=F32)
        o = oa[:tq, 0:HEAD_W] / oa[:tq, HEAD_W:] - oa[tq:, 0:HEAD_W] * (lam / oa[tq:, HEAD_W:])
        ms = jnp.mean(o * o, axis=-1, keepdims=True)
        o_ref[...] = (o * lax.rsqrt(ms + SUBLN_EPS) * g_ref[...] * (1.0 - lam_init)).astype(o_ref.dtype)

    bufs = ((s0_ref, m0_ref), (s1_ref, m1_ref))

    @pl.when(i == 0)
    def _():
        stage_a(*bufs[0])

    for par in range(2):
        @pl.when(jnp.logical_and(jnp.logical_and(i > 0, i < n_blocks), i % 2 == par))
        def _():
            stage_a(*bufs[par])
            stage_b(*bufs[1 - par])

    @pl.when(i == n_blocks)
    def _():
        stage_b(*bufs[(n_blocks - 1) % 2])


def _attn_ctx_kernel(q_ref, k_ref, v_ref, lam_ref, g_ref, o_ref, *, lam_init):
    q = q_ref[...].astype(F32) * QK_SCALE
    lam = _diff_lambda(lam_ref, lam_init)
    o_ref[...] = _attend(q, k_ref[...], v_ref[...], lam, g_ref[...], lam_init).astype(o_ref.dtype)


def _attn_lat(p, kv_ctx, kc_blk, vc_blk, ctx_rb0, cos, sin, lam4, subg, *, batch, seq, n_ctx, lam_init, tq=256):
    nq = seq // tq
    n_keys = n_ctx + seq
    hb = lambda off: off // HEAD_W
    qblk = lambda i: jnp.minimum(i, nq - 1)
    kernel = functools.partial(_attn_lat_kernel, n_ctx=n_ctx, lam_init=lam_init, n_blocks=nq)
    return pl.pallas_call(
        kernel,
        out_shape=jax.ShapeDtypeStruct((batch * seq, ATTN_W), BF16),
        grid=(batch, N_HEADS, nq + 1),
        in_specs=[
            pl.BlockSpec((tq, HEAD_W), lambda b, h, i: (b * nq + qblk(i), hb(OFF_Q) + h)),
            pl.BlockSpec((seq, HEAD_W), lambda b, h, i: (b, hb(OFF_K) + h)),
            pl.BlockSpec((seq, HEAD_W), lambda b, h, i: (b, hb(OFF_V) + h)),
            pl.BlockSpec((n_ctx, HEAD_W), lambda b, h, i: (ctx_rb0 + b, kc_blk + h)),
            pl.BlockSpec((n_ctx, HEAD_W), lambda b, h, i: (ctx_rb0 + b, vc_blk + h)),
            pl.BlockSpec((tq, HEAD_W), lambda b, h, i: (qblk(i), 0)),
            pl.BlockSpec((tq, HEAD_W), lambda b, h, i: (qblk(i), 0)),
            pl.BlockSpec((seq, HEAD_W), lambda b, h, i: (0, 0)),
            pl.BlockSpec((seq, HEAD_W), lambda b, h, i: (0, 0)),
            pl.BlockSpec((4, HEAD_DIM), lambda b, h, i: (0, 0)),
            pl.BlockSpec((1, HEAD_W), lambda b, h, i: (0, 0)),
        ],
        out_specs=pl.BlockSpec((tq, HEAD_W), lambda b, h, i: (b * nq + jnp.maximum(i - 1, 0), h)),
        scratch_shapes=[pltpu.VMEM((n_keys, HEAD_W), BF16), pltpu.VMEM((n_keys, 2 * HEAD_W), BF16),
                        pltpu.VMEM((2 * tq, n_keys), F32), pltpu.VMEM((2 * tq, LANES), F32),
                        pltpu.VMEM((2 * tq, n_keys), F32), pltpu.VMEM((2 * tq, LANES), F32)],
        compiler_params=_params(("arbitrary", "arbitrary", "arbitrary")),
        name="attn_lat",
    )(p, p, p, kv_ctx, kv_ctx, cos, sin, cos, sin, lam4, subg.reshape(1, HEAD_W))


def _attn_ctx(p, ctx_rb0, lam4, subg, *, batch, n_ctx, lam_init):
    hb = lambda off: off // HEAD_W
    kernel = functools.partial(_attn_ctx_kernel, lam_init=lam_init)
    return pl.pallas_call(
        kernel,
        out_shape=jax.ShapeDtypeStruct((batch * n_ctx, ATTN_W), BF16),
        grid=(batch, N_HEADS),
        in_specs=[
            pl.BlockSpec((n_ctx, HEAD_W), lambda b, h: (ctx_rb0 + b, hb(OFF_Q) + h)),
            pl.BlockSpec((n_ctx, HEAD_W), lambda b, h: (ctx_rb0 + b, hb(OFF_K) + h)),
            pl.BlockSpec((n_ctx, HEAD_W), lambda b, h: (ctx_rb0 + b, hb(OFF_V) + h)),
            pl.BlockSpec((4, HEAD_DIM), lambda b, h: (0, 0)),
            pl.BlockSpec((1, HEAD_W), lambda b, h: (0, 0)),
        ],
        out_specs=pl.BlockSpec((n_ctx, HEAD_W), lambda b, h: (b, h)),
        compiler_params=_params(("arbitrary", "arbitrary")),
        name="attn_ctx",
    )(p, p, p, lam4, subg.reshape(1, HEAD_W))


def _conv_kernel(bg_ref, cg_ref, xi_ref, a_ref, g_ref,
                 cg_p, xi_p, a_p, g_p, cg_n, xi_n, a_n, g_n,
                 scw_ref, cfw_ref, cfb_ref, lng_ref, lnb_ref,
                 sco_ref, cfo_ref, pad_ref, *, lat_blocks, blocks_per_seq):
    i = pl.program_id(0)
    in_lat = i < lat_blocks
    pos = i % blocks_per_seq
    has_prev = jnp.logical_and(in_lat, pos != 0)
    has_next = jnp.logical_and(in_lat, pos != blocks_per_seq - 1)
    keep_prev = jnp.where(has_prev, 1.0, 0.0)
    keep_next = jnp.where(has_next, 1.0, 0.0)
    lo, hi = CONV_HALO, CONV_HALO + CONV_ROWS

    def fill(main, prev, nxt):
        pad_ref[0:lo, :] = prev * keep_prev
        pad_ref[lo:hi, :] = main
        pad_ref[hi:hi + CONV_HALO, :] = nxt * keep_next

    def f(ref):
        return ref[...].astype(F32)

    def glu(a, g):
        return a * _sigmoid(g)

    fill(f(cg_ref) * f(xi_ref), f(cg_p) * f(xi_p), f(cg_n) * f(xi_n))
    for c in range(CONV_ROWS // CONV_CHUNK):
        r0 = lo + c * CONV_CHUNK - SC_TAPS // 2
        acc = scw_ref[0:1, :] * pad_ref[r0:r0 + CONV_CHUNK, :]
        for k in range(1, SC_TAPS):
            acc = acc + scw_ref[k:k + 1, :] * pad_ref[r0 + k:r0 + k + CONV_CHUNK, :]
        rows = slice(c * CONV_CHUNK, (c + 1) * CONV_CHUNK)
        sco_ref[rows, :] = (bg_ref[rows, :].astype(F32) * acc).astype(sco_ref.dtype)

    fill(glu(f(a_ref), f(g_ref)), glu(f(a_p), f(g_p)), glu(f(a_n), f(g_n)))
    off = lo - CF_TAPS // 2
    win = CONV_CHUNK + SUBLANES
    for c in range(CONV_ROWS // CONV_CHUNK):
        base = c * CONV_CHUNK
        acc = None
        for b in range(SUBLANES):
            q = None
            for k in range(b, CF_TAPS, SUBLANES):
                term = cfw_ref[k:k + 1, :] * pad_ref[base + k - b:base + k - b + win, :]
                q = term if q is None else q + term
            part = q[off + b:off + b + CONV_CHUNK, :]
            acc = part if acc is None else acc + part
        z = acc + cfb_ref[...]
        mu = jnp.mean(z, axis=-1, keepdims=True)
        zc = z - mu
        var = jnp.mean(zc * zc, axis=-1, keepdims=True)
        y = zc * lax.rsqrt(var + CF_LN_EPS) * lng_ref[...] + lnb_ref[...]
        rows = slice(c * CONV_CHUNK, (c + 1) * CONV_CHUNK)
        cfo_ref[rows, :] = (y * _sigmoid(y)).astype(cfo_ref.dtype)


def _convs(p, sc_w, cf_w, cf_b, ln_g, ln_b, *, n_rows, n_lat, seq):
    nb = n_rows // CONV_ROWS
    halo_per_block = CONV_ROWS // CONV_HALO
    last_halo = n_rows // CONV_HALO - 1
    cb = lambda off: off // SC_W

    def main(off):
        return pl.BlockSpec((CONV_ROWS, SC_W), lambda i: (i, cb(off)))

    def prev(off):
        return pl.BlockSpec((CONV_HALO, SC_W), lambda i: (jnp.maximum(i * halo_per_block - 1, 0), cb(off)))

    def nxt(off):
        return pl.BlockSpec((CONV_HALO, SC_W), lambda i: (jnp.minimum((i + 1) * halo_per_block, last_halo), cb(off)))

    def vec(rows):
        return pl.BlockSpec((rows, SC_W), lambda i: (0, 0))

    o_bg, o_cg, o_xi, o_a, o_g = OFF_SC, OFF_SC + SC_W, OFF_SC + 2 * SC_W, OFF_CF, OFF_CF + CF_W
    kernel = functools.partial(_conv_kernel, lat_blocks=n_lat // CONV_ROWS, blocks_per_seq=seq // CONV_ROWS)
    return pl.pallas_call(
        kernel,
        out_shape=(jax.ShapeDtypeStruct((n_rows, SC_W), BF16), jax.ShapeDtypeStruct((n_rows, CF_W), BF16)),
        grid=(nb,),
        in_specs=[main(o_bg), main(o_cg), main(o_xi), main(o_a), main(o_g),
                  prev(o_cg), prev(o_xi), prev(o_a), prev(o_g),
                  nxt(o_cg), nxt(o_xi), nxt(o_a), nxt(o_g),
                  vec(SC_TAPS), vec(CF_TAPS), vec(1), vec(1), vec(1)],
        out_specs=(pl.BlockSpec((CONV_ROWS, SC_W), lambda i: (i, 0)),
                   pl.BlockSpec((CONV_ROWS, CF_W), lambda i: (i, 0))),
        scratch_shapes=[pltpu.VMEM((CONV_ROWS + 2 * CONV_HALO, SC_W), F32)],
        compiler_params=_params(("arbitrary",)),
        name="convs",
    )(p, p, p, p, p, p, p, p, p, p, p, p, p,
      sc_w, cf_w, cf_b.reshape(1, CF_W), ln_g.reshape(1, CF_W), ln_b.reshape(1, CF_W))


def _merge_kernel(*refs, a_blocks):
    n_gate = 3 * MERGE_GATE_TILES
    ata_ref, atb_ref, sc_ref, cf_ref = refs[:4]
    gate_refs = refs[4:4 + n_gate]
    bg_ref, wa_ref, wb_ref, wc_ref, wm_ref, ha_ref, hb_ref, g1_ref, o_ref = refs[4 + n_gate:]
    d = D_MODEL
    from_a = pl.program_id(0) < a_blocks
    attn = jnp.where(from_a, ata_ref[...], atb_ref[...])
    h = jnp.where(from_a, ha_ref[...], hb_ref[...])

    def gate(branch):
        tiles = gate_refs[branch * MERGE_GATE_TILES:(branch + 1) * MERGE_GATE_TILES]
        pre = jnp.concatenate([t[...] for t in tiles], axis=1).astype(F32)
        return _sigmoid(pre + bg_ref[:, branch * d:(branch + 1) * d])

    ya = jnp.dot(attn, wa_ref[...], preferred_element_type=F32)
    yb = jnp.dot(sc_ref[...], wb_ref[...], preferred_element_type=F32)
    yc = jnp.dot(cf_ref[...], wc_ref[...], preferred_element_type=F32)
    m = (gate(0) * ya + gate(1) * yb + gate(2) * yc).astype(BF16)
    o_ref[...] = h + g1_ref[...] * jnp.dot(m, wm_ref[...], preferred_element_type=F32)


def _merge(attn, sc, cf, p, b_gate, wa, wb, wc, wm, rows, mods, *, n_rows, n_lat, seq, gate_idx, tm=256):
    d = D_MODEL
    (ata, atb), (ha, hb) = attn, rows
    a_blocks = ha.shape[0] // tm
    first = lambda w: pl.BlockSpec((tm, w), lambda i: (jnp.minimum(i, a_blocks - 1), 0))
    rest = lambda w: pl.BlockSpec((tm, w), lambda i: (jnp.maximum(i - a_blocks, 0), 0))
    tg = d // MERGE_GATE_TILES
    gb0 = OFF_GATE // tg
    gates = [pl.BlockSpec((tm, tg), functools.partial(lambda i, c: (i, c), c=gb0 + t))
             for t in range(3 * MERGE_GATE_TILES)]
    whole = lambda a: pl.BlockSpec(a.shape, lambda i: (0, 0))
    return pl.pallas_call(
        functools.partial(_merge_kernel, a_blocks=a_blocks),
        out_shape=jax.ShapeDtypeStruct((n_rows, d), F32),
        grid=(n_rows // tm,),
        in_specs=[
            first(ATTN_W), rest(ATTN_W),
            pl.BlockSpec((tm, SC_W), lambda i: (i, 0)),
            pl.BlockSpec((tm, CF_W), lambda i: (i, 0)),
            *gates,
            whole(b_gate), whole(wa), whole(wb), whole(wc), whole(wm),
            first(d), rest(d),
            pl.BlockSpec((None, 1, d), lambda i: (_mod_row(i * tm, n_lat, seq), 0, gate_idx)),
        ],
        out_specs=pl.BlockSpec((tm, d), lambda i: (i, 0)),
        compiler_params=_params(("arbitrary",)),
        name="merge",
    )(ata, atb, sc, cf, *([p] * (3 * MERGE_GATE_TILES)), b_gate, wa, wb, wc, wm, ha, hb, mods)


def _split2(x):
    hi = x.astype(BF16)
    return hi, (x - hi.astype(F32)).astype(BF16)


def _router_kernel(h_ref, g_ref, sh_ref, sc_ref, rw_ref, rb_ref, f_ref, id_ref, wt_ref):
    x = h_ref[...]
    ms = jnp.mean(x * x, axis=-1, keepdims=True)
    f = x * lax.rsqrt(ms + NORM_EPS) * g_ref[...] * (1.0 + sc_ref[...]) + sh_ref[...]
    f_ref[...] = f

    f_hi, f_lo = _split2(f)
    w_hi, w_lo = _split2(rw_ref[...])
    dot = lambda a, b: jnp.dot(a, b, preferred_element_type=F32)
    logits = (dot(f_lo, w_hi) + dot(f_hi, w_lo) + dot(f_hi, w_hi)) + rb_ref[...]

    lane_i = lax.broadcasted_iota(I32, logits.shape, 1)
    lane = lane_i.astype(F32)
    neg = -jnp.inf
    big = float(LANES)
    lg = jnp.where(lane_i < N_GROUPS, logits, neg)
    mg = jnp.max(lg, axis=-1, keepdims=True)
    grp = jnp.min(jnp.where(lg == mg, lane, big), axis=-1, keepdims=True)
    p_grp = 1.0 / jnp.sum(jnp.exp(lg - mg), axis=-1, keepdims=True)

    e_lane = lane - N_GROUPS
    in_grp = jnp.logical_and(e_lane >= grp * EXPERTS_PER_GROUP, e_lane < (grp + 1.0) * EXPERTS_PER_GROUP)
    le = jnp.where(in_grp, logits, neg)
    m1 = jnp.max(le, axis=-1, keepdims=True)
    i1 = jnp.min(jnp.where(le == m1, lane, big), axis=-1, keepdims=True)
    le2 = jnp.where(lane == i1, neg, le)
    m2 = jnp.max(le2, axis=-1, keepdims=True)
    i2 = jnp.min(jnp.where(le2 == m2, lane, big), axis=-1, keepdims=True)
    e2 = jnp.exp(m2 - m1)
    w1 = p_grp / (1.0 + e2)
    w2 = p_grp * e2 / (1.0 + e2)
    ids = jnp.where(lane_i == 0, i1 - N_GROUPS, jnp.where(lane_i == 1, i2 - N_GROUPS, 0.0))
    id_ref[...] = ids.astype(I32)
    wt_ref[...] = jnp.where(lane_i == 0, w1, jnp.where(lane_i == 1, w2, 0.0))


def _router(h, gain, mods, rw, rb, *, n_rows, n_lat, seq, sh_idx, sc_idx, tm=512):
    d = D_MODEL

    def mod_map(idx):
        return lambda i: (_mod_row(i * tm, n_lat, seq), 0, idx)

    return pl.pallas_call(
        _router_kernel,
        out_shape=(jax.ShapeDtypeStruct((n_rows, d), F32),
                   jax.ShapeDtypeStruct((n_rows, LANES), I32),
                   jax.ShapeDtypeStruct((n_rows, LANES), F32)),
        grid=(n_rows // tm,),
        in_specs=[
            pl.BlockSpec((tm, d), lambda i: (i, 0)),
            pl.BlockSpec((1, d), lambda i: (0, 0)),
            pl.BlockSpec((None, 1, d), mod_map(sh_idx)),
            pl.BlockSpec((None, 1, d), mod_map(sc_idx)),
            pl.BlockSpec((d, LANES), lambda i: (0, 0)),
            pl.BlockSpec((1, LANES), lambda i: (0, 0)),
        ],
        out_specs=(pl.BlockSpec((tm, d), lambda i: (i, 0)),
                   pl.BlockSpec((tm, LANES), lambda i: (i, 0)),
                   pl.BlockSpec((tm, LANES), lambda i: (i, 0))),
        compiler_params=_params(("arbitrary",)),
        name="router",
    )(h, gain.reshape(1, d), mods, mods, rw, rb)


def _moe_plan(ids, n_tok):
    n_items = N_EXPERTS + (2 * n_tok) // MOE_ITEM_ROWS
    e_flat = ids[:, :2].reshape(-1)
    order = jnp.argsort(e_flat, stable=True).astype(I32)
    counts = jnp.sum((e_flat[:, None] == jnp.arange(N_EXPERTS, dtype=I32)[None, :]).astype(I32), axis=0)
    starts = jnp.cumsum(counts) - counts
    n_sb = (counts + MOE_ITEM_ROWS - 1) // MOE_ITEM_ROWS
    cum = jnp.cumsum(n_sb)
    total = cum[-1]
    it = jnp.arange(n_items, dtype=I32)
    e_i = jnp.minimum(jnp.searchsorted(cum, it, side="right").astype(I32), N_EXPERTS - 1)
    sb = it - (cum[e_i] - n_sb[e_i])
    valid = it < total
    last_e = e_i[jnp.maximum(total - 1, 0)]
    item_e = jnp.where(valid, e_i, last_e).astype(I32)
    item_start = jnp.where(valid, starts[e_i] + sb * MOE_ITEM_ROWS, 0).astype(I32)
    item_n = jnp.where(valid, jnp.minimum(MOE_ITEM_ROWS, counts[e_i] - sb * MOE_ITEM_ROWS), 0).astype(I32)
    front, back = jnp.zeros((MOE_ITEM_SHIFT,), I32), jnp.zeros((1,), I32)
    pad_items = lambda a: jnp.concatenate([front, a, back])
    return pad_items(item_e), pad_items(item_start), pad_items(item_n), (order // 2).astype(I32), order


def _moe_kernel(ie_ref, ist_ref, in_ref, tok_ref, slot_ref,
                f_hbm, wg_ref, wu_ref, wd_ref, out_hbm,
                xf_ref, xb_ref, yacc_ref, yout_ref, gsem, ssem):
    i, j = pl.program_id(0), pl.program_id(1)
    sh = MOE_ITEM_SHIFT
    n_prev, n, n_next = in_ref[i + sh - 1], in_ref[i + sh], in_ref[i + sh + 1]
    start, start_next = ist_ref[i + sh], ist_ref[i + sh + 1]

    def row_in(r, tok):
        return pltpu.make_async_copy(f_hbm.at[pl.ds(tok, 1), :], xf_ref.at[pl.ds(r, 1), :], gsem)

    def row_out(r, slot):
        return pltpu.make_async_copy(yout_ref.at[pl.ds(r, 1), :], out_hbm.at[slot & 1, pl.ds(slot >> 1, 1), :], ssem)

    def rows_in(size):
        return pltpu.make_async_copy(f_hbm.at[pl.ds(0, size), :], xf_ref.at[pl.ds(0, size), :], gsem)

    def rows_out(size):
        return pltpu.make_async_copy(yout_ref.at[pl.ds(0, size), :], out_hbm.at[0, pl.ds(0, size), :], ssem)

    def start_rows(make, idx_ref, base, count):
        def body8(t, c):
            r0 = pl.multiple_of(t * MOE_ISSUE_UNROLL, MOE_ISSUE_UNROLL)
            for k in range(MOE_ISSUE_UNROLL):
                make(r0 + k, idx_ref[base + r0 + k]).start()
            return c

        lax.fori_loop(0, count // MOE_ISSUE_UNROLL, body8, 0)

        def body1(r, c):
            make(r, idx_ref[base + r]).start()
            return c

        lax.fori_loop((count // MOE_ISSUE_UNROLL) * MOE_ISSUE_UNROLL, count, body1, 0)

    def gather_loop(base, count):
        start_rows(row_in, tok_ref, base, count)

    def scatter_loop(base, count):
        start_rows(row_out, slot_ref, base, count)

    def wait_rows(count, make_block):
        for b in range(MOE_ITEM_ROWS.bit_length()):
            @pl.when(((count >> b) & 1) == 1)
            def _():
                make_block(1 << b).wait()

    @pl.when(jnp.logical_and(i == 0, j == 0))
    def _():
        xf_ref[...] = jnp.zeros_like(xf_ref)
        yacc_ref[...] = jnp.zeros_like(yacc_ref)

    def publish(count):
        def body(t, c):
            rows = pl.ds(pl.multiple_of(t * MOE_ROW_PAD, MOE_ROW_PAD), MOE_ROW_PAD)
            yout_ref[rows, :] = yacc_ref[rows, :]
            return c

        lax.fori_loop(0, (count + MOE_ROW_PAD - 1) // MOE_ROW_PAD, body, 0)

    @pl.when(j == 0)
    def _():
        @pl.when(i == 0)
        def _():
            gather_loop(start, n)

        @pl.when(n > 0)
        def _():
            wait_rows(n, rows_in)
            xb_ref[...] = xf_ref[...].astype(BF16)
            gather_loop(start_next, n_next)

    def compute(m):
        x = xb_ref[0:m, :]
        g = jnp.dot(x, wg_ref[...].astype(BF16), preferred_element_type=F32)
        u = jnp.dot(x, wu_ref[...].astype(BF16), preferred_element_type=F32)
        hmid = (g * _sigmoid(g) * u).astype(BF16)
        y = jnp.dot(hmid, wd_ref[...].astype(BF16), preferred_element_type=F32)
        yacc_ref[0:m, :] = y + jnp.where(j == 0, 0.0, yacc_ref[0:m, :])

    n_pad = (n + MOE_ROW_PAD - 1) // MOE_ROW_PAD
    for k in range(1, MOE_ITEM_ROWS // MOE_ROW_PAD + 1):
        @pl.when(n_pad == k)
        def _():
            compute(k * MOE_ROW_PAD)

    @pl.when(j == pl.num_programs(1) - 1)
    def _():
        wait_rows(n_prev, rows_out)

        @pl.when(n > 0)
        def _():
            publish(n)
            scatter_loop(start, n)

        @pl.when(i == pl.num_programs(0) - 1)
        def _():
            wait_rows(n, rows_out)


def _moe(f, plan, w_gu, w_down, layer, *, n_tok):
    item_e, item_start, item_n, tok_sorted, slot_sorted = plan
    d, hid, hk = D_MODEL, EXPERT_HIDDEN, MOE_HIDDEN_BLOCK
    nj = hid // hk
    sh = MOE_ITEM_SHIFT
    n_items = item_e.shape[0] - sh - 1

    def chunk(j, nn, i):
        return jnp.where(nn[i + sh] > 0, j, nj - 1)

    grid_spec = pltpu.PrefetchScalarGridSpec(
        num_scalar_prefetch=5,
        grid=(n_items, nj),
        in_specs=[
            pl.BlockSpec(memory_space=pl.ANY),
            pl.BlockSpec((None, None, d, hk), lambda i, j, ie, ist, nn, tk, sl: (layer, ie[i + sh],0, chunk(j, nn, i))),
            pl.BlockSpec((None, None, d, hk),
                         lambda i, j, ie, ist, nn, tk, sl: (layer, ie[i + sh],0, nj + chunk(j, nn, i))),
            pl.BlockSpec((None, None, hk, d), lambda i, j, ie, ist, nn, tk, sl: (layer, ie[i + sh],chunk(j, nn, i), 0)),
        ],
        out_specs=pl.BlockSpec(memory_space=pl.ANY),
        scratch_shapes=[
            pltpu.VMEM((MOE_ITEM_ROWS, d), F32),
            pltpu.VMEM((MOE_ITEM_ROWS, d), BF16),
            pltpu.VMEM((MOE_ITEM_ROWS, d), F32),
            pltpu.VMEM((MOE_ITEM_ROWS, d), F32),
            pltpu.SemaphoreType.DMA,
            pltpu.SemaphoreType.DMA,
        ],
    )
    return pl.pallas_call(
        _moe_kernel,
        out_shape=jax.ShapeDtypeStruct((2, n_tok, d), F32),
        grid_spec=grid_spec,
        compiler_params=_params(("arbitrary", "arbitrary")),
        name="moe_experts",
    )(item_e, item_start, item_n, tok_sorted, slot_sorted, f, w_gu, w_gu, w_down)


def _combine_kernel(h_ref, y0_ref, y1_ref, wt_ref, g2_ref, fg_ref, o_ref, *, final):
    wt = wt_ref[...]
    moe = wt[:, 0:1] * y0_ref[...] + wt[:, 1:2] * y1_ref[...]
    h = h_ref[...] + g2_ref[...] * moe
    if final:
        ms = jnp.mean(h * h, axis=-1, keepdims=True)
        h = h * lax.rsqrt(ms + NORM_EPS) * fg_ref[...]
    o_ref[...] = h


def _combine(h, slots, wts, mods, final_g, *, n_rows, n_lat, seq, gate_idx, final, tm=256):
    d = D_MODEL
    kernel = functools.partial(_combine_kernel, final=final)
    return pl.pallas_call(
        kernel,
        out_shape=jax.ShapeDtypeStruct((n_rows, d), F32),
        grid=(n_rows // tm,),
        in_specs=[
            pl.BlockSpec((tm, d), lambda i: (i, 0)),
            pl.BlockSpec((None, tm, d), lambda i: (0, i, 0)),
            pl.BlockSpec((None, tm, d), lambda i: (1, i, 0)),
            pl.BlockSpec((tm, LANES), lambda i: (i, 0)),
            pl.BlockSpec((None, 1, d), lambda i: (_mod_row(i * tm, n_lat, seq), 0, gate_idx)),
            pl.BlockSpec((1, d), lambda i: (0, 0)),
        ],
        out_specs=pl.BlockSpec((tm, d), lambda i: (i, 0)),
        compiler_params=_params(("arbitrary",)),
        name="combine",
    )(h, slots, slots, wts, mods, final_g.reshape(1, d))


def kernel(x, c, ctx, c_ctx, ada_w, ada_b, norm1_g, w_in, b_gate, diff_lambda, subln_g, w_attn_out, sc_conv_w,
           w_sc_out, cf_dw_w, cf_dw_b, cf_ln_g, cf_ln_b, w_cf_out, w_mix, norm2_g, router_g_w, router_g_b,
           router_e_w, router_e_b, exp_w_gu, exp_w_down, final_g):
    batch, seq, d = x.shape
    n_ctx = ctx.shape[1]
    depth = ada_w.shape[0]
    n_lat = batch * seq
    n_all = n_lat + batch * n_ctx
    assert d == D_MODEL and batch == MOD_ROWS // 2 and w_in.shape[2] == C_TOT

    cond = jnp.concatenate([c, c_ctx[None, :], jnp.zeros((MOD_ROWS - batch - 1, d), F32)], axis=0)
    mods_all = _adaln(cond, ada_w, ada_b)
    cos, sin = _rope_tables(seq)
    rows = (x.reshape(n_lat, d), ctx.reshape(batch * n_ctx, d))
    geo = dict(n_lat=n_lat, seq=seq)

    for layer in range(depth):
        last = layer == depth - 1
        lam_init = 0.8 - 0.6 * math.exp(-0.3 * layer)
        mods = mods_all[layer].reshape(MOD_ROWS, 1, 6 * d)
        n_rows = n_lat if last else n_all

        p = _in_proj(rows, norm1_g[layer], mods, w_in, layer, row0=0, n_rows=n_rows, col0=0, n_cols=C_TOT,
                     sh_idx=0, sc_idx=1, tm=1024, **geo)
        if last:
            kv_ctx = _in_proj(rows, norm1_g[layer], mods, w_in, layer, row0=n_lat, n_rows=batch * n_ctx, col0=OFF_K,
                              n_cols=OFF_SC - OFF_K, sh_idx=0, sc_idx=1, tm=n_ctx, **geo)
            kc_blk, vc_blk, ctx_rb0 = 0, QK_W // HEAD_W, 0
        else:
            kv_ctx = p
            kc_blk, vc_blk, ctx_rb0 = OFF_K // HEAD_W, OFF_V // HEAD_W, n_lat // n_ctx
        attn = _attn_lat(p, kv_ctx, kc_blk, vc_blk, ctx_rb0, cos, sin, diff_lambda[layer], subln_g[layer],
                         batch=batch, seq=seq, n_ctx=n_ctx, lam_init=lam_init)
        if not last:
            attn = (attn, _attn_ctx(p, n_lat // n_ctx, diff_lambda[layer], subln_g[layer],
                                    batch=batch, n_ctx=n_ctx, lam_init=lam_init))
        else:
            attn = (attn, attn)
        sc, cf = _convs(p, sc_conv_w[layer], cf_dw_w[layer], cf_dw_b[layer], cf_ln_g[layer], cf_ln_b[layer],
                        n_rows=n_rows, **geo)
        h_mix = _merge(attn, sc, cf, p, b_gate[layer].reshape(1, 3 * d),
                       w_attn_out[layer].astype(BF16), w_sc_out[layer].astype(BF16),
                       w_cf_out[layer].astype(BF16), w_mix[layer].astype(BF16), rows, mods,
                       n_rows=n_rows, gate_idx=2, **geo)

        rw = jnp.concatenate([router_g_w[layer], router_e_w[layer],
                              jnp.zeros((d, LANES - N_GROUPS - N_EXPERTS), F32)], axis=1)
        rb = jnp.concatenate([router_g_b[layer], router_e_b[layer],
                              jnp.zeros((LANES - N_GROUPS - N_EXPERTS,), F32)]).reshape(1, LANES)
        f, ids, wts = _router(h_mix, norm2_g[layer], mods, rw, rb, n_rows=n_rows, sh_idx=3, sc_idx=4, **geo)
        slots = _moe(f, _moe_plan(ids, n_rows), exp_w_gu, exp_w_down, layer, n_tok=n_rows)
        h = _combine(h_mix, slots, wts, mods, final_g, n_rows=n_rows, gate_idx=5, final=last, **geo)
        rows = (h, h)

    return h.reshape(batch, seq, d)
```

```python
import functools
import math

import jax
import jax.numpy as jnp
from jax import lax
from jax.experimental import pallas as pl
from jax.experimental.pallas import tpu as pltpu

F32 = jnp.float32
BF16 = jnp.bfloat16
I32 = jnp.int32

D_MODEL = 2048
GRID_W = 64
NORM_EPS = 1e-6
N_HEADS = 8
HEAD_DIM = 64
HEAD_W = 2 * HEAD_DIM
QK_W = N_HEADS * HEAD_W
ATTN_W = N_HEADS * HEAD_W
ROPE_BASE = 10000.0
ROPE_FREQS = HEAD_DIM // 4
SUBLN_EPS = 1e-5
QK_SCALE = HEAD_DIM ** -0.5 * math.log2(math.e)
SC_W = D_MODEL // 4
CF_W = D_MODEL // 4
SC_TAPS = 3
CF_TAPS = 31
CF_LN_EPS = 1e-5
OFF_Q = 0
OFF_K = OFF_Q + QK_W
OFF_V = OFF_K + QK_W
OFF_SC = OFF_V + ATTN_W
OFF_CF = OFF_SC + 3 * SC_W
OFF_GATE = OFF_CF + 2 * CF_W
C_TOT = OFF_GATE + 3 * D_MODEL
N_GROUPS = 4
EXPERTS_PER_GROUP = 8
N_EXPERTS = N_GROUPS * EXPERTS_PER_GROUP
EXPERT_HIDDEN = D_MODEL // 2

LANES = 128
SUBLANES = 8
MOD_ROWS = 8
ATTN_SUB_ROWS = 256
MERGE_GATE_TILES = 4
CONV_HALO = 16
CONV_ROWS = 256
CONV_CHUNK = 32
MOE_ITEM_ROWS = 1024
MOE_ITEM_CAP = MOE_ITEM_ROWS - 8
MOE_ROW_PAD = 128
MOE_HIDDEN_BLOCK = 256
MOE_ITEM_SHIFT = 1
SLOT_BLOCK_ROWS = 256
VMEM_LIMIT = 56 * 1024 * 1024


def _params(sem, vmem=VMEM_LIMIT):
    return pltpu.CompilerParams(dimension_semantics=sem, vmem_limit_bytes=vmem)


def _sigmoid(x):
    return 1.0 / (1.0 + jnp.exp(-x))


def _adaln_kernel(s_ref, w_ref, b_ref, o_ref):
    s = s_ref[...]
    s = s * _sigmoid(s)
    s_hi = s.astype(BF16)
    s_lo = (s - s_hi.astype(F32)).astype(BF16)
    w = w_ref[...]
    w_hi = w.astype(BF16)
    w_lo = (w - w_hi.astype(F32)).astype(BF16)
    lhs = jnp.concatenate([s_hi.astype(F32), s - s_hi.astype(F32)], axis=0).astype(BF16)
    r = jnp.dot(lhs, w_hi, preferred_element_type=F32)
    r2 = jnp.dot(s_hi, w_lo, preferred_element_type=F32)
    o_ref[...] = r[:MOD_ROWS] + r[MOD_ROWS:] + r2 + b_ref[...]


def _adaln(cond, ada_w, ada_b):
    n_layers, d, n = ada_w.shape
    tn = 512
    return pl.pallas_call(
        _adaln_kernel,
        out_shape=jax.ShapeDtypeStruct((n_layers, MOD_ROWS, n), F32),
        grid=(n_layers, n // tn),
        in_specs=[
            pl.BlockSpec((MOD_ROWS, d), lambda l, j: (0, 0)),
            pl.BlockSpec((None, d, tn), lambda l, j: (l, 0, j)),
            pl.BlockSpec((None, 1, tn), lambda l, j: (l, 0, j)),
        ],
        out_specs=pl.BlockSpec((None, MOD_ROWS, tn), lambda l, j: (l, 0, j)),
        compiler_params=_params(("arbitrary", "arbitrary")),
        name="adaln",
    )(cond, ada_w, ada_b.reshape(n_layers, 1, n))


def _mod_row(row0, n_lat, seq):
    return jnp.where(row0 < n_lat, row0 // seq, MOD_ROWS // 2)


def _in_proj_kernel(xa_ref, xb_ref, g_ref, sh_ref, sc_ref, w_ref, o_ref, u_ref, *, a_blocks):
    first_col = pl.program_id(1) == 0
    from_a = pl.program_id(0) < a_blocks

    def normed(x_ref):
        x = x_ref[...]
        ms = jnp.mean(x * x, axis=-1, keepdims=True)
        y = x * lax.rsqrt(ms + NORM_EPS) * g_ref[...]
        u_ref[...] = (y * (1.0 + sc_ref[...]) + sh_ref[...]).astype(BF16)

    @pl.when(jnp.logical_and(first_col, from_a))
    def _():
        normed(xa_ref)

    @pl.when(jnp.logical_and(first_col, jnp.logical_not(from_a)))
    def _():
        normed(xb_ref)

    o_ref[...] = jnp.dot(u_ref[...], w_ref[...].astype(BF16), preferred_element_type=F32).astype(o_ref.dtype)


def _in_proj(rows, gain, mods, w_in, layer, *, row0, n_rows, col0, n_cols, n_lat, seq, sh_idx, sc_idx, tm, tn=512):
    xa, xb = rows
    d = xa.shape[1]
    rb0, cb0 = row0 // tm, col0 // tn
    a_blocks = xa.shape[0] // tm - rb0

    def mod_map(idx):
        return lambda i, j: (_mod_row((i + rb0) * tm, n_lat, seq), 0, idx)

    return pl.pallas_call(
        functools.partial(_in_proj_kernel, a_blocks=a_blocks),
        out_shape=jax.ShapeDtypeStruct((n_rows, n_cols), BF16),
        grid=(n_rows // tm, n_cols // tn),
        in_specs=[
            pl.BlockSpec((tm, d), lambda i, j: (jnp.minimum(i, a_blocks - 1) + rb0, 0)),
            pl.BlockSpec((tm, d), lambda i, j: (jnp.maximum(i - a_blocks, 0), 0)),
            pl.BlockSpec((1, d), lambda i, j: (0, 0)),
            pl.BlockSpec((None, 1, d), mod_map(sh_idx)),
            pl.BlockSpec((None, 1, d), mod_map(sc_idx)),
            pl.BlockSpec((None, d, tn), lambda i, j: (layer, 0, j + cb0)),
        ],
        out_specs=pl.BlockSpec((tm, tn), lambda i, j: (i, j)),
        scratch_shapes=[pltpu.VMEM((tm, d), BF16)],
        compiler_params=_params(("arbitrary", "arbitrary")),
        name="in_proj",
    )(xa, xb, gain.reshape(1, d), mods, mods, w_in)


def _rope_tables(n_tokens):
    t = jnp.arange(n_tokens, dtype=I32)
    pos = jnp.stack([t // GRID_W, t % GRID_W], axis=-1).astype(F32)
    inv_freq = ROPE_BASE ** (-jnp.arange(ROPE_FREQS, dtype=F32) / ROPE_FREQS)
    ang = pos[:, :, None] * inv_freq
    cos, sin = jnp.cos(ang), jnp.sin(ang)
    c = jnp.stack([cos, cos], axis=2).reshape(n_tokens, HEAD_DIM)
    s = jnp.stack([-sin, sin], axis=2).reshape(n_tokens, HEAD_DIM)
    return jnp.tile(c, (1, 2)), jnp.tile(s, (1, 2))


def _rope(x, c, s):
    lane = lax.broadcasted_iota(I32, x.shape, 1)
    first_half = (lane % (2 * ROPE_FREQS)) < ROPE_FREQS
    partner = jnp.where(first_half, pltpu.roll(x, LANES - ROPE_FREQS, 1), pltpu.roll(x, ROPE_FREQS, 1))
    return x * c + partner * s


def _diff_lambda(lam_ref, lam_init):
    lv = lam_ref[...]
    a = jnp.sum(lv[0:1] * lv[1:2], axis=-1, keepdims=True)
    b = jnp.sum(lv[2:3] * lv[3:4], axis=-1, keepdims=True)
    return jnp.exp(a) - jnp.exp(b) + lam_init


def _attend(q, k_all, v_all, lam, subg, lam_init):
    tq = q.shape[0]
    lane = lax.broadcasted_iota(I32, q.shape, 1)
    q0 = jnp.where(lane < HEAD_DIM, q, 0.0).astype(BF16)
    q1 = jnp.where(lane >= HEAD_DIM, q, 0.0).astype(BF16)
    qq = jnp.concatenate([q0, q1], axis=0)
    s = lax.dot_general(qq, k_all, (((1,), (1,)), ((), ())), preferred_element_type=F32)
    m = jnp.max(s, axis=-1, keepdims=True)
    e = jnp.exp2(s - m)
    l = jnp.sum(e, axis=-1, keepdims=True)
    a = e[:tq] - e[tq:] * (lam * l[:tq] / l[tq:])
    o = jnp.dot(a.astype(BF16), v_all, preferred_element_type=F32) / l[:tq]
    ms = jnp.mean(o * o, axis=-1, keepdims=True)
    return o * lax.rsqrt(ms + SUBLN_EPS) * subg * (1.0 - lam_init)


def _attn_lat_kernel(q_ref, kl_ref, vl_ref, kc_ref, vc_ref, cq_ref, sq_ref, ck_ref, sk_ref, lam_ref, g_ref,
                     o_ref, k_all, v_aug, s0_ref, m0_ref, s1_ref, m1_ref, *, n_ctx, lam_init, n_blocks):
    i = pl.program_id(2)
    tq = q_ref.shape[0]

    @pl.when(i == 0)
    def _():
        k_all[0:n_ctx, :] = kc_ref[...]
        k_all[n_ctx:, :] = _rope(kl_ref[...].astype(F32), ck_ref[...], sk_ref[...]).astype(BF16)
        v_aug[0:n_ctx, 0:HEAD_W] = vc_ref[...]
        v_aug[n_ctx:, 0:HEAD_W] = vl_ref[...]
        v_aug[:, HEAD_W:2 * HEAD_W] = jnp.ones((k_all.shape[0], HEAD_W), BF16)

    lam = _diff_lambda(lam_ref, lam_init)

    def stage_a(sa_ref, ma_ref):
        q = _rope(q_ref[...].astype(F32), cq_ref[...], sq_ref[...]) * QK_SCALE
        lane = lax.broadcasted_iota(I32, q.shape, 1)
        q0 = jnp.where(lane < HEAD_DIM, q, 0.0).astype(BF16)
        q1 = jnp.where(lane >= HEAD_DIM, q, 0.0).astype(BF16)
        qq = jnp.concatenate([q0, q1], axis=0)
        s = lax.dot_general(qq, k_all[...], (((1,), (1,)), ((), ())), preferred_element_type=F32)
        sa_ref[...] = s
        ma_ref[...] = jnp.broadcast_to(jnp.max(s, axis=-1, keepdims=True), ma_ref.shape)

    def stage_b(sb_ref, mb_ref):
        mb = jnp.concatenate([mb_ref[...]] * (sb_ref.shape[1] // LANES), axis=1)
        e = jnp.exp2(sb_ref[...] - mb).astype(BF16)
        oa = jnp.dot(e, v_aug[...], preferred_element_type=F32)
        o = oa[:tq, 0:HEAD_W] / oa[:tq, HEAD_W:] - oa[tq:, 0:HEAD_W] * (lam / oa[tq:, HEAD_W:])
        ms = jnp.mean(o * o, axis=-1, keepdims=True)
        o_ref[...] = (o * lax.rsqrt(ms + SUBLN_EPS) * g_ref[...] * (1.0 - lam_init)).astype(o_ref.dtype)

    bufs = ((s0_ref, m0_ref), (s1_ref, m1_ref))

    @pl.when(i == 0)
    def _():
        stage_a(*bufs[0])

    for par in range(2):
        @pl.when(jnp.logical_and(jnp.logical_and(i > 0, i < n_blocks), i % 2 == par))
        def _():
            stage_a(*bufs[par])
            stage_b(*bufs[1 - par])

    @pl.when(i == n_blocks)
    def _():
        stage_b(*bufs[(n_blocks - 1) % 2])


def _attn_ctx_kernel(q_ref, k_ref, v_ref, lam_ref, g_ref, o_ref, *, lam_init):
    q = q_ref[...].astype(F32) * QK_SCALE
    lam = _diff_lambda(lam_ref, lam_init)
    o_ref[...] = _attend(q, k_ref[...], v_ref[...], lam, g_ref[...], lam_init).astype(o_ref.dtype)


def _attn_lat(p, kv_ctx, kc_blk, vc_blk, ctx_rb0, cos, sin, lam4, subg, *, batch, seq, n_ctx, lam_init, tq=512):
    nq = seq // tq
    n_keys = n_ctx + seq
    hb = lambda off: off // HEAD_W
    qblk = lambda i: jnp.minimum(i, nq - 1)
    kernel = functools.partial(_attn_lat_kernel, n_ctx=n_ctx, lam_init=lam_init, n_blocks=nq)
    return pl.pallas_call(
        kernel,
        out_shape=jax.ShapeDtypeStruct((batch * seq, ATTN_W), BF16),
        grid=(batch, N_HEADS, nq + 1),
        in_specs=[
            pl.BlockSpec((tq, HEAD_W), lambda b, h, i: (b * nq + qblk(i), hb(OFF_Q) + h)),
            pl.BlockSpec((seq, HEAD_W), lambda b, h, i: (b, hb(OFF_K) + h)),
            pl.BlockSpec((seq, HEAD_W), lambda b, h, i: (b, hb(OFF_V) + h)),
            pl.BlockSpec((n_ctx, HEAD_W), lambda b, h, i: (ctx_rb0 + b, kc_blk + h)),
            pl.BlockSpec((n_ctx, HEAD_W), lambda b, h, i: (ctx_rb0 + b, vc_blk + h)),
            pl.BlockSpec((tq, HEAD_W), lambda b, h, i: (qblk(i), 0)),
            pl.BlockSpec((tq, HEAD_W), lambda b, h, i: (qblk(i), 0)),
            pl.BlockSpec((seq, HEAD_W), lambda b, h, i: (0, 0)),
            pl.BlockSpec((seq, HEAD_W), lambda b, h, i: (0, 0)),
            pl.BlockSpec((4, HEAD_DIM), lambda b, h, i: (0, 0)),
            pl.BlockSpec((1, HEAD_W), lambda b, h, i: (0, 0)),
        ],
        out_specs=pl.BlockSpec((tq, HEAD_W), lambda b, h, i: (b * nq + jnp.maximum(i - 1, 0), h)),
        scratch_shapes=[pltpu.VMEM((n_keys, HEAD_W), BF16), pltpu.VMEM((n_keys, 2 * HEAD_W), BF16),
                        pltpu.VMEM((2 * tq, n_keys), F32), pltpu.VMEM((2 * tq, LANES), F32),
                        pltpu.VMEM((2 * tq, n_keys), F32), pltpu.VMEM((2 * tq, LANES), F32)],
        compiler_params=_params(("arbitrary", "arbitrary", "arbitrary")),
        name="attn_lat",
    )(p, p, p, kv_ctx, kv_ctx, cos, sin, cos, sin, lam4, subg.reshape(1, HEAD_W))


def _attn_ctx(p, ctx_rb0, lam4, subg, *, batch, n_ctx, lam_init):
    hb = lambda off: off // HEAD_W
    kernel = functools.partial(_attn_ctx_kernel, lam_init=lam_init)
    return pl.pallas_call(
        kernel,
        out_shape=jax.ShapeDtypeStruct((batch * n_ctx, ATTN_W), BF16),
        grid=(batch, N_HEADS),
        in_specs=[
            pl.BlockSpec((n_ctx, HEAD_W), lambda b, h: (ctx_rb0 + b, hb(OFF_Q) + h)),
            pl.BlockSpec((n_ctx, HEAD_W), lambda b, h: (ctx_rb0 + b, hb(OFF_K) + h)),
            pl.BlockSpec((n_ctx, HEAD_W), lambda b, h: (ctx_rb0 + b, hb(OFF_V) + h)),
            pl.BlockSpec((4, HEAD_DIM), lambda b, h: (0, 0)),
            pl.BlockSpec((1, HEAD_W), lambda b, h: (0, 0)),
        ],
        out_specs=pl.BlockSpec((n_ctx, HEAD_W), lambda b, h: (b, h)),
        compiler_params=_params(("arbitrary", "arbitrary")),
        name="attn_ctx",
    )(p, p, p, lam4, subg.reshape(1, HEAD_W))


def _conv_kernel(bg_ref, cg_ref, xi_ref, a_ref, g_ref,
                 cg_p, xi_p, a_p, g_p, cg_n, xi_n, a_n, g_n,
                 scw_ref, cfw_ref, cfb_ref, lng_ref, lnb_ref,
                 sco_ref, cfo_ref, pad_ref, *, lat_blocks, blocks_per_seq):
    i = pl.program_id(0)
    in_lat = i < lat_blocks
    pos = i % blocks_per_seq
    has_prev = jnp.logical_and(in_lat, pos != 0)
    has_next = jnp.logical_and(in_lat, pos != blocks_per_seq - 1)
    keep_prev = jnp.where(has_prev, 1.0, 0.0)
    keep_next = jnp.where(has_next, 1.0, 0.0)
    lo, hi = CONV_HALO, CONV_HALO + CONV_ROWS

    def fill(main, prev, nxt):
        pad_ref[0:lo, :] = prev * keep_prev
        pad_ref[lo:hi, :] = main
        pad_ref[hi:hi + CONV_HALO, :] = nxt * keep_next

    def f(ref):
        return ref[...].astype(F32)

    def glu(a, g):
        return a * _sigmoid(g)

    fill(f(cg_ref) * f(xi_ref), f(cg_p) * f(xi_p), f(cg_n) * f(xi_n))
    for c in range(CONV_ROWS // CONV_CHUNK):
        r0 = lo + c * CONV_CHUNK - SC_TAPS // 2
        acc = scw_ref[0:1, :] * pad_ref[r0:r0 + CONV_CHUNK, :]
        for k in range(1, SC_TAPS):
            acc = acc + scw_ref[k:k + 1, :] * pad_ref[r0 + k:r0 + k + CONV_CHUNK, :]
        rows = slice(c * CONV_CHUNK, (c + 1) * CONV_CHUNK)
        sco_ref[rows, :] = (bg_ref[rows, :].astype(F32) * acc).astype(sco_ref.dtype)

    fill(glu(f(a_ref), f(g_ref)), glu(f(a_p), f(g_p)), glu(f(a_n), f(g_n)))
    off = lo - CF_TAPS // 2
    win = CONV_CHUNK + SUBLANES
    for c in range(CONV_ROWS // CONV_CHUNK):
        base = c * CONV_CHUNK
        acc = None
        for b in range(SUBLANES):
            q = None
            for k in range(b, CF_TAPS, SUBLANES):
                term = cfw_ref[k:k + 1, :] * pad_ref[base + k - b:base + k - b + win, :]
                q = term if q is None else q + term
            part = q[off + b:off + b + CONV_CHUNK, :]
            acc = part if acc is None else acc + part
        z = acc + cfb_ref[...]
        mu = jnp.mean(z, axis=-1, keepdims=True)
        zc = z - mu
        var = jnp.mean(zc * zc, axis=-1, keepdims=True)
        y = zc * lax.rsqrt(var + CF_LN_EPS) * lng_ref[...] + lnb_ref[...]
        rows = slice(c * CONV_CHUNK, (c + 1) * CONV_CHUNK)
        cfo_ref[rows, :] = (y * _sigmoid(y)).astype(cfo_ref.dtype)


def _convs(p, sc_w, cf_w, cf_b, ln_g, ln_b, *, n_rows, n_lat, seq):
    nb = n_rows // CONV_ROWS
    halo_per_block = CONV_ROWS // CONV_HALO
    last_halo = n_rows // CONV_HALO - 1
    cb = lambda off: off // SC_W

    def main(off):
        return pl.BlockSpec((CONV_ROWS, SC_W), lambda i: (i, cb(off)))

    def prev(off):
        return pl.BlockSpec((CONV_HALO, SC_W), lambda i: (jnp.maximum(i * halo_per_block - 1, 0), cb(off)))

    def nxt(off):
        return pl.BlockSpec((CONV_HALO, SC_W), lambda i: (jnp.minimum((i + 1) * halo_per_block, last_halo), cb(off)))

    def vec(rows):
        return pl.BlockSpec((rows, SC_W), lambda i: (0, 0))

    o_bg, o_cg, o_xi, o_a, o_g = OFF_SC, OFF_SC + SC_W, OFF_SC + 2 * SC_W, OFF_CF, OFF_CF + CF_W
    kernel = functools.partial(_conv_kernel, lat_blocks=n_lat // CONV_ROWS, blocks_per_seq=seq // CONV_ROWS)
    return pl.pallas_call(
        kernel,
        out_shape=(jax.ShapeDtypeStruct((n_rows, SC_W), BF16), jax.ShapeDtypeStruct((n_rows, CF_W), BF16)),
        grid=(nb,),
        in_specs=[main(o_bg), main(o_cg), main(o_xi), main(o_a), main(o_g),
                  prev(o_cg), prev(o_xi), prev(o_a), prev(o_g),
                  nxt(o_cg), nxt(o_xi), nxt(o_a), nxt(o_g),
                  vec(SC_TAPS), vec(CF_TAPS), vec(1), vec(1), vec(1)],
        out_specs=(pl.BlockSpec((CONV_ROWS, SC_W), lambda i: (i, 0)),
                   pl.BlockSpec((CONV_ROWS, CF_W), lambda i: (i, 0))),
        scratch_shapes=[pltpu.VMEM((CONV_ROWS + 2 * CONV_HALO, SC_W), F32)],
        compiler_params=_params(("arbitrary",)),
        name="convs",
    )(p, p, p, p, p, p, p, p, p, p, p, p, p,
      sc_w, cf_w, cf_b.reshape(1, CF_W), ln_g.reshape(1, CF_W), ln_b.reshape(1, CF_W))


def _merge_kernel(*refs, a_blocks):
    n_gate = 3 * MERGE_GATE_TILES
    ata_ref, atb_ref, sc_ref, cf_ref = refs[:4]
    gate_refs = refs[4:4 + n_gate]
    bg_ref, wa_ref, wb_ref, wc_ref, wm_ref, ha_ref, hb_ref, g1_ref, o_ref = refs[4 + n_gate:]
    d = D_MODEL
    from_a = pl.program_id(0) < a_blocks
    attn = jnp.where(from_a, ata_ref[...], atb_ref[...])
    h = jnp.where(from_a, ha_ref[...], hb_ref[...])

    def gate(branch):
        tiles = gate_refs[branch * MERGE_GATE_TILES:(branch + 1) * MERGE_GATE_TILES]
        pre = jnp.concatenate([t[...] for t in tiles], axis=1).astype(F32)
        return _sigmoid(pre + bg_ref[:, branch * d:(branch + 1) * d])

    ya = jnp.dot(attn, wa_ref[...], preferred_element_type=F32)
    yb = jnp.dot(sc_ref[...], wb_ref[...], preferred_element_type=F32)
    yc = jnp.dot(cf_ref[...], wc_ref[...], preferred_element_type=F32)
    m = (gate(0) * ya + gate(1) * yb + gate(2) * yc).astype(BF16)
    o_ref[...] = h + g1_ref[...] * jnp.dot(m, wm_ref[...], preferred_element_type=F32)


def _merge(attn, sc, cf, p, b_gate, wa, wb, wc, wm, rows, mods, *, n_rows, n_lat, seq, gate_idx, tm=256):
    d = D_MODEL
    (ata, atb), (ha, hb) = attn, rows
    a_blocks = ha.shape[0] // tm
    first = lambda w: pl.BlockSpec((tm, w), lambda i: (jnp.minimum(i, a_blocks - 1), 0))
    rest = lambda w: pl.BlockSpec((tm, w), lambda i: (jnp.maximum(i - a_blocks, 0), 0))
    tg = d // MERGE_GATE_TILES
    gb0 = OFF_GATE // tg
    gates = [pl.BlockSpec((tm, tg), functools.partial(lambda i, c: (i, c), c=gb0 + t))
             for t in range(3 * MERGE_GATE_TILES)]
    whole = lambda a: pl.BlockSpec(a.shape, lambda i: (0, 0))
    return pl.pallas_call(
        functools.partial(_merge_kernel, a_blocks=a_blocks),
        out_shape=jax.ShapeDtypeStruct((n_rows, d), F32),
        grid=(n_rows // tm,),
        in_specs=[
            first(ATTN_W), rest(ATTN_W),
            pl.BlockSpec((tm, SC_W), lambda i: (i, 0)),
            pl.BlockSpec((tm, CF_W), lambda i: (i, 0)),
            *gates,
            whole(b_gate), whole(wa), whole(wb), whole(wc), whole(wm),
            first(d), rest(d),
            pl.BlockSpec((None, 1, d), lambda i: (_mod_row(i * tm, n_lat, seq), 0, gate_idx)),
        ],
        out_specs=pl.BlockSpec((tm, d), lambda i: (i, 0)),
        compiler_params=_params(("arbitrary",)),
        name="merge",
    )(ata, atb, sc, cf, *([p] * (3 * MERGE_GATE_TILES)), b_gate, wa, wb, wc, wm, ha, hb, mods)


def _split2(x):
    hi = x.astype(BF16)
    return hi, (x - hi.astype(F32)).astype(BF16)


def _norm_mod(x, g, sh, sc):
    ms = jnp.mean(x * x, axis=-1, keepdims=True)
    return x * lax.rsqrt(ms + NORM_EPS) * g * (1.0 + sc) + sh


def _router_kernel(h_ref, g_ref, sh_ref, sc_ref, rw_ref, rb_ref, id_ref, wt_ref):
    f = _norm_mod(h_ref[...], g_ref[...], sh_ref[...], sc_ref[...])

    f_hi, f_lo = _split2(f)
    w_hi, w_lo = _split2(rw_ref[...])
    dot = lambda a, b: jnp.dot(a, b, preferred_element_type=F32)
    logits = (dot(f_lo, w_hi) + dot(f_hi, w_lo) + dot(f_hi, w_hi)) + rb_ref[...]

    lane_i = lax.broadcasted_iota(I32, logits.shape, 1)
    lane = lane_i.astype(F32)
    neg = -jnp.inf
    big = float(LANES)
    lg = jnp.where(lane_i < N_GROUPS, logits, neg)
    mg = jnp.max(lg, axis=-1, keepdims=True)
    grp = jnp.min(jnp.where(lg == mg, lane, big), axis=-1, keepdims=True)
    p_grp = 1.0 / jnp.sum(jnp.exp(lg - mg), axis=-1, keepdims=True)

    e_lane = lane - N_GROUPS
    in_grp = jnp.logical_and(e_lane >= grp * EXPERTS_PER_GROUP, e_lane < (grp + 1.0) * EXPERTS_PER_GROUP)
    le = jnp.where(in_grp, logits, neg)
    m1 = jnp.max(le, axis=-1, keepdims=True)
    i1 = jnp.min(jnp.where(le == m1, lane, big), axis=-1, keepdims=True)
    le2 = jnp.where(lane == i1, neg, le)
    m2 = jnp.max(le2, axis=-1, keepdims=True)
    i2 = jnp.min(jnp.where(le2 == m2, lane, big), axis=-1, keepdims=True)
    e2 = jnp.exp(m2 - m1)
    w1 = p_grp / (1.0 + e2)
    w2 = p_grp * e2 / (1.0 + e2)
    ids = jnp.where(lane_i == 0, i1 - N_GROUPS, jnp.where(lane_i == 1, i2 - N_GROUPS, 0.0))
    id_ref[...] = ids.astype(I32)
    wt_ref[...] = jnp.where(lane_i == 0, w1, jnp.where(lane_i == 1, w2, 0.0))


def _router(h, gain, mods, rw, rb, *, n_rows, n_lat, seq, sh_idx, sc_idx, tm=512):
    d = D_MODEL

    def mod_map(idx):
        return lambda i: (_mod_row(i * tm, n_lat, seq), 0, idx)

    return pl.pallas_call(
        _router_kernel,
        out_shape=(jax.ShapeDtypeStruct((n_rows, LANES), I32),
                   jax.ShapeDtypeStruct((n_rows, LANES), F32)),
        grid=(n_rows // tm,),
        in_specs=[
            pl.BlockSpec((tm, d), lambda i: (i, 0)),
            pl.BlockSpec((1, d), lambda i: (0, 0)),
            pl.BlockSpec((None, 1, d), mod_map(sh_idx)),
            pl.BlockSpec((None, 1, d), mod_map(sc_idx)),
            pl.BlockSpec((d, LANES), lambda i: (0, 0)),
            pl.BlockSpec((1, LANES), lambda i: (0, 0)),
        ],
        out_specs=(pl.BlockSpec((tm, LANES), lambda i: (i, 0)),
                   pl.BlockSpec((tm, LANES), lambda i: (i, 0))),
        compiler_params=_params(("arbitrary",)),
        name="router",
    )(h, gain.reshape(1, d), mods, mods, rw, rb)


def _dispatch_kernel(pos_ref, h_ref, g_ref, sh_ref, sc_ref, xs_hbm, f0_ref, f1_ref, sem):
    i = pl.program_id(0)
    tm = h_ref.shape[0]

    def run(f_ref):
        def wait_block_copies():
            for _ in range(2):
                pltpu.make_async_copy(f_ref, xs_hbm.at[pl.ds(0, tm), :], sem).wait()

        f_ref[...] = _norm_mod(h_ref[...], g_ref[...], sh_ref[...], sc_ref[...])

        @pl.when(i > 0)
        def _():
            wait_block_copies()

        base = i * (2 * tm)
        for r in range(tm):
            for k in range(2):
                dst = pos_ref[base + 2 * r + k]
                pltpu.make_async_copy(f_ref.at[pl.ds(r, 1), :], xs_hbm.at[pl.ds(dst, 1), :], sem).start()

        @pl.when(i == pl.num_programs(0) - 1)
        def _():
            wait_block_copies()

    @pl.when(i % 2 == 0)
    def _():
        run(f0_ref)

    @pl.when(i % 2 == 1)
    def _():
        run(f1_ref)


def _dispatch(h, gain, mods, pos, *, n_rows, n_lat, seq, sh_idx, sc_idx, tm=SLOT_BLOCK_ROWS):
    d = D_MODEL

    def mod_map(idx):
        return lambda i, pos_ref: (_mod_row(i * tm, n_lat, seq), 0, idx)

    grid_spec = pltpu.PrefetchScalarGridSpec(
        num_scalar_prefetch=1,
        grid=(n_rows // tm,),
        in_specs=[
            pl.BlockSpec((tm, d), lambda i, pos_ref: (i, 0)),
            pl.BlockSpec((1, d), lambda i, pos_ref: (0, 0)),
            pl.BlockSpec((None, 1, d), mod_map(sh_idx)),
            pl.BlockSpec((None, 1, d), mod_map(sc_idx)),
        ],
        out_specs=pl.BlockSpec(memory_space=pl.ANY),
        scratch_shapes=[pltpu.VMEM((tm, d), F32), pltpu.VMEM((tm, d), F32), pltpu.SemaphoreType.DMA],
    )
    return pl.pallas_call(
        _dispatch_kernel,
        out_shape=jax.ShapeDtypeStruct((2 * n_rows, d), F32),
        grid_spec=grid_spec,
        compiler_params=_params(("arbitrary",)),
        name="dispatch",
    )(pos, h, gain.reshape(1, d), mods, mods)


def _moe_plan(ids, n_tok):
    n_items = N_EXPERTS + (2 * n_tok) // MOE_ITEM_CAP
    e_flat = ids[:, :2].reshape(-1)
    order = jnp.argsort(e_flat, stable=True).astype(I32)
    pos = jnp.argsort(order).astype(I32)
    pos = jnp.concatenate([pos, jnp.zeros((2 * SLOT_BLOCK_ROWS,), I32)])
    counts = jnp.sum((e_flat[:, None] == jnp.arange(N_EXPERTS, dtype=I32)[None, :]).astype(I32), axis=0)
    starts = jnp.cumsum(counts) - counts
    n_sb = (counts + MOE_ITEM_CAP - 1) // MOE_ITEM_CAP
    cum = jnp.cumsum(n_sb)
    total = cum[-1]
    it = jnp.arange(n_items, dtype=I32)
    e_i = jnp.minimum(jnp.searchsorted(cum, it, side="right").astype(I32), N_EXPERTS - 1)
    sb = it - (cum[e_i] - n_sb[e_i])
    valid = it < total
    last_e = e_i[jnp.maximum(total - 1, 0)]
    item_e = jnp.where(valid, e_i, last_e).astype(I32)
    item_start = jnp.where(valid, starts[e_i] + sb * MOE_ITEM_CAP, 0).astype(I32)
    item_n = jnp.where(valid, jnp.minimum(MOE_ITEM_CAP, counts[e_i] - sb * MOE_ITEM_CAP), 0).astype(I32)
    front, back = jnp.zeros((MOE_ITEM_SHIFT,), I32), jnp.zeros((1,), I32)
    pad_items = lambda a: jnp.concatenate([front, a, back])
    return (pad_items(item_e), pad_items(item_start), pad_items(item_n)), pos


def _moe_kernel(ie_ref, ist_ref, in_ref,
                xs_hbm, wg_ref, wu_ref, wd_ref, ys_hbm,
                xf_ref, xb_ref, yacc_ref, yout_ref, gsem, ssem):
    i, j = pl.program_id(0), pl.program_id(1)
    sh = MOE_ITEM_SHIFT
    n_prev, n, n_next = in_ref[i + sh - 1], in_ref[i + sh], in_ref[i + sh + 1]
    start_prev, start, start_next = ist_ref[i + sh - 1], ist_ref[i + sh], ist_ref[i + sh + 1]

    def rows_in(hrow, vrow, size):
        return pltpu.make_async_copy(xs_hbm.at[pl.ds(hrow, size), :], xf_ref.at[pl.ds(vrow, size), :], gsem)

    def rows_out(hrow, vrow, size):
        return pltpu.make_async_copy(yout_ref.at[pl.ds(vrow, size), :], ys_hbm.at[pl.ds(hrow, size), :], ssem)

    def row_groups(make, base, count, act):
        shift = base % SUBLANES
        head = jnp.minimum((SUBLANES - shift) % SUBLANES, count)
        body = ((count - head) // SUBLANES) * SUBLANES
        for t in range(SUBLANES - 1):
            @pl.when(t < head)
            def _():
                getattr(make(base + t, shift + t, 1), act)()

        for b in reversed(range(SUBLANES.bit_length() - 1, MOE_ITEM_ROWS.bit_length())):
            @pl.when(((body >> b) & 1) == 1)
            def _():
                off = head + ((body >> (b + 1)) << (b + 1))
                getattr(make(pl.multiple_of(base + off, SUBLANES), pl.multiple_of(shift + off, SUBLANES), 1 << b),
                        act)()

        done = head + body
        for t in range(SUBLANES - 1):
            @pl.when(t < count - done)
            def _():
                getattr(make(base + done + t, shift + done + t, 1), act)()

    @pl.when(jnp.logical_and(i == 0, j == 0))
    def _():
        xf_ref[...] = jnp.zeros_like(xf_ref)
        yacc_ref[...] = jnp.zeros_like(yacc_ref)

    def publish(count):
        def body(t, c):
            rows = pl.ds(pl.multiple_of(t * MOE_ROW_PAD, MOE_ROW_PAD), MOE_ROW_PAD)
            yout_ref[rows, :] = yacc_ref[rows, :]
            return c

        lax.fori_loop(0, (count + MOE_ROW_PAD - 1) // MOE_ROW_PAD, body, 0)

    @pl.when(j == 0)
    def _():
        @pl.when(i == 0)
        def _():
            row_groups(rows_in, start, n, "start")

        @pl.when(n > 0)
        def _():
            row_groups(rows_in, start, n, "wait")
            xb_ref[...] = xf_ref[...].astype(BF16)
            row_groups(rows_in, start_next, n_next, "start")

    def compute(m):
        x = xb_ref[0:m, :]
        g = jnp.dot(x, wg_ref[...].astype(BF16), preferred_element_type=F32)
        u = jnp.dot(x, wu_ref[...].astype(BF16), preferred_element_type=F32)
        hmid = (g * _sigmoid(g) * u).astype(BF16)
        y = jnp.dot(hmid, wd_ref[...].astype(BF16), preferred_element_type=F32)
        yacc_ref[0:m, :] = y + jnp.where(j == 0, 0.0, yacc_ref[0:m, :])

    used = jnp.where(n > 0, start % SUBLANES + n, 0)
    n_pad = (used + MOE_ROW_PAD - 1) // MOE_ROW_PAD
    for k in range(1, MOE_ITEM_ROWS // MOE_ROW_PAD + 1):
        @pl.when(n_pad == k)
        def _():
            compute(k * MOE_ROW_PAD)

    @pl.when(j == pl.num_programs(1) - 1)
    def _():
        row_groups(rows_out, start_prev, n_prev, "wait")

        @pl.when(n > 0)
        def _():
            publish(used)
            row_groups(rows_out, start, n, "start")

        @pl.when(i == pl.num_programs(0) - 1)
        def _():
            row_groups(rows_out, start, n, "wait")


def _moe(xs, items, w_gu, w_down, layer):
    item_e, item_start, item_n = items
    d, hid, hk = D_MODEL, EXPERT_HIDDEN, MOE_HIDDEN_BLOCK
    nj = hid // hk
    sh = MOE_ITEM_SHIFT
    n_items = item_e.shape[0] - sh - 1

    def chunk(j, nn, i):
        return jnp.where(nn[i + sh] > 0, j, nj - 1)

    grid_spec = pltpu.PrefetchScalarGridSpec(
        num_scalar_prefetch=3,
        grid=(n_items, nj),
        in_specs=[
            pl.BlockSpec(memory_space=pl.ANY),
            pl.BlockSpec((None, None, d, hk), lambda i, j, ie, ist, nn: (layer, ie[i + sh], 0, chunk(j, nn, i))),
            pl.BlockSpec((None, None, d, hk), lambda i, j, ie, ist, nn: (layer, ie[i + sh], 0, nj + chunk(j, nn, i))),
            pl.BlockSpec((None, None, hk, d), lambda i, j, ie, ist, nn: (layer, ie[i + sh], chunk(j, nn, i), 0)),
        ],
        out_specs=pl.BlockSpec(memory_space=pl.ANY),
        scratch_shapes=[
            pltpu.VMEM((MOE_ITEM_ROWS, d), F32),
            pltpu.VMEM((MOE_ITEM_ROWS, d), BF16),
            pltpu.VMEM((MOE_ITEM_ROWS, d), F32),
            pltpu.VMEM((MOE_ITEM_ROWS, d), F32),
            pltpu.SemaphoreType.DMA,
            pltpu.SemaphoreType.DMA,
        ],
    )
    return pl.pallas_call(
        _moe_kernel,
        out_shape=jax.ShapeDtypeStruct(xs.shape, F32),
        grid_spec=grid_spec,
        compiler_params=_params(("arbitrary", "arbitrary")),
        name="moe_experts",
    )(item_e, item_start, item_n, xs, w_gu, w_gu, w_down)


def _combine_kernel(pos_ref, h_ref, wt_ref, g2_ref, fg_ref, ys_hbm, o_ref, ybuf, sem, *, final):
    i = pl.program_id(0)
    tm = h_ref.shape[0]
    par = i % 2

    def fetch(step, half):
        base = step * (2 * tm)
        for r in range(tm):
            for k in range(2):
                src = pos_ref[base + 2 * r + k]
                pltpu.make_async_copy(ys_hbm.at[pl.ds(src, 1), :], ybuf.at[half, k, pl.ds(r, 1), :],
                                      sem.at[half]).start()

    @pl.when(i == 0)
    def _():
        fetch(0, 0)

    @pl.when(i < pl.num_programs(0) - 1)
    def _():
        fetch(i + 1, 1 - par)

    for k in range(2):
        pltpu.make_async_copy(ys_hbm.at[pl.ds(0, tm), :], ybuf.at[par, k], sem.at[par]).wait()
    wt = wt_ref[...]
    moe = wt[:, 0:1] * ybuf[par, 0] + wt[:, 1:2] * ybuf[par, 1]
    h = h_ref[...] + g2_ref[...] * moe
    if final:
        ms = jnp.mean(h * h, axis=-1, keepdims=True)
        h = h * lax.rsqrt(ms + NORM_EPS) * fg_ref[...]
    o_ref[...] = h


def _combine(h, ys, pos, wts, mods, final_g, *, n_rows, n_lat, seq, gate_idx, final, tm=SLOT_BLOCK_ROWS):
    d = D_MODEL
    kernel = functools.partial(_combine_kernel, final=final)
    grid_spec = pltpu.PrefetchScalarGridSpec(
        num_scalar_prefetch=1,
        grid=(n_rows // tm,),
        in_specs=[
            pl.BlockSpec((tm, d), lambda i, pos_ref: (i, 0)),
            pl.BlockSpec((tm, LANES), lambda i, pos_ref: (i, 0)),
            pl.BlockSpec((None, 1, d), lambda i, pos_ref: (_mod_row(i * tm, n_lat, seq), 0, gate_idx)),
            pl.BlockSpec((1, d), lambda i, pos_ref: (0, 0)),
            pl.BlockSpec(memory_space=pl.ANY),
        ],
        out_specs=pl.BlockSpec((tm, d), lambda i, pos_ref: (i, 0)),
        scratch_shapes=[pltpu.VMEM((2, 2, tm, d), F32), pltpu.SemaphoreType.DMA((2,))],
    )
    return pl.pallas_call(
        kernel,
        out_shape=jax.ShapeDtypeStruct((n_rows, d), F32),
        grid_spec=grid_spec,
        compiler_params=_params(("arbitrary",)),
        name="combine",
    )(pos, h, wts, mods, final_g.reshape(1, d), ys)


def kernel(x, c, ctx, c_ctx, ada_w, ada_b, norm1_g, w_in, b_gate, diff_lambda, subln_g, w_attn_out, sc_conv_w,
           w_sc_out, cf_dw_w, cf_dw_b, cf_ln_g, cf_ln_b, w_cf_out, w_mix, norm2_g, router_g_w, router_g_b,
           router_e_w, router_e_b, exp_w_gu, exp_w_down, final_g):
    batch, seq, d = x.shape
    n_ctx = ctx.shape[1]
    depth = ada_w.shape[0]
    n_lat = batch * seq
    n_all = n_lat + batch * n_ctx
    assert d == D_MODEL and batch == MOD_ROWS // 2 and w_in.shape[2] == C_TOT

    cond = jnp.concatenate([c, c_ctx[None, :], jnp.zeros((MOD_ROWS - batch - 1, d), F32)], axis=0)
    mods_all = _adaln(cond, ada_w, ada_b)
    cos, sin = _rope_tables(seq)
    rows = (x.reshape(n_lat, d), ctx.reshape(batch * n_ctx, d))
    geo = dict(n_lat=n_lat, seq=seq)

    for layer in range(depth):
        last = layer == depth - 1
        lam_init = 0.8 - 0.6 * math.exp(-0.3 * layer)
        mods = mods_all[layer].reshape(MOD_ROWS, 1, 6 * d)
        n_rows = n_lat if last else n_all

        p = _in_proj(rows, norm1_g[layer], mods, w_in, layer, row0=0, n_rows=n_rows, col0=0, n_cols=C_TOT,
                     sh_idx=0, sc_idx=1, tm=1024, **geo)
        if last:
            kv_ctx = _in_proj(rows, norm1_g[layer], mods, w_in, layer, row0=n_lat, n_rows=batch * n_ctx, col0=OFF_K,
                              n_cols=OFF_SC - OFF_K, sh_idx=0, sc_idx=1, tm=n_ctx, **geo)
            kc_blk, vc_blk, ctx_rb0 = 0, QK_W // HEAD_W, 0
        else:
            kv_ctx = p
            kc_blk, vc_blk, ctx_rb0 = OFF_K // HEAD_W, OFF_V // HEAD_W, n_lat // n_ctx
        attn = _attn_lat(p, kv_ctx, kc_blk, vc_blk, ctx_rb0, cos, sin, diff_lambda[layer], subln_g[layer],
                         batch=batch, seq=seq, n_ctx=n_ctx, lam_init=lam_init)
        if not last:
            attn = (attn, _attn_ctx(p, n_lat // n_ctx, diff_lambda[layer], subln_g[layer],
                                    batch=batch, n_ctx=n_ctx, lam_init=lam_init))
        else:
            attn = (attn, attn)
        sc, cf = _convs(p, sc_conv_w[layer], cf_dw_w[layer], cf_dw_b[layer], cf_ln_g[layer], cf_ln_b[layer],
                        n_rows=n_rows, **geo)
        h_mix = _merge(attn, sc, cf, p, b_gate[layer].reshape(1, 3 * d),
                       w_attn_out[layer].astype(BF16), w_sc_out[layer].astype(BF16),
                       w_cf_out[layer].astype(BF16), w_mix[layer].astype(BF16), rows, mods,
                       n_rows=n_rows, gate_idx=2, **geo)

        rw = jnp.concatenate([router_g_w[layer], router_e_w[layer],
                              jnp.zeros((d, LANES - N_GROUPS - N_EXPERTS), F32)], axis=1)
        rb = jnp.concatenate([router_g_b[layer], router_e_b[layer],
                              jnp.zeros((LANES - N_GROUPS - N_EXPERTS,), F32)]).reshape(1, LANES)
        ids, wts = _router(h_mix, norm2_g[layer], mods, rw, rb, n_rows=n_rows, sh_idx=3, sc_idx=4, **geo)
        items, pos = _moe_plan(ids, n_rows)
        xs = _dispatch(h_mix, norm2_g[layer], mods, pos, n_rows=n_rows, sh_idx=3, sc_idx=4, **geo)
        ys = _moe(xs, items, exp_w_gu, exp_w_down, layer)
        h = _combine(h_mix, ys, pos, wts, mods, final_g, n_rows=n_rows, gate_idx=5, final=last, **geo)
        rows = (h, h)

    return h.reshape(batch, seq, d)
```

```python
import functools
import math

import jax
import jax.numpy as jnp
from jax import lax
from jax.experimental import pallas as pl
from jax.experimental.pallas import tpu as pltpu

F32 = jnp.float32
BF16 = jnp.bfloat16
I32 = jnp.int32

D_MODEL = 2048
GRID_W = 64
NORM_EPS = 1e-6
N_HEADS = 8
HEAD_DIM = 64
HEAD_W = 2 * HEAD_DIM
QK_W = N_HEADS * HEAD_W
ATTN_W = N_HEADS * HEAD_W
ROPE_BASE = 10000.0
ROPE_FREQS = HEAD_DIM // 4
SUBLN_EPS = 1e-5
QK_SCALE = HEAD_DIM ** -0.5 * math.log2(math.e)
SC_W = D_MODEL // 4
CF_W = D_MODEL // 4
SC_TAPS = 3
CF_TAPS = 31
CF_LN_EPS = 1e-5
OFF_Q = 0
OFF_K = OFF_Q + QK_W
OFF_V = OFF_K + QK_W
OFF_SC = OFF_V + ATTN_W
OFF_CF = OFF_SC + 3 * SC_W
OFF_GATE = OFF_CF + 2 * CF_W
C_TOT = OFF_GATE + 3 * D_MODEL
N_GROUPS = 4
EXPERTS_PER_GROUP = 8
N_EXPERTS = N_GROUPS * EXPERTS_PER_GROUP
EXPERT_HIDDEN = D_MODEL // 2

LANES = 128
SUBLANES = 8
MOD_ROWS = 8
ATTN_SUB_ROWS = 256
MERGE_GATE_TILES = 4
CONV_HALO = 16
CONV_ROWS = 256
CONV_CHUNK = 32
MOE_ITEM_ROWS = 1024
MOE_ITEM_CAP = MOE_ITEM_ROWS - 8
MOE_ROW_PAD = 64
MOE_HIDDEN_BLOCK = 256
MOE_ITEM_SHIFT = 1
SLOT_BLOCK_ROWS = 256
VMEM_LIMIT = 56 * 1024 * 1024


def _params(sem, vmem=VMEM_LIMIT):
    return pltpu.CompilerParams(dimension_semantics=sem, vmem_limit_bytes=vmem)


def _sigmoid(x):
    return 1.0 / (1.0 + jnp.exp(-x))


def _adaln_kernel(s_ref, w_ref, b_ref, o_ref):
    s = s_ref[...]
    s = s * _sigmoid(s)
    s_hi = s.astype(BF16)
    s_lo = (s - s_hi.astype(F32)).astype(BF16)
    w = w_ref[...]
    w_hi = w.astype(BF16)
    w_lo = (w - w_hi.astype(F32)).astype(BF16)
    lhs = jnp.concatenate([s_hi.astype(F32), s - s_hi.astype(F32)], axis=0).astype(BF16)
    r = jnp.dot(lhs, w_hi, preferred_element_type=F32)
    r2 = jnp.dot(s_hi, w_lo, preferred_element_type=F32)
    o_ref[...] = r[:MOD_ROWS] + r[MOD_ROWS:] + r2 + b_ref[...]


def _adaln(cond, ada_w, ada_b):
    n_layers, d, n = ada_w.shape
    tn = 512
    return pl.pallas_call(
        _adaln_kernel,
        out_shape=jax.ShapeDtypeStruct((n_layers, MOD_ROWS, n), F32),
        grid=(n_layers, n // tn),
        in_specs=[
            pl.BlockSpec((MOD_ROWS, d), lambda l, j: (0, 0)),
            pl.BlockSpec((None, d, tn), lambda l, j: (l, 0, j)),
            pl.BlockSpec((None, 1, tn), lambda l, j: (l, 0, j)),
        ],
        out_specs=pl.BlockSpec((None, MOD_ROWS, tn), lambda l, j: (l, 0, j)),
        compiler_params=_params(("arbitrary", "arbitrary")),
        name="adaln",
    )(cond, ada_w, ada_b.reshape(n_layers, 1, n))


def _mod_row(row0, n_lat, seq):
    return jnp.where(row0 < n_lat, row0 // seq, MOD_ROWS // 2)


def _in_proj_kernel(xa_ref, xb_ref, g_ref, sh_ref, sc_ref, w_ref, o_ref, u_ref, *, a_blocks):
    first_col = pl.program_id(1) == 0
    from_a = pl.program_id(0) < a_blocks

    def normed(x_ref):
        x = x_ref[...]
        ms = jnp.mean(x * x, axis=-1, keepdims=True)
        y = x * lax.rsqrt(ms + NORM_EPS) * g_ref[...]
        u_ref[...] = (y * (1.0 + sc_ref[...]) + sh_ref[...]).astype(BF16)

    @pl.when(jnp.logical_and(first_col, from_a))
    def _():
        normed(xa_ref)

    @pl.when(jnp.logical_and(first_col, jnp.logical_not(from_a)))
    def _():
        normed(xb_ref)

    o_ref[...] = jnp.dot(u_ref[...], w_ref[...].astype(BF16), preferred_element_type=F32).astype(o_ref.dtype)


def _in_proj(rows, gain, mods, w_in, layer, *, row0, n_rows, col0, n_cols, n_lat, seq, sh_idx, sc_idx, tm, tn=512):
    xa, xb = rows
    d = xa.shape[1]
    rb0, cb0 = row0 // tm, col0 // tn
    a_blocks = xa.shape[0] // tm - rb0

    def mod_map(idx):
        return lambda i, j: (_mod_row((i + rb0) * tm, n_lat, seq), 0, idx)

    return pl.pallas_call(
        functools.partial(_in_proj_kernel, a_blocks=a_blocks),
        out_shape=jax.ShapeDtypeStruct((n_rows, n_cols), BF16),
        grid=(n_rows // tm, n_cols // tn),
        in_specs=[
            pl.BlockSpec((tm, d), lambda i, j: (jnp.minimum(i, a_blocks - 1) + rb0, 0)),
            pl.BlockSpec((tm, d), lambda i, j: (jnp.maximum(i - a_blocks, 0), 0)),
            pl.BlockSpec((1, d), lambda i, j: (0, 0)),
            pl.BlockSpec((None, 1, d), mod_map(sh_idx)),
            pl.BlockSpec((None, 1, d), mod_map(sc_idx)),
            pl.BlockSpec((None, d, tn), lambda i, j: (layer, 0, j + cb0)),
        ],
        out_specs=pl.BlockSpec((tm, tn), lambda i, j: (i, j)),
        scratch_shapes=[pltpu.VMEM((tm, d), BF16)],
        compiler_params=_params(("arbitrary", "arbitrary")),
        name="in_proj",
    )(xa, xb, gain.reshape(1, d), mods, mods, w_in)


def _rope_tables(n_tokens):
    t = jnp.arange(n_tokens, dtype=I32)
    pos = jnp.stack([t // GRID_W, t % GRID_W], axis=-1).astype(F32)
    inv_freq = ROPE_BASE ** (-jnp.arange(ROPE_FREQS, dtype=F32) / ROPE_FREQS)
    ang = pos[:, :, None] * inv_freq
    cos, sin = jnp.cos(ang), jnp.sin(ang)
    c = jnp.stack([cos, cos], axis=2).reshape(n_tokens, HEAD_DIM)
    s = jnp.stack([-sin, sin], axis=2).reshape(n_tokens, HEAD_DIM)
    return jnp.tile(c, (1, 2)), jnp.tile(s, (1, 2))


def _rope(x, c, s):
    lane = lax.broadcasted_iota(I32, x.shape, 1)
    first_half = (lane % (2 * ROPE_FREQS)) < ROPE_FREQS
    partner = jnp.where(first_half, pltpu.roll(x, LANES - ROPE_FREQS, 1), pltpu.roll(x, ROPE_FREQS, 1))
    return x * c + partner * s


def _diff_lambda(lam_ref, lam_init):
    lv = lam_ref[...]
    a = jnp.sum(lv[0:1] * lv[1:2], axis=-1, keepdims=True)
    b = jnp.sum(lv[2:3] * lv[3:4], axis=-1, keepdims=True)
    return jnp.exp(a) - jnp.exp(b) + lam_init


def _attend(q, k_all, v_all, lam, subg, lam_init):
    tq = q.shape[0]
    lane = lax.broadcasted_iota(I32, q.shape, 1)
    q0 = jnp.where(lane < HEAD_DIM, q, 0.0).astype(BF16)
    q1 = jnp.where(lane >= HEAD_DIM, q, 0.0).astype(BF16)
    qq = jnp.concatenate([q0, q1], axis=0)
    s = lax.dot_general(qq, k_all, (((1,), (1,)), ((), ())), preferred_element_type=F32)
    m = jnp.max(s, axis=-1, keepdims=True)
    e = jnp.exp2(s - m)
    l = jnp.sum(e, axis=-1, keepdims=True)
    a = e[:tq] - e[tq:] * (lam * l[:tq] / l[tq:])
    o = jnp.dot(a.astype(BF16), v_all, preferred_element_type=F32) / l[:tq]
    ms = jnp.mean(o * o, axis=-1, keepdims=True)
    return o * lax.rsqrt(ms + SUBLN_EPS) * subg * (1.0 - lam_init)


def _attn_lat_kernel(q_ref, kl_ref, vl_ref, kc_ref, vc_ref, cq_ref, sq_ref, ck_ref, sk_ref, lam_ref, g_ref,
                     o_ref, k_all, v_aug, s0_ref, m0_ref, s1_ref, m1_ref, *, n_ctx, lam_init, n_blocks):
    i = pl.program_id(2)
    tq = q_ref.shape[0]

    @pl.when(i == 0)
    def _():
        k_all[0:n_ctx, :] = kc_ref[...]
        k_all[n_ctx:, :] = _rope(kl_ref[...].astype(F32), ck_ref[...], sk_ref[...]).astype(BF16)
        v_aug[0:n_ctx, 0:HEAD_W] = vc_ref[...]
        v_aug[n_ctx:, 0:HEAD_W] = vl_ref[...]
        v_aug[:, HEAD_W:2 * HEAD_W] = jnp.ones((k_all.shape[0], HEAD_W), BF16)

    lam = _diff_lambda(lam_ref, lam_init)

    def stage_a(sa_ref, ma_ref):
        q = _rope(q_ref[...].astype(F32), cq_ref[...], sq_ref[...]) * QK_SCALE
        lane = lax.broadcasted_iota(I32, q.shape, 1)
        q0 = jnp.where(lane < HEAD_DIM, q, 0.0).astype(BF16)
        q1 = jnp.where(lane >= HEAD_DIM, q, 0.0).astype(BF16)
        qq = jnp.concatenate([q0, q1], axis=0)
        s = lax.dot_general(qq, k_all[...], (((1,), (1,)), ((), ())), preferred_element_type=F32)
        sa_ref[...] = s
        ma_ref[...] = jnp.broadcast_to(jnp.max(s, axis=-1, keepdims=True), ma_ref.shape)

    def stage_b(sb_ref, mb_ref):
        mb = jnp.concatenate([mb_ref[...]] * (sb_ref.shape[1] // LANES), axis=1)
        e = jnp.exp2(sb_ref[...] - mb).astype(BF16)
        oa = jnp.dot(e, v_aug[...], preferred_element_type=F32)
        o = oa[:tq, 0:HEAD_W] / oa[:tq, HEAD_W:] - oa[tq:, 0:HEAD_W] * (lam / oa[tq:, HEAD_W:])
        ms = jnp.mean(o * o, axis=-1, keepdims=True)
        o_ref[...] = (o * lax.rsqrt(ms + SUBLN_EPS) * g_ref[...] * (1.0 - lam_init)).astype(o_ref.dtype)

    bufs = ((s0_ref, m0_ref), (s1_ref, m1_ref))

    @pl.when(i == 0)
    def _():
        stage_a(*bufs[0])

    for par in range(2):
        @pl.when(jnp.logical_and(jnp.logical_and(i > 0, i < n_blocks), i % 2 == par))
        def _():
            stage_a(*bufs[par])
            stage_b(*bufs[1 - par])

    @pl.when(i == n_blocks)
    def _():
        stage_b(*bufs[(n_blocks - 1) % 2])


def _attn_ctx_kernel(q_ref, k_ref, v_ref, lam_ref, g_ref, o_ref, *, lam_init):
    q = q_ref[...].astype(F32) * QK_SCALE
    lam = _diff_lambda(lam_ref, lam_init)
    o_ref[...] = _attend(q, k_ref[...], v_ref[...], lam, g_ref[...], lam_init).astype(o_ref.dtype)


def _attn_lat(p, kv_ctx, kc_blk, vc_blk, ctx_rb0, cos, sin, lam4, subg, *, batch, seq, n_ctx, lam_init, tq=512):
    nq = seq // tq
    n_keys = n_ctx + seq
    hb = lambda off: off // HEAD_W
    qblk = lambda i: jnp.minimum(i, nq - 1)
    kernel = functools.partial(_attn_lat_kernel, n_ctx=n_ctx, lam_init=lam_init, n_blocks=nq)
    return pl.pallas_call(
        kernel,
        out_shape=jax.ShapeDtypeStruct((batch * seq, ATTN_W), BF16),
        grid=(batch, N_HEADS, nq + 1),
        in_specs=[
            pl.BlockSpec((tq, HEAD_W), lambda b, h, i: (b * nq + qblk(i), hb(OFF_Q) + h)),
            pl.BlockSpec((seq, HEAD_W), lambda b, h, i: (b, hb(OFF_K) + h)),
            pl.BlockSpec((seq, HEAD_W), lambda b, h, i: (b, hb(OFF_V) + h)),
            pl.BlockSpec((n_ctx, HEAD_W), lambda b, h, i: (ctx_rb0 + b, kc_blk + h)),
            pl.BlockSpec((n_ctx, HEAD_W), lambda b, h, i: (ctx_rb0 + b, vc_blk + h)),
            pl.BlockSpec((tq, HEAD_W), lambda b, h, i: (qblk(i), 0)),
            pl.BlockSpec((tq, HEAD_W), lambda b, h, i: (qblk(i), 0)),
            pl.BlockSpec((seq, HEAD_W), lambda b, h, i: (0, 0)),
            pl.BlockSpec((seq, HEAD_W), lambda b, h, i: (0, 0)),
            pl.BlockSpec((4, HEAD_DIM), lambda b, h, i: (0, 0)),
            pl.BlockSpec((1, HEAD_W), lambda b, h, i: (0, 0)),
        ],
        out_specs=pl.BlockSpec((tq, HEAD_W), lambda b, h, i: (b * nq + jnp.maximum(i - 1, 0), h)),
        scratch_shapes=[pltpu.VMEM((n_keys, HEAD_W), BF16), pltpu.VMEM((n_keys, 2 * HEAD_W), BF16),
                        pltpu.VMEM((2 * tq, n_keys), F32), pltpu.VMEM((2 * tq, LANES), F32),
                        pltpu.VMEM((2 * tq, n_keys), F32), pltpu.VMEM((2 * tq, LANES), F32)],
        compiler_params=_params(("arbitrary", "arbitrary", "arbitrary")),
        name="attn_lat",
    )(p, p, p, kv_ctx, kv_ctx, cos, sin, cos, sin, lam4, subg.reshape(1, HEAD_W))


def _attn_ctx(p, ctx_rb0, lam4, subg, *, batch, n_ctx, lam_init):
    hb = lambda off: off // HEAD_W
    kernel = functools.partial(_attn_ctx_kernel, lam_init=lam_init)
    return pl.pallas_call(
        kernel,
        out_shape=jax.ShapeDtypeStruct((batch * n_ctx, ATTN_W), BF16),
        grid=(batch, N_HEADS),
        in_specs=[
            pl.BlockSpec((n_ctx, HEAD_W), lambda b, h: (ctx_rb0 + b, hb(OFF_Q) + h)),
            pl.BlockSpec((n_ctx, HEAD_W), lambda b, h: (ctx_rb0 + b, hb(OFF_K) + h)),
            pl.BlockSpec((n_ctx, HEAD_W), lambda b, h: (ctx_rb0 + b, hb(OFF_V) + h)),
            pl.BlockSpec((4, HEAD_DIM), lambda b, h: (0, 0)),
            pl.BlockSpec((1, HEAD_W), lambda b, h: (0, 0)),
        ],
        out_specs=pl.BlockSpec((n_ctx, HEAD_W), lambda b, h: (b, h)),
        compiler_params=_params(("arbitrary", "arbitrary")),
        name="attn_ctx",
    )(p, p, p, lam4, subg.reshape(1, HEAD_W))


def _conv_kernel(bg_ref, cg_ref, xi_ref, a_ref, g_ref,
                 cg_p, xi_p, a_p, g_p, cg_n, xi_n, a_n, g_n,
                 scw_ref, cfw_ref, cfb_ref, lng_ref, lnb_ref,
                 sco_ref, cfo_ref, pad_ref, *, lat_blocks, blocks_per_seq):
    i = pl.program_id(0)
    in_lat = i < lat_blocks
    pos = i % blocks_per_seq
    has_prev = jnp.logical_and(in_lat, pos != 0)
    has_next = jnp.logical_and(in_lat, pos != blocks_per_seq - 1)
    keep_prev = jnp.where(has_prev, 1.0, 0.0)
    keep_next = jnp.where(has_next, 1.0, 0.0)
    lo, hi = CONV_HALO, CONV_HALO + CONV_ROWS

    def fill(main, prev, nxt):
        pad_ref[0:lo, :] = prev * keep_prev
        pad_ref[lo:hi, :] = main
        pad_ref[hi:hi + CONV_HALO, :] = nxt * keep_next

    def f(ref):
        return ref[...].astype(F32)

    def glu(a, g):
        return a * _sigmoid(g)

    fill(f(cg_ref) * f(xi_ref), f(cg_p) * f(xi_p), f(cg_n) * f(xi_n))
    for c in range(CONV_ROWS // CONV_CHUNK):
        r0 = lo + c * CONV_CHUNK - SC_TAPS // 2
        acc = scw_ref[0:1, :] * pad_ref[r0:r0 + CONV_CHUNK, :]
        for k in range(1, SC_TAPS):
            acc = acc + scw_ref[k:k + 1, :] * pad_ref[r0 + k:r0 + k + CONV_CHUNK, :]
        rows = slice(c * CONV_CHUNK, (c + 1) * CONV_CHUNK)
        sco_ref[rows, :] = (bg_ref[rows, :].astype(F32) * acc).astype(sco_ref.dtype)

    fill(glu(f(a_ref), f(g_ref)), glu(f(a_p), f(g_p)), glu(f(a_n), f(g_n)))
    off = lo - CF_TAPS // 2
    win = CONV_CHUNK + SUBLANES
    for c in range(CONV_ROWS // CONV_CHUNK):
        base = c * CONV_CHUNK
        acc = None
        for b in range(SUBLANES):
            q = None
            for k in range(b, CF_TAPS, SUBLANES):
                term = cfw_ref[k:k + 1, :] * pad_ref[base + k - b:base + k - b + win, :]
                q = term if q is None else q + term
            part = q[off + b:off + b + CONV_CHUNK, :]
            acc = part if acc is None else acc + part
        z = acc + cfb_ref[...]
        mu = jnp.mean(z, axis=-1, keepdims=True)
        zc = z - mu
        var = jnp.mean(zc * zc, axis=-1, keepdims=True)
        y = zc * lax.rsqrt(var + CF_LN_EPS) * lng_ref[...] + lnb_ref[...]
        rows = slice(c * CONV_CHUNK, (c + 1) * CONV_CHUNK)
        cfo_ref[rows, :] = (y * _sigmoid(y)).astype(cfo_ref.dtype)


def _convs(p, sc_w, cf_w, cf_b, ln_g, ln_b, *, n_rows, n_lat, seq):
    nb = n_rows // CONV_ROWS
    halo_per_block = CONV_ROWS // CONV_HALO
    last_halo = n_rows // CONV_HALO - 1
    cb = lambda off: off // SC_W

    def main(off):
        return pl.BlockSpec((CONV_ROWS, SC_W), lambda i: (i, cb(off)))

    def prev(off):
        return pl.BlockSpec((CONV_HALO, SC_W), lambda i: (jnp.maximum(i * halo_per_block - 1, 0), cb(off)))

    def nxt(off):
        return pl.BlockSpec((CONV_HALO, SC_W), lambda i: (jnp.minimum((i + 1) * halo_per_block, last_halo), cb(off)))

    def vec(rows):
        return pl.BlockSpec((rows, SC_W), lambda i: (0, 0))

    o_bg, o_cg, o_xi, o_a, o_g = OFF_SC, OFF_SC + SC_W, OFF_SC + 2 * SC_W, OFF_CF, OFF_CF + CF_W
    kernel = functools.partial(_conv_kernel, lat_blocks=n_lat // CONV_ROWS, blocks_per_seq=seq // CONV_ROWS)
    return pl.pallas_call(
        kernel,
        out_shape=(jax.ShapeDtypeStruct((n_rows, SC_W), BF16), jax.ShapeDtypeStruct((n_rows, CF_W), BF16)),
        grid=(nb,),
        in_specs=[main(o_bg), main(o_cg), main(o_xi), main(o_a), main(o_g),
                  prev(o_cg), prev(o_xi), prev(o_a), prev(o_g),
                  nxt(o_cg), nxt(o_xi), nxt(o_a), nxt(o_g),
                  vec(SC_TAPS), vec(CF_TAPS), vec(1), vec(1), vec(1)],
        out_specs=(pl.BlockSpec((CONV_ROWS, SC_W), lambda i: (i, 0)),
                   pl.BlockSpec((CONV_ROWS, CF_W), lambda i: (i, 0))),
        scratch_shapes=[pltpu.VMEM((CONV_ROWS + 2 * CONV_HALO, SC_W), F32)],
        compiler_params=_params(("arbitrary",)),
        name="convs",
    )(p, p, p, p, p, p, p, p, p, p, p, p, p,
      sc_w, cf_w, cf_b.reshape(1, CF_W), ln_g.reshape(1, CF_W), ln_b.reshape(1, CF_W))


def _merge_kernel(*refs, a_blocks):
    n_gate = 3 * MERGE_GATE_TILES
    ata_ref, atb_ref, sc_ref, cf_ref = refs[:4]
    gate_refs = refs[4:4 + n_gate]
    bg_ref, wa_ref, wb_ref, wc_ref, wm_ref, ha_ref, hb_ref, g1_ref, o_ref = refs[4 + n_gate:]
    d = D_MODEL
    from_a = pl.program_id(0) < a_blocks
    attn = jnp.where(from_a, ata_ref[...], atb_ref[...])
    h = jnp.where(from_a, ha_ref[...], hb_ref[...])

    def gate(branch):
        tiles = gate_refs[branch * MERGE_GATE_TILES:(branch + 1) * MERGE_GATE_TILES]
        pre = jnp.concatenate([t[...] for t in tiles], axis=1).astype(F32)
        return _sigmoid(pre + bg_ref[:, branch * d:(branch + 1) * d])

    ya = jnp.dot(attn, wa_ref[...], preferred_element_type=F32)
    yb = jnp.dot(sc_ref[...], wb_ref[...], preferred_element_type=F32)
    yc = jnp.dot(cf_ref[...], wc_ref[...], preferred_element_type=F32)
    m = (gate(0) * ya + gate(1) * yb + gate(2) * yc).astype(BF16)
    o_ref[...] = h + g1_ref[...] * jnp.dot(m, wm_ref[...], preferred_element_type=F32)


def _merge(attn, sc, cf, p, b_gate, wa, wb, wc, wm, rows, mods, *, n_rows, n_lat, seq, gate_idx, tm=256):
    d = D_MODEL
    (ata, atb), (ha, hb) = attn, rows
    a_blocks = ha.shape[0] // tm
    first = lambda w: pl.BlockSpec((tm, w), lambda i: (jnp.minimum(i, a_blocks - 1), 0))
    rest = lambda w: pl.BlockSpec((tm, w), lambda i: (jnp.maximum(i - a_blocks, 0), 0))
    tg = d // MERGE_GATE_TILES
    gb0 = OFF_GATE // tg
    gates = [pl.BlockSpec((tm, tg), functools.partial(lambda i, c: (i, c), c=gb0 + t))
             for t in range(3 * MERGE_GATE_TILES)]
    whole = lambda a: pl.BlockSpec(a.shape, lambda i: (0, 0))
    return pl.pallas_call(
        functools.partial(_merge_kernel, a_blocks=a_blocks),
        out_shape=jax.ShapeDtypeStruct((n_rows, d), F32),
        grid=(n_rows // tm,),
        in_specs=[
            first(ATTN_W), rest(ATTN_W),
            pl.BlockSpec((tm, SC_W), lambda i: (i, 0)),
            pl.BlockSpec((tm, CF_W), lambda i: (i, 0)),
            *gates,
            whole(b_gate), whole(wa), whole(wb), whole(wc), whole(wm),
            first(d), rest(d),
            pl.BlockSpec((None, 1, d), lambda i: (_mod_row(i * tm, n_lat, seq), 0, gate_idx)),
        ],
        out_specs=pl.BlockSpec((tm, d), lambda i: (i, 0)),
        compiler_params=_params(("arbitrary",)),
        name="merge",
    )(ata, atb, sc, cf, *([p] * (3 * MERGE_GATE_TILES)), b_gate, wa, wb, wc, wm, ha, hb, mods)


def _split2(x):
    hi = x.astype(BF16)
    return hi, (x - hi.astype(F32)).astype(BF16)


def _norm_mod(x, g, sh, sc):
    ms = jnp.mean(x * x, axis=-1, keepdims=True)
    return x * lax.rsqrt(ms + NORM_EPS) * g * (1.0 + sc) + sh


def _router_kernel(h_ref, g_ref, sh_ref, sc_ref, rw_ref, rb_ref, id_ref, wt_ref):
    f = _norm_mod(h_ref[...], g_ref[...], sh_ref[...], sc_ref[...])

    f_hi, f_lo = _split2(f)
    w_hi, w_lo = _split2(rw_ref[...])
    dot = lambda a, b: jnp.dot(a, b, preferred_element_type=F32)
    logits = (dot(f_lo, w_hi) + dot(f_hi, w_lo) + dot(f_hi, w_hi)) + rb_ref[...]

    lane_i = lax.broadcasted_iota(I32, logits.shape, 1)
    lane = lane_i.astype(F32)
    neg = -jnp.inf
    big = float(LANES)
    lg = jnp.where(lane_i < N_GROUPS, logits, neg)
    mg = jnp.max(lg, axis=-1, keepdims=True)
    grp = jnp.min(jnp.where(lg == mg, lane, big), axis=-1, keepdims=True)
    p_grp = 1.0 / jnp.sum(jnp.exp(lg - mg), axis=-1, keepdims=True)

    e_lane = lane - N_GROUPS
    in_grp = jnp.logical_and(e_lane >= grp * EXPERTS_PER_GROUP, e_lane < (grp + 1.0) * EXPERTS_PER_GROUP)
    le = jnp.where(in_grp, logits, neg)
    m1 = jnp.max(le, axis=-1, keepdims=True)
    i1 = jnp.min(jnp.where(le == m1, lane, big), axis=-1, keepdims=True)
    le2 = jnp.where(lane == i1, neg, le)
    m2 = jnp.max(le2, axis=-1, keepdims=True)
    i2 = jnp.min(jnp.where(le2 == m2, lane, big), axis=-1, keepdims=True)
    e2 = jnp.exp(m2 - m1)
    w1 = p_grp / (1.0 + e2)
    w2 = p_grp * e2 / (1.0 + e2)
    ids = jnp.where(lane_i == 0, i1 - N_GROUPS, jnp.where(lane_i == 1, i2 - N_GROUPS, 0.0))
    id_ref[...] = ids.astype(I32)
    wt_ref[...] = jnp.where(lane_i == 0, w1, jnp.where(lane_i == 1, w2, 0.0))


def _router(h, gain, mods, rw, rb, *, n_rows, n_lat, seq, sh_idx, sc_idx, tm=512):
    d = D_MODEL

    def mod_map(idx):
        return lambda i: (_mod_row(i * tm, n_lat, seq), 0, idx)

    return pl.pallas_call(
        _router_kernel,
        out_shape=(jax.ShapeDtypeStruct((n_rows, LANES), I32),
                   jax.ShapeDtypeStruct((n_rows, LANES), F32)),
        grid=(n_rows // tm,),
        in_specs=[
            pl.BlockSpec((tm, d), lambda i: (i, 0)),
            pl.BlockSpec((1, d), lambda i: (0, 0)),
            pl.BlockSpec((None, 1, d), mod_map(sh_idx)),
            pl.BlockSpec((None, 1, d), mod_map(sc_idx)),
            pl.BlockSpec((d, LANES), lambda i: (0, 0)),
            pl.BlockSpec((1, LANES), lambda i: (0, 0)),
        ],
        out_specs=(pl.BlockSpec((tm, LANES), lambda i: (i, 0)),
                   pl.BlockSpec((tm, LANES), lambda i: (i, 0))),
        compiler_params=_params(("arbitrary",)),
        name="router",
    )(h, gain.reshape(1, d), mods, mods, rw, rb)


def _dispatch_kernel(pos_ref, h_ref, g_ref, sh_ref, sc_ref, xs_hbm, f0_ref, f1_ref, sem):
    i = pl.program_id(0)
    tm = h_ref.shape[0]

    def run(f_ref):
        def wait_block_copies():
            for _ in range(2):
                pltpu.make_async_copy(f_ref, xs_hbm.at[pl.ds(0, tm), :], sem).wait()

        f_ref[...] = _norm_mod(h_ref[...], g_ref[...], sh_ref[...], sc_ref[...])

        @pl.when(i > 0)
        def _():
            wait_block_copies()

        base = i * (2 * tm)
        for r in range(tm):
            for k in range(2):
                dst = pos_ref[base + 2 * r + k]
                pltpu.make_async_copy(f_ref.at[pl.ds(r, 1), :], xs_hbm.at[pl.ds(dst, 1), :], sem).start()

        @pl.when(i == pl.num_programs(0) - 1)
        def _():
            wait_block_copies()

    @pl.when(i % 2 == 0)
    def _():
        run(f0_ref)

    @pl.when(i % 2 == 1)
    def _():
        run(f1_ref)


def _dispatch(h, gain, mods, pos, *, n_rows, n_lat, seq, sh_idx, sc_idx, tm=SLOT_BLOCK_ROWS):
    d = D_MODEL

    def mod_map(idx):
        return lambda i, pos_ref: (_mod_row(i * tm, n_lat, seq), 0, idx)

    grid_spec = pltpu.PrefetchScalarGridSpec(
        num_scalar_prefetch=1,
        grid=(n_rows // tm,),
        in_specs=[
            pl.BlockSpec((tm, d), lambda i, pos_ref: (i, 0)),
            pl.BlockSpec((1, d), lambda i, pos_ref: (0, 0)),
            pl.BlockSpec((None, 1, d), mod_map(sh_idx)),
            pl.BlockSpec((None, 1, d), mod_map(sc_idx)),
        ],
        out_specs=pl.BlockSpec(memory_space=pl.ANY),
        scratch_shapes=[pltpu.VMEM((tm, d), F32), pltpu.VMEM((tm, d), F32), pltpu.SemaphoreType.DMA],
    )
    return pl.pallas_call(
        _dispatch_kernel,
        out_shape=jax.ShapeDtypeStruct((2 * n_rows, d), F32),
        grid_spec=grid_spec,
        compiler_params=_params(("arbitrary",)),
        name="dispatch",
    )(pos, h, gain.reshape(1, d), mods, mods)


def _moe_plan(ids, n_tok):
    n_items = N_EXPERTS + (2 * n_tok) // MOE_ITEM_CAP
    e_flat = ids[:, :2].reshape(-1)
    order = jnp.argsort(e_flat, stable=True).astype(I32)
    pos = jnp.argsort(order).astype(I32)
    pos = jnp.concatenate([pos, jnp.zeros((2 * SLOT_BLOCK_ROWS,), I32)])
    counts = jnp.sum((e_flat[:, None] == jnp.arange(N_EXPERTS, dtype=I32)[None, :]).astype(I32), axis=0)
    starts = jnp.cumsum(counts) - counts
    n_sb = (counts + MOE_ITEM_CAP - 1) // MOE_ITEM_CAP
    cum = jnp.cumsum(n_sb)
    total = cum[-1]
    it = jnp.arange(n_items, dtype=I32)
    e_i = jnp.minimum(jnp.searchsorted(cum, it, side="right").astype(I32), N_EXPERTS - 1)
    sb = it - (cum[e_i] - n_sb[e_i])
    valid = it < total
    last_e = e_i[jnp.maximum(total - 1, 0)]
    item_e = jnp.where(valid, e_i, last_e).astype(I32)
    item_start = jnp.where(valid, starts[e_i] + sb * MOE_ITEM_CAP, 0).astype(I32)
    item_n = jnp.where(valid, jnp.minimum(MOE_ITEM_CAP, counts[e_i] - sb * MOE_ITEM_CAP), 0).astype(I32)
    front, back = jnp.zeros((MOE_ITEM_SHIFT,), I32), jnp.zeros((1,), I32)
    pad_items = lambda a: jnp.concatenate([front, a, back])
    return (pad_items(item_e), pad_items(item_start), pad_items(item_n)), pos


def _moe_kernel(ie_ref, ist_ref, in_ref,
                xs_hbm, wg_ref, wu_ref, wd_ref, ys_hbm,
                xf_ref, xb_ref, y_ref, gsem, ssem):
    i, j = pl.program_id(0), pl.program_id(1)
    sh = MOE_ITEM_SHIFT
    n_prev, n, n_next = in_ref[i + sh - 1], in_ref[i + sh], in_ref[i + sh + 1]
    start_prev, start, start_next = ist_ref[i + sh - 1], ist_ref[i + sh], ist_ref[i + sh + 1]
    par = i % 2

    def rows_in(hrow, vrow, size):
        return pltpu.make_async_copy(xs_hbm.at[pl.ds(hrow, size), :], xf_ref.at[pl.ds(vrow, size), :], gsem)

    def rows_out(hrow, vrow, size):
        return pltpu.make_async_copy(y_ref.at[par, pl.ds(vrow, size), :], ys_hbm.at[pl.ds(hrow, size), :], ssem)

    def row_groups(make, base, count, act):
        shift = base % SUBLANES
        head = jnp.minimum((SUBLANES - shift) % SUBLANES, count)
        body = ((count - head) // SUBLANES) * SUBLANES
        for t in range(SUBLANES - 1):
            @pl.when(t < head)
            def _():
                getattr(make(base + t, shift + t, 1), act)()

        for b in reversed(range(SUBLANES.bit_length() - 1, MOE_ITEM_ROWS.bit_length())):
            @pl.when(((body >> b) & 1) == 1)
            def _():
                off = head + ((body >> (b + 1)) << (b + 1))
                getattr(make(pl.multiple_of(base + off, SUBLANES), pl.multiple_of(shift + off, SUBLANES), 1 << b),
                        act)()

        done = head + body
        for t in range(SUBLANES - 1):
            @pl.when(t < count - done)
            def _():
                getattr(make(base + done + t, shift + done + t, 1), act)()

    @pl.when(jnp.logical_and(i == 0, j == 0))
    def _():
        xf_ref[...] = jnp.zeros_like(xf_ref)
        y_ref[...] = jnp.zeros_like(y_ref)

    @pl.when(j == 0)
    def _():
        @pl.when(i == 0)
        def _():
            row_groups(rows_in, start, n, "start")

        @pl.when(n > 0)
        def _():
            row_groups(rows_in, start, n, "wait")
            xb_ref[...] = xf_ref[...].astype(BF16)
            row_groups(rows_in, start_next, n_next, "start")

    def compute(m):
        x = xb_ref[0:m, :]
        g = jnp.dot(x, wg_ref[...].astype(BF16), preferred_element_type=F32)
        u = jnp.dot(x, wu_ref[...].astype(BF16), preferred_element_type=F32)
        hmid = (g * _sigmoid(g) * u).astype(BF16)
        y = jnp.dot(hmid, wd_ref[...].astype(BF16), preferred_element_type=F32)
        acc = y_ref.at[par]
        acc[0:m, :] = y + jnp.where(j == 0, 0.0, acc[0:m, :])

    used = jnp.where(n > 0, start % SUBLANES + n, 0)
    n_pad = (used + MOE_ROW_PAD - 1) // MOE_ROW_PAD
    for k in range(1, MOE_ITEM_ROWS // MOE_ROW_PAD + 1):
        @pl.when(n_pad == k)
        def _():
            compute(k * MOE_ROW_PAD)

    @pl.when(j == pl.num_programs(1) - 1)
    def _():
        row_groups(rows_out, start_prev, n_prev, "wait")

        row_groups(rows_out, start, n, "start")

        @pl.when(i == pl.num_programs(0) - 1)
        def _():
            row_groups(rows_out, start, n, "wait")


def _moe(xs, items, w_gu, w_down, layer):
    item_e, item_start, item_n = items
    d, hid, hk = D_MODEL, EXPERT_HIDDEN, MOE_HIDDEN_BLOCK
    nj = hid // hk
    sh = MOE_ITEM_SHIFT
    n_items = item_e.shape[0] - sh - 1

    def chunk(j, nn, i):
        return jnp.where(nn[i + sh] > 0, j, nj - 1)

    grid_spec = pltpu.PrefetchScalarGridSpec(
        num_scalar_prefetch=3,
        grid=(n_items, nj),
        in_specs=[
            pl.BlockSpec(memory_space=pl.ANY),
            pl.BlockSpec((None, None, d, hk), lambda i, j, ie, ist, nn: (layer, ie[i + sh], 0, chunk(j, nn, i))),
            pl.BlockSpec((None, None, d, hk), lambda i, j, ie, ist, nn: (layer, ie[i + sh], 0, nj + chunk(j, nn, i))),
            pl.BlockSpec((None, None, hk, d), lambda i, j, ie, ist, nn: (layer, ie[i + sh], chunk(j, nn, i), 0)),
        ],
        out_specs=pl.BlockSpec(memory_space=pl.ANY),
        scratch_shapes=[
            pltpu.VMEM((MOE_ITEM_ROWS, d), F32),
            pltpu.VMEM((MOE_ITEM_ROWS, d), BF16),
            pltpu.VMEM((2, MOE_ITEM_ROWS, d), F32),
            pltpu.SemaphoreType.DMA,
            pltpu.SemaphoreType.DMA,
        ],
    )
    return pl.pallas_call(
        _moe_kernel,
        out_shape=jax.ShapeDtypeStruct(xs.shape, F32),
        grid_spec=grid_spec,
        compiler_params=_params(("arbitrary", "arbitrary")),
        name="moe_experts",
    )(item_e, item_start, item_n, xs, w_gu, w_gu, w_down)


def _combine_kernel(pos_ref, h_ref, wt_ref, g2_ref, fg_ref, ys_hbm, o_ref, ybuf, sem, *, final):
    i = pl.program_id(0)
    tm = h_ref.shape[0]
    par = i % 2

    def fetch(step, half):
        base = step * (2 * tm)
        for r in range(tm):
            for k in range(2):
                src = pos_ref[base + 2 * r + k]
                pltpu.make_async_copy(ys_hbm.at[pl.ds(src, 1), :], ybuf.at[half, k, pl.ds(r, 1), :],
                                      sem.at[half]).start()

    @pl.when(i == 0)
    def _():
        fetch(0, 0)

    @pl.when(i < pl.num_programs(0) - 1)
    def _():
        fetch(i + 1, 1 - par)

    for k in range(2):
        pltpu.make_async_copy(ys_hbm.at[pl.ds(0, tm), :], ybuf.at[par, k], sem.at[par]).wait()
    wt = wt_ref[...]
    moe = wt[:, 0:1] * ybuf[par, 0] + wt[:, 1:2] * ybuf[par, 1]
    h = h_ref[...] + g2_ref[...] * moe
    if final:
        ms = jnp.mean(h * h, axis=-1, keepdims=True)
        h = h * lax.rsqrt(ms + NORM_EPS) * fg_ref[...]
    o_ref[...] = h


def _combine(h, ys, pos, wts, mods, final_g, *, n_rows, n_lat, seq, gate_idx, final, tm=SLOT_BLOCK_ROWS):
    d = D_MODEL
    kernel = functools.partial(_combine_kernel, final=final)
    grid_spec = pltpu.PrefetchScalarGridSpec(
        num_scalar_prefetch=1,
        grid=(n_rows // tm,),
        in_specs=[
            pl.BlockSpec((tm, d), lambda i, pos_ref: (i, 0)),
            pl.BlockSpec((tm, LANES), lambda i, pos_ref: (i, 0)),
            pl.BlockSpec((None, 1, d), lambda i, pos_ref: (_mod_row(i * tm, n_lat, seq), 0, gate_idx)),
            pl.BlockSpec((1, d), lambda i, pos_ref: (0, 0)),
            pl.BlockSpec(memory_space=pl.ANY),
        ],
        out_specs=pl.BlockSpec((tm, d), lambda i, pos_ref: (i, 0)),
        scratch_shapes=[pltpu.VMEM((2, 2, tm, d), F32), pltpu.SemaphoreType.DMA((2,))],
    )
    return pl.pallas_call(
        kernel,
        out_shape=jax.ShapeDtypeStruct((n_rows, d), F32),
        grid_spec=grid_spec,
        compiler_params=_params(("arbitrary",)),
        name="combine",
    )(pos, h, wts, mods, final_g.reshape(1, d), ys)


def kernel(x, c, ctx, c_ctx, ada_w, ada_b, norm1_g, w_in, b_gate, diff_lambda, subln_g, w_attn_out, sc_conv_w,
           w_sc_out, cf_dw_w, cf_dw_b, cf_ln_g, cf_ln_b, w_cf_out, w_mix, norm2_g, router_g_w, router_g_b,
           router_e_w, router_e_b, exp_w_gu, exp_w_down, final_g):
    batch, seq, d = x.shape
    n_ctx = ctx.shape[1]
    depth = ada_w.shape[0]
    n_lat = batch * seq
    n_all = n_lat + batch * n_ctx
    assert d == D_MODEL and batch == MOD_ROWS // 2 and w_in.shape[2] == C_TOT

    cond = jnp.concatenate([c, c_ctx[None, :], jnp.zeros((MOD_ROWS - batch - 1, d), F32)], axis=0)
    mods_all = _adaln(cond, ada_w, ada_b)
    cos, sin = _rope_tables(seq)
    rows = (x.reshape(n_lat, d), ctx.reshape(batch * n_ctx, d))
    geo = dict(n_lat=n_lat, seq=seq)

    for layer in range(depth):
        last = layer == depth - 1
        lam_init = 0.8 - 0.6 * math.exp(-0.3 * layer)
        mods = mods_all[layer].reshape(MOD_ROWS, 1, 6 * d)
        n_rows = n_lat if last else n_all

        p = _in_proj(rows, norm1_g[layer], mods, w_in, layer, row0=0, n_rows=n_rows, col0=0, n_cols=C_TOT,
                     sh_idx=0, sc_idx=1, tm=1024, **geo)
        if last:
            kv_ctx = _in_proj(rows, norm1_g[layer], mods, w_in, layer, row0=n_lat, n_rows=batch * n_ctx, col0=OFF_K,
                              n_cols=OFF_SC - OFF_K, sh_idx=0, sc_idx=1, tm=n_ctx, **geo)
            kc_blk, vc_blk, ctx_rb0 = 0, QK_W // HEAD_W, 0
        else:
            kv_ctx = p
            kc_blk, vc_blk, ctx_rb0 = OFF_K // HEAD_W, OFF_V // HEAD_W, n_lat // n_ctx
        attn = _attn_lat(p, kv_ctx, kc_blk, vc_blk, ctx_rb0, cos, sin, diff_lambda[layer], subln_g[layer],
                         batch=batch, seq=seq, n_ctx=n_ctx, lam_init=lam_init)
        if not last:
            attn = (attn, _attn_ctx(p, n_lat // n_ctx, diff_lambda[layer], subln_g[layer],
                                    batch=batch, n_ctx=n_ctx, lam_init=lam_init))
        else:
            attn = (attn, attn)
        sc, cf = _convs(p, sc_conv_w[layer], cf_dw_w[layer], cf_dw_b[layer], cf_ln_g[layer], cf_ln_b[layer],
                        n_rows=n_rows, **geo)
        h_mix = _merge(attn, sc, cf, p, b_gate[layer].reshape(1, 3 * d),
                       w_attn_out[layer].astype(BF16), w_sc_out[layer].astype(BF16),
                       w_cf_out[layer].astype(BF16), w_mix[layer].astype(BF16), rows, mods,
                       n_rows=n_rows, gate_idx=2, **geo)

        rw = jnp.concatenate([router_g_w[layer], router_e_w[layer],
                              jnp.zeros((d, LANES - N_GROUPS - N_EXPERTS), F32)], axis=1)
        rb = jnp.concatenate([router_g_b[layer], router_e_b[layer],
                              jnp.zeros((LANES - N_GROUPS - N_EXPERTS,), F32)]).reshape(1, LANES)
        ids, wts = _router(h_mix, norm2_g[layer], mods, rw, rb, n_rows=n_rows, sh_idx=3, sc_idx=4, **geo)
        items, pos = _moe_plan(ids, n_rows)
        xs = _dispatch(h_mix, norm2_g[layer], mods, pos, n_rows=n_rows, sh_idx=3, sc_idx=4, **geo)
        ys = _moe(xs, items, exp_w_gu, exp_w_down, layer)
        h = _combine(h_mix, ys, pos, wts, mods, final_g, n_rows=n_rows, gate_idx=5, final=last, **geo)
        rows = (h, h)

    return h.reshape(batch, seq, d)
```

```python
import functools
import math

import jax
import jax.numpy as jnp
from jax import lax
from jax.experimental import pallas as pl
from jax.experimental.pallas import tpu as pltpu

F32 = jnp.float32
BF16 = jnp.bfloat16
I32 = jnp.int32
U32 = jnp.uint32

D_MODEL = 2048
GRID_W = 64
NORM_EPS = 1e-6
N_HEADS = 8
HEAD_DIM = 64
HEAD_W = 2 * HEAD_DIM
QK_W = N_HEADS * HEAD_W
ATTN_W = N_HEADS * HEAD_W
ROPE_BASE = 10000.0
ROPE_FREQS = HEAD_DIM // 4
SUBLN_EPS = 1e-5
QK_SCALE = HEAD_DIM ** -0.5 * math.log2(math.e)
SC_W = D_MODEL // 4
CF_W = D_MODEL // 4
SC_TAPS = 3
CF_TAPS = 31
CF_LN_EPS = 1e-5
OFF_Q = 0
OFF_K = OFF_Q + QK_W
OFF_V = OFF_K + QK_W
OFF_SC = OFF_V + ATTN_W
OFF_CF = OFF_SC + 3 * SC_W
OFF_GATE = OFF_CF + 2 * CF_W
C_TOT = OFF_GATE + 3 * D_MODEL
N_GROUPS = 4
EXPERTS_PER_GROUP = 8
N_EXPERTS = N_GROUPS * EXPERTS_PER_GROUP
EXPERT_HIDDEN = D_MODEL // 2

LANES = 128
SUBLANES = 8
MOD_ROWS = 8
ATTN_SUB_ROWS = 256
MERGE_GATE_TILES = 4
CONV_HALO = 16
CONV_ROWS = 256
CONV_CHUNK = 32
MOE_ITEM_ROWS = 1024
MOE_ITEM_CAP = MOE_ITEM_ROWS - 8
MOE_ROW_PAD = 128
MOE_GU_STEPS = 4
MOE_DOWN_STEPS = 2
MOE_ITEM_SHIFT = 1
SLOT_BLOCK_ROWS = 256
VMEM_LIMIT = 56 * 1024 * 1024


def _params(sem, vmem=VMEM_LIMIT):
    return pltpu.CompilerParams(dimension_semantics=sem, vmem_limit_bytes=vmem)


def _sigmoid(x):
    return 1.0 / (1.0 + jnp.exp(-x))


def _adaln_kernel(s_ref, w_ref, b_ref, o_ref):
    s = s_ref[...]
    s = s * _sigmoid(s)
    s_hi = s.astype(BF16)
    s_lo = (s - s_hi.astype(F32)).astype(BF16)
    w = w_ref[...]
    w_hi = w.astype(BF16)
    w_lo = (w - w_hi.astype(F32)).astype(BF16)
    lhs = jnp.concatenate([s_hi.astype(F32), s - s_hi.astype(F32)], axis=0).astype(BF16)
    r = jnp.dot(lhs, w_hi, preferred_element_type=F32)
    r2 = jnp.dot(s_hi, w_lo, preferred_element_type=F32)
    o_ref[...] = r[:MOD_ROWS] + r[MOD_ROWS:] + r2 + b_ref[...]


def _adaln(cond, ada_w, ada_b):
    n_layers, d, n = ada_w.shape
    tn = 512
    return pl.pallas_call(
        _adaln_kernel,
        out_shape=jax.ShapeDtypeStruct((n_layers, MOD_ROWS, n), F32),
        grid=(n_layers, n // tn),
        in_specs=[
            pl.BlockSpec((MOD_ROWS, d), lambda l, j: (0, 0)),
            pl.BlockSpec((None, d, tn), lambda l, j: (l, 0, j)),
            pl.BlockSpec((None, 1, tn), lambda l, j: (l, 0, j)),
        ],
        out_specs=pl.BlockSpec((None, MOD_ROWS, tn), lambda l, j: (l, 0, j)),
        compiler_params=_params(("arbitrary", "arbitrary")),
        name="adaln",
    )(cond, ada_w, ada_b.reshape(n_layers, 1, n))


def _mod_row(row0, n_lat, seq):
    return jnp.where(row0 < n_lat, row0 // seq, MOD_ROWS // 2)


def _in_proj_kernel(xa_ref, xb_ref, g_ref, sh_ref, sc_ref, w_ref, o_ref, u_ref, *, a_blocks):
    first_col = pl.program_id(1) == 0
    from_a = pl.program_id(0) < a_blocks

    def normed(x_ref):
        x = x_ref[...]
        ms = jnp.mean(x * x, axis=-1, keepdims=True)
        y = x * lax.rsqrt(ms + NORM_EPS) * g_ref[...]
        u_ref[...] = (y * (1.0 + sc_ref[...]) + sh_ref[...]).astype(BF16)

    @pl.when(jnp.logical_and(first_col, from_a))
    def _():
        normed(xa_ref)

    @pl.when(jnp.logical_and(first_col, jnp.logical_not(from_a)))
    def _():
        normed(xb_ref)

    o_ref[...] = jnp.dot(u_ref[...], w_ref[...].astype(BF16), preferred_element_type=F32).astype(o_ref.dtype)


def _in_proj(rows, gain, mods, w_in, layer, *, row0, n_rows, col0, n_cols, n_lat, seq, sh_idx, sc_idx, tm, tn=512):
    xa, xb = rows
    d = xa.shape[1]
    rb0, cb0 = row0 // tm, col0 // tn
    a_blocks = xa.shape[0] // tm - rb0

    def mod_map(idx):
        return lambda i, j: (_mod_row((i + rb0) * tm, n_lat, seq), 0, idx)

    return pl.pallas_call(
        functools.partial(_in_proj_kernel, a_blocks=a_blocks),
        out_shape=jax.ShapeDtypeStruct((n_rows, n_cols), BF16),
        grid=(n_rows // tm, n_cols // tn),
        in_specs=[
            pl.BlockSpec((tm, d), lambda i, j: (jnp.minimum(i, a_blocks - 1) + rb0, 0)),
            pl.BlockSpec((tm, d), lambda i, j: (jnp.maximum(i - a_blocks, 0), 0)),
            pl.BlockSpec((1, d), lambda i, j: (0, 0)),
            pl.BlockSpec((None, 1, d), mod_map(sh_idx)),
            pl.BlockSpec((None, 1, d), mod_map(sc_idx)),
            pl.BlockSpec((None, d, tn), lambda i, j: (layer, 0, j + cb0)),
        ],
        out_specs=pl.BlockSpec((tm, tn), lambda i, j: (i, j)),
        scratch_shapes=[pltpu.VMEM((tm, d), BF16)],
        compiler_params=_params(("arbitrary", "arbitrary")),
        name="in_proj",
    )(xa, xb, gain.reshape(1, d), mods, mods, w_in)


def _rope_tables(n_tokens):
    t = jnp.arange(n_tokens, dtype=I32)
    pos = jnp.stack([t // GRID_W, t % GRID_W], axis=-1).astype(F32)
    inv_freq = ROPE_BASE ** (-jnp.arange(ROPE_FREQS, dtype=F32) / ROPE_FREQS)
    ang = pos[:, :, None] * inv_freq
    cos, sin = jnp.cos(ang), jnp.sin(ang)
    c = jnp.stack([cos, cos], axis=2).reshape(n_tokens, HEAD_DIM)
    s = jnp.stack([-sin, sin], axis=2).reshape(n_tokens, HEAD_DIM)
    return jnp.tile(c, (1, 2)), jnp.tile(s, (1, 2))


def _rope(x, c, s):
    lane = lax.broadcasted_iota(I32, x.shape, 1)
    first_half = (lane % (2 * ROPE_FREQS)) < ROPE_FREQS
    partner = jnp.where(first_half, pltpu.roll(x, LANES - ROPE_FREQS, 1), pltpu.roll(x, ROPE_FREQS, 1))
    return x * c + partner * s


def _diff_lambda(lam_ref, lam_init):
    lv = lam_ref[...]
    a = jnp.sum(lv[0:1] * lv[1:2], axis=-1, keepdims=True)
    b = jnp.sum(lv[2:3] * lv[3:4], axis=-1, keepdims=True)
    return jnp.exp(a) - jnp.exp(b) + lam_init


def _attend(q, k_all, v_all, lam, subg, lam_init):
    tq = q.shape[0]
    lane = lax.broadcasted_iota(I32, q.shape, 1)
    q0 = jnp.where(lane < HEAD_DIM, q, 0.0).astype(BF16)
    q1 = jnp.where(lane >= HEAD_DIM, q, 0.0).astype(BF16)
    qq = jnp.concatenate([q0, q1], axis=0)
    s = lax.dot_general(qq, k_all, (((1,), (1,)), ((), ())), preferred_element_type=F32)
    m = jnp.max(s, axis=-1, keepdims=True)
    e = jnp.exp2(s - m)
    l = jnp.sum(e, axis=-1, keepdims=True)
    a = e[:tq] - e[tq:] * (lam * l[:tq] / l[tq:])
    o = jnp.dot(a.astype(BF16), v_all, preferred_element_type=F32) / l[:tq]
    ms = jnp.mean(o * o, axis=-1, keepdims=True)
    return o * lax.rsqrt(ms + SUBLN_EPS) * subg * (1.0 - lam_init)


def _attn_lat_kernel(q_ref, kl_ref, vl_ref, kc_ref, vc_ref, cq_ref, sq_ref, ck_ref, sk_ref, lam_ref, g_ref,
                     o_ref, k_all, v_aug, s0_ref, m0_ref, s1_ref, m1_ref, *, n_ctx, lam_init, n_blocks):
    i = pl.program_id(2)
    tq = q_ref.shape[0]

    @pl.when(i == 0)
    def _():
        k_all[0:n_ctx, :] = kc_ref[...]
        k_all[n_ctx:, :] = _rope(kl_ref[...].astype(F32), ck_ref[...], sk_ref[...]).astype(BF16)
        v_aug[0:n_ctx, 0:HEAD_W] = vc_ref[...]
        v_aug[n_ctx:, 0:HEAD_W] = vl_ref[...]
        v_aug[:, HEAD_W:2 * HEAD_W] = jnp.ones((k_all.shape[0], HEAD_W), BF16)

    lam = _diff_lambda(lam_ref, lam_init)

    def stage_a(sa_ref, ma_ref):
        q = _rope(q_ref[...].astype(F32), cq_ref[...], sq_ref[...]) * QK_SCALE
        lane = lax.broadcasted_iota(I32, q.shape, 1)
        q0 = jnp.where(lane < HEAD_DIM, q, 0.0).astype(BF16)
        q1 = jnp.where(lane >= HEAD_DIM, q, 0.0).astype(BF16)
        qq = jnp.concatenate([q0, q1], axis=0)
        s = lax.dot_general(qq, k_all[...], (((1,), (1,)), ((), ())), preferred_element_type=F32)
        sa_ref[...] = s
        ma_ref[...] = jnp.broadcast_to(jnp.max(s, axis=-1, keepdims=True), ma_ref.shape)

    def stage_b(sb_ref, mb_ref):
        mb = jnp.concatenate([mb_ref[...]] * (sb_ref.shape[1] // LANES), axis=1)
        e = jnp.exp2(sb_ref[...] - mb).astype(BF16)
        oa = jnp.dot(e, v_aug[...], preferred_element_type=F32)
        o = oa[:tq, 0:HEAD_W] / oa[:tq, HEAD_W:] - oa[tq:, 0:HEAD_W] * (lam / oa[tq:, HEAD_W:])
        ms = jnp.mean(o * o, axis=-1, keepdims=True)
        o_ref[...] = (o * lax.rsqrt(ms + SUBLN_EPS) * g_ref[...] * (1.0 - lam_init)).astype(o_ref.dtype)

    bufs = ((s0_ref, m0_ref), (s1_ref, m1_ref))

    @pl.when(i == 0)
    def _():
        stage_a(*bufs[0])

    for par in range(2):
        @pl.when(jnp.logical_and(jnp.logical_and(i > 0, i < n_blocks), i % 2 == par))
        def _():
            stage_a(*bufs[par])
            stage_b(*bufs[1 - par])

    @pl.when(i == n_blocks)
    def _():
        stage_b(*bufs[(n_blocks - 1) % 2])


def _attn_ctx_kernel(q_ref, k_ref, v_ref, lam_ref, g_ref, o_ref, *, lam_init):
    q = q_ref[...].astype(F32) * QK_SCALE
    lam = _diff_lambda(lam_ref, lam_init)
    o_ref[...] = _attend(q, k_ref[...], v_ref[...], lam, g_ref[...], lam_init).astype(o_ref.dtype)


def _attn_lat(p, kv_ctx, kc_blk, vc_blk, ctx_rb0, cos, sin, lam4, subg, *, batch, seq, n_ctx, lam_init, tq=512):
    nq = seq // tq
    n_keys = n_ctx + seq
    hb = lambda off: off // HEAD_W
    qblk = lambda i: jnp.minimum(i, nq - 1)
    kernel = functools.partial(_attn_lat_kernel, n_ctx=n_ctx, lam_init=lam_init, n_blocks=nq)
    return pl.pallas_call(
        kernel,
        out_shape=jax.ShapeDtypeStruct((batch * seq, ATTN_W), BF16),
        grid=(batch, N_HEADS, nq + 1),
        in_specs=[
            pl.BlockSpec((tq, HEAD_W), lambda b, h, i: (b * nq + qblk(i), hb(OFF_Q) + h)),
            pl.BlockSpec((seq, HEAD_W), lambda b, h, i: (b, hb(OFF_K) + h)),
            pl.BlockSpec((seq, HEAD_W), lambda b, h, i: (b, hb(OFF_V) + h)),
            pl.BlockSpec((n_ctx, HEAD_W), lambda b, h, i: (ctx_rb0 + b, kc_blk + h)),
            pl.BlockSpec((n_ctx, HEAD_W), lambda b, h, i: (ctx_rb0 + b, vc_blk + h)),
            pl.BlockSpec((tq, HEAD_W), lambda b, h, i: (qblk(i), 0)),
            pl.BlockSpec((tq, HEAD_W), lambda b, h, i: (qblk(i), 0)),
            pl.BlockSpec((seq, HEAD_W), lambda b, h, i: (0, 0)),
            pl.BlockSpec((seq, HEAD_W), lambda b, h, i: (0, 0)),
            pl.BlockSpec((4, HEAD_DIM), lambda b, h, i: (0, 0)),
            pl.BlockSpec((1, HEAD_W), lambda b, h, i: (0, 0)),
        ],
        out_specs=pl.BlockSpec((tq, HEAD_W), lambda b, h, i: (b * nq + jnp.maximum(i - 1, 0), h)),
        scratch_shapes=[pltpu.VMEM((n_keys, HEAD_W), BF16), pltpu.VMEM((n_keys, 2 * HEAD_W), BF16),
                        pltpu.VMEM((2 * tq, n_keys), F32), pltpu.VMEM((2 * tq, LANES), F32),
                        pltpu.VMEM((2 * tq, n_keys), F32), pltpu.VMEM((2 * tq, LANES), F32)],
        compiler_params=_params(("arbitrary", "arbitrary", "arbitrary")),
        name="attn_lat",
    )(p, p, p, kv_ctx, kv_ctx, cos, sin, cos, sin, lam4, subg.reshape(1, HEAD_W))


def _attn_ctx(p, ctx_rb0, lam4, subg, *, batch, n_ctx, lam_init):
    hb = lambda off: off // HEAD_W
    kernel = functools.partial(_attn_ctx_kernel, lam_init=lam_init)
    return pl.pallas_call(
        kernel,
        out_shape=jax.ShapeDtypeStruct((batch * n_ctx, ATTN_W), BF16),
        grid=(batch, N_HEADS),
        in_specs=[
            pl.BlockSpec((n_ctx, HEAD_W), lambda b, h: (ctx_rb0 + b, hb(OFF_Q) + h)),
            pl.BlockSpec((n_ctx, HEAD_W), lambda b, h: (ctx_rb0 + b, hb(OFF_K) + h)),
            pl.BlockSpec((n_ctx, HEAD_W), lambda b, h: (ctx_rb0 + b, hb(OFF_V) + h)),
            pl.BlockSpec((4, HEAD_DIM), lambda b, h: (0, 0)),
            pl.BlockSpec((1, HEAD_W), lambda b, h: (0, 0)),
        ],
        out_specs=pl.BlockSpec((n_ctx, HEAD_W), lambda b, h: (b, h)),
        compiler_params=_params(("arbitrary", "arbitrary")),
        name="attn_ctx",
    )(p, p, p, lam4, subg.reshape(1, HEAD_W))


def _conv_kernel(bg_ref, cg_ref, xi_ref, a_ref, g_ref,
                 cg_p, xi_p, a_p, g_p, cg_n, xi_n, a_n, g_n,
                 scw_ref, cfw_ref, cfb_ref, lng_ref, lnb_ref,
                 sco_ref, cfo_ref, pad_ref, *, lat_blocks, blocks_per_seq):
    i = pl.program_id(0)
    in_lat = i < lat_blocks
    pos = i % blocks_per_seq
    has_prev = jnp.logical_and(in_lat, pos != 0)
    has_next = jnp.logical_and(in_lat, pos != blocks_per_seq - 1)
    keep_prev = jnp.where(has_prev, 1.0, 0.0)
    keep_next = jnp.where(has_next, 1.0, 0.0)
    lo, hi = CONV_HALO, CONV_HALO + CONV_ROWS

    def fill(main, prev, nxt):
        pad_ref[0:lo, :] = prev * keep_prev
        pad_ref[lo:hi, :] = main
        pad_ref[hi:hi + CONV_HALO, :] = nxt * keep_next

    def f(ref):
        return ref[...].astype(F32)

    def glu(a, g):
        return a * _sigmoid(g)

    fill(f(cg_ref) * f(xi_ref), f(cg_p) * f(xi_p), f(cg_n) * f(xi_n))
    for c in range(CONV_ROWS // CONV_CHUNK):
        r0 = lo + c * CONV_CHUNK - SC_TAPS // 2
        acc = scw_ref[0:1, :] * pad_ref[r0:r0 + CONV_CHUNK, :]
        for k in range(1, SC_TAPS):
            acc = acc + scw_ref[k:k + 1, :] * pad_ref[r0 + k:r0 + k + CONV_CHUNK, :]
        rows = slice(c * CONV_CHUNK, (c + 1) * CONV_CHUNK)
        sco_ref[rows, :] = (bg_ref[rows, :].astype(F32) * acc).astype(sco_ref.dtype)

    fill(glu(f(a_ref), f(g_ref)), glu(f(a_p), f(g_p)), glu(f(a_n), f(g_n)))
    off = lo - CF_TAPS // 2
    win = CONV_CHUNK + SUBLANES
    for c in range(CONV_ROWS // CONV_CHUNK):
        base = c * CONV_CHUNK
        acc = None
        for b in range(SUBLANES):
            q = None
            for k in range(b, CF_TAPS, SUBLANES):
                term = cfw_ref[k:k + 1, :] * pad_ref[base + k - b:base + k - b + win, :]
                q = term if q is None else q + term
            part = q[off + b:off + b + CONV_CHUNK, :]
            acc = part if acc is None else acc + part
        z = acc + cfb_ref[...]
        mu = jnp.mean(z, axis=-1, keepdims=True)
        zc = z - mu
        var = jnp.mean(zc * zc, axis=-1, keepdims=True)
        y = zc * lax.rsqrt(var + CF_LN_EPS) * lng_ref[...] + lnb_ref[...]
        rows = slice(c * CONV_CHUNK, (c + 1) * CONV_CHUNK)
        cfo_ref[rows, :] = (y * _sigmoid(y)).astype(cfo_ref.dtype)


def _convs(p, sc_w, cf_w, cf_b, ln_g, ln_b, *, n_rows, n_lat, seq):
    nb = n_rows // CONV_ROWS
    halo_per_block = CONV_ROWS // CONV_HALO
    last_halo = n_rows // CONV_HALO - 1
    cb = lambda off: off // SC_W

    def main(off):
        return pl.BlockSpec((CONV_ROWS, SC_W), lambda i: (i, cb(off)))

    def prev(off):
        return pl.BlockSpec((CONV_HALO, SC_W), lambda i: (jnp.maximum(i * halo_per_block - 1, 0), cb(off)))

    def nxt(off):
        return pl.BlockSpec((CONV_HALO, SC_W), lambda i: (jnp.minimum((i + 1) * halo_per_block, last_halo), cb(off)))

    def vec(rows):
        return pl.BlockSpec((rows, SC_W), lambda i: (0, 0))

    o_bg, o_cg, o_xi, o_a, o_g = OFF_SC, OFF_SC + SC_W, OFF_SC + 2 * SC_W, OFF_CF, OFF_CF + CF_W
    kernel = functools.partial(_conv_kernel, lat_blocks=n_lat // CONV_ROWS, blocks_per_seq=seq // CONV_ROWS)
    return pl.pallas_call(
        kernel,
        out_shape=(jax.ShapeDtypeStruct((n_rows, SC_W), BF16), jax.ShapeDtypeStruct((n_rows, CF_W), BF16)),
        grid=(nb,),
        in_specs=[main(o_bg), main(o_cg), main(o_xi), main(o_a), main(o_g),
                  prev(o_cg), prev(o_xi), prev(o_a), prev(o_g),
                  nxt(o_cg), nxt(o_xi), nxt(o_a), nxt(o_g),
                  vec(SC_TAPS), vec(CF_TAPS), vec(1), vec(1), vec(1)],
        out_specs=(pl.BlockSpec((CONV_ROWS, SC_W), lambda i: (i, 0)),
                   pl.BlockSpec((CONV_ROWS, CF_W), lambda i: (i, 0))),
        scratch_shapes=[pltpu.VMEM((CONV_ROWS + 2 * CONV_HALO, SC_W), F32)],
        compiler_params=_params(("arbitrary",)),
        name="convs",
    )(p, p, p, p, p, p, p, p, p, p, p, p, p,
      sc_w, cf_w, cf_b.reshape(1, CF_W), ln_g.reshape(1, CF_W), ln_b.reshape(1, CF_W))


def _merge_kernel(*refs, a_blocks):
    n_gate = 3 * MERGE_GATE_TILES
    ata_ref, atb_ref, sc_ref, cf_ref = refs[:4]
    gate_refs = refs[4:4 + n_gate]
    bg_ref, wa_ref, wb_ref, wc_ref, wm_ref, ha_ref, hb_ref, g1_ref, o_ref = refs[4 + n_gate:]
    d = D_MODEL
    from_a = pl.program_id(0) < a_blocks
    attn = jnp.where(from_a, ata_ref[...], atb_ref[...])
    h = jnp.where(from_a, ha_ref[...], hb_ref[...])

    def gate(branch):
        tiles = gate_refs[branch * MERGE_GATE_TILES:(branch + 1) * MERGE_GATE_TILES]
        pre = jnp.concatenate([t[...] for t in tiles], axis=1).astype(F32)
        return _sigmoid(pre + bg_ref[:, branch * d:(branch + 1) * d])

    ya = jnp.dot(attn, wa_ref[...], preferred_element_type=F32)
    yb = jnp.dot(sc_ref[...], wb_ref[...], preferred_element_type=F32)
    yc = jnp.dot(cf_ref[...], wc_ref[...], preferred_element_type=F32)
    m = (gate(0) * ya + gate(1) * yb + gate(2) * yc).astype(BF16)
    o_ref[...] = h + g1_ref[...] * jnp.dot(m, wm_ref[...], preferred_element_type=F32)


def _merge(attn, sc, cf, p, b_gate, wa, wb, wc, wm, rows, mods, *, n_rows, n_lat, seq, gate_idx, tm=256):
    d = D_MODEL
    (ata, atb), (ha, hb) = attn, rows
    a_blocks = ha.shape[0] // tm
    first = lambda w: pl.BlockSpec((tm, w), lambda i: (jnp.minimum(i, a_blocks - 1), 0))
    rest = lambda w: pl.BlockSpec((tm, w), lambda i: (jnp.maximum(i - a_blocks, 0), 0))
    tg = d // MERGE_GATE_TILES
    gb0 = OFF_GATE // tg
    gates = [pl.BlockSpec((tm, tg), functools.partial(lambda i, c: (i, c), c=gb0 + t))
             for t in range(3 * MERGE_GATE_TILES)]
    whole = lambda a: pl.BlockSpec(a.shape, lambda i: (0, 0))
    return pl.pallas_call(
        functools.partial(_merge_kernel, a_blocks=a_blocks),
        out_shape=jax.ShapeDtypeStruct((n_rows, d), F32),
        grid=(n_rows // tm,),
        in_specs=[
            first(ATTN_W), rest(ATTN_W),
            pl.BlockSpec((tm, SC_W), lambda i: (i, 0)),
            pl.BlockSpec((tm, CF_W), lambda i: (i, 0)),
            *gates,
            whole(b_gate), whole(wa), whole(wb), whole(wc), whole(wm),
            first(d), rest(d),
            pl.BlockSpec((None, 1, d), lambda i: (_mod_row(i * tm, n_lat, seq), 0, gate_idx)),
        ],
        out_specs=pl.BlockSpec((tm, d), lambda i: (i, 0)),
        compiler_params=_params(("arbitrary",)),
        name="merge",
    )(ata, atb, sc, cf, *([p] * (3 * MERGE_GATE_TILES)), b_gate, wa, wb, wc, wm, ha, hb, mods)


def _split2(x):
    hi = x.astype(BF16)
    return hi, (x - hi.astype(F32)).astype(BF16)


def _norm_mod(x, g, sh, sc):
    ms = jnp.mean(x * x, axis=-1, keepdims=True)
    return x * lax.rsqrt(ms + NORM_EPS) * g * (1.0 + sc) + sh


def _router_kernel(h_ref, g_ref, sh_ref, sc_ref, rw_ref, rb_ref, id_ref, wt_ref):
    f = _norm_mod(h_ref[...], g_ref[...], sh_ref[...], sc_ref[...])

    f_hi, f_lo = _split2(f)
    w_hi, w_lo = _split2(rw_ref[...])
    dot = lambda a, b: jnp.dot(a, b, preferred_element_type=F32)
    logits = (dot(f_lo, w_hi) + dot(f_hi, w_lo) + dot(f_hi, w_hi)) + rb_ref[...]

    lane_i = lax.broadcasted_iota(I32, logits.shape, 1)
    lane = lane_i.astype(F32)
    neg = -jnp.inf
    big = float(LANES)
    lg = jnp.where(lane_i < N_GROUPS, logits, neg)
    mg = jnp.max(lg, axis=-1, keepdims=True)
    grp = jnp.min(jnp.where(lg == mg, lane, big), axis=-1, keepdims=True)
    p_grp = 1.0 / jnp.sum(jnp.exp(lg - mg), axis=-1, keepdims=True)

    e_lane = lane - N_GROUPS
    in_grp = jnp.logical_and(e_lane >= grp * EXPERTS_PER_GROUP, e_lane < (grp + 1.0) * EXPERTS_PER_GROUP)
    le = jnp.where(in_grp, logits, neg)
    m1 = jnp.max(le, axis=-1, keepdims=True)
    i1 = jnp.min(jnp.where(le == m1, lane, big), axis=-1, keepdims=True)
    le2 = jnp.where(lane == i1, neg, le)
    m2 = jnp.max(le2, axis=-1, keepdims=True)
    i2 = jnp.min(jnp.where(le2 == m2, lane, big), axis=-1, keepdims=True)
    e2 = jnp.exp(m2 - m1)
    w1 = p_grp / (1.0 + e2)
    w2 = p_grp * e2 / (1.0 + e2)
    ids = jnp.where(lane_i == 0, i1 - N_GROUPS, jnp.where(lane_i == 1, i2 - N_GROUPS, 0.0))
    id_ref[...] = ids.astype(I32)
    wt_ref[...] = jnp.where(lane_i == 0, w1, jnp.where(lane_i == 1, w2, 0.0))


def _router(h, gain, mods, rw, rb, *, n_rows, n_lat, seq, sh_idx, sc_idx, tm=512):
    d = D_MODEL

    def mod_map(idx):
        return lambda i: (_mod_row(i * tm, n_lat, seq), 0, idx)

    return pl.pallas_call(
        _router_kernel,
        out_shape=(jax.ShapeDtypeStruct((n_rows, LANES), I32),
                   jax.ShapeDtypeStruct((n_rows, LANES), F32)),
        grid=(n_rows // tm,),
        in_specs=[
            pl.BlockSpec((tm, d), lambda i: (i, 0)),
            pl.BlockSpec((1, d), lambda i: (0, 0)),
            pl.BlockSpec((None, 1, d), mod_map(sh_idx)),
            pl.BlockSpec((None, 1, d), mod_map(sc_idx)),
            pl.BlockSpec((d, LANES), lambda i: (0, 0)),
            pl.BlockSpec((1, LANES), lambda i: (0, 0)),
        ],
        out_specs=(pl.BlockSpec((tm, LANES), lambda i: (i, 0)),
                   pl.BlockSpec((tm, LANES), lambda i: (i, 0))),
        compiler_params=_params(("arbitrary",)),
        name="router",
    )(h, gain.reshape(1, d), mods, mods, rw, rb)


def _pack_rows(x):
    c = x.shape[1] // 2
    lo = lax.bitcast_convert_type(x[:, :c].astype(BF16).astype(F32), U32) >> 16
    hi = lax.bitcast_convert_type(x[:, c:].astype(BF16).astype(F32), U32) & jnp.uint32(0xFFFF0000)
    return hi | lo


def _unpack_rows(p):
    return (lax.bitcast_convert_type(p << 16, F32),
            lax.bitcast_convert_type(p & jnp.uint32(0xFFFF0000), F32))


def _dispatch_kernel(pos_ref, h_ref, g_ref, sh_ref, sc_ref, xs_hbm, f0_ref, f1_ref, sem):
    i = pl.program_id(0)
    tm = h_ref.shape[0]

    def run(f_ref):
        def wait_block_copies():
            for _ in range(2):
                pltpu.make_async_copy(f_ref, xs_hbm.at[pl.ds(0, tm), :], sem).wait()

        f_ref[...] = _pack_rows(_norm_mod(h_ref[...], g_ref[...], sh_ref[...], sc_ref[...]))

        @pl.when(i > 0)
        def _():
            wait_block_copies()

        base = i * (2 * tm)
        for r in range(tm):
            for k in range(2):
                dst = pos_ref[base + 2 * r + k]
                pltpu.make_async_copy(f_ref.at[pl.ds(r, 1), :], xs_hbm.at[pl.ds(dst, 1), :], sem).start()

        @pl.when(i == pl.num_programs(0) - 1)
        def _():
            wait_block_copies()

    @pl.when(i % 2 == 0)
    def _():
        run(f0_ref)

    @pl.when(i % 2 == 1)
    def _():
        run(f1_ref)


def _dispatch(h, gain, mods, pos, *, n_rows, n_lat, seq, sh_idx, sc_idx, tm=SLOT_BLOCK_ROWS):
    d = D_MODEL

    def mod_map(idx):
        return lambda i, pos_ref: (_mod_row(i * tm, n_lat, seq), 0, idx)

    grid_spec = pltpu.PrefetchScalarGridSpec(
        num_scalar_prefetch=1,
        grid=(n_rows // tm,),
        in_specs=[
            pl.BlockSpec((tm, d), lambda i, pos_ref: (i, 0)),
            pl.BlockSpec((1, d), lambda i, pos_ref: (0, 0)),
            pl.BlockSpec((None, 1, d), mod_map(sh_idx)),
            pl.BlockSpec((None, 1, d), mod_map(sc_idx)),
        ],
        out_specs=pl.BlockSpec(memory_space=pl.ANY),
        scratch_shapes=[pltpu.VMEM((tm, d // 2), U32), pltpu.VMEM((tm, d // 2), U32), pltpu.SemaphoreType.DMA],
    )
    return pl.pallas_call(
        _dispatch_kernel,
        out_shape=jax.ShapeDtypeStruct((2 * n_rows, d // 2), U32),
        grid_spec=grid_spec,
        compiler_params=_params(("arbitrary",)),
        name="dispatch",
    )(pos, h, gain.reshape(1, d), mods, mods)


def _moe_plan(ids, n_tok):
    n_items = N_EXPERTS + (2 * n_tok) // MOE_ITEM_CAP
    e_flat = ids[:, :2].reshape(-1)
    order = jnp.argsort(e_flat, stable=True).astype(I32)
    pos = jnp.argsort(order).astype(I32)
    pos = jnp.concatenate([pos, jnp.zeros((2 * SLOT_BLOCK_ROWS,), I32)])
    counts = jnp.sum((e_flat[:, None] == jnp.arange(N_EXPERTS, dtype=I32)[None, :]).astype(I32), axis=0)
    starts = jnp.cumsum(counts) - counts
    n_sb = (counts + MOE_ITEM_CAP - 1) // MOE_ITEM_CAP
    cum = jnp.cumsum(n_sb)
    total = cum[-1]
    it = jnp.arange(n_items, dtype=I32)
    e_i = jnp.minimum(jnp.searchsorted(cum, it, side="right").astype(I32), N_EXPERTS - 1)
    sb = it - (cum[e_i] - n_sb[e_i])
    valid = it < total
    last_e = e_i[jnp.maximum(total - 1, 0)]
    item_e = jnp.where(valid, e_i, last_e).astype(I32)
    item_start = jnp.where(valid, starts[e_i] + sb * MOE_ITEM_CAP, 0).astype(I32)
    item_n = jnp.where(valid, jnp.minimum(MOE_ITEM_CAP, counts[e_i] - sb * MOE_ITEM_CAP), 0).astype(I32)
    front, back = jnp.zeros((MOE_ITEM_SHIFT,), I32), jnp.zeros((1,), I32)
    pad_items = lambda a: jnp.concatenate([front, a, back])
    return (pad_items(item_e), pad_items(item_start), pad_items(item_n)), pos


def _moe_kernel(ie_ref, ist_ref, in_ref,
                xs_hbm, wgu_ref, wd_ref, ys_hbm,
                xf_ref, xb_ref, gu_ref, hm_ref, y_ref, yp_ref, gsem, ssem):
    i, j = pl.program_id(0), pl.program_id(1)
    sh = MOE_ITEM_SHIFT
    n_prev, n, n_next = in_ref[i + sh - 1], in_ref[i + sh], in_ref[i + sh + 1]
    start_prev, start, start_next = ist_ref[i + sh - 1], ist_ref[i + sh], ist_ref[i + sh + 1]
    kc_w = D_MODEL // MOE_GU_STEPS
    hc_w = EXPERT_HIDDEN // MOE_DOWN_STEPS

    def rows_in(hrow, vrow, size):
        return pltpu.make_async_copy(xs_hbm.at[pl.ds(hrow, size), :], xf_ref.at[pl.ds(vrow, size), :], gsem)

    def rows_out(hrow, vrow, size):
        return pltpu.make_async_copy(yp_ref.at[pl.ds(vrow, size), :], ys_hbm.at[pl.ds(hrow, size), :], ssem)

    def row_groups(make, base, count, act):
        shift = base % SUBLANES
        head = jnp.minimum((SUBLANES - shift) % SUBLANES, count)
        body = ((count - head) // SUBLANES) * SUBLANES
        for t in range(SUBLANES - 1):
            @pl.when(t < head)
            def _():
                getattr(make(base + t, shift + t, 1), act)()

        for b in reversed(range(SUBLANES.bit_length() - 1, MOE_ITEM_ROWS.bit_length())):
            @pl.when(((body >> b) & 1) == 1)
            def _():
                off = head + ((body >> (b + 1)) << (b + 1))
                getattr(make(pl.multiple_of(base + off, SUBLANES), pl.multiple_of(shift + off, SUBLANES), 1 << b),
                        act)()

        done = head + body
        for t in range(SUBLANES - 1):
            @pl.when(t < count - done)
            def _():
                getattr(make(base + done + t, shift + done + t, 1), act)()

    @pl.when(jnp.logical_and(i == 0, j == 0))
    def _():
        xf_ref[...] = jnp.zeros_like(xf_ref)

    @pl.when(j == 0)
    def _():
        @pl.when(i == 0)
        def _():
            row_groups(rows_in, start, n, "start")

        @pl.when(n > 0)
        def _():
            row_groups(rows_in, start, n, "wait")
            halves = _unpack_rows(xf_ref[...])
            per_half = MOE_GU_STEPS // 2
            for c in range(MOE_GU_STEPS):
                col = (c % per_half) * kc_w
                xb_ref[c] = halves[c // per_half][:, col:col + kc_w].astype(BF16)
            row_groups(rows_in, start_next, n_next, "start")

    @pl.when(j == MOE_GU_STEPS)
    def _():
        row_groups(rows_out, start_prev, n_prev, "wait")

    def gate_up(m, c, first, last):
        part = jnp.dot(xb_ref[c, 0:m, :], wgu_ref[...].astype(BF16), preferred_element_type=F32)
        gu = part if first else part + gu_ref[0:m, :]
        if not last:
            gu_ref[0:m, :] = gu
            return
        g, u = gu[:, 0:EXPERT_HIDDEN], gu[:, EXPERT_HIDDEN:]
        hidden = (g * _sigmoid(g) * u).astype(BF16)
        for c2 in range(MOE_DOWN_STEPS):
            hm_ref[c2, 0:m, :] = hidden[:, c2 * hc_w:(c2 + 1) * hc_w]

    def down(m, c, first, last):
        part = jnp.dot(hm_ref[c, 0:m, :], wd_ref[...].astype(BF16), preferred_element_type=F32)
        y = part if first else part + y_ref[0:m, :]
        if last:
            yp_ref[0:m, :] = _pack_rows(y)
        else:
            y_ref[0:m, :] = y

    used = jnp.where(n > 0, start % SUBLANES + n, 0)
    n_pad = (used + MOE_ROW_PAD - 1) // MOE_ROW_PAD
    for k in range(1, MOE_ITEM_ROWS // MOE_ROW_PAD + 1):
        m = k * MOE_ROW_PAD
        for c in range(MOE_GU_STEPS):
            @pl.when(jnp.logical_and(n_pad == k, j == c))
            def _():
                gate_up(m, c, c == 0, c == MOE_GU_STEPS - 1)

        for c in range(MOE_DOWN_STEPS):
            @pl.when(jnp.logical_and(n_pad == k, j == MOE_GU_STEPS + c))
            def _():
                down(m, c, c == 0, c == MOE_DOWN_STEPS - 1)

    @pl.when(j == pl.num_programs(1) - 1)
    def _():
        row_groups(rows_out, start, n, "start")

        @pl.when(i == pl.num_programs(0) - 1)
        def _():
            row_groups(rows_out, start, n, "wait")


def _moe(xs, items, w_gu, w_down, layer):
    item_e, item_start, item_n = items
    d, hid = D_MODEL, EXPERT_HIDDEN
    n_gu, n_dn = MOE_GU_STEPS, MOE_DOWN_STEPS
    sh = MOE_ITEM_SHIFT
    n_items = item_e.shape[0] - sh - 1

    def gu_map(i, j, ie, ist, nn):
        busy = nn[i + sh] > 0
        return layer, ie[i + sh], jnp.where(busy, jnp.minimum(j, n_gu - 1), n_gu - 1), 0

    def dn_map(i, j, ie, ist, nn):
        now = jnp.logical_and(nn[i + sh] > 0, j >= n_gu)
        return (layer, jnp.where(now, ie[i + sh], ie[i + sh - 1]),
                jnp.where(now, j - n_gu, n_dn - 1), 0)

    grid_spec = pltpu.PrefetchScalarGridSpec(
        num_scalar_prefetch=3,
        grid=(n_items, n_gu + n_dn),
        in_specs=[
            pl.BlockSpec(memory_space=pl.ANY),
            pl.BlockSpec((None, None, d // n_gu, 2 * hid), gu_map),
            pl.BlockSpec((None, None, hid // n_dn, d), dn_map),
        ],
        out_specs=pl.BlockSpec(memory_space=pl.ANY),
        scratch_shapes=[
            pltpu.VMEM((MOE_ITEM_ROWS, d // 2), U32),
            pltpu.VMEM((n_gu, MOE_ITEM_ROWS, d // n_gu), BF16),
            pltpu.VMEM((MOE_ITEM_ROWS, 2 * hid), F32),
            pltpu.VMEM((n_dn, MOE_ITEM_ROWS, hid // n_dn), BF16),
            pltpu.VMEM((MOE_ITEM_ROWS, d), F32),
            pltpu.VMEM((MOE_ITEM_ROWS, d // 2), U32),
            pltpu.SemaphoreType.DMA,
            pltpu.SemaphoreType.DMA,
        ],
    )
    return pl.pallas_call(
        _moe_kernel,
        out_shape=jax.ShapeDtypeStruct(xs.shape, U32),
        grid_spec=grid_spec,
        compiler_params=_params(("arbitrary", "arbitrary")),
        name="moe_experts",
    )(item_e, item_start, item_n, xs, w_gu, w_down)


def _combine_kernel(pos_ref, h_ref, wt_ref, g2_ref, fg_ref, ys_hbm, o_ref, ybuf, sem, *, final):
    i = pl.program_id(0)
    tm = h_ref.shape[0]
    par = i % 2

    def fetch(step, half):
        base = step * (2 * tm)
        for r in range(tm):
            for k in range(2):
                src = pos_ref[base + 2 * r + k]
                pltpu.make_async_copy(ys_hbm.at[pl.ds(src, 1), :], ybuf.at[half, k, pl.ds(r, 1), :],
                                      sem.at[half]).start()

    @pl.when(i == 0)
    def _():
        fetch(0, 0)

    @pl.when(i < pl.num_programs(0) - 1)
    def _():
        fetch(i + 1, 1 - par)

    for k in range(2):
        pltpu.make_async_copy(ys_hbm.at[pl.ds(0, tm), :], ybuf.at[par, k], sem.at[par]).wait()
    wt = wt_ref[...]
    y0, y1 = _unpack_rows(ybuf[par, 0]), _unpack_rows(ybuf[par, 1])
    half = o_ref.shape[1] // 2
    cols = (slice(0, half), slice(half, 2 * half))
    hs = [h_ref[:, c] + g2_ref[:, c] * (wt[:, 0:1] * y0[s] + wt[:, 1:2] * y1[s]) for s, c in enumerate(cols)]
    if final:
        ms = (jnp.sum(hs[0] * hs[0], axis=-1, keepdims=True)
              + jnp.sum(hs[1] * hs[1], axis=-1, keepdims=True)) * (1.0 / (2 * half))
        scale = lax.rsqrt(ms + NORM_EPS)
        hs = [hs[s] * scale * fg_ref[:, c] for s, c in enumerate(cols)]
    for s, c in enumerate(cols):
        o_ref[:, c] = hs[s]


def _combine(h, ys, pos, wts, mods, final_g, *, n_rows, n_lat, seq, gate_idx, final, tm=SLOT_BLOCK_ROWS):
    d = D_MODEL
    kernel = functools.partial(_combine_kernel, final=final)
    grid_spec = pltpu.PrefetchScalarGridSpec(
        num_scalar_prefetch=1,
        grid=(n_rows // tm,),
        in_specs=[
            pl.BlockSpec((tm, d), lambda i, pos_ref: (i, 0)),
            pl.BlockSpec((tm, LANES), lambda i, pos_ref: (i, 0)),
            pl.BlockSpec((None, 1, d), lambda i, pos_ref: (_mod_row(i * tm, n_lat, seq), 0, gate_idx)),
            pl.BlockSpec((1, d), lambda i, pos_ref: (0, 0)),
            pl.BlockSpec(memory_space=pl.ANY),
        ],
        out_specs=pl.BlockSpec((tm, d), lambda i, pos_ref: (i, 0)),
        scratch_shapes=[pltpu.VMEM((2, 2, tm, d // 2), U32), pltpu.SemaphoreType.DMA((2,))],
    )
    return pl.pallas_call(
        kernel,
        out_shape=jax.ShapeDtypeStruct((n_rows, d), F32),
        grid_spec=grid_spec,
        compiler_params=_params(("arbitrary",)),
        name="combine",
    )(pos, h, wts, mods, final_g.reshape(1, d), ys)


def kernel(x, c, ctx, c_ctx, ada_w, ada_b, norm1_g, w_in, b_gate, diff_lambda, subln_g, w_attn_out, sc_conv_w,
           w_sc_out, cf_dw_w, cf_dw_b, cf_ln_g, cf_ln_b, w_cf_out, w_mix, norm2_g, router_g_w, router_g_b,
           router_e_w, router_e_b, exp_w_gu, exp_w_down, final_g):
    batch, seq, d = x.shape
    n_ctx = ctx.shape[1]
    depth = ada_w.shape[0]
    n_lat = batch * seq
    n_all = n_lat + batch * n_ctx
    assert d == D_MODEL and batch == MOD_ROWS // 2 and w_in.shape[2] == C_TOT

    cond = jnp.concatenate([c, c_ctx[None, :], jnp.zeros((MOD_ROWS - batch - 1, d), F32)], axis=0)
    mods_all = _adaln(cond, ada_w, ada_b)
    cos, sin = _rope_tables(seq)
    rows = (x.reshape(n_lat, d), ctx.reshape(batch * n_ctx, d))
    geo = dict(n_lat=n_lat, seq=seq)

    for layer in range(depth):
        last = layer == depth - 1
        lam_init = 0.8 - 0.6 * math.exp(-0.3 * layer)
        mods = mods_all[layer].reshape(MOD_ROWS, 1, 6 * d)
        n_rows = n_lat if last else n_all

        p = _in_proj(rows, norm1_g[layer], mods, w_in, layer, row0=0, n_rows=n_rows, col0=0, n_cols=C_TOT,
                     sh_idx=0, sc_idx=1, tm=1024, **geo)
        if last:
            kv_ctx = _in_proj(rows, norm1_g[layer], mods, w_in, layer, row0=n_lat, n_rows=batch * n_ctx, col0=OFF_K,
                              n_cols=OFF_SC - OFF_K, sh_idx=0, sc_idx=1, tm=n_ctx, **geo)
            kc_blk, vc_blk, ctx_rb0 = 0, QK_W // HEAD_W, 0
        else:
            kv_ctx = p
            kc_blk, vc_blk, ctx_rb0 = OFF_K // HEAD_W, OFF_V // HEAD_W, n_lat // n_ctx
        attn = _attn_lat(p, kv_ctx, kc_blk, vc_blk, ctx_rb0, cos, sin, diff_lambda[layer], subln_g[layer],
                         batch=batch, seq=seq, n_ctx=n_ctx, lam_init=lam_init)
        if not last:
            attn = (attn, _attn_ctx(p, n_lat // n_ctx, diff_lambda[layer], subln_g[layer],
                                    batch=batch, n_ctx=n_ctx, lam_init=lam_init))
        else:
            attn = (attn, attn)
        sc, cf = _convs(p, sc_conv_w[layer], cf_dw_w[layer], cf_dw_b[layer], cf_ln_g[layer], cf_ln_b[layer],
                        n_rows=n_rows, **geo)
        h_mix = _merge(attn, sc, cf, p, b_gate[layer].reshape(1, 3 * d),
                       w_attn_out[layer].astype(BF16), w_sc_out[layer].astype(BF16),
                       w_cf_out[layer].astype(BF16), w_mix[layer].astype(BF16), rows, mods,
                       n_rows=n_rows, gate_idx=2, **geo)

        rw = jnp.concatenate([router_g_w[layer], router_e_w[layer],
                              jnp.zeros((d, LANES - N_GROUPS - N_EXPERTS), F32)], axis=1)
        rb = jnp.concatenate([router_g_b[layer], router_e_b[layer],
                              jnp.zeros((LANES - N_GROUPS - N_EXPERTS,), F32)]).reshape(1, LANES)
        ids, wts = _router(h_mix, norm2_g[layer], mods, rw, rb, n_rows=n_rows, sh_idx=3, sc_idx=4, **geo)
        items, pos = _moe_plan(ids, n_rows)
        xs = _dispatch(h_mix, norm2_g[layer], mods, pos, n_rows=n_rows, sh_idx=3, sc_idx=4, **geo)
        ys = _moe(xs, items, exp_w_gu, exp_w_down, layer)
        h = _combine(h_mix, ys, pos, wts, mods, final_g, n_rows=n_rows, gate_idx=5, final=last, **geo)
        rows = (h, h)

    return h.reshape(batch, seq, d)
```

```python
import functools
import math

import jax
import jax.numpy as jnp
from jax import lax
from jax.experimental import pallas as pl
from jax.experimental.pallas import tpu as pltpu

F32 = jnp.float32
BF16 = jnp.bfloat16
I32 = jnp.int32
U32 = jnp.uint32

D_MODEL = 2048
GRID_W = 64
NORM_EPS = 1e-6
N_HEADS = 8
HEAD_DIM = 64
HEAD_W = 2 * HEAD_DIM
QK_W = N_HEADS * HEAD_W
ATTN_W = N_HEADS * HEAD_W
ROPE_BASE = 10000.0
ROPE_FREQS = HEAD_DIM // 4
SUBLN_EPS = 1e-5
QK_SCALE = HEAD_DIM ** -0.5 * math.log2(math.e)
SC_W = D_MODEL // 4
CF_W = D_MODEL // 4
SC_TAPS = 3
CF_TAPS = 31
CF_LN_EPS = 1e-5
OFF_Q = 0
OFF_K = OFF_Q + QK_W
OFF_V = OFF_K + QK_W
OFF_SC = OFF_V + ATTN_W
OFF_CF = OFF_SC + 3 * SC_W
OFF_GATE = OFF_CF + 2 * CF_W
C_TOT = OFF_GATE + 3 * D_MODEL
N_GROUPS = 4
EXPERTS_PER_GROUP = 8
N_EXPERTS = N_GROUPS * EXPERTS_PER_GROUP
EXPERT_HIDDEN = D_MODEL // 2

LANES = 128
SUBLANES = 8
MOD_ROWS = 8
ATTN_SUB_ROWS = 256
MERGE_GATE_TILES = 4
CONV_HALO = 16
CONV_ROWS = 256
CONV_CHUNK = 32
MOE_ITEM_ROWS = 1024
MOE_ITEM_CAP = MOE_ITEM_ROWS - 8
MOE_ROW_PAD = 128
MOE_GU_STEPS = 4
MOE_DOWN_STEPS = 2
MOE_ITEM_SHIFT = 1
SLOT_BLOCK_ROWS = 256
VMEM_LIMIT = 56 * 1024 * 1024


def _params(sem, vmem=VMEM_LIMIT):
    return pltpu.CompilerParams(dimension_semantics=sem, vmem_limit_bytes=vmem)


def _sigmoid(x):
    return 1.0 / (1.0 + jnp.exp(-x))


def _adaln_kernel(s_ref, w_ref, b_ref, o_ref):
    s = s_ref[...]
    s = s * _sigmoid(s)
    s_hi = s.astype(BF16)
    s_lo = (s - s_hi.astype(F32)).astype(BF16)
    w = w_ref[...]
    w_hi = w.astype(BF16)
    w_lo = (w - w_hi.astype(F32)).astype(BF16)
    lhs = jnp.concatenate([s_hi.astype(F32), s - s_hi.astype(F32)], axis=0).astype(BF16)
    r = jnp.dot(lhs, w_hi, preferred_element_type=F32)
    r2 = jnp.dot(s_hi, w_lo, preferred_element_type=F32)
    o_ref[...] = r[:MOD_ROWS] + r[MOD_ROWS:] + r2 + b_ref[...]


def _adaln(cond, ada_w, ada_b):
    n_layers, d, n = ada_w.shape
    tn = 512
    return pl.pallas_call(
        _adaln_kernel,
        out_shape=jax.ShapeDtypeStruct((n_layers, MOD_ROWS, n), F32),
        grid=(n_layers, n // tn),
        in_specs=[
            pl.BlockSpec((MOD_ROWS, d), lambda l, j: (0, 0)),
            pl.BlockSpec((None, d, tn), lambda l, j: (l, 0, j)),
            pl.BlockSpec((None, 1, tn), lambda l, j: (l, 0, j)),
        ],
        out_specs=pl.BlockSpec((None, MOD_ROWS, tn), lambda l, j: (l, 0, j)),
        compiler_params=_params(("arbitrary", "arbitrary")),
        name="adaln",
    )(cond, ada_w, ada_b.reshape(n_layers, 1, n))


def _mod_row(row0, n_lat, seq):
    return jnp.where(row0 < n_lat, row0 // seq, MOD_ROWS // 2)


def _in_proj_kernel(xa_ref, xb_ref, g_ref, sh_ref, sc_ref, w_ref, o_ref, u_ref, *, a_blocks):
    first_col = pl.program_id(1) == 0
    from_a = pl.program_id(0) < a_blocks

    def normed(x_ref):
        x = x_ref[...]
        ms = jnp.mean(x * x, axis=-1, keepdims=True)
        y = x * lax.rsqrt(ms + NORM_EPS) * g_ref[...]
        u_ref[...] = (y * (1.0 + sc_ref[...]) + sh_ref[...]).astype(BF16)

    @pl.when(jnp.logical_and(first_col, from_a))
    def _():
        normed(xa_ref)

    @pl.when(jnp.logical_and(first_col, jnp.logical_not(from_a)))
    def _():
        normed(xb_ref)

    o_ref[...] = jnp.dot(u_ref[...], w_ref[...].astype(BF16), preferred_element_type=F32).astype(o_ref.dtype)


def _in_proj(rows, gain, mods, w_in, layer, *, row0, n_rows, col0, n_cols, n_lat, seq, sh_idx, sc_idx, tm, tn=512):
    xa, xb = rows
    d = xa.shape[1]
    rb0, cb0 = row0 // tm, col0 // tn
    a_blocks = xa.shape[0] // tm - rb0

    def mod_map(idx):
        return lambda i, j: (_mod_row((i + rb0) * tm, n_lat, seq), 0, idx)

    return pl.pallas_call(
        functools.partial(_in_proj_kernel, a_blocks=a_blocks),
        out_shape=jax.ShapeDtypeStruct((n_rows, n_cols), BF16),
        grid=(n_rows // tm, n_cols // tn),
        in_specs=[
            pl.BlockSpec((tm, d), lambda i, j: (jnp.minimum(i, a_blocks - 1) + rb0, 0)),
            pl.BlockSpec((tm, d), lambda i, j: (jnp.maximum(i - a_blocks, 0), 0)),
            pl.BlockSpec((1, d), lambda i, j: (0, 0)),
            pl.BlockSpec((None, 1, d), mod_map(sh_idx)),
            pl.BlockSpec((None, 1, d), mod_map(sc_idx)),
            pl.BlockSpec((None, d, tn), lambda i, j: (layer, 0, j + cb0)),
        ],
        out_specs=pl.BlockSpec((tm, tn), lambda i, j: (i, j)),
        scratch_shapes=[pltpu.VMEM((tm, d), BF16)],
        compiler_params=_params(("arbitrary", "arbitrary")),
        name="in_proj",
    )(xa, xb, gain.reshape(1, d), mods, mods, w_in)


def _rope_tables(n_tokens):
    t = jnp.arange(n_tokens, dtype=I32)
    pos = jnp.stack([t // GRID_W, t % GRID_W], axis=-1).astype(F32)
    inv_freq = ROPE_BASE ** (-jnp.arange(ROPE_FREQS, dtype=F32) / ROPE_FREQS)
    ang = pos[:, :, None] * inv_freq
    cos, sin = jnp.cos(ang), jnp.sin(ang)
    c = jnp.stack([cos, cos], axis=2).reshape(n_tokens, HEAD_DIM)
    s = jnp.stack([-sin, sin], axis=2).reshape(n_tokens, HEAD_DIM)
    return jnp.tile(c, (1, 2)), jnp.tile(s, (1, 2))


def _rope(x, c, s):
    lane = lax.broadcasted_iota(I32, x.shape, 1)
    first_half = (lane % (2 * ROPE_FREQS)) < ROPE_FREQS
    partner = jnp.where(first_half, pltpu.roll(x, LANES - ROPE_FREQS, 1), pltpu.roll(x, ROPE_FREQS, 1))
    return x * c + partner * s


def _diff_lambda(lam_ref, lam_init):
    lv = lam_ref[...]
    a = jnp.sum(lv[0:1] * lv[1:2], axis=-1, keepdims=True)
    b = jnp.sum(lv[2:3] * lv[3:4], axis=-1, keepdims=True)
    return jnp.exp(a) - jnp.exp(b) + lam_init


def _attend(q, k_all, v_all, lam, subg, lam_init):
    tq = q.shape[0]
    lane = lax.broadcasted_iota(I32, q.shape, 1)
    q0 = jnp.where(lane < HEAD_DIM, q, 0.0).astype(BF16)
    q1 = jnp.where(lane >= HEAD_DIM, q, 0.0).astype(BF16)
    qq = jnp.concatenate([q0, q1], axis=0)
    s = lax.dot_general(qq, k_all, (((1,), (1,)), ((), ())), preferred_element_type=F32)
    m = jnp.max(s, axis=-1, keepdims=True)
    e = jnp.exp2(s - m)
    l = jnp.sum(e, axis=-1, keepdims=True)
    a = e[:tq] - e[tq:] * (lam * l[:tq] / l[tq:])
    o = jnp.dot(a.astype(BF16), v_all, preferred_element_type=F32) / l[:tq]
    ms = jnp.mean(o * o, axis=-1, keepdims=True)
    return o * lax.rsqrt(ms + SUBLN_EPS) * subg * (1.0 - lam_init)


def _attn_lat_kernel(q_ref, kl_ref, vl_ref, kc_ref, vc_ref, cq_ref, sq_ref, ck_ref, sk_ref, lam_ref, g_ref,
                     o_ref, k_all, v_aug, s0_ref, m0_ref, s1_ref, m1_ref, *, n_ctx, lam_init, n_blocks):
    i = pl.program_id(2)
    tq = q_ref.shape[0]

    @pl.when(i == 0)
    def _():
        k_all[0:n_ctx, :] = kc_ref[...]
        k_all[n_ctx:, :] = _rope(kl_ref[...].astype(F32), ck_ref[...], sk_ref[...]).astype(BF16)
        v_aug[0:n_ctx, 0:HEAD_W] = vc_ref[...]
        v_aug[n_ctx:, 0:HEAD_W] = vl_ref[...]
        v_aug[:, HEAD_W:2 * HEAD_W] = jnp.ones((k_all.shape[0], HEAD_W), BF16)

    lam = _diff_lambda(lam_ref, lam_init)

    def stage_a(sa_ref, ma_ref):
        q = _rope(q_ref[...].astype(F32), cq_ref[...], sq_ref[...]) * QK_SCALE
        lane = lax.broadcasted_iota(I32, q.shape, 1)
        q0 = jnp.where(lane < HEAD_DIM, q, 0.0).astype(BF16)
        q1 = jnp.where(lane >= HEAD_DIM, q, 0.0).astype(BF16)
        qq = jnp.concatenate([q0, q1], axis=0)
        s = lax.dot_general(qq, k_all[...], (((1,), (1,)), ((), ())), preferred_element_type=F32)
        sa_ref[...] = s
        ma_ref[...] = jnp.broadcast_to(jnp.max(s, axis=-1, keepdims=True), ma_ref.shape)

    def stage_b(sb_ref, mb_ref):
        mb = jnp.concatenate([mb_ref[...]] * (sb_ref.shape[1] // LANES), axis=1)
        e = jnp.exp2(sb_ref[...] - mb).astype(BF16)
        oa = jnp.dot(e, v_aug[...], preferred_element_type=F32)
        o = oa[:tq, 0:HEAD_W] / oa[:tq, HEAD_W:] - oa[tq:, 0:HEAD_W] * (lam / oa[tq:, HEAD_W:])
        ms = jnp.mean(o * o, axis=-1, keepdims=True)
        o_ref[...] = (o * lax.rsqrt(ms + SUBLN_EPS) * g_ref[...] * (1.0 - lam_init)).astype(o_ref.dtype)

    bufs = ((s0_ref, m0_ref), (s1_ref, m1_ref))

    @pl.when(i == 0)
    def _():
        stage_a(*bufs[0])

    for par in range(2):
        @pl.when(jnp.logical_and(jnp.logical_and(i > 0, i < n_blocks), i % 2 == par))
        def _():
            stage_a(*bufs[par])
            stage_b(*bufs[1 - par])

    @pl.when(i == n_blocks)
    def _():
        stage_b(*bufs[(n_blocks - 1) % 2])


def _attn_ctx_kernel(q_ref, k_ref, v_ref, lam_ref, g_ref, o_ref, *, lam_init):
    q = q_ref[...].astype(F32) * QK_SCALE
    lam = _diff_lambda(lam_ref, lam_init)
    o_ref[...] = _attend(q, k_ref[...], v_ref[...], lam, g_ref[...], lam_init).astype(o_ref.dtype)


def _attn_lat(p, kv_ctx, kc_blk, vc_blk, ctx_rb0, cos, sin, lam4, subg, *, batch, seq, n_ctx, lam_init, tq=512):
    nq = seq // tq
    n_keys = n_ctx + seq
    hb = lambda off: off // HEAD_W
    qblk = lambda i: jnp.minimum(i, nq - 1)
    kernel = functools.partial(_attn_lat_kernel, n_ctx=n_ctx, lam_init=lam_init, n_blocks=nq)
    return pl.pallas_call(
        kernel,
        out_shape=jax.ShapeDtypeStruct((batch * seq, ATTN_W), BF16),
        grid=(batch, N_HEADS, nq + 1),
        in_specs=[
            pl.BlockSpec((tq, HEAD_W), lambda b, h, i: (b * nq + qblk(i), hb(OFF_Q) + h)),
            pl.BlockSpec((seq, HEAD_W), lambda b, h, i: (b, hb(OFF_K) + h)),
            pl.BlockSpec((seq, HEAD_W), lambda b, h, i: (b, hb(OFF_V) + h)),
            pl.BlockSpec((n_ctx, HEAD_W), lambda b, h, i: (ctx_rb0 + b, kc_blk + h)),
            pl.BlockSpec((n_ctx, HEAD_W), lambda b, h, i: (ctx_rb0 + b, vc_blk + h)),
            pl.BlockSpec((tq, HEAD_W), lambda b, h, i: (qblk(i), 0)),
            pl.BlockSpec((tq, HEAD_W), lambda b, h, i: (qblk(i), 0)),
            pl.BlockSpec((seq, HEAD_W), lambda b, h, i: (0, 0)),
            pl.BlockSpec((seq, HEAD_W), lambda b, h, i: (0, 0)),
            pl.BlockSpec((4, HEAD_DIM), lambda b, h, i: (0, 0)),
            pl.BlockSpec((1, HEAD_W), lambda b, h, i: (0, 0)),
        ],
        out_specs=pl.BlockSpec((tq, HEAD_W), lambda b, h, i: (b * nq + jnp.maximum(i - 1, 0), h)),
        scratch_shapes=[pltpu.VMEM((n_keys, HEAD_W), BF16), pltpu.VMEM((n_keys, 2 * HEAD_W), BF16),
                        pltpu.VMEM((2 * tq, n_keys), F32), pltpu.VMEM((2 * tq, LANES), F32),
                        pltpu.VMEM((2 * tq, n_keys), F32), pltpu.VMEM((2 * tq, LANES), F32)],
        compiler_params=_params(("arbitrary", "arbitrary", "arbitrary")),
        name="attn_lat",
    )(p, p, p, kv_ctx, kv_ctx, cos, sin, cos, sin, lam4, subg.reshape(1, HEAD_W))


def _attn_ctx(p, ctx_rb0, lam4, subg, *, batch, n_ctx, lam_init):
    hb = lambda off: off // HEAD_W
    kernel = functools.partial(_attn_ctx_kernel, lam_init=lam_init)
    return pl.pallas_call(
        kernel,
        out_shape=jax.ShapeDtypeStruct((batch * n_ctx, ATTN_W), BF16),
        grid=(batch, N_HEADS),
        in_specs=[
            pl.BlockSpec((n_ctx, HEAD_W), lambda b, h: (ctx_rb0 + b, hb(OFF_Q) + h)),
            pl.BlockSpec((n_ctx, HEAD_W), lambda b, h: (ctx_rb0 + b, hb(OFF_K) + h)),
            pl.BlockSpec((n_ctx, HEAD_W), lambda b, h: (ctx_rb0 + b, hb(OFF_V) + h)),
            pl.BlockSpec((4, HEAD_DIM), lambda b, h: (0, 0)),
            pl.BlockSpec((1, HEAD_W), lambda b, h: (0, 0)),
        ],
        out_specs=pl.BlockSpec((n_ctx, HEAD_W), lambda b, h: (b, h)),
        compiler_params=_params(("arbitrary", "arbitrary")),
        name="attn_ctx",
    )(p, p, p, lam4, subg.reshape(1, HEAD_W))


def _conv_kernel(bg_ref, cg_ref, xi_ref, a_ref, g_ref,
                 cg_p, xi_p, a_p, g_p, cg_n, xi_n, a_n, g_n,
                 scw_ref, cfw_ref, cfb_ref, lng_ref, lnb_ref,
                 sco_ref, cfo_ref, pad_ref, *, lat_blocks, blocks_per_seq):
    i = pl.program_id(0)
    in_lat = i < lat_blocks
    pos = i % blocks_per_seq
    has_prev = jnp.logical_and(in_lat, pos != 0)
    has_next = jnp.logical_and(in_lat, pos != blocks_per_seq - 1)
    keep_prev = jnp.where(has_prev, 1.0, 0.0)
    keep_next = jnp.where(has_next, 1.0, 0.0)
    lo, hi = CONV_HALO, CONV_HALO + CONV_ROWS

    def fill(main, prev, nxt):
        pad_ref[0:lo, :] = prev * keep_prev
        pad_ref[lo:hi, :] = main
        pad_ref[hi:hi + CONV_HALO, :] = nxt * keep_next

    def f(ref):
        return ref[...].astype(F32)

    def glu(a, g):
        return a * _sigmoid(g)

    fill(f(cg_ref) * f(xi_ref), f(cg_p) * f(xi_p), f(cg_n) * f(xi_n))
    for c in range(CONV_ROWS // CONV_CHUNK):
        r0 = lo + c * CONV_CHUNK - SC_TAPS // 2
        acc = scw_ref[0:1, :] * pad_ref[r0:r0 + CONV_CHUNK, :]
        for k in range(1, SC_TAPS):
            acc = acc + scw_ref[k:k + 1, :] * pad_ref[r0 + k:r0 + k + CONV_CHUNK, :]
        rows = slice(c * CONV_CHUNK, (c + 1) * CONV_CHUNK)
        sco_ref[rows, :] = (bg_ref[rows, :].astype(F32) * acc).astype(sco_ref.dtype)

    fill(glu(f(a_ref), f(g_ref)), glu(f(a_p), f(g_p)), glu(f(a_n), f(g_n)))
    off = lo - CF_TAPS // 2
    win = CONV_CHUNK + SUBLANES
    for c in range(CONV_ROWS // CONV_CHUNK):
        base = c * CONV_CHUNK
        acc = None
        for b in range(SUBLANES):
            q = None
            for k in range(b, CF_TAPS, SUBLANES):
                term = cfw_ref[k:k + 1, :] * pad_ref[base + k - b:base + k - b + win, :]
                q = term if q is None else q + term
            part = q[off + b:off + b + CONV_CHUNK, :]
            acc = part if acc is None else acc + part
        z = acc + cfb_ref[...]
        mu = jnp.mean(z, axis=-1, keepdims=True)
        zc = z - mu
        var = jnp.mean(zc * zc, axis=-1, keepdims=True)
        y = zc * lax.rsqrt(var + CF_LN_EPS) * lng_ref[...] + lnb_ref[...]
        rows = slice(c * CONV_CHUNK, (c + 1) * CONV_CHUNK)
        cfo_ref[rows, :] = (y * _sigmoid(y)).astype(cfo_ref.dtype)


def _convs(p, sc_w, cf_w, cf_b, ln_g, ln_b, *, n_rows, n_lat, seq):
    nb = n_rows // CONV_ROWS
    halo_per_block = CONV_ROWS // CONV_HALO
    last_halo = n_rows // CONV_HALO - 1
    cb = lambda off: off // SC_W

    def main(off):
        return pl.BlockSpec((CONV_ROWS, SC_W), lambda i: (i, cb(off)))

    def prev(off):
        return pl.BlockSpec((CONV_HALO, SC_W), lambda i: (jnp.maximum(i * halo_per_block - 1, 0), cb(off)))

    def nxt(off):
        return pl.BlockSpec((CONV_HALO, SC_W), lambda i: (jnp.minimum((i + 1) * halo_per_block, last_halo), cb(off)))

    def vec(rows):
        return pl.BlockSpec((rows, SC_W), lambda i: (0, 0))

    o_bg, o_cg, o_xi, o_a, o_g = OFF_SC, OFF_SC + SC_W, OFF_SC + 2 * SC_W, OFF_CF, OFF_CF + CF_W
    kernel = functools.partial(_conv_kernel, lat_blocks=n_lat // CONV_ROWS, blocks_per_seq=seq // CONV_ROWS)
    return pl.pallas_call(
        kernel,
        out_shape=(jax.ShapeDtypeStruct((n_rows, SC_W), BF16), jax.ShapeDtypeStruct((n_rows, CF_W), BF16)),
        grid=(nb,),
        in_specs=[main(o_bg), main(o_cg), main(o_xi), main(o_a), main(o_g),
                  prev(o_cg), prev(o_xi), prev(o_a), prev(o_g),
                  nxt(o_cg), nxt(o_xi), nxt(o_a), nxt(o_g),
                  vec(SC_TAPS), vec(CF_TAPS), vec(1), vec(1), vec(1)],
        out_specs=(pl.BlockSpec((CONV_ROWS, SC_W), lambda i: (i, 0)),
                   pl.BlockSpec((CONV_ROWS, CF_W), lambda i: (i, 0))),
        scratch_shapes=[pltpu.VMEM((CONV_ROWS + 2 * CONV_HALO, SC_W), F32)],
        compiler_params=_params(("arbitrary",)),
        name="convs",
    )(p, p, p, p, p, p, p, p, p, p, p, p, p,
      sc_w, cf_w, cf_b.reshape(1, CF_W), ln_g.reshape(1, CF_W), ln_b.reshape(1, CF_W))


def _merge_kernel(*refs, a_blocks):
    n_gate = 3 * MERGE_GATE_TILES
    ata_ref, atb_ref, sc_ref, cf_ref = refs[:4]
    gate_refs = refs[4:4 + n_gate]
    bg_ref, wa_ref, wb_ref, wc_ref, wm_ref, ha_ref, hb_ref, g1_ref, o_ref = refs[4 + n_gate:]
    d = D_MODEL
    from_a = pl.program_id(0) < a_blocks
    attn = jnp.where(from_a, ata_ref[...], atb_ref[...])
    h = jnp.where(from_a, ha_ref[...], hb_ref[...])

    def gate(branch):
        tiles = gate_refs[branch * MERGE_GATE_TILES:(branch + 1) * MERGE_GATE_TILES]
        pre = jnp.concatenate([t[...] for t in tiles], axis=1).astype(F32)
        return _sigmoid(pre + bg_ref[:, branch * d:(branch + 1) * d])

    ya = jnp.dot(attn, wa_ref[...], preferred_element_type=F32)
    yb = jnp.dot(sc_ref[...], wb_ref[...], preferred_element_type=F32)
    yc = jnp.dot(cf_ref[...], wc_ref[...], preferred_element_type=F32)
    m = (gate(0) * ya + gate(1) * yb + gate(2) * yc).astype(BF16)
    o_ref[...] = h + g1_ref[...] * jnp.dot(m, wm_ref[...], preferred_element_type=F32)


def _merge(attn, sc, cf, p, b_gate, wa, wb, wc, wm, rows, mods, *, n_rows, n_lat, seq, gate_idx, tm=256):
    d = D_MODEL
    (ata, atb), (ha, hb) = attn, rows
    a_blocks = ha.shape[0] // tm
    first = lambda w: pl.BlockSpec((tm, w), lambda i: (jnp.minimum(i, a_blocks - 1), 0))
    rest = lambda w: pl.BlockSpec((tm, w), lambda i: (jnp.maximum(i - a_blocks, 0), 0))
    tg = d // MERGE_GATE_TILES
    gb0 = OFF_GATE // tg
    gates = [pl.BlockSpec((tm, tg), functools.partial(lambda i, c: (i, c), c=gb0 + t))
             for t in range(3 * MERGE_GATE_TILES)]
    whole = lambda a: pl.BlockSpec(a.shape, lambda i: (0, 0))
    return pl.pallas_call(
        functools.partial(_merge_kernel, a_blocks=a_blocks),
        out_shape=jax.ShapeDtypeStruct((n_rows, d), F32),
        grid=(n_rows // tm,),
        in_specs=[
            first(ATTN_W), rest(ATTN_W),
            pl.BlockSpec((tm, SC_W), lambda i: (i, 0)),
            pl.BlockSpec((tm, CF_W), lambda i: (i, 0)),
            *gates,
            whole(b_gate), whole(wa), whole(wb), whole(wc), whole(wm),
            first(d), rest(d),
            pl.BlockSpec((None, 1, d), lambda i: (_mod_row(i * tm, n_lat, seq), 0, gate_idx)),
        ],
        out_specs=pl.BlockSpec((tm, d), lambda i: (i, 0)),
        compiler_params=_params(("arbitrary",)),
        name="merge",
    )(ata, atb, sc, cf, *([p] * (3 * MERGE_GATE_TILES)), b_gate, wa, wb, wc, wm, ha, hb, mods)


def _split2(x):
    hi = x.astype(BF16)
    return hi, (x - hi.astype(F32)).astype(BF16)


def _norm_mod(x, g, sh, sc):
    ms = jnp.mean(x * x, axis=-1, keepdims=True)
    return x * lax.rsqrt(ms + NORM_EPS) * g * (1.0 + sc) + sh


def _router_kernel(h_ref, g_ref, sh_ref, sc_ref, rw_ref, rb_ref, id_ref, wt_ref):
    f = _norm_mod(h_ref[...], g_ref[...], sh_ref[...], sc_ref[...])

    f_hi, f_lo = _split2(f)
    w_hi, w_lo = _split2(rw_ref[...])
    dot = lambda a, b: jnp.dot(a, b, preferred_element_type=F32)
    logits = (dot(f_lo, w_hi) + dot(f_hi, w_lo) + dot(f_hi, w_hi)) + rb_ref[...]

    lane_i = lax.broadcasted_iota(I32, logits.shape, 1)
    lane = lane_i.astype(F32)
    neg = -jnp.inf
    big = float(LANES)
    lg = jnp.where(lane_i < N_GROUPS, logits, neg)
    mg = jnp.max(lg, axis=-1, keepdims=True)
    grp = jnp.min(jnp.where(lg == mg, lane, big), axis=-1, keepdims=True)
    p_grp = 1.0 / jnp.sum(jnp.exp(lg - mg), axis=-1, keepdims=True)

    e_lane = lane - N_GROUPS
    in_grp = jnp.logical_and(e_lane >= grp * EXPERTS_PER_GROUP, e_lane < (grp + 1.0) * EXPERTS_PER_GROUP)
    le = jnp.where(in_grp, logits, neg)
    m1 = jnp.max(le, axis=-1, keepdims=True)
    i1 = jnp.min(jnp.where(le == m1, lane, big), axis=-1, keepdims=True)
    le2 = jnp.where(lane == i1, neg, le)
    m2 = jnp.max(le2, axis=-1, keepdims=True)
    i2 = jnp.min(jnp.where(le2 == m2, lane, big), axis=-1, keepdims=True)
    e2 = jnp.exp(m2 - m1)
    w1 = p_grp / (1.0 + e2)
    w2 = p_grp * e2 / (1.0 + e2)
    ids = jnp.where(lane_i == 0, i1 - N_GROUPS, jnp.where(lane_i == 1, i2 - N_GROUPS, 0.0))
    id_ref[...] = ids.astype(I32)
    wt_ref[...] = jnp.where(lane_i == 0, w1, jnp.where(lane_i == 1, w2, 0.0))


def _router(h, gain, mods, rw, rb, *, n_rows, n_lat, seq, sh_idx, sc_idx, tm=512):
    d = D_MODEL

    def mod_map(idx):
        return lambda i: (_mod_row(i * tm, n_lat, seq), 0, idx)

    return pl.pallas_call(
        _router_kernel,
        out_shape=(jax.ShapeDtypeStruct((n_rows, LANES), I32),
                   jax.ShapeDtypeStruct((n_rows, LANES), F32)),
        grid=(n_rows // tm,),
        in_specs=[
            pl.BlockSpec((tm, d), lambda i: (i, 0)),
            pl.BlockSpec((1, d), lambda i: (0, 0)),
            pl.BlockSpec((None, 1, d), mod_map(sh_idx)),
            pl.BlockSpec((None, 1, d), mod_map(sc_idx)),
            pl.BlockSpec((d, LANES), lambda i: (0, 0)),
            pl.BlockSpec((1, LANES), lambda i: (0, 0)),
        ],
        out_specs=(pl.BlockSpec((tm, LANES), lambda i: (i, 0)),
                   pl.BlockSpec((tm, LANES), lambda i: (i, 0))),
        compiler_params=_params(("arbitrary",)),
        name="router",
    )(h, gain.reshape(1, d), mods, mods, rw, rb)


def _pack_rows(x):
    c = x.shape[1] // 2
    lo = lax.bitcast_convert_type(x[:, :c].astype(BF16).astype(F32), U32) >> 16
    hi = lax.bitcast_convert_type(x[:, c:].astype(BF16).astype(F32), U32) & jnp.uint32(0xFFFF0000)
    return hi | lo


def _unpack_rows(p):
    return (lax.bitcast_convert_type(p << 16, F32),
            lax.bitcast_convert_type(p & jnp.uint32(0xFFFF0000), F32))


def _dispatch_kernel(pos_ref, h_ref, g_ref, sh_ref, sc_ref, xs_hbm, f0_ref, f1_ref, sem):
    i = pl.program_id(0)
    tm = h_ref.shape[0]

    def run(f_ref):
        def wait_block_copies():
            for _ in range(2):
                pltpu.make_async_copy(f_ref, xs_hbm.at[pl.ds(0, tm), :], sem).wait()

        f_ref[...] = _pack_rows(_norm_mod(h_ref[...], g_ref[...], sh_ref[...], sc_ref[...]))

        @pl.when(i > 0)
        def _():
            wait_block_copies()

        base = i * (2 * tm)
        for r in range(tm):
            for k in range(2):
                dst = pos_ref[base + 2 * r + k]
                pltpu.make_async_copy(f_ref.at[pl.ds(r, 1), :], xs_hbm.at[pl.ds(dst, 1), :], sem).start()

        @pl.when(i == pl.num_programs(0) - 1)
        def _():
            wait_block_copies()

    @pl.when(i % 2 == 0)
    def _():
        run(f0_ref)

    @pl.when(i % 2 == 1)
    def _():
        run(f1_ref)


def _dispatch(h, gain, mods, pos, *, n_rows, n_lat, seq, sh_idx, sc_idx, tm=SLOT_BLOCK_ROWS):
    d = D_MODEL

    def mod_map(idx):
        return lambda i, pos_ref: (_mod_row(i * tm, n_lat, seq), 0, idx)

    grid_spec = pltpu.PrefetchScalarGridSpec(
        num_scalar_prefetch=1,
        grid=(n_rows // tm,),
        in_specs=[
            pl.BlockSpec((tm, d), lambda i, pos_ref: (i, 0)),
            pl.BlockSpec((1, d), lambda i, pos_ref: (0, 0)),
            pl.BlockSpec((None, 1, d), mod_map(sh_idx)),
            pl.BlockSpec((None, 1, d), mod_map(sc_idx)),
        ],
        out_specs=pl.BlockSpec(memory_space=pl.ANY),
        scratch_shapes=[pltpu.VMEM((tm, d // 2), U32), pltpu.VMEM((tm, d // 2), U32), pltpu.SemaphoreType.DMA],
    )
    return pl.pallas_call(
        _dispatch_kernel,
        out_shape=jax.ShapeDtypeStruct((2 * n_rows, d // 2), U32),
        grid_spec=grid_spec,
        compiler_params=_params(("arbitrary",)),
        name="dispatch",
    )(pos, h, gain.reshape(1, d), mods, mods)


def _moe_plan(ids, n_tok):
    n_items = N_EXPERTS + (2 * n_tok) // MOE_ITEM_CAP
    e_flat = ids[:, :2].reshape(-1)
    order = jnp.argsort(e_flat, stable=True).astype(I32)
    pos = jnp.argsort(order).astype(I32)
    pos = jnp.concatenate([pos, jnp.zeros((2 * SLOT_BLOCK_ROWS,), I32)])
    counts = jnp.sum((e_flat[:, None] == jnp.arange(N_EXPERTS, dtype=I32)[None, :]).astype(I32), axis=0)
    starts = jnp.cumsum(counts) - counts
    n_sb = (counts + MOE_ITEM_CAP - 1) // MOE_ITEM_CAP
    cum = jnp.cumsum(n_sb)
    total = cum[-1]
    it = jnp.arange(n_items, dtype=I32)
    e_i = jnp.minimum(jnp.searchsorted(cum, it, side="right").astype(I32), N_EXPERTS - 1)
    sb = it - (cum[e_i] - n_sb[e_i])
    valid = it < total
    last_e = e_i[jnp.maximum(total - 1, 0)]
    item_e = jnp.where(valid, e_i, last_e).astype(I32)
    item_start = jnp.where(valid, starts[e_i] + sb * MOE_ITEM_CAP, 0).astype(I32)
    item_n = jnp.where(valid, jnp.minimum(MOE_ITEM_CAP, counts[e_i] - sb * MOE_ITEM_CAP), 0).astype(I32)
    front, back = jnp.zeros((MOE_ITEM_SHIFT,), I32), jnp.zeros((1,), I32)
    pad_items = lambda a: jnp.concatenate([front, a, back])
    return (pad_items(item_e), pad_items(item_start), pad_items(item_n)), pos


def _moe_kernel(ie_ref, ist_ref, in_ref,
                xs_hbm, wgu_ref, wd_ref, ys_hbm,
                xf_ref, xb_ref, gu_ref, hm_ref, y_ref, yp_ref, gsem, ssem):
    i, j = pl.program_id(0), pl.program_id(1)
    sh = MOE_ITEM_SHIFT
    n_prev, n, n_next = in_ref[i + sh - 1], in_ref[i + sh], in_ref[i + sh + 1]
    start_prev, start, start_next = ist_ref[i + sh - 1], ist_ref[i + sh], ist_ref[i + sh + 1]
    kc_w = D_MODEL // MOE_GU_STEPS
    hc_w = EXPERT_HIDDEN // MOE_DOWN_STEPS

    def rows_in(hrow, vrow, size):
        return pltpu.make_async_copy(xs_hbm.at[pl.ds(hrow, size), :], xf_ref.at[pl.ds(vrow, size), :], gsem)

    def rows_out(hrow, vrow, size):
        return pltpu.make_async_copy(yp_ref.at[pl.ds(vrow, size), :], ys_hbm.at[pl.ds(hrow, size), :], ssem)

    def row_groups(make, base, count, act):
        shift = base % SUBLANES
        head = jnp.minimum((SUBLANES - shift) % SUBLANES, count)
        body = ((count - head) // SUBLANES) * SUBLANES
        for t in range(SUBLANES - 1):
            @pl.when(t < head)
            def _():
                getattr(make(base + t, shift + t, 1), act)()

        for b in reversed(range(SUBLANES.bit_length() - 1, MOE_ITEM_ROWS.bit_length())):
            @pl.when(((body >> b) & 1) == 1)
            def _():
                off = head + ((body >> (b + 1)) << (b + 1))
                getattr(make(pl.multiple_of(base + off, SUBLANES), pl.multiple_of(shift + off, SUBLANES), 1 << b),
                        act)()

        done = head + body
        for t in range(SUBLANES - 1):
            @pl.when(t < count - done)
            def _():
                getattr(make(base + done + t, shift + done + t, 1), act)()

    @pl.when(jnp.logical_and(i == 0, j == 0))
    def _():
        xf_ref[...] = jnp.zeros_like(xf_ref)
        gu_ref[...] = jnp.zeros_like(gu_ref)
        y_ref[...] = jnp.zeros_like(y_ref)

    @pl.when(j == 0)
    def _():
        @pl.when(i == 0)
        def _():
            row_groups(rows_in, start, n, "start")

        @pl.when(n > 0)
        def _():
            row_groups(rows_in, start, n, "wait")
            halves = _unpack_rows(xf_ref[...])
            per_half = MOE_GU_STEPS // 2
            for c in range(MOE_GU_STEPS):
                col = (c % per_half) * kc_w
                xb_ref[c] = halves[c // per_half][:, col:col + kc_w].astype(BF16)
            row_groups(rows_in, start_next, n_next, "start")

    @pl.when(j == MOE_GU_STEPS)
    def _():
        row_groups(rows_out, start_prev, n_prev, "wait")

    used = jnp.where(n > 0, start % SUBLANES + n, 0)
    n_pad = (used + MOE_ROW_PAD - 1) // MOE_ROW_PAD

    def row_blocks(fn):
        def body(t, c):
            fn(pl.ds(pl.multiple_of(t * MOE_ROW_PAD, MOE_ROW_PAD), MOE_ROW_PAD))
            return c

        lax.fori_loop(0, n_pad, body, 0)

    @pl.when(j == MOE_GU_STEPS)
    def _():
        def activate(rows):
            gu = gu_ref[rows, :]
            g, u = gu[:, 0:EXPERT_HIDDEN], gu[:, EXPERT_HIDDEN:]
            hidden = (g * _sigmoid(g) * u).astype(BF16)
            for c in range(MOE_DOWN_STEPS):
                hm_ref[c, rows, :] = hidden[:, c * hc_w:(c + 1) * hc_w]

        row_blocks(activate)

    def gate_up(m):
        part = jnp.dot(xb_ref[j, 0:m, :], wgu_ref[...].astype(BF16), preferred_element_type=F32)
        gu_ref[0:m, :] = part + jnp.where(j == 0, 0.0, gu_ref[0:m, :])

    def down(m):
        c = j - MOE_GU_STEPS
        part = jnp.dot(hm_ref[c, 0:m, :], wd_ref[...].astype(BF16), preferred_element_type=F32)
        y_ref[0:m, :] = part + jnp.where(c == 0, 0.0, y_ref[0:m, :])

    for k in range(1, MOE_ITEM_ROWS // MOE_ROW_PAD + 1):
        @pl.when(jnp.logical_and(n_pad == k, j < MOE_GU_STEPS))
        def _():
            gate_up(k * MOE_ROW_PAD)

        @pl.when(jnp.logical_and(n_pad == k, j >= MOE_GU_STEPS))
        def _():
            down(k * MOE_ROW_PAD)

    @pl.when(j == pl.num_programs(1) - 1)
    def _():
        def pack(rows):
            yp_ref[rows, :] = _pack_rows(y_ref[rows, :])

        row_blocks(pack)
        row_groups(rows_out, start, n, "start")

        @pl.when(i == pl.num_programs(0) - 1)
        def _():
            row_groups(rows_out, start, n, "wait")


def _moe(xs, items, w_gu, w_down, layer):
    item_e, item_start, item_n = items
    d, hid = D_MODEL, EXPERT_HIDDEN
    n_gu, n_dn = MOE_GU_STEPS, MOE_DOWN_STEPS
    sh = MOE_ITEM_SHIFT
    n_items = item_e.shape[0] - sh - 1

    def gu_map(i, j, ie, ist, nn):
        busy = nn[i + sh] > 0
        return layer, ie[i + sh], jnp.where(busy, jnp.minimum(j, n_gu - 1), n_gu - 1), 0

    def dn_map(i, j, ie, ist, nn):
        now = jnp.logical_and(nn[i + sh] > 0, j >= n_gu)
        return (layer, jnp.where(now, ie[i + sh], ie[i + sh - 1]),
                jnp.where(now, j - n_gu, n_dn - 1), 0)

    grid_spec = pltpu.PrefetchScalarGridSpec(
        num_scalar_prefetch=3,
        grid=(n_items, n_gu + n_dn),
        in_specs=[
            pl.BlockSpec(memory_space=pl.ANY),
            pl.BlockSpec((None, None, d // n_gu, 2 * hid), gu_map),
            pl.BlockSpec((None, None, hid // n_dn, d), dn_map),
        ],
        out_specs=pl.BlockSpec(memory_space=pl.ANY),
        scratch_shapes=[
            pltpu.VMEM((MOE_ITEM_ROWS, d // 2), U32),
            pltpu.VMEM((n_gu, MOE_ITEM_ROWS, d // n_gu), BF16),
            pltpu.VMEM((MOE_ITEM_ROWS, 2 * hid), F32),
            pltpu.VMEM((n_dn, MOE_ITEM_ROWS, hid // n_dn), BF16),
            pltpu.VMEM((MOE_ITEM_ROWS, d), F32),
            pltpu.VMEM((MOE_ITEM_ROWS, d // 2), U32),
            pltpu.SemaphoreType.DMA,
            pltpu.SemaphoreType.DMA,
        ],
    )
    return pl.pallas_call(
        _moe_kernel,
        out_shape=jax.ShapeDtypeStruct(xs.shape, U32),
        grid_spec=grid_spec,
        compiler_params=_params(("arbitrary", "arbitrary")),
        name="moe_experts",
    )(item_e, item_start, item_n, xs, w_gu, w_down)


def _combine_kernel(pos_ref, h_ref, wt_ref, g2_ref, fg_ref, ys_hbm, o_ref, ybuf, sem, *, final):
    i = pl.program_id(0)
    tm = h_ref.shape[0]
    par = i % 2

    def fetch(step, half):
        base = step * (2 * tm)
        for r in range(tm):
            for k in range(2):
                src = pos_ref[base + 2 * r + k]
                pltpu.make_async_copy(ys_hbm.at[pl.ds(src, 1), :], ybuf.at[half, k, pl.ds(r, 1), :],
                                      sem.at[half]).start()

    @pl.when(i == 0)
    def _():
        fetch(0, 0)

    @pl.when(i < pl.num_programs(0) - 1)
    def _():
        fetch(i + 1, 1 - par)

    for k in range(2):
        pltpu.make_async_copy(ys_hbm.at[pl.ds(0, tm), :], ybuf.at[par, k], sem.at[par]).wait()
    wt = wt_ref[...]
    y0, y1 = _unpack_rows(ybuf[par, 0]), _unpack_rows(ybuf[par, 1])
    half = o_ref.shape[1] // 2
    cols = (slice(0, half), slice(half, 2 * half))
    hs = [h_ref[:, c] + g2_ref[:, c] * (wt[:, 0:1] * y0[s] + wt[:, 1:2] * y1[s]) for s, c in enumerate(cols)]
    if final:
        ms = (jnp.sum(hs[0] * hs[0], axis=-1, keepdims=True)
              + jnp.sum(hs[1] * hs[1], axis=-1, keepdims=True)) * (1.0 / (2 * half))
        scale = lax.rsqrt(ms + NORM_EPS)
        hs = [hs[s] * scale * fg_ref[:, c] for s, c in enumerate(cols)]
    for s, c in enumerate(cols):
        o_ref[:, c] = hs[s]


def _combine(h, ys, pos, wts, mods, final_g, *, n_rows, n_lat, seq, gate_idx, final, tm=SLOT_BLOCK_ROWS):
    d = D_MODEL
    kernel = functools.partial(_combine_kernel, final=final)
    grid_spec = pltpu.PrefetchScalarGridSpec(
        num_scalar_prefetch=1,
        grid=(n_rows // tm,),
        in_specs=[
            pl.BlockSpec((tm, d), lambda i, pos_ref: (i, 0)),
            pl.BlockSpec((tm, LANES), lambda i, pos_ref: (i, 0)),
            pl.BlockSpec((None, 1, d), lambda i, pos_ref: (_mod_row(i * tm, n_lat, seq), 0, gate_idx)),
            pl.BlockSpec((1, d), lambda i, pos_ref: (0, 0)),
            pl.BlockSpec(memory_space=pl.ANY),
        ],
        out_specs=pl.BlockSpec((tm, d), lambda i, pos_ref: (i, 0)),
        scratch_shapes=[pltpu.VMEM((2, 2, tm, d // 2), U32), pltpu.SemaphoreType.DMA((2,))],
    )
    return pl.pallas_call(
        kernel,
        out_shape=jax.ShapeDtypeStruct((n_rows, d), F32),
        grid_spec=grid_spec,
        compiler_params=_params(("arbitrary",)),
        name="combine",
    )(pos, h, wts, mods, final_g.reshape(1, d), ys)


def kernel(x, c, ctx, c_ctx, ada_w, ada_b, norm1_g, w_in, b_gate, diff_lambda, subln_g, w_attn_out, sc_conv_w,
           w_sc_out, cf_dw_w, cf_dw_b, cf_ln_g, cf_ln_b, w_cf_out, w_mix, norm2_g, router_g_w, router_g_b,
           router_e_w, router_e_b, exp_w_gu, exp_w_down, final_g):
    batch, seq, d = x.shape
    n_ctx = ctx.shape[1]
    depth = ada_w.shape[0]
    n_lat = batch * seq
    n_all = n_lat + batch * n_ctx
    assert d == D_MODEL and batch == MOD_ROWS // 2 and w_in.shape[2] == C_TOT

    cond = jnp.concatenate([c, c_ctx[None, :], jnp.zeros((MOD_ROWS - batch - 1, d), F32)], axis=0)
    mods_all = _adaln(cond, ada_w, ada_b)
    cos, sin = _rope_tables(seq)
    rows = (x.reshape(n_lat, d), ctx.reshape(batch * n_ctx, d))
    geo = dict(n_lat=n_lat, seq=seq)

    for layer in range(depth):
        last = layer == depth - 1
        lam_init = 0.8 - 0.6 * math.exp(-0.3 * layer)
        mods = mods_all[layer].reshape(MOD_ROWS, 1, 6 * d)
        n_rows = n_lat if last else n_all

        p = _in_proj(rows, norm1_g[layer], mods, w_in, layer, row0=0, n_rows=n_rows, col0=0, n_cols=C_TOT,
                     sh_idx=0, sc_idx=1, tm=1024, **geo)
        if last:
            kv_ctx = _in_proj(rows, norm1_g[layer], mods, w_in, layer, row0=n_lat, n_rows=batch * n_ctx, col0=OFF_K,
                              n_cols=OFF_SC - OFF_K, sh_idx=0, sc_idx=1, tm=n_ctx, **geo)
            kc_blk, vc_blk, ctx_rb0 = 0, QK_W // HEAD_W, 0
        else:
            kv_ctx = p
            kc_blk, vc_blk, ctx_rb0 = OFF_K // HEAD_W, OFF_V // HEAD_W, n_lat // n_ctx
        attn = _attn_lat(p, kv_ctx, kc_blk, vc_blk, ctx_rb0, cos, sin, diff_lambda[layer], subln_g[layer],
                         batch=batch, seq=seq, n_ctx=n_ctx, lam_init=lam_init)
        if not last:
            attn = (attn, _attn_ctx(p, n_lat // n_ctx, diff_lambda[layer], subln_g[layer],
                                    batch=batch, n_ctx=n_ctx, lam_init=lam_init))
        else:
            attn = (attn, attn)
        sc, cf = _convs(p, sc_conv_w[layer], cf_dw_w[layer], cf_dw_b[layer], cf_ln_g[layer], cf_ln_b[layer],
                        n_rows=n_rows, **geo)
        h_mix = _merge(attn, sc, cf, p, b_gate[layer].reshape(1, 3 * d),
                       w_attn_out[layer].astype(BF16), w_sc_out[layer].astype(BF16),
                       w_cf_out[layer].astype(BF16), w_mix[layer].astype(BF16), rows, mods,
                       n_rows=n_rows, gate_idx=2, **geo)

        rw = jnp.concatenate([router_g_w[layer], router_e_w[layer],
                              jnp.zeros((d, LANES - N_GROUPS - N_EXPERTS), F32)], axis=1)
        rb = jnp.concatenate([router_g_b[layer], router_e_b[layer],
                              jnp.zeros((LANES - N_GROUPS - N_EXPERTS,), F32)]).reshape(1, LANES)
        ids, wts = _router(h_mix, norm2_g[layer], mods, rw, rb, n_rows=n_rows, sh_idx=3, sc_idx=4, **geo)
        items, pos = _moe_plan(ids, n_rows)
        xs = _dispatch(h_mix, norm2_g[layer], mods, pos, n_rows=n_rows, sh_idx=3, sc_idx=4, **geo)
        ys = _moe(xs, items, exp_w_gu, exp_w_down, layer)
        h = _combine(h_mix, ys, pos, wts, mods, final_g, n_rows=n_rows, gate_idx=5, final=last, **geo)
        rows = (h, h)

    return h.reshape(batch, seq, d)
```

```python
import functools
import math

import jax
import jax.numpy as jnp
from jax import lax
from jax.experimental import pallas as pl
from jax.experimental.pallas import tpu as pltpu

F32 = jnp.float32
BF16 = jnp.bfloat16
I32 = jnp.int32

D_MODEL = 2048
GRID_W = 64
NORM_EPS = 1e-6
N_HEADS = 8
HEAD_DIM = 64
HEAD_W = 2 * HEAD_DIM
QK_W = N_HEADS * HEAD_W
ATTN_W = N_HEADS * HEAD_W
ROPE_BASE = 10000.0
ROPE_FREQS = HEAD_DIM // 4
SUBLN_EPS = 1e-5
QK_SCALE = HEAD_DIM ** -0.5 * math.log2(math.e)
SC_W = D_MODEL // 4
CF_W = D_MODEL // 4
SC_TAPS = 3
CF_TAPS = 31
CF_LN_EPS = 1e-5
OFF_Q = 0
OFF_K = OFF_Q + QK_W
OFF_V = OFF_K + QK_W
OFF_SC = OFF_V + ATTN_W
OFF_CF = OFF_SC + 3 * SC_W
OFF_GATE = OFF_CF + 2 * CF_W
C_TOT = OFF_GATE + 3 * D_MODEL
N_GROUPS = 4
EXPERTS_PER_GROUP = 8
N_EXPERTS = N_GROUPS * EXPERTS_PER_GROUP
EXPERT_HIDDEN = D_MODEL // 2

LANES = 128
SUBLANES = 8
MOD_ROWS = 8
MERGE_GATE_TILES = 4
CONV_HALO = 16
CONV_ROWS = 256
CONV_CHUNK = 32
MOE_ITEM_ROWS = 1024
MOE_ITEM_CAP = MOE_ITEM_ROWS - 8
MOE_ROW_PAD = 128
MOE_HIDDEN_BLOCK = 256
MOE_ITEM_SHIFT = 1
SLOT_BLOCK_ROWS = 256
VMEM_LIMIT = 56 * 1024 * 1024


def _params(sem, vmem=VMEM_LIMIT):
    return pltpu.CompilerParams(dimension_semantics=sem, vmem_limit_bytes=vmem)


def _sigmoid(x):
    return 1.0 / (1.0 + jnp.exp(-x))


def _adaln_kernel(s_ref, w_ref, b_ref, o_ref):
    s = s_ref[...]
    s = s * _sigmoid(s)
    s_hi = s.astype(BF16)
    w = w_ref[...]
    w_hi = w.astype(BF16)
    w_lo = (w - w_hi.astype(F32)).astype(BF16)
    lhs = jnp.concatenate([s_hi.astype(F32), s - s_hi.astype(F32)], axis=0).astype(BF16)
    r = jnp.dot(lhs, w_hi, preferred_element_type=F32)
    r2 = jnp.dot(s_hi, w_lo, preferred_element_type=F32)
    o_ref[...] = r[:MOD_ROWS] + r[MOD_ROWS:] + r2 + b_ref[...]


def _adaln(cond, ada_w, ada_b):
    n_layers, d, n = ada_w.shape
    tn = 512
    return pl.pallas_call(
        _adaln_kernel,
        out_shape=jax.ShapeDtypeStruct((n_layers, MOD_ROWS, n), F32),
        grid=(n_layers, n // tn),
        in_specs=[
            pl.BlockSpec((MOD_ROWS, d), lambda l, j: (0, 0)),
            pl.BlockSpec((None, d, tn), lambda l, j: (l, 0, j)),
            pl.BlockSpec((None, 1, tn), lambda l, j: (l, 0, j)),
        ],
        out_specs=pl.BlockSpec((None, MOD_ROWS, tn), lambda l, j: (l, 0, j)),
        compiler_params=_params(("arbitrary", "arbitrary")),
        name="adaln",
    )(cond, ada_w, ada_b.reshape(n_layers, 1, n))


def _mod_row(row0, n_lat, seq):
    return jnp.where(row0 < n_lat, row0 // seq, MOD_ROWS // 2)


def _in_proj_kernel(xa_ref, xb_ref, g_ref, sh_ref, sc_ref, w_ref, o_ref, u_ref, *, a_blocks):
    first_col = pl.program_id(1) == 0
    from_a = pl.program_id(0) < a_blocks

    def normed(x_ref):
        x = x_ref[...]
        ms = jnp.mean(x * x, axis=-1, keepdims=True)
        y = x * lax.rsqrt(ms + NORM_EPS) * g_ref[...]
        u_ref[...] = (y * (1.0 + sc_ref[...]) + sh_ref[...]).astype(BF16)

    @pl.when(jnp.logical_and(first_col, from_a))
    def _():
        normed(xa_ref)

    @pl.when(jnp.logical_and(first_col, jnp.logical_not(from_a)))
    def _():
        normed(xb_ref)

    o_ref[...] = jnp.dot(u_ref[...], w_ref[...].astype(BF16), preferred_element_type=F32).astype(o_ref.dtype)


def _in_proj(rows, gain, mods, w_in, layer, *, row0, n_rows, col0, n_cols, n_lat, seq, sh_idx, sc_idx, tm, tn=512):
    xa, xb = rows
    d = xa.shape[1]
    rb0, cb0 = row0 // tm, col0 // tn
    a_blocks = xa.shape[0] // tm - rb0

    def mod_map(idx):
        return lambda i, j: (_mod_row((i + rb0) * tm, n_lat, seq), 0, idx)

    return pl.pallas_call(
        functools.partial(_in_proj_kernel, a_blocks=a_blocks),
        out_shape=jax.ShapeDtypeStruct((n_rows, n_cols), BF16),
        grid=(n_rows // tm, n_cols // tn),
        in_specs=[
            pl.BlockSpec((tm, d), lambda i, j: (jnp.minimum(i, a_blocks - 1) + rb0, 0)),
            pl.BlockSpec((tm, d), lambda i, j: (jnp.maximum(i - a_blocks, 0), 0)),
            pl.BlockSpec((1, d), lambda i, j: (0, 0)),
            pl.BlockSpec((None, 1, d), mod_map(sh_idx)),
            pl.BlockSpec((None, 1, d), mod_map(sc_idx)),
            pl.BlockSpec((None, d, tn), lambda i, j: (layer, 0, j + cb0)),
        ],
        out_specs=pl.BlockSpec((tm, tn), lambda i, j: (i, j)),
        scratch_shapes=[pltpu.VMEM((tm, d), BF16)],
        compiler_params=_params(("arbitrary", "arbitrary")),
        name="in_proj",
    )(xa, xb, gain.reshape(1, d), mods, mods, w_in)


def _rope_tables(n_tokens):
    t = jnp.arange(n_tokens, dtype=I32)
    pos = jnp.stack([t // GRID_W, t % GRID_W], axis=-1).astype(F32)
    inv_freq = ROPE_BASE ** (-jnp.arange(ROPE_FREQS, dtype=F32) / ROPE_FREQS)
    ang = pos[:, :, None] * inv_freq
    cos, sin = jnp.cos(ang), jnp.sin(ang)
    c = jnp.stack([cos, cos], axis=2).reshape(n_tokens, HEAD_DIM)
    s = jnp.stack([-sin, sin], axis=2).reshape(n_tokens, HEAD_DIM)
    return jnp.tile(c, (1, 2)), jnp.tile(s, (1, 2))


def _rope(x, c, s):
    lane = lax.broadcasted_iota(I32, x.shape, 1)
    first_half = (lane % (2 * ROPE_FREQS)) < ROPE_FREQS
    partner = jnp.where(first_half, pltpu.roll(x, LANES - ROPE_FREQS, 1), pltpu.roll(x, ROPE_FREQS, 1))
    return x * c + partner * s


def _diff_lambda(lam_ref, lam_init):
    lv = lam_ref[...]
    a = jnp.sum(lv[0:1] * lv[1:2], axis=-1, keepdims=True)
    b = jnp.sum(lv[2:3] * lv[3:4], axis=-1, keepdims=True)
    return jnp.exp(a) - jnp.exp(b) + lam_init


def _attend(q, k_all, v_all, lam, subg, lam_init):
    tq = q.shape[0]
    lane = lax.broadcasted_iota(I32, q.shape, 1)
    q0 = jnp.where(lane < HEAD_DIM, q, 0.0).astype(BF16)
    q1 = jnp.where(lane >= HEAD_DIM, q, 0.0).astype(BF16)
    qq = jnp.concatenate([q0, q1], axis=0)
    s = lax.dot_general(qq, k_all, (((1,), (1,)), ((), ())), preferred_element_type=F32)
    m = jnp.max(s, axis=-1, keepdims=True)
    e = jnp.exp2(s - m)
    l = jnp.sum(e, axis=-1, keepdims=True)
    a = e[:tq] - e[tq:] * (lam * l[:tq] / l[tq:])
    o = jnp.dot(a.astype(BF16), v_all, preferred_element_type=F32) / l[:tq]
    ms = jnp.mean(o * o, axis=-1, keepdims=True)
    return o * lax.rsqrt(ms + SUBLN_EPS) * subg * (1.0 - lam_init)


def _attn_lat_kernel(q_ref, kl_ref, vl_ref, kc_ref, vc_ref, cq_ref, sq_ref, ck_ref, sk_ref, lam_ref, g_ref,
                     o_ref, k_all, v_aug, s0_ref, m0_ref, s1_ref, m1_ref, *, n_ctx, lam_init, n_blocks):
    i = pl.program_id(2)
    tq = q_ref.shape[0]

    @pl.when(i == 0)
    def _():
        k_all[0:n_ctx, :] = kc_ref[...]
        k_all[n_ctx:, :] = _rope(kl_ref[...].astype(F32), ck_ref[...], sk_ref[...]).astype(BF16)
        v_aug[0:n_ctx, 0:HEAD_W] = vc_ref[...]
        v_aug[n_ctx:, 0:HEAD_W] = vl_ref[...]
        v_aug[:, HEAD_W:2 * HEAD_W] = jnp.ones((k_all.shape[0], HEAD_W), BF16)

    lam = _diff_lambda(lam_ref, lam_init)

    def stage_a(sa_ref, ma_ref):
        q = _rope(q_ref[...].astype(F32), cq_ref[...], sq_ref[...]) * QK_SCALE
        lane = lax.broadcasted_iota(I32, q.shape, 1)
        q0 = jnp.where(lane < HEAD_DIM, q, 0.0).astype(BF16)
        q1 = jnp.where(lane >= HEAD_DIM, q, 0.0).astype(BF16)
        qq = jnp.concatenate([q0, q1], axis=0)
        s = lax.dot_general(qq, k_all[...], (((1,), (1,)), ((), ())), preferred_element_type=F32)
        sa_ref[...] = s
        ma_ref[...] = jnp.broadcast_to(jnp.max(s, axis=-1, keepdims=True), ma_ref.shape)

    def stage_b(sb_ref, mb_ref):
        mb = jnp.concatenate([mb_ref[...]] * (sb_ref.shape[1] // LANES), axis=1)
        e = jnp.exp2(sb_ref[...] - mb).astype(BF16)
        oa = jnp.dot(e, v_aug[...], preferred_element_type=F32)
        o = oa[:tq, 0:HEAD_W] / oa[:tq, HEAD_W:] - oa[tq:, 0:HEAD_W] * (lam / oa[tq:, HEAD_W:])
        ms = jnp.mean(o * o, axis=-1, keepdims=True)
        o_ref[...] = (o * lax.rsqrt(ms + SUBLN_EPS) * g_ref[...] * (1.0 - lam_init)).astype(o_ref.dtype)

    bufs = ((s0_ref, m0_ref), (s1_ref, m1_ref))

    @pl.when(i == 0)
    def _():
        stage_a(*bufs[0])

    for par in range(2):
        @pl.when(jnp.logical_and(jnp.logical_and(i > 0, i < n_blocks), i % 2 == par))
        def _():
            stage_a(*bufs[par])
            stage_b(*bufs[1 - par])

    @pl.when(i == n_blocks)
    def _():
        stage_b(*bufs[(n_blocks - 1) % 2])


def _attn_ctx_kernel(q_ref, k_ref, v_ref, lam_ref, g_ref, o_ref, *, lam_init):
    q = q_ref[...].astype(F32) * QK_SCALE
    lam = _diff_lambda(lam_ref, lam_init)
    o_ref[...] = _attend(q, k_ref[...], v_ref[...], lam, g_ref[...], lam_init).astype(o_ref.dtype)


def _attn_lat(p, kv_ctx, kc_blk, vc_blk, ctx_rb0, cos, sin, lam4, subg, *, batch, seq, n_ctx, lam_init, tq=512):
    nq = seq // tq
    n_keys = n_ctx + seq
    hb = lambda off: off // HEAD_W
    qblk = lambda i: jnp.minimum(i, nq - 1)
    kernel = functools.partial(_attn_lat_kernel, n_ctx=n_ctx, lam_init=lam_init, n_blocks=nq)
    return pl.pallas_call(
        kernel,
        out_shape=jax.ShapeDtypeStruct((batch * seq, ATTN_W), BF16),
        grid=(batch, N_HEADS, nq + 1),
        in_specs=[
            pl.BlockSpec((tq, HEAD_W), lambda b, h, i: (b * nq + qblk(i), hb(OFF_Q) + h)),
            pl.BlockSpec((seq, HEAD_W), lambda b, h, i: (b, hb(OFF_K) + h)),
            pl.BlockSpec((seq, HEAD_W), lambda b, h, i: (b, hb(OFF_V) + h)),
            pl.BlockSpec((n_ctx, HEAD_W), lambda b, h, i: (ctx_rb0 + b, kc_blk + h)),
            pl.BlockSpec((n_ctx, HEAD_W), lambda b, h, i: (ctx_rb0 + b, vc_blk + h)),
            pl.BlockSpec((tq, HEAD_W), lambda b, h, i: (qblk(i), 0)),
            pl.BlockSpec((tq, HEAD_W), lambda b, h, i: (qblk(i), 0)),
            pl.BlockSpec((seq, HEAD_W), lambda b, h, i: (0, 0)),
            pl.BlockSpec((seq, HEAD_W), lambda b, h, i: (0, 0)),
            pl.BlockSpec((4, HEAD_DIM), lambda b, h, i: (0, 0)),
            pl.BlockSpec((1, HEAD_W), lambda b, h, i: (0, 0)),
        ],
        out_specs=pl.BlockSpec((tq, HEAD_W), lambda b, h, i: (b * nq + jnp.maximum(i - 1, 0), h)),
        scratch_shapes=[pltpu.VMEM((n_keys, HEAD_W), BF16), pltpu.VMEM((n_keys, 2 * HEAD_W), BF16),
                        pltpu.VMEM((2 * tq, n_keys), F32), pltpu.VMEM((2 * tq, LANES), F32),
                        pltpu.VMEM((2 * tq, n_keys), F32), pltpu.VMEM((2 * tq, LANES), F32)],
        compiler_params=_params(("arbitrary", "arbitrary", "arbitrary")),
        name="attn_lat",
    )(p, p, p, kv_ctx, kv_ctx, cos, sin, cos, sin, lam4, subg.reshape(1, HEAD_W))


def _attn_ctx(p, ctx_rb0, lam4, subg, *, batch, n_ctx, lam_init):
    hb = lambda off: off // HEAD_W
    kernel = functools.partial(_attn_ctx_kernel, lam_init=lam_init)
    return pl.pallas_call(
        kernel,
        out_shape=jax.ShapeDtypeStruct((batch * n_ctx, ATTN_W), BF16),
        grid=(batch, N_HEADS),
        in_specs=[
            pl.BlockSpec((n_ctx, HEAD_W), lambda b, h: (ctx_rb0 + b, hb(OFF_Q) + h)),
            pl.BlockSpec((n_ctx, HEAD_W), lambda b, h: (ctx_rb0 + b, hb(OFF_K) + h)),
            pl.BlockSpec((n_ctx, HEAD_W), lambda b, h: (ctx_rb0 + b, hb(OFF_V) + h)),
            pl.BlockSpec((4, HEAD_DIM), lambda b, h: (0, 0)),
            pl.BlockSpec((1, HEAD_W), lambda b, h: (0, 0)),
        ],
        out_specs=pl.BlockSpec((n_ctx, HEAD_W), lambda b, h: (b, h)),
        compiler_params=_params(("arbitrary", "arbitrary")),
        name="attn_ctx",
    )(p, p, p, lam4, subg.reshape(1, HEAD_W))


def _conv_kernel(bg_ref, cg_ref, xi_ref, a_ref, g_ref,
                 cg_p, xi_p, a_p, g_p, cg_n, xi_n, a_n, g_n,
                 scw_ref, cfw_ref, cfb_ref, lng_ref, lnb_ref,
                 sco_ref, cfo_ref, pad_ref, *, lat_blocks, blocks_per_seq):
    i = pl.program_id(0)
    in_lat = i < lat_blocks
    pos = i % blocks_per_seq
    has_prev = jnp.logical_and(in_lat, pos != 0)
    has_next = jnp.logical_and(in_lat, pos != blocks_per_seq - 1)
    keep_prev = jnp.where(has_prev, 1.0, 0.0)
    keep_next = jnp.where(has_next, 1.0, 0.0)
    lo, hi = CONV_HALO, CONV_HALO + CONV_ROWS

    def fill(main, prev, nxt):
        pad_ref[0:lo, :] = prev * keep_prev
        pad_ref[lo:hi, :] = main
        pad_ref[hi:hi + CONV_HALO, :] = nxt * keep_next

    def f(ref):
        return ref[...].astype(F32)

    def glu(a, g):
        return a * _sigmoid(g)

    fill(f(cg_ref) * f(xi_ref), f(cg_p) * f(xi_p), f(cg_n) * f(xi_n))
    for c in range(CONV_ROWS // CONV_CHUNK):
        r0 = lo + c * CONV_CHUNK - SC_TAPS // 2
        acc = scw_ref[0:1, :] * pad_ref[r0:r0 + CONV_CHUNK, :]
        for k in range(1, SC_TAPS):
            acc = acc + scw_ref[k:k + 1, :] * pad_ref[r0 + k:r0 + k + CONV_CHUNK, :]
        rows = slice(c * CONV_CHUNK, (c + 1) * CONV_CHUNK)
        sco_ref[rows, :] = (bg_ref[rows, :].astype(F32) * acc).astype(sco_ref.dtype)

    fill(glu(f(a_ref), f(g_ref)), glu(f(a_p), f(g_p)), glu(f(a_n), f(g_n)))
    off = lo - CF_TAPS // 2
    win = CONV_CHUNK + SUBLANES
    for c in range(CONV_ROWS // CONV_CHUNK):
        base = c * CONV_CHUNK
        acc = None
        for b in range(SUBLANES):
            q = None
            for k in range(b, CF_TAPS, SUBLANES):
                term = cfw_ref[k:k + 1, :] * pad_ref[base + k - b:base + k - b + win, :]
                q = term if q is None else q + term
            part = q[off + b:off + b + CONV_CHUNK, :]
            acc = part if acc is None else acc + part
        z = acc + cfb_ref[...]
        mu = jnp.mean(z, axis=-1, keepdims=True)
        zc = z - mu
        var = jnp.mean(zc * zc, axis=-1, keepdims=True)
        y = zc * lax.rsqrt(var + CF_LN_EPS) * lng_ref[...] + lnb_ref[...]
        rows = slice(c * CONV_CHUNK, (c + 1) * CONV_CHUNK)
        cfo_ref[rows, :] = (y * _sigmoid(y)).astype(cfo_ref.dtype)


def _convs(p, sc_w, cf_w, cf_b, ln_g, ln_b, *, n_rows, n_lat, seq):
    nb = n_rows // CONV_ROWS
    halo_per_block = CONV_ROWS // CONV_HALO
    last_halo = n_rows // CONV_HALO - 1
    cb = lambda off: off // SC_W

    def main(off):
        return pl.BlockSpec((CONV_ROWS, SC_W), lambda i: (i, cb(off)))

    def prev(off):
        return pl.BlockSpec((CONV_HALO, SC_W), lambda i: (jnp.maximum(i * halo_per_block - 1, 0), cb(off)))

    def nxt(off):
        return pl.BlockSpec((CONV_HALO, SC_W), lambda i: (jnp.minimum((i + 1) * halo_per_block, last_halo), cb(off)))

    def vec(rows):
        return pl.BlockSpec((rows, SC_W), lambda i: (0, 0))

    o_bg, o_cg, o_xi, o_a, o_g = OFF_SC, OFF_SC + SC_W, OFF_SC + 2 * SC_W, OFF_CF, OFF_CF + CF_W
    kernel = functools.partial(_conv_kernel, lat_blocks=n_lat // CONV_ROWS, blocks_per_seq=seq // CONV_ROWS)
    return pl.pallas_call(
        kernel,
        out_shape=(jax.ShapeDtypeStruct((n_rows, SC_W), BF16), jax.ShapeDtypeStruct((n_rows, CF_W), BF16)),
        grid=(nb,),
        in_specs=[main(o_bg), main(o_cg), main(o_xi), main(o_a), main(o_g),
                  prev(o_cg), prev(o_xi), prev(o_a), prev(o_g),
                  nxt(o_cg), nxt(o_xi), nxt(o_a), nxt(o_g),
                  vec(SC_TAPS), vec(CF_TAPS), vec(1), vec(1), vec(1)],
        out_specs=(pl.BlockSpec((CONV_ROWS, SC_W), lambda i: (i, 0)),
                   pl.BlockSpec((CONV_ROWS, CF_W), lambda i: (i, 0))),
        scratch_shapes=[pltpu.VMEM((CONV_ROWS + 2 * CONV_HALO, SC_W), F32)],
        compiler_params=_params(("arbitrary",)),
        name="convs",
    )(p, p, p, p, p, p, p, p, p, p, p, p, p,
      sc_w, cf_w, cf_b.reshape(1, CF_W), ln_g.reshape(1, CF_W), ln_b.reshape(1, CF_W))


def _merge_kernel(*refs, a_blocks):
    n_gate = 3 * MERGE_GATE_TILES
    ata_ref, atb_ref, sc_ref, cf_ref = refs[:4]
    gate_refs = refs[4:4 + n_gate]
    bg_ref, wa_ref, wb_ref, wc_ref, wm_ref, ha_ref, hb_ref, g1_ref, o_ref = refs[4 + n_gate:]
    d = D_MODEL
    from_a = pl.program_id(0) < a_blocks
    attn = jnp.where(from_a, ata_ref[...], atb_ref[...])
    h = jnp.where(from_a, ha_ref[...], hb_ref[...])

    def gate(branch):
        tiles = gate_refs[branch * MERGE_GATE_TILES:(branch + 1) * MERGE_GATE_TILES]
        pre = jnp.concatenate([t[...] for t in tiles], axis=1).astype(F32)
        return _sigmoid(pre + bg_ref[:, branch * d:(branch + 1) * d])

    ya = jnp.dot(attn, wa_ref[...], preferred_element_type=F32)
    yb = jnp.dot(sc_ref[...], wb_ref[...], preferred_element_type=F32)
    yc = jnp.dot(cf_ref[...], wc_ref[...], preferred_element_type=F32)
    m = (gate(0) * ya + gate(1) * yb + gate(2) * yc).astype(BF16)
    o_ref[...] = h + g1_ref[...] * jnp.dot(m, wm_ref[...], preferred_element_type=F32)


def _merge(attn, sc, cf, p, b_gate, wa, wb, wc, wm, rows, mods, *, n_rows, n_lat, seq, gate_idx, tm=256):
    d = D_MODEL
    (ata, atb), (ha, hb) = attn, rows
    a_blocks = ha.shape[0] // tm
    first = lambda w: pl.BlockSpec((tm, w), lambda i: (jnp.minimum(i, a_blocks - 1), 0))
    rest = lambda w: pl.BlockSpec((tm, w), lambda i: (jnp.maximum(i - a_blocks, 0), 0))
    tg = d // MERGE_GATE_TILES
    gb0 = OFF_GATE // tg
    gates = [pl.BlockSpec((tm, tg), functools.partial(lambda i, c: (i, c), c=gb0 + t))
             for t in range(3 * MERGE_GATE_TILES)]
    whole = lambda a: pl.BlockSpec(a.shape, lambda i: (0, 0))
    return pl.pallas_call(
        functools.partial(_merge_kernel, a_blocks=a_blocks),
        out_shape=jax.ShapeDtypeStruct((n_rows, d), F32),
        grid=(n_rows // tm,),
        in_specs=[
            first(ATTN_W), rest(ATTN_W),
            pl.BlockSpec((tm, SC_W), lambda i: (i, 0)),
            pl.BlockSpec((tm, CF_W), lambda i: (i, 0)),
            *gates,
            whole(b_gate), whole(wa), whole(wb), whole(wc), whole(wm),
            first(d), rest(d),
            pl.BlockSpec((None, 1, d), lambda i: (_mod_row(i * tm, n_lat, seq), 0, gate_idx)),
        ],
        out_specs=pl.BlockSpec((tm, d), lambda i: (i, 0)),
        compiler_params=_params(("arbitrary",)),
        name="merge",
    )(ata, atb, sc, cf, *([p] * (3 * MERGE_GATE_TILES)), b_gate, wa, wb, wc, wm, ha, hb, mods)


def _split2(x):
    hi = x.astype(BF16)
    return hi, (x - hi.astype(F32)).astype(BF16)


def _norm_mod(x, g, sh, sc):
    ms = jnp.mean(x * x, axis=-1, keepdims=True)
    return x * lax.rsqrt(ms + NORM_EPS) * g * (1.0 + sc) + sh


def _router_kernel(h_ref, g_ref, sh_ref, sc_ref, rw_ref, rb_ref, id_ref, wt_ref):
    f = _norm_mod(h_ref[...], g_ref[...], sh_ref[...], sc_ref[...])

    f_hi, f_lo = _split2(f)
    w_hi, w_lo = _split2(rw_ref[...])
    dot = lambda a, b: jnp.dot(a, b, preferred_element_type=F32)
    both = dot(f_hi, jnp.concatenate([w_hi, w_lo], axis=1))
    logits = (dot(f_lo, w_hi) + both[:, LANES:] + both[:, :LANES]) + rb_ref[...]

    lane_i = lax.broadcasted_iota(I32, logits.shape, 1)
    lane = lane_i.astype(F32)
    neg = -jnp.inf
    big = float(LANES)
    lg = jnp.where(lane_i < N_GROUPS, logits, neg)
    mg = jnp.max(lg, axis=-1, keepdims=True)
    grp = jnp.min(jnp.where(lg == mg, lane, big), axis=-1, keepdims=True)
    p_grp = 1.0 / jnp.sum(jnp.exp(lg - mg), axis=-1, keepdims=True)

    e_lane = lane - N_GROUPS
    in_grp = jnp.logical_and(e_lane >= grp * EXPERTS_PER_GROUP, e_lane < (grp + 1.0) * EXPERTS_PER_GROUP)
    le = jnp.where(in_grp, logits, neg)
    m1 = jnp.max(le, axis=-1, keepdims=True)
    i1 = jnp.min(jnp.where(le == m1, lane, big), axis=-1, keepdims=True)
    le2 = jnp.where(lane == i1, neg, le)
    m2 = jnp.max(le2, axis=-1, keepdims=True)
    i2 = jnp.min(jnp.where(le2 == m2, lane, big), axis=-1, keepdims=True)
    e2 = jnp.exp(m2 - m1)
    w1 = p_grp / (1.0 + e2)
    w2 = p_grp * e2 / (1.0 + e2)
    ids = jnp.where(lane_i == 0, i1 - N_GROUPS, jnp.where(lane_i == 1, i2 - N_GROUPS, 0.0))
    id_ref[...] = ids.astype(I32)
    wt_ref[...] = jnp.where(lane_i == 0, w1, jnp.where(lane_i == 1, w2, 0.0))


def _router(h, gain, mods, rw, rb, *, n_rows, n_lat, seq, sh_idx, sc_idx, tm=512):
    d = D_MODEL

    def mod_map(idx):
        return lambda i: (_mod_row(i * tm, n_lat, seq), 0, idx)

    return pl.pallas_call(
        _router_kernel,
        out_shape=(jax.ShapeDtypeStruct((n_rows, LANES), I32),
                   jax.ShapeDtypeStruct((n_rows, LANES), F32)),
        grid=(n_rows // tm,),
        in_specs=[
            pl.BlockSpec((tm, d), lambda i: (i, 0)),
            pl.BlockSpec((1, d), lambda i: (0, 0)),
            pl.BlockSpec((None, 1, d), mod_map(sh_idx)),
            pl.BlockSpec((None, 1, d), mod_map(sc_idx)),
            pl.BlockSpec((d, LANES), lambda i: (0, 0)),
            pl.BlockSpec((1, LANES), lambda i: (0, 0)),
        ],
        out_specs=(pl.BlockSpec((tm, LANES), lambda i: (i, 0)),
                   pl.BlockSpec((tm, LANES), lambda i: (i, 0))),
        compiler_params=_params(("arbitrary",)),
        name="router",
    )(h, gain.reshape(1, d), mods, mods, rw, rb)


def _dispatch_kernel(pos_ref, h_ref, g_ref, sh_ref, sc_ref, xs_hbm, f0_ref, f1_ref, sem):
    i = pl.program_id(0)
    tm = h_ref.shape[0]

    def run(f_ref):
        def wait_block_copies():
            for _ in range(2):
                pltpu.make_async_copy(f_ref, xs_hbm.at[pl.ds(0, tm), :], sem).wait()

        f_ref[...] = _norm_mod(h_ref[...], g_ref[...], sh_ref[...], sc_ref[...])

        @pl.when(i > 0)
        def _():
            wait_block_copies()

        base = i * (2 * tm)
        for r in range(tm):
            for k in range(2):
                dst = pos_ref[base + 2 * r + k]
                pltpu.make_async_copy(f_ref.at[pl.ds(r, 1), :], xs_hbm.at[pl.ds(dst, 1), :], sem).start()

        @pl.when(i == pl.num_programs(0) - 1)
        def _():
            wait_block_copies()

    @pl.when(i % 2 == 0)
    def _():
        run(f0_ref)

    @pl.when(i % 2 == 1)
    def _():
        run(f1_ref)


def _dispatch(h, gain, mods, pos, *, n_rows, n_lat, seq, sh_idx, sc_idx, tm=SLOT_BLOCK_ROWS):
    d = D_MODEL

    def mod_map(idx):
        return lambda i, pos_ref: (_mod_row(i * tm, n_lat, seq), 0, idx)

    grid_spec = pltpu.PrefetchScalarGridSpec(
        num_scalar_prefetch=1,
        grid=(n_rows // tm,),
        in_specs=[
            pl.BlockSpec((tm, d), lambda i, pos_ref: (i, 0)),
            pl.BlockSpec((1, d), lambda i, pos_ref: (0, 0)),
            pl.BlockSpec((None, 1, d), mod_map(sh_idx)),
            pl.BlockSpec((None, 1, d), mod_map(sc_idx)),
        ],
        out_specs=pl.BlockSpec(memory_space=pl.ANY),
        scratch_shapes=[pltpu.VMEM((tm, d), F32), pltpu.VMEM((tm, d), F32), pltpu.SemaphoreType.DMA],
    )
    return pl.pallas_call(
        _dispatch_kernel,
        out_shape=jax.ShapeDtypeStruct((2 * n_rows, d), F32),
        grid_spec=grid_spec,
        compiler_params=_params(("arbitrary",)),
        name="dispatch",
    )(pos, h, gain.reshape(1, d), mods, mods)


def _moe_plan(ids, n_tok):
    n_items = N_EXPERTS + (2 * n_tok) // MOE_ITEM_CAP
    e_flat = ids[:, :2].reshape(-1)
    order = jnp.argsort(e_flat, stable=True).astype(I32)
    pos = jnp.argsort(order).astype(I32)
    pos = jnp.concatenate([pos, jnp.zeros((2 * SLOT_BLOCK_ROWS,), I32)])
    counts = jnp.sum((e_flat[:, None] == jnp.arange(N_EXPERTS, dtype=I32)[None, :]).astype(I32), axis=0)
    starts = jnp.cumsum(counts) - counts
    n_sb = (counts + MOE_ITEM_CAP - 1) // MOE_ITEM_CAP
    cum = jnp.cumsum(n_sb)
    total = cum[-1]
    it = jnp.arange(n_items, dtype=I32)
    e_i = jnp.minimum(jnp.searchsorted(cum, it, side="right").astype(I32), N_EXPERTS - 1)
    sb = it - (cum[e_i] - n_sb[e_i])
    valid = it < total
    last_e = e_i[jnp.maximum(total - 1, 0)]
    item_e = jnp.where(valid, e_i, last_e).astype(I32)
    item_start = jnp.where(valid, starts[e_i] + sb * MOE_ITEM_CAP, 0).astype(I32)
    item_n = jnp.where(valid, jnp.minimum(MOE_ITEM_CAP, counts[e_i] - sb * MOE_ITEM_CAP), 0).astype(I32)
    front, back = jnp.zeros((MOE_ITEM_SHIFT,), I32), jnp.zeros((1,), I32)
    pad_items = lambda a: jnp.concatenate([front, a, back])
    return (pad_items(item_e), pad_items(item_start), pad_items(item_n)), pos


def _moe_kernel(ie_ref, ist_ref, in_ref,
                xs_hbm, wg_ref, wu_ref, wd_ref, ys_hbm,
                xf_ref, xb_ref, y_ref, gsem, ssem):
    i, j = pl.program_id(0), pl.program_id(1)
    sh = MOE_ITEM_SHIFT
    n_prev, n, n_next = in_ref[i + sh - 1], in_ref[i + sh], in_ref[i + sh + 1]
    start_prev, start, start_next = ist_ref[i + sh - 1], ist_ref[i + sh], ist_ref[i + sh + 1]
    par = i % 2

    def rows_in(hrow, vrow, size):
        return pltpu.make_async_copy(xs_hbm.at[pl.ds(hrow, size), :], xf_ref.at[pl.ds(vrow, size), :], gsem)

    def rows_out(hrow, vrow, size):
        return pltpu.make_async_copy(y_ref.at[par, pl.ds(vrow, size), :], ys_hbm.at[pl.ds(hrow, size), :], ssem)

    def row_groups(make, base, count, act):
        shift = base % SUBLANES
        head = jnp.minimum((SUBLANES - shift) % SUBLANES, count)
        body = ((count - head) // SUBLANES) * SUBLANES
        for t in range(SUBLANES - 1):
            @pl.when(t < head)
            def _():
                getattr(make(base + t, shift + t, 1), act)()

        for b in reversed(range(SUBLANES.bit_length() - 1, MOE_ITEM_ROWS.bit_length())):
            @pl.when(((body >> b) & 1) == 1)
            def _():
                off = head + ((body >> (b + 1)) << (b + 1))
                getattr(make(pl.multiple_of(base + off, SUBLANES), pl.multiple_of(shift + off, SUBLANES), 1 << b),
                        act)()

        done = head + body
        for t in range(SUBLANES - 1):
            @pl.when(t < count - done)
            def _():
                getattr(make(base + done + t, shift + done + t, 1), act)()

    @pl.when(jnp.logical_and(i == 0, j == 0))
    def _():
        xf_ref[...] = jnp.zeros_like(xf_ref)
        y_ref[...] = jnp.zeros_like(y_ref)

    @pl.when(j == 0)
    def _():
        @pl.when(i == 0)
        def _():
            row_groups(rows_in, start, n, "start")

        @pl.when(n > 0)
        def _():
            row_groups(rows_in, start, n, "wait")
            xb_ref[...] = xf_ref[...].astype(BF16)
            row_groups(rows_in, start_next, n_next, "start")

    def compute(m):
        x = xb_ref[0:m, :]
        g = jnp.dot(x, wg_ref[...].astype(BF16), preferred_element_type=F32)
        u = jnp.dot(x, wu_ref[...].astype(BF16), preferred_element_type=F32)
        hmid = (g * _sigmoid(g) * u).astype(BF16)
        y = jnp.dot(hmid, wd_ref[...].astype(BF16), preferred_element_type=F32)
        acc = y_ref.at[par]
        acc[0:m, :] = y + jnp.where(j == 0, 0.0, acc[0:m, :])

    used = jnp.where(n > 0, start % SUBLANES + n, 0)
    n_pad = (used + MOE_ROW_PAD - 1) // MOE_ROW_PAD
    for k in range(1, MOE_ITEM_ROWS // MOE_ROW_PAD + 1):
        @pl.when(n_pad == k)
        def _():
            compute(k * MOE_ROW_PAD)

    @pl.when(j == pl.num_programs(1) - 1)
    def _():
        row_groups(rows_out, start_prev, n_prev, "wait")

        row_groups(rows_out, start, n, "start")

        @pl.when(i == pl.num_programs(0) - 1)
        def _():
            row_groups(rows_out, start, n, "wait")


def _moe(xs, items, w_gu, w_down, layer):
    item_e, item_start, item_n = items
    d, hid, hk = D_MODEL, EXPERT_HIDDEN, MOE_HIDDEN_BLOCK
    nj = hid // hk
    sh = MOE_ITEM_SHIFT
    n_items = item_e.shape[0] - sh - 1

    def chunk(j, nn, i):
        return jnp.where(nn[i + sh] > 0, j, nj - 1)

    grid_spec = pltpu.PrefetchScalarGridSpec(
        num_scalar_prefetch=3,
        grid=(n_items, nj),
        in_specs=[
            pl.BlockSpec(memory_space=pl.ANY),
            pl.BlockSpec((None, None, d, hk), lambda i, j, ie, ist, nn: (layer, ie[i + sh], 0, chunk(j, nn, i))),
            pl.BlockSpec((None, None, d, hk), lambda i, j, ie, ist, nn: (layer, ie[i + sh], 0, nj + chunk(j, nn, i))),
            pl.BlockSpec((None, None, hk, d), lambda i, j, ie, ist, nn: (layer, ie[i + sh], chunk(j, nn, i), 0)),
        ],
        out_specs=pl.BlockSpec(memory_space=pl.ANY),
        scratch_shapes=[
            pltpu.VMEM((MOE_ITEM_ROWS, d), F32),
            pltpu.VMEM((MOE_ITEM_ROWS, d), BF16),
            pltpu.VMEM((2, MOE_ITEM_ROWS, d), F32),
            pltpu.SemaphoreType.DMA,
            pltpu.SemaphoreType.DMA,
        ],
    )
    return pl.pallas_call(
        _moe_kernel,
        out_shape=jax.ShapeDtypeStruct(xs.shape, F32),
        grid_spec=grid_spec,
        compiler_params=_params(("arbitrary", "arbitrary")),
        name="moe_experts",
    )(item_e, item_start, item_n, xs, w_gu, w_gu, w_down)


def _combine_kernel(pos_ref, h_ref, wt_ref, g2_ref, fg_ref, ys_hbm, o_ref, ybuf, sem, *, final):
    i = pl.program_id(0)
    tm = h_ref.shape[0]
    par = i % 2

    def fetch(step, half):
        base = step * (2 * tm)
        for r in range(tm):
            for k in range(2):
                src = pos_ref[base + 2 * r + k]
                pltpu.make_async_copy(ys_hbm.at[pl.ds(src, 1), :], ybuf.at[half, k, pl.ds(r, 1), :],
                                      sem.at[half]).start()

    @pl.when(i == 0)
    def _():
        fetch(0, 0)

    @pl.when(i < pl.num_programs(0) - 1)
    def _():
        fetch(i + 1, 1 - par)

    for k in range(2):
        pltpu.make_async_copy(ys_hbm.at[pl.ds(0, tm), :], ybuf.at[par, k], sem.at[par]).wait()
    wt = wt_ref[...]
    moe = wt[:, 0:1] * ybuf[par, 0] + wt[:, 1:2] * ybuf[par, 1]
    h = h_ref[...] + g2_ref[...] * moe
    if final:
        ms = jnp.mean(h * h, axis=-1, keepdims=True)
        h = h * lax.rsqrt(ms + NORM_EPS) * fg_ref[...]
    o_ref[...] = h


def _combine(h, ys, pos, wts, mods, final_g, *, n_rows, n_lat, seq, gate_idx, final, tm=SLOT_BLOCK_ROWS):
    d = D_MODEL
    kernel = functools.partial(_combine_kernel, final=final)
    grid_spec = pltpu.PrefetchScalarGridSpec(
        num_scalar_prefetch=1,
        grid=(n_rows // tm,),
        in_specs=[
            pl.BlockSpec((tm, d), lambda i, pos_ref: (i, 0)),
            pl.BlockSpec((tm, LANES), lambda i, pos_ref: (i, 0)),
            pl.BlockSpec((None, 1, d), lambda i, pos_ref: (_mod_row(i * tm, n_lat, seq), 0, gate_idx)),
            pl.BlockSpec((1, d), lambda i, pos_ref: (0, 0)),
            pl.BlockSpec(memory_space=pl.ANY),
        ],
        out_specs=pl.BlockSpec((tm, d), lambda i, pos_ref: (i, 0)),
        scratch_shapes=[pltpu.VMEM((2, 2, tm, d), F32), pltpu.SemaphoreType.DMA((2,))],
    )
    return pl.pallas_call(
        kernel,
        out_shape=jax.ShapeDtypeStruct((n_rows, d), F32),
        grid_spec=grid_spec,
        compiler_params=_params(("arbitrary",)),
        name="combine",
    )(pos, h, wts, mods, final_g.reshape(1, d), ys)


def kernel(x, c, ctx, c_ctx, ada_w, ada_b, norm1_g, w_in, b_gate, diff_lambda, subln_g, w_attn_out, sc_conv_w,
           w_sc_out, cf_dw_w, cf_dw_b, cf_ln_g, cf_ln_b, w_cf_out, w_mix, norm2_g, router_g_w, router_g_b,
           router_e_w, router_e_b, exp_w_gu, exp_w_down, final_g):
    batch, seq, d = x.shape
    n_ctx = ctx.shape[1]
    depth = ada_w.shape[0]
    n_lat = batch * seq
    n_all = n_lat + batch * n_ctx
    assert d == D_MODEL and batch == MOD_ROWS // 2 and w_in.shape[2] == C_TOT

    cond = jnp.concatenate([c, c_ctx[None, :], jnp.zeros((MOD_ROWS - batch - 1, d), F32)], axis=0)
    mods_all = _adaln(cond, ada_w, ada_b)
    cos, sin = _rope_tables(seq)
    rows = (x.reshape(n_lat, d), ctx.reshape(batch * n_ctx, d))
    geo = dict(n_lat=n_lat, seq=seq)

    for layer in range(depth):
        last = layer == depth - 1
        lam_init = 0.8 - 0.6 * math.exp(-0.3 * layer)
        mods = mods_all[layer].reshape(MOD_ROWS, 1, 6 * d)
        n_rows = n_lat if last else n_all

        p = _in_proj(rows, norm1_g[layer], mods, w_in, layer, row0=0, n_rows=n_rows, col0=0, n_cols=C_TOT,
                     sh_idx=0, sc_idx=1, tm=1024, **geo)
        if last:
            kv_ctx = _in_proj(rows, norm1_g[layer], mods, w_in, layer, row0=n_lat, n_rows=batch * n_ctx, col0=OFF_K,
                              n_cols=OFF_SC - OFF_K, sh_idx=0, sc_idx=1, tm=n_ctx, **geo)
            kc_blk, vc_blk, ctx_rb0 = 0, QK_W // HEAD_W, 0
        else:
            kv_ctx = p
            kc_blk, vc_blk, ctx_rb0 = OFF_K // HEAD_W, OFF_V // HEAD_W, n_lat // n_ctx
        attn = _attn_lat(p, kv_ctx, kc_blk, vc_blk, ctx_rb0, cos, sin, diff_lambda[layer], subln_g[layer],
                         batch=batch, seq=seq, n_ctx=n_ctx, lam_init=lam_init)
        if not last:
            attn = (attn, _attn_ctx(p, n_lat // n_ctx, diff_lambda[layer], subln_g[layer],
                                    batch=batch, n_ctx=n_ctx, lam_init=lam_init))
        else:
            attn = (attn, attn)
        sc, cf = _convs(p, sc_conv_w[layer], cf_dw_w[layer], cf_dw_b[layer], cf_ln_g[layer], cf_ln_b[layer],
                        n_rows=n_rows, **geo)
        h_mix = _merge(attn, sc, cf, p, b_gate[layer].reshape(1, 3 * d),
                       w_attn_out[layer].astype(BF16), w_sc_out[layer].astype(BF16),
                       w_cf_out[layer].astype(BF16), w_mix[layer].astype(BF16), rows, mods,
                       n_rows=n_rows, gate_idx=2, **geo)

        rw = jnp.concatenate([router_g_w[layer], router_e_w[layer],
                              jnp.zeros((d, LANES - N_GROUPS - N_EXPERTS), F32)], axis=1)
        rb = jnp.concatenate([router_g_b[layer], router_e_b[layer],
                              jnp.zeros((LANES - N_GROUPS - N_EXPERTS,), F32)]).reshape(1, LANES)
        ids, wts = _router(h_mix, norm2_g[layer], mods, rw, rb, n_rows=n_rows, sh_idx=3, sc_idx=4, **geo)
        items, pos = _moe_plan(ids, n_rows)
        xs = _dispatch(h_mix, norm2_g[layer], mods, pos, n_rows=n_rows, sh_idx=3, sc_idx=4, **geo)
        ys = _moe(xs, items, exp_w_gu, exp_w_down, layer)
        h = _combine(h_mix, ys, pos, wts, mods, final_g, n_rows=n_rows, gate_idx=5, final=last, **geo)
        rows = (h, h)

    return h.reshape(batch, seq, d)
```

```python
import functools
import math

import jax
import jax.numpy as jnp
from jax import lax
from jax.experimental import pallas as pl
from jax.experimental.pallas import tpu as pltpu

F32 = jnp.float32
BF16 = jnp.bfloat16
I32 = jnp.int32

D_MODEL = 2048
GRID_W = 64
NORM_EPS = 1e-6
N_HEADS = 8
HEAD_DIM = 64
HEAD_W = 2 * HEAD_DIM
QK_W = N_HEADS * HEAD_W
ATTN_W = N_HEADS * HEAD_W
ROPE_BASE = 10000.0
ROPE_FREQS = HEAD_DIM // 4
SUBLN_EPS = 1e-5
QK_SCALE = HEAD_DIM ** -0.5 * math.log2(math.e)
SC_W = D_MODEL // 4
CF_W = D_MODEL // 4
SC_TAPS = 3
CF_TAPS = 31
CF_LN_EPS = 1e-5
OFF_Q = 0
OFF_K = OFF_Q + QK_W
OFF_V = OFF_K + QK_W
OFF_SC = OFF_V + ATTN_W
OFF_CF = OFF_SC + 3 * SC_W
OFF_GATE = OFF_CF + 2 * CF_W
C_TOT = OFF_GATE + 3 * D_MODEL
N_GROUPS = 4
EXPERTS_PER_GROUP = 8
N_EXPERTS = N_GROUPS * EXPERTS_PER_GROUP
EXPERT_HIDDEN = D_MODEL // 2

LANES = 128
SUBLANES = 8
MOD_ROWS = 8
MERGE_GATE_TILES = 4
CONV_HALO = 16
CONV_ROWS = 256
CONV_CHUNK = 32
MOE_ITEM_ROWS = 1024
MOE_ITEM_CAP = MOE_ITEM_ROWS - 8
MOE_ROW_PAD = 128
MOE_HIDDEN_BLOCK = 256
MOE_ITEM_SHIFT = 1
SLOT_BLOCK_ROWS = 256
VMEM_LIMIT = 56 * 1024 * 1024


def _params(sem, vmem=VMEM_LIMIT):
    return pltpu.CompilerParams(dimension_semantics=sem, vmem_limit_bytes=vmem)


def _sigmoid(x):
    return 1.0 / (1.0 + jnp.exp(-x))


def _adaln_kernel(s_ref, w_ref, b_ref, o_ref):
    s = s_ref[...]
    s = s * _sigmoid(s)
    s_hi = s.astype(BF16)
    w = w_ref[...]
    w_hi = w.astype(BF16)
    w_lo = (w - w_hi.astype(F32)).astype(BF16)
    lhs = jnp.concatenate([s_hi.astype(F32), s - s_hi.astype(F32)], axis=0).astype(BF16)
    r = jnp.dot(lhs, w_hi, preferred_element_type=F32)
    r2 = jnp.dot(s_hi, w_lo, preferred_element_type=F32)
    o_ref[...] = r[:MOD_ROWS] + r[MOD_ROWS:] + r2 + b_ref[...]


def _adaln(cond, ada_w, ada_b):
    n_layers, d, n = ada_w.shape
    tn = 512
    return pl.pallas_call(
        _adaln_kernel,
        out_shape=jax.ShapeDtypeStruct((n_layers, MOD_ROWS, n), F32),
        grid=(n_layers, n // tn),
        in_specs=[
            pl.BlockSpec((MOD_ROWS, d), lambda l, j: (0, 0)),
            pl.BlockSpec((None, d, tn), lambda l, j: (l, 0, j)),
            pl.BlockSpec((None, 1, tn), lambda l, j: (l, 0, j)),
        ],
        out_specs=pl.BlockSpec((None, MOD_ROWS, tn), lambda l, j: (l, 0, j)),
        compiler_params=_params(("arbitrary", "arbitrary")),
        name="adaln",
    )(cond, ada_w, ada_b.reshape(n_layers, 1, n))


def _mod_row(row0, n_lat, seq):
    return jnp.where(row0 < n_lat, row0 // seq, MOD_ROWS // 2)


def _in_proj_kernel(xa_ref, xb_ref, g_ref, sh_ref, sc_ref, w_ref, o_ref, u_ref, *, a_blocks):
    first_col = pl.program_id(1) == 0
    from_a = pl.program_id(0) < a_blocks

    def normed(x_ref):
        x = x_ref[...]
        ms = jnp.mean(x * x, axis=-1, keepdims=True)
        y = x * lax.rsqrt(ms + NORM_EPS) * g_ref[...]
        u_ref[...] = (y * (1.0 + sc_ref[...]) + sh_ref[...]).astype(BF16)

    @pl.when(jnp.logical_and(first_col, from_a))
    def _():
        normed(xa_ref)

    @pl.when(jnp.logical_and(first_col, jnp.logical_not(from_a)))
    def _():
        normed(xb_ref)

    o_ref[...] = jnp.dot(u_ref[...], w_ref[...].astype(BF16), preferred_element_type=F32).astype(o_ref.dtype)


def _in_proj(rows, gain, mods, w_in, layer, *, row0, n_rows, col0, n_cols, n_lat, seq, sh_idx, sc_idx, tm, tn=512):
    xa, xb = rows
    d = xa.shape[1]
    rb0, cb0 = row0 // tm, col0 // tn
    a_blocks = xa.shape[0] // tm - rb0

    def mod_map(idx):
        return lambda i, j: (_mod_row((i + rb0) * tm, n_lat, seq), 0, idx)

    return pl.pallas_call(
        functools.partial(_in_proj_kernel, a_blocks=a_blocks),
        out_shape=jax.ShapeDtypeStruct((n_rows, n_cols), BF16),
        grid=(n_rows // tm, n_cols // tn),
        in_specs=[
            pl.BlockSpec((tm, d), lambda i, j: (jnp.minimum(i, a_blocks - 1) + rb0, 0)),
            pl.BlockSpec((tm, d), lambda i, j: (jnp.maximum(i - a_blocks, 0), 0)),
            pl.BlockSpec((1, d), lambda i, j: (0, 0)),
            pl.BlockSpec((None, 1, d), mod_map(sh_idx)),
            pl.BlockSpec((None, 1, d), mod_map(sc_idx)),
            pl.BlockSpec((None, d, tn), lambda i, j: (layer, 0, j + cb0)),
        ],
        out_specs=pl.BlockSpec((tm, tn), lambda i, j: (i, j)),
        scratch_shapes=[pltpu.VMEM((tm, d), BF16)],
        compiler_params=_params(("arbitrary", "arbitrary")),
        name="in_proj",
    )(xa, xb, gain.reshape(1, d), mods, mods, w_in)


def _rope_tables(n_tokens):
    t = jnp.arange(n_tokens, dtype=I32)
    pos = jnp.stack([t // GRID_W, t % GRID_W], axis=-1).astype(F32)
    inv_freq = ROPE_BASE ** (-jnp.arange(ROPE_FREQS, dtype=F32) / ROPE_FREQS)
    ang = pos[:, :, None] * inv_freq
    cos, sin = jnp.cos(ang), jnp.sin(ang)
    c = jnp.stack([cos, cos], axis=2).reshape(n_tokens, HEAD_DIM)
    s = jnp.stack([-sin, sin], axis=2).reshape(n_tokens, HEAD_DIM)
    return jnp.tile(c, (1, 2)), jnp.tile(s, (1, 2))


def _rope(x, c, s):
    lane = lax.broadcasted_iota(I32, x.shape, 1)
    first_half = (lane % (2 * ROPE_FREQS)) < ROPE_FREQS
    partner = jnp.where(first_half, pltpu.roll(x, LANES - ROPE_FREQS, 1), pltpu.roll(x, ROPE_FREQS, 1))
    return x * c + partner * s


def _diff_lambda(lam_ref, lam_init):
    lv = lam_ref[...]
    a = jnp.sum(lv[0:1] * lv[1:2], axis=-1, keepdims=True)
    b = jnp.sum(lv[2:3] * lv[3:4], axis=-1, keepdims=True)
    return jnp.exp(a) - jnp.exp(b) + lam_init


def _attend(q, k_all, v_all, lam, subg, lam_init):
    tq = q.shape[0]
    lane = lax.broadcasted_iota(I32, q.shape, 1)
    q0 = jnp.where(lane < HEAD_DIM, q, 0.0).astype(BF16)
    q1 = jnp.where(lane >= HEAD_DIM, q, 0.0).astype(BF16)
    qq = jnp.concatenate([q0, q1], axis=0)
    s = lax.dot_general(qq, k_all, (((1,), (1,)), ((), ())), preferred_element_type=F32)
    m = jnp.max(s, axis=-1, keepdims=True)
    e = jnp.exp2(s - m)
    l = jnp.sum(e, axis=-1, keepdims=True)
    a = e[:tq] - e[tq:] * (lam * l[:tq] / l[tq:])
    o = jnp.dot(a.astype(BF16), v_all, preferred_element_type=F32) / l[:tq]
    ms = jnp.mean(o * o, axis=-1, keepdims=True)
    return o * lax.rsqrt(ms + SUBLN_EPS) * subg * (1.0 - lam_init)


def _attn_lat_kernel(q_ref, kl_ref, vl_ref, kc_ref, vc_ref, cq_ref, sq_ref, ck_ref, sk_ref, lam_ref, g_ref,
                     o_ref, k_all, v_aug, s0_ref, m0_ref, s1_ref, m1_ref, *, n_ctx, lam_init, n_blocks):
    i = pl.program_id(2)
    tq = q_ref.shape[0]

    @pl.when(i == 0)
    def _():
        k_all[0:n_ctx, :] = kc_ref[...]
        k_all[n_ctx:, :] = _rope(kl_ref[...].astype(F32), ck_ref[...], sk_ref[...]).astype(BF16)
        v_aug[0:n_ctx, 0:HEAD_W] = vc_ref[...]
        v_aug[n_ctx:, 0:HEAD_W] = vl_ref[...]
        v_aug[:, HEAD_W:2 * HEAD_W] = jnp.ones((k_all.shape[0], HEAD_W), BF16)

    lam = _diff_lambda(lam_ref, lam_init)

    def stage_a(sa_ref, ma_ref):
        q = _rope(q_ref[...].astype(F32), cq_ref[...], sq_ref[...]) * QK_SCALE
        lane = lax.broadcasted_iota(I32, q.shape, 1)
        q0 = jnp.where(lane < HEAD_DIM, q, 0.0).astype(BF16)
        q1 = jnp.where(lane >= HEAD_DIM, q, 0.0).astype(BF16)
        qq = jnp.concatenate([q0, q1], axis=0)
        s = lax.dot_general(qq, k_all[...], (((1,), (1,)), ((), ())), preferred_element_type=F32)
        sa_ref[...] = s
        ma_ref[...] = jnp.broadcast_to(jnp.max(s, axis=-1, keepdims=True), ma_ref.shape)

    def stage_b(sb_ref, mb_ref):
        mb = jnp.concatenate([mb_ref[...]] * (sb_ref.shape[1] // LANES), axis=1)
        e = jnp.exp2(sb_ref[...] - mb).astype(BF16)
        oa = jnp.dot(e, v_aug[...], preferred_element_type=F32)
        o = oa[:tq, 0:HEAD_W] / oa[:tq, HEAD_W:] - oa[tq:, 0:HEAD_W] * (lam / oa[tq:, HEAD_W:])
        ms = jnp.mean(o * o, axis=-1, keepdims=True)
        o_ref[...] = (o * lax.rsqrt(ms + SUBLN_EPS) * g_ref[...] * (1.0 - lam_init)).astype(o_ref.dtype)

    bufs = ((s0_ref, m0_ref), (s1_ref, m1_ref))

    @pl.when(i == 0)
    def _():
        stage_a(*bufs[0])

    for par in range(2):
        @pl.when(jnp.logical_and(jnp.logical_and(i > 0, i < n_blocks), i % 2 == par))
        def _():
            stage_a(*bufs[par])
            stage_b(*bufs[1 - par])

    @pl.when(i == n_blocks)
    def _():
        stage_b(*bufs[(n_blocks - 1) % 2])


def _attn_ctx_kernel(q_ref, k_ref, v_ref, lam_ref, g_ref, o_ref, *, lam_init):
    q = q_ref[...].astype(F32) * QK_SCALE
    lam = _diff_lambda(lam_ref, lam_init)
    o_ref[...] = _attend(q, k_ref[...], v_ref[...], lam, g_ref[...], lam_init).astype(o_ref.dtype)


def _attn_lat(p, kv_ctx, kc_blk, vc_blk, ctx_rb0, cos, sin, lam4, subg, *, batch, seq, n_ctx, lam_init, tq=512):
    nq = seq // tq
    n_keys = n_ctx + seq
    hb = lambda off: off // HEAD_W
    qblk = lambda i: jnp.minimum(i, nq - 1)
    kernel = functools.partial(_attn_lat_kernel, n_ctx=n_ctx, lam_init=lam_init, n_blocks=nq)
    return pl.pallas_call(
        kernel,
        out_shape=jax.ShapeDtypeStruct((batch * seq, ATTN_W), BF16),
        grid=(batch, N_HEADS, nq + 1),
        in_specs=[
            pl.BlockSpec((tq, HEAD_W), lambda b, h, i: (b * nq + qblk(i), hb(OFF_Q) + h)),
            pl.BlockSpec((seq, HEAD_W), lambda b, h, i: (b, hb(OFF_K) + h)),
            pl.BlockSpec((seq, HEAD_W), lambda b, h, i: (b, hb(OFF_V) + h)),
            pl.BlockSpec((n_ctx, HEAD_W), lambda b, h, i: (ctx_rb0 + b, kc_blk + h)),
            pl.BlockSpec((n_ctx, HEAD_W), lambda b, h, i: (ctx_rb0 + b, vc_blk + h)),
            pl.BlockSpec((tq, HEAD_W), lambda b, h, i: (qblk(i), 0)),
            pl.BlockSpec((tq, HEAD_W), lambda b, h, i: (qblk(i), 0)),
            pl.BlockSpec((seq, HEAD_W), lambda b, h, i: (0, 0)),
            pl.BlockSpec((seq, HEAD_W), lambda b, h, i: (0, 0)),
            pl.BlockSpec((4, HEAD_DIM), lambda b, h, i: (0, 0)),
            pl.BlockSpec((1, HEAD_W), lambda b, h, i: (0, 0)),
        ],
        out_specs=pl.BlockSpec((tq, HEAD_W), lambda b, h, i: (b * nq + jnp.maximum(i - 1, 0), h)),
        scratch_shapes=[pltpu.VMEM((n_keys, HEAD_W), BF16), pltpu.VMEM((n_keys, 2 * HEAD_W), BF16),
                        pltpu.VMEM((2 * tq, n_keys), F32), pltpu.VMEM((2 * tq, LANES), F32),
                        pltpu.VMEM((2 * tq, n_keys), F32), pltpu.VMEM((2 * tq, LANES), F32)],
        compiler_params=_params(("arbitrary", "arbitrary", "arbitrary")),
        name="attn_lat",
    )(p, p, p, kv_ctx, kv_ctx, cos, sin, cos, sin, lam4, subg.reshape(1, HEAD_W))


def _attn_ctx(p, ctx_rb0, lam4, subg, *, batch, n_ctx, lam_init):
    hb = lambda off: off // HEAD_W
    kernel = functools.partial(_attn_ctx_kernel, lam_init=lam_init)
    return pl.pallas_call(
        kernel,
        out_shape=jax.ShapeDtypeStruct((batch * n_ctx, ATTN_W), BF16),
        grid=(batch, N_HEADS),
        in_specs=[
            pl.BlockSpec((n_ctx, HEAD_W), lambda b, h: (ctx_rb0 + b, hb(OFF_Q) + h)),
            pl.BlockSpec((n_ctx, HEAD_W), lambda b, h: (ctx_rb0 + b, hb(OFF_K) + h)),
            pl.BlockSpec((n_ctx, HEAD_W), lambda b, h: (ctx_rb0 + b, hb(OFF_V) + h)),
            pl.BlockSpec((4, HEAD_DIM), lambda b, h: (0, 0)),
            pl.BlockSpec((1, HEAD_W), lambda b, h: (0, 0)),
        ],
        out_specs=pl.BlockSpec((n_ctx, HEAD_W), lambda b, h: (b, h)),
        compiler_params=_params(("arbitrary", "arbitrary")),
        name="attn_ctx",
    )(p, p, p, lam4, subg.reshape(1, HEAD_W))


def _conv_kernel(bg_ref, cg_ref, xi_ref, a_ref, g_ref,
                 cg_p, xi_p, a_p, g_p, cg_n, xi_n, a_n, g_n,
                 scw_ref, cfw_ref, cfb_ref, lng_ref, lnb_ref,
                 sco_ref, cfo_ref, pad_ref, *, lat_blocks, blocks_per_seq):
    i = pl.program_id(0)
    in_lat = i < lat_blocks
    pos = i % blocks_per_seq
    has_prev = jnp.logical_and(in_lat, pos != 0)
    has_next = jnp.logical_and(in_lat, pos != blocks_per_seq - 1)
    keep_prev = jnp.where(has_prev, 1.0, 0.0)
    keep_next = jnp.where(has_next, 1.0, 0.0)
    lo, hi = CONV_HALO, CONV_HALO + CONV_ROWS

    def fill(main, prev, nxt):
        pad_ref[0:lo, :] = prev * keep_prev
        pad_ref[lo:hi, :] = main
        pad_ref[hi:hi + CONV_HALO, :] = nxt * keep_next

    def f(ref):
        return ref[...].astype(F32)

    def glu(a, g):
        return a * _sigmoid(g)

    fill(f(cg_ref) * f(xi_ref), f(cg_p) * f(xi_p), f(cg_n) * f(xi_n))
    for c in range(CONV_ROWS // CONV_CHUNK):
        r0 = lo + c * CONV_CHUNK - SC_TAPS // 2
        acc = scw_ref[0:1, :] * pad_ref[r0:r0 + CONV_CHUNK, :]
        for k in range(1, SC_TAPS):
            acc = acc + scw_ref[k:k + 1, :] * pad_ref[r0 + k:r0 + k + CONV_CHUNK, :]
        rows = slice(c * CONV_CHUNK, (c + 1) * CONV_CHUNK)
        sco_ref[rows, :] = (bg_ref[rows, :].astype(F32) * acc).astype(sco_ref.dtype)

    fill(glu(f(a_ref), f(g_ref)), glu(f(a_p), f(g_p)), glu(f(a_n), f(g_n)))
    off = lo - CF_TAPS // 2
    win = CONV_CHUNK + SUBLANES
    for c in range(CONV_ROWS // CONV_CHUNK):
        base = c * CONV_CHUNK
        acc = None
        for b in range(SUBLANES):
            q = None
            for k in range(b, CF_TAPS, SUBLANES):
                term = cfw_ref[k:k + 1, :] * pad_ref[base + k - b:base + k - b + win, :]
                q = term if q is None else q + term
            part = q[off + b:off + b + CONV_CHUNK, :]
            acc = part if acc is None else acc + part
        z = acc + cfb_ref[...]
        mu = jnp.mean(z, axis=-1, keepdims=True)
        zc = z - mu
        var = jnp.mean(zc * zc, axis=-1, keepdims=True)
        y = zc * lax.rsqrt(var + CF_LN_EPS) * lng_ref[...] + lnb_ref[...]
        rows = slice(c * CONV_CHUNK, (c + 1) * CONV_CHUNK)
        cfo_ref[rows, :] = (y * _sigmoid(y)).astype(cfo_ref.dtype)


def _convs(p, sc_w, cf_w, cf_b, ln_g, ln_b, *, n_rows, n_lat, seq):
    nb = n_rows // CONV_ROWS
    halo_per_block = CONV_ROWS // CONV_HALO
    last_halo = n_rows // CONV_HALO - 1
    cb = lambda off: off // SC_W

    def main(off):
        return pl.BlockSpec((CONV_ROWS, SC_W), lambda i: (i, cb(off)))

    def prev(off):
        return pl.BlockSpec((CONV_HALO, SC_W), lambda i: (jnp.maximum(i * halo_per_block - 1, 0), cb(off)))

    def nxt(off):
        return pl.BlockSpec((CONV_HALO, SC_W), lambda i: (jnp.minimum((i + 1) * halo_per_block, last_halo), cb(off)))

    def vec(rows):
        return pl.BlockSpec((rows, SC_W), lambda i: (0, 0))

    o_bg, o_cg, o_xi, o_a, o_g = OFF_SC, OFF_SC + SC_W, OFF_SC + 2 * SC_W, OFF_CF, OFF_CF + CF_W
    kernel = functools.partial(_conv_kernel, lat_blocks=n_lat // CONV_ROWS, blocks_per_seq=seq // CONV_ROWS)
    return pl.pallas_call(
        kernel,
        out_shape=(jax.ShapeDtypeStruct((n_rows, SC_W), BF16), jax.ShapeDtypeStruct((n_rows, CF_W), BF16)),
        grid=(nb,),
        in_specs=[main(o_bg), main(o_cg), main(o_xi), main(o_a), main(o_g),
                  prev(o_cg), prev(o_xi), prev(o_a), prev(o_g),
                  nxt(o_cg), nxt(o_xi), nxt(o_a), nxt(o_g),
                  vec(SC_TAPS), vec(CF_TAPS), vec(1), vec(1), vec(1)],
        out_specs=(pl.BlockSpec((CONV_ROWS, SC_W), lambda i: (i, 0)),
                   pl.BlockSpec((CONV_ROWS, CF_W), lambda i: (i, 0))),
        scratch_shapes=[pltpu.VMEM((CONV_ROWS + 2 * CONV_HALO, SC_W), F32)],
        compiler_params=_params(("arbitrary",)),
        name="convs",
    )(p, p, p, p, p, p, p, p, p, p, p, p, p,
      sc_w, cf_w, cf_b.reshape(1, CF_W), ln_g.reshape(1, CF_W), ln_b.reshape(1, CF_W))


def _merge_kernel(*refs, a_blocks):
    n_gate = 3 * MERGE_GATE_TILES
    ata_ref, atb_ref, sc_ref, cf_ref = refs[:4]
    gate_refs = refs[4:4 + n_gate]
    bg_ref, wa_ref, wb_ref, wc_ref, wm_ref, ha_ref, hb_ref, g1_ref, o_ref = refs[4 + n_gate:]
    d = D_MODEL
    from_a = pl.program_id(0) < a_blocks
    attn = jnp.where(from_a, ata_ref[...], atb_ref[...])
    h = jnp.where(from_a, ha_ref[...], hb_ref[...])

    def gate(branch):
        tiles = gate_refs[branch * MERGE_GATE_TILES:(branch + 1) * MERGE_GATE_TILES]
        pre = jnp.concatenate([t[...] for t in tiles], axis=1).astype(F32)
        return _sigmoid(pre + bg_ref[:, branch * d:(branch + 1) * d])

    ya = jnp.dot(attn, wa_ref[...], preferred_element_type=F32)
    yb = jnp.dot(sc_ref[...], wb_ref[...], preferred_element_type=F32)
    yc = jnp.dot(cf_ref[...], wc_ref[...], preferred_element_type=F32)
    m = (gate(0) * ya + gate(1) * yb + gate(2) * yc).astype(BF16)
    o_ref[...] = h + g1_ref[...] * jnp.dot(m, wm_ref[...], preferred_element_type=F32)


def _merge(attn, sc, cf, p, b_gate, wa, wb, wc, wm, rows, mods, *, n_rows, n_lat, seq, gate_idx, tm=256):
    d = D_MODEL
    (ata, atb), (ha, hb) = attn, rows
    a_blocks = ha.shape[0] // tm
    first = lambda w: pl.BlockSpec((tm, w), lambda i: (jnp.minimum(i, a_blocks - 1), 0))
    rest = lambda w: pl.BlockSpec((tm, w), lambda i: (jnp.maximum(i - a_blocks, 0), 0))
    tg = d // MERGE_GATE_TILES
    gb0 = OFF_GATE // tg
    gates = [pl.BlockSpec((tm, tg), functools.partial(lambda i, c: (i, c), c=gb0 + t))
             for t in range(3 * MERGE_GATE_TILES)]
    whole = lambda a: pl.BlockSpec(a.shape, lambda i: (0, 0))
    return pl.pallas_call(
        functools.partial(_merge_kernel, a_blocks=a_blocks),
        out_shape=jax.ShapeDtypeStruct((n_rows, d), F32),
        grid=(n_rows // tm,),
        in_specs=[
            first(ATTN_W), rest(ATTN_W),
            pl.BlockSpec((tm, SC_W), lambda i: (i, 0)),
            pl.BlockSpec((tm, CF_W), lambda i: (i, 0)),
            *gates,
            whole(b_gate), whole(wa), whole(wb), whole(wc), whole(wm),
            first(d), rest(d),
            pl.BlockSpec((None, 1, d), lambda i: (_mod_row(i * tm, n_lat, seq), 0, gate_idx)),
        ],
        out_specs=pl.BlockSpec((tm, d), lambda i: (i, 0)),
        compiler_params=_params(("arbitrary",)),
        name="merge",
    )(ata, atb, sc, cf, *([p] * (3 * MERGE_GATE_TILES)), b_gate, wa, wb, wc, wm, ha, hb, mods)


def _split2(x):
    hi = x.astype(BF16)
    return hi, (x - hi.astype(F32)).astype(BF16)


def _norm_mod(x, g, sh, sc):
    ms = jnp.mean(x * x, axis=-1, keepdims=True)
    return x * lax.rsqrt(ms + NORM_EPS) * g * (1.0 + sc) + sh


def _router_kernel(h_ref, g_ref, sh_ref, sc_ref, rw_ref, rb_ref, tri_ref, id_ref, wt_ref, cnt_ref, run_ref):
    @pl.when(pl.program_id(0) == 0)
    def _():
        run_ref[...] = jnp.zeros_like(run_ref)

    f = _norm_mod(h_ref[...], g_ref[...], sh_ref[...], sc_ref[...])

    f_hi, f_lo = _split2(f)
    w_hi, w_lo = _split2(rw_ref[...])
    dot = lambda a, b: jnp.dot(a, b, preferred_element_type=F32)
    both = dot(f_hi, jnp.concatenate([w_hi, w_lo], axis=1))
    logits = (dot(f_lo, w_hi) + both[:, LANES:] + both[:, :LANES]) + rb_ref[...]

    lane_i = lax.broadcasted_iota(I32, logits.shape, 1)
    lane = lane_i.astype(F32)
    neg = -jnp.inf
    big = float(LANES)
    lg = jnp.where(lane_i < N_GROUPS, logits, neg)
    mg = jnp.max(lg, axis=-1, keepdims=True)
    grp = jnp.min(jnp.where(lg == mg, lane, big), axis=-1, keepdims=True)
    p_grp = 1.0 / jnp.sum(jnp.exp(lg - mg), axis=-1, keepdims=True)

    e_lane = lane - N_GROUPS
    in_grp = jnp.logical_and(e_lane >= grp * EXPERTS_PER_GROUP, e_lane < (grp + 1.0) * EXPERTS_PER_GROUP)
    le = jnp.where(in_grp, logits, neg)
    m1 = jnp.max(le, axis=-1, keepdims=True)
    i1 = jnp.min(jnp.where(le == m1, lane, big), axis=-1, keepdims=True)
    le2 = jnp.where(lane == i1, neg, le)
    m2 = jnp.max(le2, axis=-1, keepdims=True)
    i2 = jnp.min(jnp.where(le2 == m2, lane, big), axis=-1, keepdims=True)
    e2 = jnp.exp(m2 - m1)
    w1 = p_grp / (1.0 + e2)
    w2 = p_grp * e2 / (1.0 + e2)
    e1, e2 = i1 - N_GROUPS, i2 - N_GROUPS
    hot1 = jnp.where(lane == e1, 1.0, 0.0)
    hot2 = jnp.where(lane == e2, 1.0, 0.0)
    tri = tri_ref[...]
    before1 = jnp.dot(tri, hot1.astype(BF16), preferred_element_type=F32)
    before2 = jnp.dot(tri, hot2.astype(BF16), preferred_element_type=F32)
    tot1 = jnp.sum(hot1, axis=0, keepdims=True)
    tot2 = jnp.sum(hot2, axis=0, keepdims=True)
    run = run_ref[...]
    r1 = jnp.sum(hot1 * (run + before1), axis=-1, keepdims=True)
    r2 = jnp.sum(hot2 * (run + tot1 + before2), axis=-1, keepdims=True)
    run = run + tot1 + tot2
    run_ref[...] = run
    cnt_ref[...] = run

    ids = jnp.where(lane_i == 0, e1, jnp.where(lane_i == 1, e2, jnp.where(lane_i == 2, r1,
                                                                        jnp.where(lane_i == 3, r2, 0.0))))
    id_ref[...] = ids.astype(I32)
    wt_ref[...] = jnp.where(lane_i == 0, w1, jnp.where(lane_i == 1, w2, 0.0))


def _router(h, gain, mods, rw, rb, *, n_rows, n_lat, seq, sh_idx, sc_idx, tm=512):
    d = D_MODEL

    def mod_map(idx):
        return lambda i: (_mod_row(i * tm, n_lat, seq), 0, idx)

    return pl.pallas_call(
        _router_kernel,
        out_shape=(jax.ShapeDtypeStruct((n_rows, LANES), I32),
                   jax.ShapeDtypeStruct((n_rows, LANES), F32),
                   jax.ShapeDtypeStruct((1, LANES), F32)),
        grid=(n_rows // tm,),
        in_specs=[
            pl.BlockSpec((tm, d), lambda i: (i, 0)),
            pl.BlockSpec((1, d), lambda i: (0, 0)),
            pl.BlockSpec((None, 1, d), mod_map(sh_idx)),
            pl.BlockSpec((None, 1, d), mod_map(sc_idx)),
            pl.BlockSpec((d, LANES), lambda i: (0, 0)),
            pl.BlockSpec((1, LANES), lambda i: (0, 0)),
            pl.BlockSpec((tm, tm), lambda i: (0, 0)),
        ],
        out_specs=(pl.BlockSpec((tm, LANES), lambda i: (i, 0)),
                   pl.BlockSpec((tm, LANES), lambda i: (i, 0)),
                   pl.BlockSpec((1, LANES), lambda i: (0, 0))),
        scratch_shapes=[pltpu.VMEM((1, LANES), F32)],
        compiler_params=_params(("arbitrary",)),
        name="router",
    )(h, gain.reshape(1, d), mods, mods, rw, rb, jnp.tril(jnp.ones((tm, tm), BF16), -1))


def _dispatch_kernel(pos_ref, h_ref, g_ref, sh_ref, sc_ref, xs_hbm, f0_ref, f1_ref, sem):
    i = pl.program_id(0)
    tm = h_ref.shape[0]

    def run(f_ref):
        def wait_block_copies():
            for _ in range(2):
                pltpu.make_async_copy(f_ref, xs_hbm.at[pl.ds(0, tm), :], sem).wait()

        f_ref[...] = _norm_mod(h_ref[...], g_ref[...], sh_ref[...], sc_ref[...])

        @pl.when(i > 0)
        def _():
            wait_block_copies()

        base = i * (2 * tm)
        for r in range(tm):
            for k in range(2):
                dst = pos_ref[base + 2 * r + k]
                pltpu.make_async_copy(f_ref.at[pl.ds(r, 1), :], xs_hbm.at[pl.ds(dst, 1), :], sem).start()

        @pl.when(i == pl.num_programs(0) - 1)
        def _():
            wait_block_copies()

    @pl.when(i % 2 == 0)
    def _():
        run(f0_ref)

    @pl.when(i % 2 == 1)
    def _():
        run(f1_ref)


def _dispatch(h, gain, mods, pos, *, n_rows, n_lat, seq, sh_idx, sc_idx, tm=SLOT_BLOCK_ROWS):
    d = D_MODEL

    def mod_map(idx):
        return lambda i, pos_ref: (_mod_row(i * tm, n_lat, seq), 0, idx)

    grid_spec = pltpu.PrefetchScalarGridSpec(
        num_scalar_prefetch=1,
        grid=(n_rows // tm,),
        in_specs=[
            pl.BlockSpec((tm, d), lambda i, pos_ref: (i, 0)),
            pl.BlockSpec((1, d), lambda i, pos_ref: (0, 0)),
            pl.BlockSpec((None, 1, d), mod_map(sh_idx)),
            pl.BlockSpec((None, 1, d), mod_map(sc_idx)),
        ],
        out_specs=pl.BlockSpec(memory_space=pl.ANY),
        scratch_shapes=[pltpu.VMEM((tm, d), F32), pltpu.VMEM((tm, d), F32), pltpu.SemaphoreType.DMA],
    )
    return pl.pallas_call(
        _dispatch_kernel,
        out_shape=jax.ShapeDtypeStruct((2 * n_rows, d), F32),
        grid_spec=grid_spec,
        compiler_params=_params(("arbitrary",)),
        name="dispatch",
    )(pos, h, gain.reshape(1, d), mods, mods)


def _moe_plan(ids, cnt, n_tok):
    n_items = N_EXPERTS + (2 * n_tok) // MOE_ITEM_CAP
    experts = jnp.arange(N_EXPERTS, dtype=I32)
    counts = cnt[0, :N_EXPERTS].astype(I32)
    starts = jnp.cumsum(counts) - counts
    e_slot, rank = ids[:, 0:2], ids[:, 2:4]
    start_of_slot = jnp.sum(jnp.where(e_slot[:, :, None] == experts, starts, 0), axis=-1)
    pos = (start_of_slot + rank).reshape(-1).astype(I32)
    pos = jnp.concatenate([pos, jnp.zeros((2 * SLOT_BLOCK_ROWS,), I32)])
    n_sb = (counts + MOE_ITEM_CAP - 1) // MOE_ITEM_CAP
    cum = jnp.cumsum(n_sb)
    total = cum[-1]
    it = jnp.arange(n_items, dtype=I32)
    e_i = jnp.minimum(jnp.sum((it[:, None] >= cum[None, :]).astype(I32), axis=1), N_EXPERTS - 1)
    sb = it - (cum[e_i] - n_sb[e_i])
    valid = it < total
    last_e = e_i[jnp.maximum(total - 1, 0)]
    item_e = jnp.where(valid, e_i, last_e).astype(I32)
    item_start = jnp.where(valid, starts[e_i] + sb * MOE_ITEM_CAP, 0).astype(I32)
    item_n = jnp.where(valid, jnp.minimum(MOE_ITEM_CAP, counts[e_i] - sb * MOE_ITEM_CAP), 0).astype(I32)
    front, back = jnp.zeros((MOE_ITEM_SHIFT,), I32), jnp.zeros((1,), I32)
    pad_items = lambda a: jnp.concatenate([front, a, back])
    return (pad_items(item_e), pad_items(item_start), pad_items(item_n)), pos


def _moe_kernel(ie_ref, ist_ref, in_ref,
                xs_hbm, wg_ref, wu_ref, wd_ref, ys_hbm,
                xf_ref, xb_ref, y_ref, gsem, ssem):
    i, j = pl.program_id(0), pl.program_id(1)
    sh = MOE_ITEM_SHIFT
    n_prev, n, n_next = in_ref[i + sh - 1], in_ref[i + sh], in_ref[i + sh + 1]
    start_prev, start, start_next = ist_ref[i + sh - 1], ist_ref[i + sh], ist_ref[i + sh + 1]
    par = i % 2

    def rows_in(hrow, vrow, size):
        return pltpu.make_async_copy(xs_hbm.at[pl.ds(hrow, size), :], xf_ref.at[pl.ds(vrow, size), :], gsem)

    def rows_out(hrow, vrow, size):
        return pltpu.make_async_copy(y_ref.at[par, pl.ds(vrow, size), :], ys_hbm.at[pl.ds(hrow, size), :], ssem)

    def row_groups(make, base, count, act):
        shift = base % SUBLANES
        head = jnp.minimum((SUBLANES - shift) % SUBLANES, count)
        body = ((count - head) // SUBLANES) * SUBLANES
        for t in range(SUBLANES - 1):
            @pl.when(t < head)
            def _():
                getattr(make(base + t, shift + t, 1), act)()

        for b in reversed(range(SUBLANES.bit_length() - 1, MOE_ITEM_ROWS.bit_length())):
            @pl.when(((body >> b) & 1) == 1)
            def _():
                off = head + ((body >> (b + 1)) << (b + 1))
                getattr(make(pl.multiple_of(base + off, SUBLANES), pl.multiple_of(shift + off, SUBLANES), 1 << b),
                        act)()

        done = head + body
        for t in range(SUBLANES - 1):
            @pl.when(t < count - done)
            def _():
                getattr(make(base + done + t, shift + done + t, 1), act)()

    @pl.when(jnp.logical_and(i == 0, j == 0))
    def _():
        xf_ref[...] = jnp.zeros_like(xf_ref)
        y_ref[...] = jnp.zeros_like(y_ref)

    @pl.when(j == 0)
    def _():
        @pl.when(i == 0)
        def _():
            row_groups(rows_in, start, n, "start")

        @pl.when(n > 0)
        def _():
            row_groups(rows_in, start, n, "wait")
            xb_ref[...] = xf_ref[...].astype(BF16)
            row_groups(rows_in, start_next, n_next, "start")

    def compute(m):
        x = xb_ref[0:m, :]
        g = jnp.dot(x, wg_ref[...].astype(BF16), preferred_element_type=F32)
        u = jnp.dot(x, wu_ref[...].astype(BF16), preferred_element_type=F32)
        hmid = (g * _sigmoid(g) * u).astype(BF16)
        y = jnp.dot(hmid, wd_ref[...].astype(BF16), preferred_element_type=F32)
        acc = y_ref.at[par]
        acc[0:m, :] = y + jnp.where(j == 0, 0.0, acc[0:m, :])

    used = jnp.where(n > 0, start % SUBLANES + n, 0)
    n_pad = (used + MOE_ROW_PAD - 1) // MOE_ROW_PAD
    for k in range(1, MOE_ITEM_ROWS // MOE_ROW_PAD + 1):
        @pl.when(n_pad == k)
        def _():
            compute(k * MOE_ROW_PAD)

    @pl.when(j == pl.num_programs(1) - 1)
    def _():
        row_groups(rows_out, start_prev, n_prev, "wait")

        row_groups(rows_out, start, n, "start")

        @pl.when(i == pl.num_programs(0) - 1)
        def _():
            row_groups(rows_out, start, n, "wait")


def _moe(xs, items, w_gu, w_down, layer):
    item_e, item_start, item_n = items
    d, hid, hk = D_MODEL, EXPERT_HIDDEN, MOE_HIDDEN_BLOCK
    nj = hid // hk
    sh = MOE_ITEM_SHIFT
    n_items = item_e.shape[0] - sh - 1

    def chunk(j, nn, i):
        return jnp.where(nn[i + sh] > 0, j, nj - 1)

    grid_spec = pltpu.PrefetchScalarGridSpec(
        num_scalar_prefetch=3,
        grid=(n_items, nj),
        in_specs=[
            pl.BlockSpec(memory_space=pl.ANY),
            pl.BlockSpec((None, None, d, hk), lambda i, j, ie, ist, nn: (layer, ie[i + sh], 0, chunk(j, nn, i))),
            pl.BlockSpec((None, None, d, hk), lambda i, j, ie, ist, nn: (layer, ie[i + sh], 0, nj + chunk(j, nn, i))),
            pl.BlockSpec((None, None, hk, d), lambda i, j, ie, ist, nn: (layer, ie[i + sh], chunk(j, nn, i), 0)),
        ],
        out_specs=pl.BlockSpec(memory_space=pl.ANY),
        scratch_shapes=[
            pltpu.VMEM((MOE_ITEM_ROWS, d), F32),
            pltpu.VMEM((MOE_ITEM_ROWS, d), BF16),
            pltpu.VMEM((2, MOE_ITEM_ROWS, d), F32),
            pltpu.SemaphoreType.DMA,
            pltpu.SemaphoreType.DMA,
        ],
    )
    return pl.pallas_call(
        _moe_kernel,
        out_shape=jax.ShapeDtypeStruct(xs.shape, F32),
        grid_spec=grid_spec,
        compiler_params=_params(("arbitrary", "arbitrary")),
        name="moe_experts",
    )(item_e, item_start, item_n, xs, w_gu, w_gu, w_down)


def _combine_kernel(pos_ref, h_ref, wt_ref, g2_ref, fg_ref, ys_hbm, o_ref, ybuf, sem, *, final):
    i = pl.program_id(0)
    tm = h_ref.shape[0]
    par = i % 2

    def fetch(step, half):
        base = step * (2 * tm)
        for r in range(tm):
            for k in range(2):
                src = pos_ref[base + 2 * r + k]
                pltpu.make_async_copy(ys_hbm.at[pl.ds(src, 1), :], ybuf.at[half, k, pl.ds(r, 1), :],
                                      sem.at[half]).start()

    @pl.when(i == 0)
    def _():
        fetch(0, 0)

    @pl.when(i < pl.num_programs(0) - 1)
    def _():
        fetch(i + 1, 1 - par)

    for k in range(2):
        pltpu.make_async_copy(ys_hbm.at[pl.ds(0, tm), :], ybuf.at[par, k], sem.at[par]).wait()
    wt = wt_ref[...]
    moe = wt[:, 0:1] * ybuf[par, 0] + wt[:, 1:2] * ybuf[par, 1]
    h = h_ref[...] + g2_ref[...] * moe
    if final:
        ms = jnp.mean(h * h, axis=-1, keepdims=True)
        h = h * lax.rsqrt(ms + NORM_EPS) * fg_ref[...]
    o_ref[...] = h


def _combine(h, ys, pos, wts, mods, final_g, *, n_rows, n_lat, seq, gate_idx, final, tm=SLOT_BLOCK_ROWS):
    d = D_MODEL
    kernel = functools.partial(_combine_kernel, final=final)
    grid_spec = pltpu.PrefetchScalarGridSpec(
        num_scalar_prefetch=1,
        grid=(n_rows // tm,),
        in_specs=[
            pl.BlockSpec((tm, d), lambda i, pos_ref: (i, 0)),
            pl.BlockSpec((tm, LANES), lambda i, pos_ref: (i, 0)),
            pl.BlockSpec((None, 1, d), lambda i, pos_ref: (_mod_row(i * tm, n_lat, seq), 0, gate_idx)),
            pl.BlockSpec((1, d), lambda i, pos_ref: (0, 0)),
            pl.BlockSpec(memory_space=pl.ANY),
        ],
        out_specs=pl.BlockSpec((tm, d), lambda i, pos_ref: (i, 0)),
        scratch_shapes=[pltpu.VMEM((2, 2, tm, d), F32), pltpu.SemaphoreType.DMA((2,))],
    )
    return pl.pallas_call(
        kernel,
        out_shape=jax.ShapeDtypeStruct((n_rows, d), F32),
        grid_spec=grid_spec,
        compiler_params=_params(("arbitrary",)),
        name="combine",
    )(pos, h, wts, mods, final_g.reshape(1, d), ys)


def kernel(x, c, ctx, c_ctx, ada_w, ada_b, norm1_g, w_in, b_gate, diff_lambda, subln_g, w_attn_out, sc_conv_w,
           w_sc_out, cf_dw_w, cf_dw_b, cf_ln_g, cf_ln_b, w_cf_out, w_mix, norm2_g, router_g_w, router_g_b,
           router_e_w, router_e_b, exp_w_gu, exp_w_down, final_g):
    batch, seq, d = x.shape
    n_ctx = ctx.shape[1]
    depth = ada_w.shape[0]
    n_lat = batch * seq
    n_all = n_lat + batch * n_ctx
    assert d == D_MODEL and batch == MOD_ROWS // 2 and w_in.shape[2] == C_TOT

    cond = jnp.concatenate([c, c_ctx[None, :], jnp.zeros((MOD_ROWS - batch - 1, d), F32)], axis=0)
    mods_all = _adaln(cond, ada_w, ada_b)
    cos, sin = _rope_tables(seq)
    rows = (x.reshape(n_lat, d), ctx.reshape(batch * n_ctx, d))
    geo = dict(n_lat=n_lat, seq=seq)

    for layer in range(depth):
        last = layer == depth - 1
        lam_init = 0.8 - 0.6 * math.exp(-0.3 * layer)
        mods = mods_all[layer].reshape(MOD_ROWS, 1, 6 * d)
        n_rows = n_lat if last else n_all

        p = _in_proj(rows, norm1_g[layer], mods, w_in, layer, row0=0, n_rows=n_rows, col0=0, n_cols=C_TOT,
                     sh_idx=0, sc_idx=1, tm=1024, **geo)
        if last:
            kv_ctx = _in_proj(rows, norm1_g[layer], mods, w_in, layer, row0=n_lat, n_rows=batch * n_ctx, col0=OFF_K,
                              n_cols=OFF_SC - OFF_K, sh_idx=0, sc_idx=1, tm=n_ctx, **geo)
            kc_blk, vc_blk, ctx_rb0 = 0, QK_W // HEAD_W, 0
        else:
            kv_ctx = p
            kc_blk, vc_blk, ctx_rb0 = OFF_K // HEAD_W, OFF_V // HEAD_W, n_lat // n_ctx
        attn = _attn_lat(p, kv_ctx, kc_blk, vc_blk, ctx_rb0, cos, sin, diff_lambda[layer], subln_g[layer],
                         batch=batch, seq=seq, n_ctx=n_ctx, lam_init=lam_init)
        if not last:
            attn = (attn, _attn_ctx(p, n_lat // n_ctx, diff_lambda[layer], subln_g[layer],
                                    batch=batch, n_ctx=n_ctx, lam_init=lam_init))
        else:
            attn = (attn, attn)
        sc, cf = _convs(p, sc_conv_w[layer], cf_dw_w[layer], cf_dw_b[layer], cf_ln_g[layer], cf_ln_b[layer],
                        n_rows=n_rows, **geo)
        h_mix = _merge(attn, sc, cf, p, b_gate[layer].reshape(1, 3 * d),
                       w_attn_out[layer].astype(BF16), w_sc_out[layer].astype(BF16),
                       w_cf_out[layer].astype(BF16), w_mix[layer].astype(BF16), rows, mods,
                       n_rows=n_rows, gate_idx=2, **geo)

        rw = jnp.concatenate([router_g_w[layer], router_e_w[layer],
                              jnp.zeros((d, LANES - N_GROUPS - N_EXPERTS), F32)], axis=1)
        rb = jnp.concatenate([router_g_b[layer], router_e_b[layer],
                              jnp.zeros((LANES - N_GROUPS - N_EXPERTS,), F32)]).reshape(1, LANES)
        ids, wts, cnt = _router(h_mix, norm2_g[layer], mods, rw, rb, n_rows=n_rows, sh_idx=3, sc_idx=4, **geo)
        items, pos = _moe_plan(ids, cnt, n_rows)
        xs = _dispatch(h_mix, norm2_g[layer], mods, pos, n_rows=n_rows, sh_idx=3, sc_idx=4, **geo)
        ys = _moe(xs, items, exp_w_gu, exp_w_down, layer)
        h = _combine(h_mix, ys, pos, wts, mods, final_g, n_rows=n_rows, gate_idx=5, final=last, **geo)
        rows = (h, h)

    return h.reshape(batch, seq, d)
```

```python
import functools
import math

import jax
import jax.numpy as jnp
from jax import lax
from jax.experimental import pallas as pl
from jax.experimental.pallas import tpu as pltpu

F32 = jnp.float32
BF16 = jnp.bfloat16
I32 = jnp.int32

D_MODEL = 2048
GRID_W = 64
NORM_EPS = 1e-6
N_HEADS = 8
HEAD_DIM = 64
HEAD_W = 2 * HEAD_DIM
QK_W = N_HEADS * HEAD_W
ATTN_W = N_HEADS * HEAD_W
ROPE_BASE = 10000.0
ROPE_FREQS = HEAD_DIM // 4
SUBLN_EPS = 1e-5
QK_SCALE = HEAD_DIM ** -0.5 * math.log2(math.e)
SC_W = D_MODEL // 4
CF_W = D_MODEL // 4
SC_TAPS = 3
CF_TAPS = 31
CF_LN_EPS = 1e-5
OFF_Q = 0
OFF_K = OFF_Q + QK_W
OFF_V = OFF_K + QK_W
OFF_SC = OFF_V + ATTN_W
OFF_CF = OFF_SC + 3 * SC_W
OFF_GATE = OFF_CF + 2 * CF_W
C_TOT = OFF_GATE + 3 * D_MODEL
N_GROUPS = 4
EXPERTS_PER_GROUP = 8
N_EXPERTS = N_GROUPS * EXPERTS_PER_GROUP
EXPERT_HIDDEN = D_MODEL // 2

LANES = 128
SUBLANES = 8
MOD_ROWS = 8
MERGE_GATE_TILES = 4
CONV_HALO = 16
CONV_ROWS = 256
CONV_CHUNK = 32
MOE_ITEM_ROWS = 1024
MOE_ITEM_CAP = MOE_ITEM_ROWS - 8
MOE_ROW_PAD = 128
MOE_HIDDEN_BLOCK = 256
MOE_ITEM_SHIFT = 1
SLOT_BLOCK_ROWS = 512
VMEM_LIMIT = 56 * 1024 * 1024


def _params(sem, vmem=VMEM_LIMIT):
    return pltpu.CompilerParams(dimension_semantics=sem, vmem_limit_bytes=vmem)


def _sigmoid(x):
    return 1.0 / (1.0 + jnp.exp(-x))


def _adaln_kernel(s_ref, w_ref, b_ref, o_ref):
    s = s_ref[...]
    s = s * _sigmoid(s)
    s_hi = s.astype(BF16)
    w = w_ref[...]
    w_hi = w.astype(BF16)
    w_lo = (w - w_hi.astype(F32)).astype(BF16)
    lhs = jnp.concatenate([s_hi.astype(F32), s - s_hi.astype(F32)], axis=0).astype(BF16)
    r = jnp.dot(lhs, w_hi, preferred_element_type=F32)
    r2 = jnp.dot(s_hi, w_lo, preferred_element_type=F32)
    o_ref[...] = r[:MOD_ROWS] + r[MOD_ROWS:] + r2 + b_ref[...]


def _adaln(cond, ada_w, ada_b):
    n_layers, d, n = ada_w.shape
    tn = 512
    return pl.pallas_call(
        _adaln_kernel,
        out_shape=jax.ShapeDtypeStruct((n_layers, MOD_ROWS, n), F32),
        grid=(n_layers, n // tn),
        in_specs=[
            pl.BlockSpec((MOD_ROWS, d), lambda l, j: (0, 0)),
            pl.BlockSpec((None, d, tn), lambda l, j: (l, 0, j)),
            pl.BlockSpec((None, 1, tn), lambda l, j: (l, 0, j)),
        ],
        out_specs=pl.BlockSpec((None, MOD_ROWS, tn), lambda l, j: (l, 0, j)),
        compiler_params=_params(("arbitrary", "arbitrary")),
        name="adaln",
    )(cond, ada_w, ada_b.reshape(n_layers, 1, n))


def _mod_row(row0, n_lat, seq):
    return jnp.where(row0 < n_lat, row0 // seq, MOD_ROWS // 2)


def _in_proj_kernel(xa_ref, xb_ref, g_ref, sh_ref, sc_ref, w_ref, o_ref, u_ref, *, a_blocks):
    first_col = pl.program_id(1) == 0
    from_a = pl.program_id(0) < a_blocks

    def normed(x_ref):
        x = x_ref[...]
        ms = jnp.mean(x * x, axis=-1, keepdims=True)
        y = x * lax.rsqrt(ms + NORM_EPS) * g_ref[...]
        u_ref[...] = (y * (1.0 + sc_ref[...]) + sh_ref[...]).astype(BF16)

    @pl.when(jnp.logical_and(first_col, from_a))
    def _():
        normed(xa_ref)

    @pl.when(jnp.logical_and(first_col, jnp.logical_not(from_a)))
    def _():
        normed(xb_ref)

    o_ref[...] = jnp.dot(u_ref[...], w_ref[...].astype(BF16), preferred_element_type=F32).astype(o_ref.dtype)


def _in_proj(rows, gain, mods, w_in, layer, *, row0, n_rows, col0, n_cols, n_lat, seq, sh_idx, sc_idx, tm, tn=512):
    xa, xb = rows
    d = xa.shape[1]
    rb0, cb0 = row0 // tm, col0 // tn
    a_blocks = xa.shape[0] // tm - rb0

    def mod_map(idx):
        return lambda i, j: (_mod_row((i + rb0) * tm, n_lat, seq), 0, idx)

    return pl.pallas_call(
        functools.partial(_in_proj_kernel, a_blocks=a_blocks),
        out_shape=jax.ShapeDtypeStruct((n_rows, n_cols), BF16),
        grid=(n_rows // tm, n_cols // tn),
        in_specs=[
            pl.BlockSpec((tm, d), lambda i, j: (jnp.minimum(i, a_blocks - 1) + rb0, 0)),
            pl.BlockSpec((tm, d), lambda i, j: (jnp.maximum(i - a_blocks, 0), 0)),
            pl.BlockSpec((1, d), lambda i, j: (0, 0)),
            pl.BlockSpec((None, 1, d), mod_map(sh_idx)),
            pl.BlockSpec((None, 1, d), mod_map(sc_idx)),
            pl.BlockSpec((None, d, tn), lambda i, j: (layer, 0, j + cb0)),
        ],
        out_specs=pl.BlockSpec((tm, tn), lambda i, j: (i, j)),
        scratch_shapes=[pltpu.VMEM((tm, d), BF16)],
        compiler_params=_params(("arbitrary", "arbitrary")),
        name="in_proj",
    )(xa, xb, gain.reshape(1, d), mods, mods, w_in)


def _rope_tables(n_tokens):
    t = jnp.arange(n_tokens, dtype=I32)
    pos = jnp.stack([t // GRID_W, t % GRID_W], axis=-1).astype(F32)
    inv_freq = ROPE_BASE ** (-jnp.arange(ROPE_FREQS, dtype=F32) / ROPE_FREQS)
    ang = pos[:, :, None] * inv_freq
    cos, sin = jnp.cos(ang), jnp.sin(ang)
    c = jnp.stack([cos, cos], axis=2).reshape(n_tokens, HEAD_DIM)
    s = jnp.stack([-sin, sin], axis=2).reshape(n_tokens, HEAD_DIM)
    return jnp.tile(c, (1, 2)), jnp.tile(s, (1, 2))


def _rope(x, c, s):
    lane = lax.broadcasted_iota(I32, x.shape, 1)
    first_half = (lane % (2 * ROPE_FREQS)) < ROPE_FREQS
    partner = jnp.where(first_half, pltpu.roll(x, LANES - ROPE_FREQS, 1), pltpu.roll(x, ROPE_FREQS, 1))
    return x * c + partner * s


def _diff_lambda(lam_ref, lam_init):
    lv = lam_ref[...]
    a = jnp.sum(lv[0:1] * lv[1:2], axis=-1, keepdims=True)
    b = jnp.sum(lv[2:3] * lv[3:4], axis=-1, keepdims=True)
    return jnp.exp(a) - jnp.exp(b) + lam_init


def _attend(q, k_all, v_all, lam, subg, lam_init):
    tq = q.shape[0]
    lane = lax.broadcasted_iota(I32, q.shape, 1)
    q0 = jnp.where(lane < HEAD_DIM, q, 0.0).astype(BF16)
    q1 = jnp.where(lane >= HEAD_DIM, q, 0.0).astype(BF16)
    qq = jnp.concatenate([q0, q1], axis=0)
    s = lax.dot_general(qq, k_all, (((1,), (1,)), ((), ())), preferred_element_type=F32)
    m = jnp.max(s, axis=-1, keepdims=True)
    e = jnp.exp2(s - m)
    l = jnp.sum(e, axis=-1, keepdims=True)
    a = e[:tq] - e[tq:] * (lam * l[:tq] / l[tq:])
    o = jnp.dot(a.astype(BF16), v_all, preferred_element_type=F32) / l[:tq]
    ms = jnp.mean(o * o, axis=-1, keepdims=True)
    return o * lax.rsqrt(ms + SUBLN_EPS) * subg * (1.0 - lam_init)


def _attn_lat_kernel(q_ref, kl_ref, vl_ref, kc_ref, vc_ref, cq_ref, sq_ref, ck_ref, sk_ref, lam_ref, g_ref,
                     o_ref, k_all, v_aug, s0_ref, m0_ref, s1_ref, m1_ref, *, n_ctx, lam_init, n_blocks):
    i = pl.program_id(2)
    tq = q_ref.shape[0]

    @pl.when(i == 0)
    def _():
        k_all[0:n_ctx, :] = kc_ref[...]
        k_all[n_ctx:, :] = _rope(kl_ref[...].astype(F32), ck_ref[...], sk_ref[...]).astype(BF16)
        v_aug[0:n_ctx, 0:HEAD_W] = vc_ref[...]
        v_aug[n_ctx:, 0:HEAD_W] = vl_ref[...]
        v_aug[:, HEAD_W:2 * HEAD_W] = jnp.ones((k_all.shape[0], HEAD_W), BF16)

    lam = _diff_lambda(lam_ref, lam_init)

    def stage_a(sa_ref, ma_ref):
        q = _rope(q_ref[...].astype(F32), cq_ref[...], sq_ref[...]) * QK_SCALE
        lane = lax.broadcasted_iota(I32, q.shape, 1)
        q0 = jnp.where(lane < HEAD_DIM, q, 0.0).astype(BF16)
        q1 = jnp.where(lane >= HEAD_DIM, q, 0.0).astype(BF16)
        qq = jnp.concatenate([q0, q1], axis=0)
        s = lax.dot_general(qq, k_all[...], (((1,), (1,)), ((), ())), preferred_element_type=F32)
        sa_ref[...] = s
        ma_ref[...] = jnp.broadcast_to(jnp.max(s, axis=-1, keepdims=True), ma_ref.shape)

    def stage_b(sb_ref, mb_ref):
        mb = jnp.concatenate([mb_ref[...]] * (sb_ref.shape[1] // LANES), axis=1)
        e = jnp.exp2(sb_ref[...] - mb).astype(BF16)
        oa = jnp.dot(e, v_aug[...], preferred_element_type=F32)
        o = oa[:tq, 0:HEAD_W] / oa[:tq, HEAD_W:] - oa[tq:, 0:HEAD_W] * (lam / oa[tq:, HEAD_W:])
        ms = jnp.mean(o * o, axis=-1, keepdims=True)
        o_ref[...] = (o * lax.rsqrt(ms + SUBLN_EPS) * g_ref[...] * (1.0 - lam_init)).astype(o_ref.dtype)

    bufs = ((s0_ref, m0_ref), (s1_ref, m1_ref))

    @pl.when(i == 0)
    def _():
        stage_a(*bufs[0])

    for par in range(2):
        @pl.when(jnp.logical_and(jnp.logical_and(i > 0, i < n_blocks), i % 2 == par))
        def _():
            stage_a(*bufs[par])
            stage_b(*bufs[1 - par])

    @pl.when(i == n_blocks)
    def _():
        stage_b(*bufs[(n_blocks - 1) % 2])


def _attn_ctx_kernel(q_ref, k_ref, v_ref, lam_ref, g_ref, o_ref, *, lam_init):
    q = q_ref[...].astype(F32) * QK_SCALE
    lam = _diff_lambda(lam_ref, lam_init)
    o_ref[...] = _attend(q, k_ref[...], v_ref[...], lam, g_ref[...], lam_init).astype(o_ref.dtype)


def _attn_lat(p, kv_ctx, kc_blk, vc_blk, ctx_rb0, cos, sin, lam4, subg, *, batch, seq, n_ctx, lam_init, tq=512):
    nq = seq // tq
    n_keys = n_ctx + seq
    hb = lambda off: off // HEAD_W
    qblk = lambda i: jnp.minimum(i, nq - 1)
    kernel = functools.partial(_attn_lat_kernel, n_ctx=n_ctx, lam_init=lam_init, n_blocks=nq)
    return pl.pallas_call(
        kernel,
        out_shape=jax.ShapeDtypeStruct((batch * seq, ATTN_W), BF16),
        grid=(batch, N_HEADS, nq + 1),
        in_specs=[
            pl.BlockSpec((tq, HEAD_W), lambda b, h, i: (b * nq + qblk(i), hb(OFF_Q) + h)),
            pl.BlockSpec((seq, HEAD_W), lambda b, h, i: (b, hb(OFF_K) + h)),
            pl.BlockSpec((seq, HEAD_W), lambda b, h, i: (b, hb(OFF_V) + h)),
            pl.BlockSpec((n_ctx, HEAD_W), lambda b, h, i: (ctx_rb0 + b, kc_blk + h)),
            pl.BlockSpec((n_ctx, HEAD_W), lambda b, h, i: (ctx_rb0 + b, vc_blk + h)),
            pl.BlockSpec((tq, HEAD_W), lambda b, h, i: (qblk(i), 0)),
            pl.BlockSpec((tq, HEAD_W), lambda b, h, i: (qblk(i), 0)),
            pl.BlockSpec((seq, HEAD_W), lambda b, h, i: (0, 0)),
            pl.BlockSpec((seq, HEAD_W), lambda b, h, i: (0, 0)),
            pl.BlockSpec((4, HEAD_DIM), lambda b, h, i: (0, 0)),
            pl.BlockSpec((1, HEAD_W), lambda b, h, i: (0, 0)),
        ],
        out_specs=pl.BlockSpec((tq, HEAD_W), lambda b, h, i: (b * nq + jnp.maximum(i - 1, 0), h)),
        scratch_shapes=[pltpu.VMEM((n_keys, HEAD_W), BF16), pltpu.VMEM((n_keys, 2 * HEAD_W), BF16),
                        pltpu.VMEM((2 * tq, n_keys), F32), pltpu.VMEM((2 * tq, LANES), F32),
                        pltpu.VMEM((2 * tq, n_keys), F32), pltpu.VMEM((2 * tq, LANES), F32)],
        compiler_params=_params(("arbitrary", "arbitrary", "arbitrary")),
        name="attn_lat",
    )(p, p, p, kv_ctx, kv_ctx, cos, sin, cos, sin, lam4, subg.reshape(1, HEAD_W))


def _attn_ctx(p, ctx_rb0, lam4, subg, *, batch, n_ctx, lam_init):
    hb = lambda off: off // HEAD_W
    kernel = functools.partial(_attn_ctx_kernel, lam_init=lam_init)
    return pl.pallas_call(
        kernel,
        out_shape=jax.ShapeDtypeStruct((batch * n_ctx, ATTN_W), BF16),
        grid=(batch, N_HEADS),
        in_specs=[
            pl.BlockSpec((n_ctx, HEAD_W), lambda b, h: (ctx_rb0 + b, hb(OFF_Q) + h)),
            pl.BlockSpec((n_ctx, HEAD_W), lambda b, h: (ctx_rb0 + b, hb(OFF_K) + h)),
            pl.BlockSpec((n_ctx, HEAD_W), lambda b, h: (ctx_rb0 + b, hb(OFF_V) + h)),
            pl.BlockSpec((4, HEAD_DIM), lambda b, h: (0, 0)),
            pl.BlockSpec((1, HEAD_W), lambda b, h: (0, 0)),
        ],
        out_specs=pl.BlockSpec((n_ctx, HEAD_W), lambda b, h: (b, h)),
        compiler_params=_params(("arbitrary", "arbitrary")),
        name="attn_ctx",
    )(p, p, p, lam4, subg.reshape(1, HEAD_W))


def _conv_kernel(bg_ref, cg_ref, xi_ref, a_ref, g_ref,
                 cg_p, xi_p, a_p, g_p, cg_n, xi_n, a_n, g_n,
                 scw_ref, cfw_ref, cfb_ref, lng_ref, lnb_ref,
                 sco_ref, cfo_ref, pad_ref, *, lat_blocks, blocks_per_seq):
    i = pl.program_id(0)
    in_lat = i < lat_blocks
    pos = i % blocks_per_seq
    has_prev = jnp.logical_and(in_lat, pos != 0)
    has_next = jnp.logical_and(in_lat, pos != blocks_per_seq - 1)
    keep_prev = jnp.where(has_prev, 1.0, 0.0)
    keep_next = jnp.where(has_next, 1.0, 0.0)
    lo, hi = CONV_HALO, CONV_HALO + CONV_ROWS

    def fill(main, prev, nxt):
        pad_ref[0:lo, :] = prev * keep_prev
        pad_ref[lo:hi, :] = main
        pad_ref[hi:hi + CONV_HALO, :] = nxt * keep_next

    def f(ref):
        return ref[...].astype(F32)

    def glu(a, g):
        return a * _sigmoid(g)

    fill(f(cg_ref) * f(xi_ref), f(cg_p) * f(xi_p), f(cg_n) * f(xi_n))
    for c in range(CONV_ROWS // CONV_CHUNK):
        r0 = lo + c * CONV_CHUNK - SC_TAPS // 2
        acc = scw_ref[0:1, :] * pad_ref[r0:r0 + CONV_CHUNK, :]
        for k in range(1, SC_TAPS):
            acc = acc + scw_ref[k:k + 1, :] * pad_ref[r0 + k:r0 + k + CONV_CHUNK, :]
        rows = slice(c * CONV_CHUNK, (c + 1) * CONV_CHUNK)
        sco_ref[rows, :] = (bg_ref[rows, :].astype(F32) * acc).astype(sco_ref.dtype)

    fill(glu(f(a_ref), f(g_ref)), glu(f(a_p), f(g_p)), glu(f(a_n), f(g_n)))
    off = lo - CF_TAPS // 2
    win = CONV_CHUNK + SUBLANES
    for c in range(CONV_ROWS // CONV_CHUNK):
        base = c * CONV_CHUNK
        acc = None
        for b in range(SUBLANES):
            q = None
            for k in range(b, CF_TAPS, SUBLANES):
                term = cfw_ref[k:k + 1, :] * pad_ref[base + k - b:base + k - b + win, :]
                q = term if q is None else q + term
            part = q[off + b:off + b + CONV_CHUNK, :]
            acc = part if acc is None else acc + part
        z = acc + cfb_ref[...]
        mu = jnp.mean(z, axis=-1, keepdims=True)
        zc = z - mu
        var = jnp.mean(zc * zc, axis=-1, keepdims=True)
        y = zc * lax.rsqrt(var + CF_LN_EPS) * lng_ref[...] + lnb_ref[...]
        rows = slice(c * CONV_CHUNK, (c + 1) * CONV_CHUNK)
        cfo_ref[rows, :] = (y * _sigmoid(y)).astype(cfo_ref.dtype)


def _convs(p, sc_w, cf_w, cf_b, ln_g, ln_b, *, n_rows, n_lat, seq):
    nb = n_rows // CONV_ROWS
    halo_per_block = CONV_ROWS // CONV_HALO
    last_halo = n_rows // CONV_HALO - 1
    cb = lambda off: off // SC_W

    def main(off):
        return pl.BlockSpec((CONV_ROWS, SC_W), lambda i: (i, cb(off)))

    def prev(off):
        return pl.BlockSpec((CONV_HALO, SC_W), lambda i: (jnp.maximum(i * halo_per_block - 1, 0), cb(off)))

    def nxt(off):
        return pl.BlockSpec((CONV_HALO, SC_W), lambda i: (jnp.minimum((i + 1) * halo_per_block, last_halo), cb(off)))

    def vec(rows):
        return pl.BlockSpec((rows, SC_W), lambda i: (0, 0))

    o_bg, o_cg, o_xi, o_a, o_g = OFF_SC, OFF_SC + SC_W, OFF_SC + 2 * SC_W, OFF_CF, OFF_CF + CF_W
    kernel = functools.partial(_conv_kernel, lat_blocks=n_lat // CONV_ROWS, blocks_per_seq=seq // CONV_ROWS)
    return pl.pallas_call(
        kernel,
        out_shape=(jax.ShapeDtypeStruct((n_rows, SC_W), BF16), jax.ShapeDtypeStruct((n_rows, CF_W), BF16)),
        grid=(nb,),
        in_specs=[main(o_bg), main(o_cg), main(o_xi), main(o_a), main(o_g),
                  prev(o_cg), prev(o_xi), prev(o_a), prev(o_g),
                  nxt(o_cg), nxt(o_xi), nxt(o_a), nxt(o_g),
                  vec(SC_TAPS), vec(CF_TAPS), vec(1), vec(1), vec(1)],
        out_specs=(pl.BlockSpec((CONV_ROWS, SC_W), lambda i: (i, 0)),
                   pl.BlockSpec((CONV_ROWS, CF_W), lambda i: (i, 0))),
        scratch_shapes=[pltpu.VMEM((CONV_ROWS + 2 * CONV_HALO, SC_W), F32)],
        compiler_params=_params(("arbitrary",)),
        name="convs",
    )(p, p, p, p, p, p, p, p, p, p, p, p, p,
      sc_w, cf_w, cf_b.reshape(1, CF_W), ln_g.reshape(1, CF_W), ln_b.reshape(1, CF_W))


def _merge_kernel(*refs, a_blocks):
    n_gate = 3 * MERGE_GATE_TILES
    ata_ref, atb_ref, sc_ref, cf_ref = refs[:4]
    gate_refs = refs[4:4 + n_gate]
    bg_ref, wa_ref, wb_ref, wc_ref, wm_ref, ha_ref, hb_ref, g1_ref, o_ref = refs[4 + n_gate:]
    d = D_MODEL
    from_a = pl.program_id(0) < a_blocks
    attn = jnp.where(from_a, ata_ref[...], atb_ref[...])
    h = jnp.where(from_a, ha_ref[...], hb_ref[...])

    def gate(branch):
        tiles = gate_refs[branch * MERGE_GATE_TILES:(branch + 1) * MERGE_GATE_TILES]
        pre = jnp.concatenate([t[...] for t in tiles], axis=1).astype(F32)
        return _sigmoid(pre + bg_ref[:, branch * d:(branch + 1) * d])

    ya = jnp.dot(attn, wa_ref[...], preferred_element_type=F32)
    yb = jnp.dot(sc_ref[...], wb_ref[...], preferred_element_type=F32)
    yc = jnp.dot(cf_ref[...], wc_ref[...], preferred_element_type=F32)
    m = (gate(0) * ya + gate(1) * yb + gate(2) * yc).astype(BF16)
    o_ref[...] = h + g1_ref[...] * jnp.dot(m, wm_ref[...], preferred_element_type=F32)


def _merge(attn, sc, cf, p, b_gate, wa, wb, wc, wm, rows, mods, *, n_rows, n_lat, seq, gate_idx, tm=256):
    d = D_MODEL
    (ata, atb), (ha, hb) = attn, rows
    a_blocks = ha.shape[0] // tm
    first = lambda w: pl.BlockSpec((tm, w), lambda i: (jnp.minimum(i, a_blocks - 1), 0))
    rest = lambda w: pl.BlockSpec((tm, w), lambda i: (jnp.maximum(i - a_blocks, 0), 0))
    tg = d // MERGE_GATE_TILES
    gb0 = OFF_GATE // tg
    gates = [pl.BlockSpec((tm, tg), functools.partial(lambda i, c: (i, c), c=gb0 + t))
             for t in range(3 * MERGE_GATE_TILES)]
    whole = lambda a: pl.BlockSpec(a.shape, lambda i: (0, 0))
    return pl.pallas_call(
        functools.partial(_merge_kernel, a_blocks=a_blocks),
        out_shape=jax.ShapeDtypeStruct((n_rows, d), F32),
        grid=(n_rows // tm,),
        in_specs=[
            first(ATTN_W), rest(ATTN_W),
            pl.BlockSpec((tm, SC_W), lambda i: (i, 0)),
            pl.BlockSpec((tm, CF_W), lambda i: (i, 0)),
            *gates,
            whole(b_gate), whole(wa), whole(wb), whole(wc), whole(wm),
            first(d), rest(d),
            pl.BlockSpec((None, 1, d), lambda i: (_mod_row(i * tm, n_lat, seq), 0, gate_idx)),
        ],
        out_specs=pl.BlockSpec((tm, d), lambda i: (i, 0)),
        compiler_params=_params(("arbitrary",)),
        name="merge",
    )(ata, atb, sc, cf, *([p] * (3 * MERGE_GATE_TILES)), b_gate, wa, wb, wc, wm, ha, hb, mods)


def _split2(x):
    hi = x.astype(BF16)
    return hi, (x - hi.astype(F32)).astype(BF16)


def _norm_mod(x, g, sh, sc):
    ms = jnp.mean(x * x, axis=-1, keepdims=True)
    return x * lax.rsqrt(ms + NORM_EPS) * g * (1.0 + sc) + sh


def _router_kernel(h_ref, g_ref, sh_ref, sc_ref, rw_ref, rb_ref, tri_ref, id_ref, wt_ref, cnt_ref, run_ref):
    @pl.when(pl.program_id(0) == 0)
    def _():
        run_ref[...] = jnp.zeros_like(run_ref)

    f = _norm_mod(h_ref[...], g_ref[...], sh_ref[...], sc_ref[...])

    f_hi, f_lo = _split2(f)
    w_hi, w_lo = _split2(rw_ref[...])
    dot = lambda a, b: jnp.dot(a, b, preferred_element_type=F32)
    both = dot(f_hi, jnp.concatenate([w_hi, w_lo], axis=1))
    logits = (dot(f_lo, w_hi) + both[:, LANES:] + both[:, :LANES]) + rb_ref[...]

    lane_i = lax.broadcasted_iota(I32, logits.shape, 1)
    lane = lane_i.astype(F32)
    neg = -jnp.inf
    big = float(LANES)
    lg = jnp.where(lane_i < N_GROUPS, logits, neg)
    mg = jnp.max(lg, axis=-1, keepdims=True)
    grp = jnp.min(jnp.where(lg == mg, lane, big), axis=-1, keepdims=True)
    p_grp = 1.0 / jnp.sum(jnp.exp(lg - mg), axis=-1, keepdims=True)

    e_lane = lane - N_GROUPS
    in_grp = jnp.logical_and(e_lane >= grp * EXPERTS_PER_GROUP, e_lane < (grp + 1.0) * EXPERTS_PER_GROUP)
    le = jnp.where(in_grp, logits, neg)
    m1 = jnp.max(le, axis=-1, keepdims=True)
    i1 = jnp.min(jnp.where(le == m1, lane, big), axis=-1, keepdims=True)
    le2 = jnp.where(lane == i1, neg, le)
    m2 = jnp.max(le2, axis=-1, keepdims=True)
    i2 = jnp.min(jnp.where(le2 == m2, lane, big), axis=-1, keepdims=True)
    e2 = jnp.exp(m2 - m1)
    w1 = p_grp / (1.0 + e2)
    w2 = p_grp * e2 / (1.0 + e2)
    e1, e2 = i1 - N_GROUPS, i2 - N_GROUPS
    hot1 = jnp.where(lane == e1, 1.0, 0.0)
    hot2 = jnp.where(lane == e2, 1.0, 0.0)
    tri = tri_ref[...]
    before1 = jnp.dot(tri, hot1.astype(BF16), preferred_element_type=F32)
    before2 = jnp.dot(tri, hot2.astype(BF16), preferred_element_type=F32)
    tot1 = jnp.sum(hot1, axis=0, keepdims=True)
    tot2 = jnp.sum(hot2, axis=0, keepdims=True)
    run = run_ref[...]
    r1 = jnp.sum(hot1 * (run + before1), axis=-1, keepdims=True)
    r2 = jnp.sum(hot2 * (run + tot1 + before2), axis=-1, keepdims=True)
    run = run + tot1 + tot2
    run_ref[...] = run
    cnt_ref[...] = run

    ids = jnp.where(lane_i == 0, e1, jnp.where(lane_i == 1, e2, jnp.where(lane_i == 2, r1,
                                                                        jnp.where(lane_i == 3, r2, 0.0))))
    id_ref[...] = ids.astype(I32)
    wt_ref[...] = jnp.where(lane_i == 0, w1, jnp.where(lane_i == 1, w2, 0.0))


def _router(h, gain, mods, rw, rb, *, n_rows, n_lat, seq, sh_idx, sc_idx, tm=1024):
    d = D_MODEL

    def mod_map(idx):
        return lambda i: (_mod_row(i * tm, n_lat, seq), 0, idx)

    return pl.pallas_call(
        _router_kernel,
        out_shape=(jax.ShapeDtypeStruct((n_rows, LANES), I32),
                   jax.ShapeDtypeStruct((n_rows, LANES), F32),
                   jax.ShapeDtypeStruct((1, LANES), F32)),
        grid=(n_rows // tm,),
        in_specs=[
            pl.BlockSpec((tm, d), lambda i: (i, 0)),
            pl.BlockSpec((1, d), lambda i: (0, 0)),
            pl.BlockSpec((None, 1, d), mod_map(sh_idx)),
            pl.BlockSpec((None, 1, d), mod_map(sc_idx)),
            pl.BlockSpec((d, LANES), lambda i: (0, 0)),
            pl.BlockSpec((1, LANES), lambda i: (0, 0)),
            pl.BlockSpec((tm, tm), lambda i: (0, 0)),
        ],
        out_specs=(pl.BlockSpec((tm, LANES), lambda i: (i, 0)),
                   pl.BlockSpec((tm, LANES), lambda i: (i, 0)),
                   pl.BlockSpec((1, LANES), lambda i: (0, 0))),
        scratch_shapes=[pltpu.VMEM((1, LANES), F32)],
        compiler_params=_params(("arbitrary",)),
        name="router",
    )(h, gain.reshape(1, d), mods, mods, rw, rb, jnp.tril(jnp.ones((tm, tm), BF16), -1))


def _dispatch_kernel(pos_ref, h_ref, g_ref, sh_ref, sc_ref, xs_hbm, f0_ref, f1_ref, sem):
    i = pl.program_id(0)
    tm = h_ref.shape[0]

    def run(f_ref):
        def wait_block_copies():
            for _ in range(2):
                pltpu.make_async_copy(f_ref, xs_hbm.at[pl.ds(0, tm), :], sem).wait()

        f_ref[...] = _norm_mod(h_ref[...], g_ref[...], sh_ref[...], sc_ref[...])

        @pl.when(i > 0)
        def _():
            wait_block_copies()

        base = i * (2 * tm)
        for r in range(tm):
            for k in range(2):
                dst = pos_ref[base + 2 * r + k]
                pltpu.make_async_copy(f_ref.at[pl.ds(r, 1), :], xs_hbm.at[pl.ds(dst, 1), :], sem).start()

        @pl.when(i == pl.num_programs(0) - 1)
        def _():
            wait_block_copies()

    @pl.when(i % 2 == 0)
    def _():
        run(f0_ref)

    @pl.when(i % 2 == 1)
    def _():
        run(f1_ref)


def _dispatch(h, gain, mods, pos, *, n_rows, n_lat, seq, sh_idx, sc_idx, tm=SLOT_BLOCK_ROWS):
    d = D_MODEL

    def mod_map(idx):
        return lambda i, pos_ref: (_mod_row(i * tm, n_lat, seq), 0, idx)

    grid_spec = pltpu.PrefetchScalarGridSpec(
        num_scalar_prefetch=1,
        grid=(n_rows // tm,),
        in_specs=[
            pl.BlockSpec((tm, d), lambda i, pos_ref: (i, 0)),
            pl.BlockSpec((1, d), lambda i, pos_ref: (0, 0)),
            pl.BlockSpec((None, 1, d), mod_map(sh_idx)),
            pl.BlockSpec((None, 1, d), mod_map(sc_idx)),
        ],
        out_specs=pl.BlockSpec(memory_space=pl.ANY),
        scratch_shapes=[pltpu.VMEM((tm, d), F32), pltpu.VMEM((tm, d), F32), pltpu.SemaphoreType.DMA],
    )
    return pl.pallas_call(
        _dispatch_kernel,
        out_shape=jax.ShapeDtypeStruct((2 * n_rows, d), F32),
        grid_spec=grid_spec,
        compiler_params=_params(("arbitrary",)),
        name="dispatch",
    )(pos, h, gain.reshape(1, d), mods, mods)


def _moe_plan(ids, cnt, n_tok):
    n_items = N_EXPERTS + (2 * n_tok) // MOE_ITEM_CAP
    experts = jnp.arange(N_EXPERTS, dtype=I32)
    counts = cnt[0, :N_EXPERTS].astype(I32)
    starts = jnp.cumsum(counts) - counts
    e_slot, rank = ids[:, 0:2], ids[:, 2:4]
    start_of_slot = jnp.sum(jnp.where(e_slot[:, :, None] == experts, starts, 0), axis=-1)
    pos = (start_of_slot + rank).reshape(-1).astype(I32)
    pos = jnp.concatenate([pos, jnp.zeros((2 * SLOT_BLOCK_ROWS,), I32)])
    n_sb = (counts + MOE_ITEM_CAP - 1) // MOE_ITEM_CAP
    cum = jnp.cumsum(n_sb)
    total = cum[-1]
    it = jnp.arange(n_items, dtype=I32)
    e_i = jnp.minimum(jnp.sum((it[:, None] >= cum[None, :]).astype(I32), axis=1), N_EXPERTS - 1)
    sb = it - (cum[e_i] - n_sb[e_i])
    valid = it < total
    last_e = e_i[jnp.maximum(total - 1, 0)]
    item_e = jnp.where(valid, e_i, last_e).astype(I32)
    item_start = jnp.where(valid, starts[e_i] + sb * MOE_ITEM_CAP, 0).astype(I32)
    item_n = jnp.where(valid, jnp.minimum(MOE_ITEM_CAP, counts[e_i] - sb * MOE_ITEM_CAP), 0).astype(I32)
    front, back = jnp.zeros((MOE_ITEM_SHIFT,), I32), jnp.zeros((1,), I32)
    pad_items = lambda a: jnp.concatenate([front, a, back])
    return (pad_items(item_e), pad_items(item_start), pad_items(item_n)), pos


def _moe_kernel(ie_ref, ist_ref, in_ref,
                xs_hbm, wg_ref, wu_ref, wd_ref, ys_hbm,
                xf_ref, xb_ref, y_ref, gsem, ssem):
    i, j = pl.program_id(0), pl.program_id(1)
    sh = MOE_ITEM_SHIFT
    n_prev, n, n_next = in_ref[i + sh - 1], in_ref[i + sh], in_ref[i + sh + 1]
    start_prev, start, start_next = ist_ref[i + sh - 1], ist_ref[i + sh], ist_ref[i + sh + 1]
    par = i % 2

    def rows_in(hrow, vrow, size):
        return pltpu.make_async_copy(xs_hbm.at[pl.ds(hrow, size), :], xf_ref.at[pl.ds(vrow, size), :], gsem)

    def rows_out(hrow, vrow, size):
        return pltpu.make_async_copy(y_ref.at[par, pl.ds(vrow, size), :], ys_hbm.at[pl.ds(hrow, size), :], ssem)

    def row_groups(make, base, count, act):
        shift = base % SUBLANES
        head = jnp.minimum((SUBLANES - shift) % SUBLANES, count)
        body = ((count - head) // SUBLANES) * SUBLANES
        for t in range(SUBLANES - 1):
            @pl.when(t < head)
            def _():
                getattr(make(base + t, shift + t, 1), act)()

        for b in reversed(range(SUBLANES.bit_length() - 1, MOE_ITEM_ROWS.bit_length())):
            @pl.when(((body >> b) & 1) == 1)
            def _():
                off = head + ((body >> (b + 1)) << (b + 1))
                getattr(make(pl.multiple_of(base + off, SUBLANES), pl.multiple_of(shift + off, SUBLANES), 1 << b),
                        act)()

        done = head + body
        for t in range(SUBLANES - 1):
            @pl.when(t < count - done)
            def _():
                getattr(make(base + done + t, shift + done + t, 1), act)()

    @pl.when(jnp.logical_and(i == 0, j == 0))
    def _():
        xf_ref[...] = jnp.zeros_like(xf_ref)
        y_ref[...] = jnp.zeros_like(y_ref)

    @pl.when(j == 0)
    def _():
        @pl.when(i == 0)
        def _():
            row_groups(rows_in, start, n, "start")

        @pl.when(n > 0)
        def _():
            row_groups(rows_in, start, n, "wait")
            xb_ref[...] = xf_ref[...].astype(BF16)
            row_groups(rows_in, start_next, n_next, "start")

    def compute(m):
        x = xb_ref[0:m, :]
        g = jnp.dot(x, wg_ref[...].astype(BF16), preferred_element_type=F32)
        u = jnp.dot(x, wu_ref[...].astype(BF16), preferred_element_type=F32)
        hmid = (g * _sigmoid(g) * u).astype(BF16)
        y = jnp.dot(hmid, wd_ref[...].astype(BF16), preferred_element_type=F32)
        acc = y_ref.at[par]
        acc[0:m, :] = y + jnp.where(j == 0, 0.0, acc[0:m, :])

    used = jnp.where(n > 0, start % SUBLANES + n, 0)
    n_pad = (used + MOE_ROW_PAD - 1) // MOE_ROW_PAD
    for k in range(1, MOE_ITEM_ROWS // MOE_ROW_PAD + 1):
        @pl.when(n_pad == k)
        def _():
            compute(k * MOE_ROW_PAD)

    @pl.when(j == pl.num_programs(1) - 1)
    def _():
        row_groups(rows_out, start_prev, n_prev, "wait")

        row_groups(rows_out, start, n, "start")

        @pl.when(i == pl.num_programs(0) - 1)
        def _():
            row_groups(rows_out, start, n, "wait")


def _moe(xs, items, w_gu, w_down, layer):
    item_e, item_start, item_n = items
    d, hid, hk = D_MODEL, EXPERT_HIDDEN, MOE_HIDDEN_BLOCK
    nj = hid // hk
    sh = MOE_ITEM_SHIFT
    n_items = item_e.shape[0] - sh - 1

    def chunk(j, nn, i):
        return jnp.where(nn[i + sh] > 0, j, nj - 1)

    grid_spec = pltpu.PrefetchScalarGridSpec(
        num_scalar_prefetch=3,
        grid=(n_items, nj),
        in_specs=[
            pl.BlockSpec(memory_space=pl.ANY),
            pl.BlockSpec((None, None, d, hk), lambda i, j, ie, ist, nn: (layer, ie[i + sh], 0, chunk(j, nn, i))),
            pl.BlockSpec((None, None, d, hk), lambda i, j, ie, ist, nn: (layer, ie[i + sh], 0, nj + chunk(j, nn, i))),
            pl.BlockSpec((None, None, hk, d), lambda i, j, ie, ist, nn: (layer, ie[i + sh], chunk(j, nn, i), 0)),
        ],
        out_specs=pl.BlockSpec(memory_space=pl.ANY),
        scratch_shapes=[
            pltpu.VMEM((MOE_ITEM_ROWS, d), F32),
            pltpu.VMEM((MOE_ITEM_ROWS, d), BF16),
            pltpu.VMEM((2, MOE_ITEM_ROWS, d), F32),
            pltpu.SemaphoreType.DMA,
            pltpu.SemaphoreType.DMA,
        ],
    )
    return pl.pallas_call(
        _moe_kernel,
        out_shape=jax.ShapeDtypeStruct(xs.shape, F32),
        grid_spec=grid_spec,
        compiler_params=_params(("arbitrary", "arbitrary")),
        name="moe_experts",
    )(item_e, item_start, item_n, xs, w_gu, w_gu, w_down)


def _combine_kernel(pos_ref, h_ref, wt_ref, g2_ref, fg_ref, ys_hbm, o_ref, ybuf, sem, *, final):
    i = pl.program_id(0)
    tm = h_ref.shape[0]
    par = i % 2

    def fetch(step, half):
        base = step * (2 * tm)
        for r in range(tm):
            for k in range(2):
                src = pos_ref[base + 2 * r + k]
                pltpu.make_async_copy(ys_hbm.at[pl.ds(src, 1), :], ybuf.at[half, k, pl.ds(r, 1), :],
                                      sem.at[half]).start()

    @pl.when(i == 0)
    def _():
        fetch(0, 0)

    @pl.when(i < pl.num_programs(0) - 1)
    def _():
        fetch(i + 1, 1 - par)

    for k in range(2):
        pltpu.make_async_copy(ys_hbm.at[pl.ds(0, tm), :], ybuf.at[par, k], sem.at[par]).wait()
    wt = wt_ref[...]
    moe = wt[:, 0:1] * ybuf[par, 0] + wt[:, 1:2] * ybuf[par, 1]
    h = h_ref[...] + g2_ref[...] * moe
    if final:
        ms = jnp.mean(h * h, axis=-1, keepdims=True)
        h = h * lax.rsqrt(ms + NORM_EPS) * fg_ref[...]
    o_ref[...] = h


def _combine(h, ys, pos, wts, mods, final_g, *, n_rows, n_lat, seq, gate_idx, final, tm=SLOT_BLOCK_ROWS):
    d = D_MODEL
    kernel = functools.partial(_combine_kernel, final=final)
    grid_spec = pltpu.PrefetchScalarGridSpec(
        num_scalar_prefetch=1,
        grid=(n_rows // tm,),
        in_specs=[
            pl.BlockSpec((tm, d), lambda i, pos_ref: (i, 0)),
            pl.BlockSpec((tm, LANES), lambda i, pos_ref: (i, 0)),
            pl.BlockSpec((None, 1, d), lambda i, pos_ref: (_mod_row(i * tm, n_lat, seq), 0, gate_idx)),
            pl.BlockSpec((1, d), lambda i, pos_ref: (0, 0)),
            pl.BlockSpec(memory_space=pl.ANY),
        ],
        out_specs=pl.BlockSpec((tm, d), lambda i, pos_ref: (i, 0)),
        scratch_shapes=[pltpu.VMEM((2, 2, tm, d), F32), pltpu.SemaphoreType.DMA((2,))],
    )
    return pl.pallas_call(
        kernel,
        out_shape=jax.ShapeDtypeStruct((n_rows, d), F32),
        grid_spec=grid_spec,
        compiler_params=_params(("arbitrary",)),
        name="combine",
    )(pos, h, wts, mods, final_g.reshape(1, d), ys)


def kernel(x, c, ctx, c_ctx, ada_w, ada_b, norm1_g, w_in, b_gate, diff_lambda, subln_g, w_attn_out, sc_conv_w,
           w_sc_out, cf_dw_w, cf_dw_b, cf_ln_g, cf_ln_b, w_cf_out, w_mix, norm2_g, router_g_w, router_g_b,
           router_e_w, router_e_b, exp_w_gu, exp_w_down, final_g):
    batch, seq, d = x.shape
    n_ctx = ctx.shape[1]
    depth = ada_w.shape[0]
    n_lat = batch * seq
    n_all = n_lat + batch * n_ctx
    assert d == D_MODEL and batch == MOD_ROWS // 2 and w_in.shape[2] == C_TOT

    cond = jnp.concatenate([c, c_ctx[None, :], jnp.zeros((MOD_ROWS - batch - 1, d), F32)], axis=0)
    mods_all = _adaln(cond, ada_w, ada_b)
    cos, sin = _rope_tables(seq)
    rows = (x.reshape(n_lat, d), ctx.reshape(batch * n_ctx, d))
    geo = dict(n_lat=n_lat, seq=seq)

    for layer in range(depth):
        last = layer == depth - 1
        lam_init = 0.8 - 0.6 * math.exp(-0.3 * layer)
        mods = mods_all[layer].reshape(MOD_ROWS, 1, 6 * d)
        n_rows = n_lat if last else n_all

        p = _in_proj(rows, norm1_g[layer], mods, w_in, layer, row0=0, n_rows=n_rows, col0=0, n_cols=C_TOT,
                     sh_idx=0, sc_idx=1, tm=1024, **geo)
        if last:
            kv_ctx = _in_proj(rows, norm1_g[layer], mods, w_in, layer, row0=n_lat, n_rows=batch * n_ctx, col0=OFF_K,
                              n_cols=OFF_SC - OFF_K, sh_idx=0, sc_idx=1, tm=n_ctx, **geo)
            kc_blk, vc_blk, ctx_rb0 = 0, QK_W // HEAD_W, 0
        else:
            kv_ctx = p
            kc_blk, vc_blk, ctx_rb0 = OFF_K // HEAD_W, OFF_V // HEAD_W, n_lat // n_ctx
        attn = _attn_lat(p, kv_ctx, kc_blk, vc_blk, ctx_rb0, cos, sin, diff_lambda[layer], subln_g[layer],
                         batch=batch, seq=seq, n_ctx=n_ctx, lam_init=lam_init)
        if not last:
            attn = (attn, _attn_ctx(p, n_lat // n_ctx, diff_lambda[layer], subln_g[layer],
                                    batch=batch, n_ctx=n_ctx, lam_init=lam_init))
        else:
            attn = (attn, attn)
        sc, cf = _convs(p, sc_conv_w[layer], cf_dw_w[layer], cf_dw_b[layer], cf_ln_g[layer], cf_ln_b[layer],
                        n_rows=n_rows, **geo)
        h_mix = _merge(attn, sc, cf, p, b_gate[layer].reshape(1, 3 * d),
                       w_attn_out[layer].astype(BF16), w_sc_out[layer].astype(BF16),
                       w_cf_out[layer].astype(BF16), w_mix[layer].astype(BF16), rows, mods,
                       n_rows=n_rows, gate_idx=2, **geo)

        rw = jnp.concatenate([router_g_w[layer], router_e_w[layer],
                              jnp.zeros((d, LANES - N_GROUPS - N_EXPERTS), F32)], axis=1)
        rb = jnp.concatenate([router_g_b[layer], router_e_b[layer],
                              jnp.zeros((LANES - N_GROUPS - N_EXPERTS,), F32)]).reshape(1, LANES)
        ids, wts, cnt = _router(h_mix, norm2_g[layer], mods, rw, rb, n_rows=n_rows, sh_idx=3, sc_idx=4, **geo)
        items, pos = _moe_plan(ids, cnt, n_rows)
        xs = _dispatch(h_mix, norm2_g[layer], mods, pos, n_rows=n_rows, sh_idx=3, sc_idx=4, **geo)
        ys = _moe(xs, items, exp_w_gu, exp_w_down, layer)
        h = _combine(h_mix, ys, pos, wts, mods, final_g, n_rows=n_rows, gate_idx=5, final=last, **geo)
        rows = (h, h)

    return h.reshape(batch, seq, d)
```

```python
import functools
import math

import jax
import jax.numpy as jnp
from jax import lax
from jax.experimental import pallas as pl
from jax.experimental.pallas import tpu as pltpu

F32 = jnp.float32
BF16 = jnp.bfloat16
I32 = jnp.int32

D_MODEL = 2048
GRID_W = 64
NORM_EPS = 1e-6
N_HEADS = 8
HEAD_DIM = 64
HEAD_W = 2 * HEAD_DIM
QK_W = N_HEADS * HEAD_W
ATTN_W = N_HEADS * HEAD_W
ROPE_BASE = 10000.0
ROPE_FREQS = HEAD_DIM // 4
SUBLN_EPS = 1e-5
QK_SCALE = HEAD_DIM ** -0.5 * math.log2(math.e)
SC_W = D_MODEL // 4
CF_W = D_MODEL // 4
SC_TAPS = 3
CF_TAPS = 31
CF_LN_EPS = 1e-5
OFF_Q = 0
OFF_K = OFF_Q + QK_W
OFF_V = OFF_K + QK_W
OFF_SC = OFF_V + ATTN_W
OFF_CF = OFF_SC + 3 * SC_W
OFF_GATE = OFF_CF + 2 * CF_W
C_TOT = OFF_GATE + 3 * D_MODEL
N_GROUPS = 4
EXPERTS_PER_GROUP = 8
N_EXPERTS = N_GROUPS * EXPERTS_PER_GROUP
EXPERT_HIDDEN = D_MODEL // 2

LANES = 128
SUBLANES = 8
MOD_ROWS = 8
MERGE_GATE_TILES = 4
CONV_HALO = 16
CONV_ROWS = 256
CONV_CHUNK = 32
MOE_ITEM_ROWS = 1024
MOE_ITEM_CAP = MOE_ITEM_ROWS - 8
MOE_ROW_PAD = 128
MOE_HIDDEN_BLOCK = 256
MOE_ITEM_SHIFT = 1
DISPATCH_ROWS = 512
COMBINE_ROWS = 256
VMEM_LIMIT = 56 * 1024 * 1024


def _params(sem, vmem=VMEM_LIMIT):
    return pltpu.CompilerParams(dimension_semantics=sem, vmem_limit_bytes=vmem)


def _sigmoid(x):
    return 1.0 / (1.0 + jnp.exp(-x))


def _adaln_kernel(s_ref, w_ref, b_ref, o_ref):
    s = s_ref[...]
    s = s * _sigmoid(s)
    s_hi = s.astype(BF16)
    w = w_ref[...]
    w_hi = w.astype(BF16)
    w_lo = (w - w_hi.astype(F32)).astype(BF16)
    lhs = jnp.concatenate([s_hi.astype(F32), s - s_hi.astype(F32)], axis=0).astype(BF16)
    r = jnp.dot(lhs, w_hi, preferred_element_type=F32)
    r2 = jnp.dot(s_hi, w_lo, preferred_element_type=F32)
    o_ref[...] = r[:MOD_ROWS] + r[MOD_ROWS:] + r2 + b_ref[...]


def _adaln(cond, ada_w, ada_b):
    n_layers, d, n = ada_w.shape
    tn = 512
    return pl.pallas_call(
        _adaln_kernel,
        out_shape=jax.ShapeDtypeStruct((n_layers, MOD_ROWS, n), F32),
        grid=(n_layers, n // tn),
        in_specs=[
            pl.BlockSpec((MOD_ROWS, d), lambda l, j: (0, 0)),
            pl.BlockSpec((None, d, tn), lambda l, j: (l, 0, j)),
            pl.BlockSpec((None, 1, tn), lambda l, j: (l, 0, j)),
        ],
        out_specs=pl.BlockSpec((None, MOD_ROWS, tn), lambda l, j: (l, 0, j)),
        compiler_params=_params(("arbitrary", "arbitrary")),
        name="adaln",
    )(cond, ada_w, ada_b.reshape(n_layers, 1, n))


def _mod_row(row0, n_lat, seq):
    return jnp.where(row0 < n_lat, row0 // seq, MOD_ROWS // 2)


def _in_proj_kernel(xa_ref, xb_ref, g_ref, sh_ref, sc_ref, w_ref, o_ref, u_ref, *, a_blocks):
    first_col = pl.program_id(1) == 0
    from_a = pl.program_id(0) < a_blocks

    def normed(x_ref):
        x = x_ref[...]
        ms = jnp.mean(x * x, axis=-1, keepdims=True)
        y = x * lax.rsqrt(ms + NORM_EPS) * g_ref[...]
        u_ref[...] = (y * (1.0 + sc_ref[...]) + sh_ref[...]).astype(BF16)

    @pl.when(jnp.logical_and(first_col, from_a))
    def _():
        normed(xa_ref)

    @pl.when(jnp.logical_and(first_col, jnp.logical_not(from_a)))
    def _():
        normed(xb_ref)

    o_ref[...] = jnp.dot(u_ref[...], w_ref[...].astype(BF16), preferred_element_type=F32).astype(o_ref.dtype)


def _in_proj(rows, gain, mods, w_in, layer, *, row0, n_rows, col0, n_cols, n_lat, seq, sh_idx, sc_idx, tm, tn=512):
    xa, xb = rows
    d = xa.shape[1]
    rb0, cb0 = row0 // tm, col0 // tn
    a_blocks = xa.shape[0] // tm - rb0

    def mod_map(idx):
        return lambda i, j: (_mod_row((i + rb0) * tm, n_lat, seq), 0, idx)

    return pl.pallas_call(
        functools.partial(_in_proj_kernel, a_blocks=a_blocks),
        out_shape=jax.ShapeDtypeStruct((n_rows, n_cols), BF16),
        grid=(n_rows // tm, n_cols // tn),
        in_specs=[
            pl.BlockSpec((tm, d), lambda i, j: (jnp.minimum(i, a_blocks - 1) + rb0, 0)),
            pl.BlockSpec((tm, d), lambda i, j: (jnp.maximum(i - a_blocks, 0), 0)),
            pl.BlockSpec((1, d), lambda i, j: (0, 0)),
            pl.BlockSpec((None, 1, d), mod_map(sh_idx)),
            pl.BlockSpec((None, 1, d), mod_map(sc_idx)),
            pl.BlockSpec((None, d, tn), lambda i, j: (layer, 0, j + cb0)),
        ],
        out_specs=pl.BlockSpec((tm, tn), lambda i, j: (i, j)),
        scratch_shapes=[pltpu.VMEM((tm, d), BF16)],
        compiler_params=_params(("arbitrary", "arbitrary")),
        name="in_proj",
    )(xa, xb, gain.reshape(1, d), mods, mods, w_in)


def _rope_tables(n_tokens):
    t = jnp.arange(n_tokens, dtype=I32)
    pos = jnp.stack([t // GRID_W, t % GRID_W], axis=-1).astype(F32)
    inv_freq = ROPE_BASE ** (-jnp.arange(ROPE_FREQS, dtype=F32) / ROPE_FREQS)
    ang = pos[:, :, None] * inv_freq
    cos, sin = jnp.cos(ang), jnp.sin(ang)
    c = jnp.stack([cos, cos], axis=2).reshape(n_tokens, HEAD_DIM)
    s = jnp.stack([-sin, sin], axis=2).reshape(n_tokens, HEAD_DIM)
    return jnp.tile(c, (1, 2)), jnp.tile(s, (1, 2))


def _rope(x, c, s):
    lane = lax.broadcasted_iota(I32, x.shape, 1)
    first_half = (lane % (2 * ROPE_FREQS)) < ROPE_FREQS
    partner = jnp.where(first_half, pltpu.roll(x, LANES - ROPE_FREQS, 1), pltpu.roll(x, ROPE_FREQS, 1))
    return x * c + partner * s


def _diff_lambda(lam_ref, lam_init):
    lv = lam_ref[...]
    a = jnp.sum(lv[0:1] * lv[1:2], axis=-1, keepdims=True)
    b = jnp.sum(lv[2:3] * lv[3:4], axis=-1, keepdims=True)
    return jnp.exp(a) - jnp.exp(b) + lam_init


def _attend(q, k_all, v_all, lam, subg, lam_init):
    tq = q.shape[0]
    lane = lax.broadcasted_iota(I32, q.shape, 1)
    q0 = jnp.where(lane < HEAD_DIM, q, 0.0).astype(BF16)
    q1 = jnp.where(lane >= HEAD_DIM, q, 0.0).astype(BF16)
    qq = jnp.concatenate([q0, q1], axis=0)
    s = lax.dot_general(qq, k_all, (((1,), (1,)), ((), ())), preferred_element_type=F32)
    m = jnp.max(s, axis=-1, keepdims=True)
    e = jnp.exp2(s - m)
    l = jnp.sum(e, axis=-1, keepdims=True)
    a = e[:tq] - e[tq:] * (lam * l[:tq] / l[tq:])
    o = jnp.dot(a.astype(BF16), v_all, preferred_element_type=F32) / l[:tq]
    ms = jnp.mean(o * o, axis=-1, keepdims=True)
    return o * lax.rsqrt(ms + SUBLN_EPS) * subg * (1.0 - lam_init)


def _attn_lat_kernel(q_ref, kl_ref, vl_ref, kc_ref, vc_ref, cq_ref, sq_ref, ck_ref, sk_ref, lam_ref, g_ref,
                     o_ref, k_all, v_aug, s0_ref, m0_ref, s1_ref, m1_ref, *, n_ctx, lam_init, n_blocks):
    i = pl.program_id(2)
    tq = q_ref.shape[0]

    @pl.when(i == 0)
    def _():
        k_all[0:n_ctx, :] = kc_ref[...]
        k_all[n_ctx:, :] = _rope(kl_ref[...].astype(F32), ck_ref[...], sk_ref[...]).astype(BF16)
        v_aug[0:n_ctx, 0:HEAD_W] = vc_ref[...]
        v_aug[n_ctx:, 0:HEAD_W] = vl_ref[...]
        v_aug[:, HEAD_W:2 * HEAD_W] = jnp.ones((k_all.shape[0], HEAD_W), BF16)

    lam = _diff_lambda(lam_ref, lam_init)

    def stage_a(sa_ref, ma_ref):
        q = _rope(q_ref[...].astype(F32), cq_ref[...], sq_ref[...]) * QK_SCALE
        lane = lax.broadcasted_iota(I32, q.shape, 1)
        q0 = jnp.where(lane < HEAD_DIM, q, 0.0).astype(BF16)
        q1 = jnp.where(lane >= HEAD_DIM, q, 0.0).astype(BF16)
        qq = jnp.concatenate([q0, q1], axis=0)
        s = lax.dot_general(qq, k_all[...], (((1,), (1,)), ((), ())), preferred_element_type=F32)
        sa_ref[...] = s
        ma_ref[...] = jnp.broadcast_to(jnp.max(s, axis=-1, keepdims=True), ma_ref.shape)

    def stage_b(sb_ref, mb_ref):
        mb = jnp.concatenate([mb_ref[...]] * (sb_ref.shape[1] // LANES), axis=1)
        e = jnp.exp2(sb_ref[...] - mb).astype(BF16)
        oa = jnp.dot(e, v_aug[...], preferred_element_type=F32)
        o = oa[:tq, 0:HEAD_W] / oa[:tq, HEAD_W:] - oa[tq:, 0:HEAD_W] * (lam / oa[tq:, HEAD_W:])
        ms = jnp.mean(o * o, axis=-1, keepdims=True)
        o_ref[...] = (o * lax.rsqrt(ms + SUBLN_EPS) * g_ref[...] * (1.0 - lam_init)).astype(o_ref.dtype)

    bufs = ((s0_ref, m0_ref), (s1_ref, m1_ref))

    @pl.when(i == 0)
    def _():
        stage_a(*bufs[0])

    for par in range(2):
        @pl.when(jnp.logical_and(jnp.logical_and(i > 0, i < n_blocks), i % 2 == par))
        def _():
            stage_a(*bufs[par])
            stage_b(*bufs[1 - par])

    @pl.when(i == n_blocks)
    def _():
        stage_b(*bufs[(n_blocks - 1) % 2])


def _attn_ctx_kernel(q_ref, k_ref, v_ref, lam_ref, g_ref, o_ref, *, lam_init):
    q = q_ref[...].astype(F32) * QK_SCALE
    lam = _diff_lambda(lam_ref, lam_init)
    o_ref[...] = _attend(q, k_ref[...], v_ref[...], lam, g_ref[...], lam_init).astype(o_ref.dtype)


def _attn_lat(p, kv_ctx, kc_blk, vc_blk, ctx_rb0, cos, sin, lam4, subg, *, batch, seq, n_ctx, lam_init, tq=512):
    nq = seq // tq
    n_keys = n_ctx + seq
    hb = lambda off: off // HEAD_W
    qblk = lambda i: jnp.minimum(i, nq - 1)
    kernel = functools.partial(_attn_lat_kernel, n_ctx=n_ctx, lam_init=lam_init, n_blocks=nq)
    return pl.pallas_call(
        kernel,
        out_shape=jax.ShapeDtypeStruct((batch * seq, ATTN_W), BF16),
        grid=(batch, N_HEADS, nq + 1),
        in_specs=[
            pl.BlockSpec((tq, HEAD_W), lambda b, h, i: (b * nq + qblk(i), hb(OFF_Q) + h)),
            pl.BlockSpec((seq, HEAD_W), lambda b, h, i: (b, hb(OFF_K) + h)),
            pl.BlockSpec((seq, HEAD_W), lambda b, h, i: (b, hb(OFF_V) + h)),
            pl.BlockSpec((n_ctx, HEAD_W), lambda b, h, i: (ctx_rb0 + b, kc_blk + h)),
            pl.BlockSpec((n_ctx, HEAD_W), lambda b, h, i: (ctx_rb0 + b, vc_blk + h)),
            pl.BlockSpec((tq, HEAD_W), lambda b, h, i: (qblk(i), 0)),
            pl.BlockSpec((tq, HEAD_W), lambda b, h, i: (qblk(i), 0)),
            pl.BlockSpec((seq, HEAD_W), lambda b, h, i: (0, 0)),
            pl.BlockSpec((seq, HEAD_W), lambda b, h, i: (0, 0)),
            pl.BlockSpec((4, HEAD_DIM), lambda b, h, i: (0, 0)),
            pl.BlockSpec((1, HEAD_W), lambda b, h, i: (0, 0)),
        ],
        out_specs=pl.BlockSpec((tq, HEAD_W), lambda b, h, i: (b * nq + jnp.maximum(i - 1, 0), h)),
        scratch_shapes=[pltpu.VMEM((n_keys, HEAD_W), BF16), pltpu.VMEM((n_keys, 2 * HEAD_W), BF16),
                        pltpu.VMEM((2 * tq, n_keys), F32), pltpu.VMEM((2 * tq, LANES), F32),
                        pltpu.VMEM((2 * tq, n_keys), F32), pltpu.VMEM((2 * tq, LANES), F32)],
        compiler_params=_params(("arbitrary", "arbitrary", "arbitrary")),
        name="attn_lat",
    )(p, p, p, kv_ctx, kv_ctx, cos, sin, cos, sin, lam4, subg.reshape(1, HEAD_W))


def _attn_ctx(p, ctx_rb0, lam4, subg, *, batch, n_ctx, lam_init):
    hb = lambda off: off // HEAD_W
    kernel = functools.partial(_attn_ctx_kernel, lam_init=lam_init)
    return pl.pallas_call(
        kernel,
        out_shape=jax.ShapeDtypeStruct((batch * n_ctx, ATTN_W), BF16),
        grid=(batch, N_HEADS),
        in_specs=[
            pl.BlockSpec((n_ctx, HEAD_W), lambda b, h: (ctx_rb0 + b, hb(OFF_Q) + h)),
            pl.BlockSpec((n_ctx, HEAD_W), lambda b, h: (ctx_rb0 + b, hb(OFF_K) + h)),
            pl.BlockSpec((n_ctx, HEAD_W), lambda b, h: (ctx_rb0 + b, hb(OFF_V) + h)),
            pl.BlockSpec((4, HEAD_DIM), lambda b, h: (0, 0)),
            pl.BlockSpec((1, HEAD_W), lambda b, h: (0, 0)),
        ],
        out_specs=pl.BlockSpec((n_ctx, HEAD_W), lambda b, h: (b, h)),
        compiler_params=_params(("arbitrary", "arbitrary")),
        name="attn_ctx",
    )(p, p, p, lam4, subg.reshape(1, HEAD_W))


def _conv_kernel(bg_ref, cg_ref, xi_ref, a_ref, g_ref,
                 cg_p, xi_p, a_p, g_p, cg_n, xi_n, a_n, g_n,
                 scw_ref, cfw_ref, cfb_ref, lng_ref, lnb_ref,
                 sco_ref, cfo_ref, pad_ref, *, lat_blocks, blocks_per_seq):
    i = pl.program_id(0)
    in_lat = i < lat_blocks
    pos = i % blocks_per_seq
    has_prev = jnp.logical_and(in_lat, pos != 0)
    has_next = jnp.logical_and(in_lat, pos != blocks_per_seq - 1)
    keep_prev = jnp.where(has_prev, 1.0, 0.0)
    keep_next = jnp.where(has_next, 1.0, 0.0)
    lo, hi = CONV_HALO, CONV_HALO + CONV_ROWS

    def fill(main, prev, nxt):
        pad_ref[0:lo, :] = prev * keep_prev
        pad_ref[lo:hi, :] = main
        pad_ref[hi:hi + CONV_HALO, :] = nxt * keep_next

    def f(ref):
        return ref[...].astype(F32)

    def glu(a, g):
        return a * _sigmoid(g)

    fill(f(cg_ref) * f(xi_ref), f(cg_p) * f(xi_p), f(cg_n) * f(xi_n))
    for c in range(CONV_ROWS // CONV_CHUNK):
        r0 = lo + c * CONV_CHUNK - SC_TAPS // 2
        acc = scw_ref[0:1, :] * pad_ref[r0:r0 + CONV_CHUNK, :]
        for k in range(1, SC_TAPS):
            acc = acc + scw_ref[k:k + 1, :] * pad_ref[r0 + k:r0 + k + CONV_CHUNK, :]
        rows = slice(c * CONV_CHUNK, (c + 1) * CONV_CHUNK)
        sco_ref[rows, :] = (bg_ref[rows, :].astype(F32) * acc).astype(sco_ref.dtype)

    fill(glu(f(a_ref), f(g_ref)), glu(f(a_p), f(g_p)), glu(f(a_n), f(g_n)))
    off = lo - CF_TAPS // 2
    win = CONV_CHUNK + SUBLANES
    for c in range(CONV_ROWS // CONV_CHUNK):
        base = c * CONV_CHUNK
        acc = None
        for b in range(SUBLANES):
            q = None
            for k in range(b, CF_TAPS, SUBLANES):
                term = cfw_ref[k:k + 1, :] * pad_ref[base + k - b:base + k - b + win, :]
                q = term if q is None else q + term
            part = q[off + b:off + b + CONV_CHUNK, :]
            acc = part if acc is None else acc + part
        z = acc + cfb_ref[...]
        mu = jnp.mean(z, axis=-1, keepdims=True)
        zc = z - mu
        var = jnp.mean(zc * zc, axis=-1, keepdims=True)
        y = zc * lax.rsqrt(var + CF_LN_EPS) * lng_ref[...] + lnb_ref[...]
        rows = slice(c * CONV_CHUNK, (c + 1) * CONV_CHUNK)
        cfo_ref[rows, :] = (y * _sigmoid(y)).astype(cfo_ref.dtype)


def _convs(p, sc_w, cf_w, cf_b, ln_g, ln_b, *, n_rows, n_lat, seq):
    nb = n_rows // CONV_ROWS
    halo_per_block = CONV_ROWS // CONV_HALO
    last_halo = n_rows // CONV_HALO - 1
    cb = lambda off: off // SC_W

    def main(off):
        return pl.BlockSpec((CONV_ROWS, SC_W), lambda i: (i, cb(off)))

    def prev(off):
        return pl.BlockSpec((CONV_HALO, SC_W), lambda i: (jnp.maximum(i * halo_per_block - 1, 0), cb(off)))

    def nxt(off):
        return pl.BlockSpec((CONV_HALO, SC_W), lambda i: (jnp.minimum((i + 1) * halo_per_block, last_halo), cb(off)))

    def vec(rows):
        return pl.BlockSpec((rows, SC_W), lambda i: (0, 0))

    o_bg, o_cg, o_xi, o_a, o_g = OFF_SC, OFF_SC + SC_W, OFF_SC + 2 * SC_W, OFF_CF, OFF_CF + CF_W
    kernel = functools.partial(_conv_kernel, lat_blocks=n_lat // CONV_ROWS, blocks_per_seq=seq // CONV_ROWS)
    return pl.pallas_call(
        kernel,
        out_shape=(jax.ShapeDtypeStruct((n_rows, SC_W), BF16), jax.ShapeDtypeStruct((n_rows, CF_W), BF16)),
        grid=(nb,),
        in_specs=[main(o_bg), main(o_cg), main(o_xi), main(o_a), main(o_g),
                  prev(o_cg), prev(o_xi), prev(o_a), prev(o_g),
                  nxt(o_cg), nxt(o_xi), nxt(o_a), nxt(o_g),
                  vec(SC_TAPS), vec(CF_TAPS), vec(1), vec(1), vec(1)],
        out_specs=(pl.BlockSpec((CONV_ROWS, SC_W), lambda i: (i, 0)),
                   pl.BlockSpec((CONV_ROWS, CF_W), lambda i: (i, 0))),
        scratch_shapes=[pltpu.VMEM((CONV_ROWS + 2 * CONV_HALO, SC_W), F32)],
        compiler_params=_params(("arbitrary",)),
        name="convs",
    )(p, p, p, p, p, p, p, p, p, p, p, p, p,
      sc_w, cf_w, cf_b.reshape(1, CF_W), ln_g.reshape(1, CF_W), ln_b.reshape(1, CF_W))


def _merge_kernel(*refs, a_blocks):
    n_gate = 3 * MERGE_GATE_TILES
    ata_ref, atb_ref, sc_ref, cf_ref = refs[:4]
    gate_refs = refs[4:4 + n_gate]
    bg_ref, wa_ref, wb_ref, wc_ref, wm_ref, ha_ref, hb_ref, g1_ref, o_ref = refs[4 + n_gate:]
    d = D_MODEL
    from_a = pl.program_id(0) < a_blocks
    attn = jnp.where(from_a, ata_ref[...], atb_ref[...])
    h = jnp.where(from_a, ha_ref[...], hb_ref[...])

    def gate(branch):
        tiles = gate_refs[branch * MERGE_GATE_TILES:(branch + 1) * MERGE_GATE_TILES]
        pre = jnp.concatenate([t[...] for t in tiles], axis=1).astype(F32)
        return _sigmoid(pre + bg_ref[:, branch * d:(branch + 1) * d])

    ya = jnp.dot(attn, wa_ref[...], preferred_element_type=F32)
    yb = jnp.dot(sc_ref[...], wb_ref[...], preferred_element_type=F32)
    yc = jnp.dot(cf_ref[...], wc_ref[...], preferred_element_type=F32)
    m = (gate(0) * ya + gate(1) * yb + gate(2) * yc).astype(BF16)
    o_ref[...] = h + g1_ref[...] * jnp.dot(m, wm_ref[...], preferred_element_type=F32)


def _merge(attn, sc, cf, p, b_gate, wa, wb, wc, wm, rows, mods, *, n_rows, n_lat, seq, gate_idx, tm=256):
    d = D_MODEL
    (ata, atb), (ha, hb) = attn, rows
    a_blocks = ha.shape[0] // tm
    first = lambda w: pl.BlockSpec((tm, w), lambda i: (jnp.minimum(i, a_blocks - 1), 0))
    rest = lambda w: pl.BlockSpec((tm, w), lambda i: (jnp.maximum(i - a_blocks, 0), 0))
    tg = d // MERGE_GATE_TILES
    gb0 = OFF_GATE // tg
    gates = [pl.BlockSpec((tm, tg), functools.partial(lambda i, c: (i, c), c=gb0 + t))
             for t in range(3 * MERGE_GATE_TILES)]
    whole = lambda a: pl.BlockSpec(a.shape, lambda i: (0, 0))
    return pl.pallas_call(
        functools.partial(_merge_kernel, a_blocks=a_blocks),
        out_shape=jax.ShapeDtypeStruct((n_rows, d), F32),
        grid=(n_rows // tm,),
        in_specs=[
            first(ATTN_W), rest(ATTN_W),
            pl.BlockSpec((tm, SC_W), lambda i: (i, 0)),
            pl.BlockSpec((tm, CF_W), lambda i: (i, 0)),
            *gates,
            whole(b_gate), whole(wa), whole(wb), whole(wc), whole(wm),
            first(d), rest(d),
            pl.BlockSpec((None, 1, d), lambda i: (_mod_row(i * tm, n_lat, seq), 0, gate_idx)),
        ],
        out_specs=pl.BlockSpec((tm, d), lambda i: (i, 0)),
        compiler_params=_params(("arbitrary",)),
        name="merge",
    )(ata, atb, sc, cf, *([p] * (3 * MERGE_GATE_TILES)), b_gate, wa, wb, wc, wm, ha, hb, mods)


def _split2(x):
    hi = x.astype(BF16)
    return hi, (x - hi.astype(F32)).astype(BF16)


def _norm_mod(x, g, sh, sc):
    ms = jnp.mean(x * x, axis=-1, keepdims=True)
    return x * lax.rsqrt(ms + NORM_EPS) * g * (1.0 + sc) + sh


def _router_kernel(h_ref, g_ref, sh_ref, sc_ref, rw_ref, rb_ref, tri_ref, id_ref, wt_ref, cnt_ref, run_ref):
    @pl.when(pl.program_id(0) == 0)
    def _():
        run_ref[...] = jnp.zeros_like(run_ref)

    f = _norm_mod(h_ref[...], g_ref[...], sh_ref[...], sc_ref[...])

    f_hi, f_lo = _split2(f)
    w_hi, w_lo = _split2(rw_ref[...])
    dot = lambda a, b: jnp.dot(a, b, preferred_element_type=F32)
    both = dot(f_hi, jnp.concatenate([w_hi, w_lo], axis=1))
    logits = (dot(f_lo, w_hi) + both[:, LANES:] + both[:, :LANES]) + rb_ref[...]

    lane_i = lax.broadcasted_iota(I32, logits.shape, 1)
    lane = lane_i.astype(F32)
    neg = -jnp.inf
    big = float(LANES)
    lg = jnp.where(lane_i < N_GROUPS, logits, neg)
    mg = jnp.max(lg, axis=-1, keepdims=True)
    grp = jnp.min(jnp.where(lg == mg, lane, big), axis=-1, keepdims=True)
    p_grp = 1.0 / jnp.sum(jnp.exp(lg - mg), axis=-1, keepdims=True)

    e_lane = lane - N_GROUPS
    in_grp = jnp.logical_and(e_lane >= grp * EXPERTS_PER_GROUP, e_lane < (grp + 1.0) * EXPERTS_PER_GROUP)
    le = jnp.where(in_grp, logits, neg)
    m1 = jnp.max(le, axis=-1, keepdims=True)
    i1 = jnp.min(jnp.where(le == m1, lane, big), axis=-1, keepdims=True)
    le2 = jnp.where(lane == i1, neg, le)
    m2 = jnp.max(le2, axis=-1, keepdims=True)
    i2 = jnp.min(jnp.where(le2 == m2, lane, big), axis=-1, keepdims=True)
    e2 = jnp.exp(m2 - m1)
    w1 = p_grp / (1.0 + e2)
    w2 = p_grp * e2 / (1.0 + e2)
    e1, e2 = i1 - N_GROUPS, i2 - N_GROUPS
    hot1 = jnp.where(lane == e1, 1.0, 0.0)
    hot2 = jnp.where(lane == e2, 1.0, 0.0)
    tri = tri_ref[...]
    before1 = jnp.dot(tri, hot1.astype(BF16), preferred_element_type=F32)
    before2 = jnp.dot(tri, hot2.astype(BF16), preferred_element_type=F32)
    tot1 = jnp.sum(hot1, axis=0, keepdims=True)
    tot2 = jnp.sum(hot2, axis=0, keepdims=True)
    run = run_ref[...]
    r1 = jnp.sum(hot1 * (run + before1), axis=-1, keepdims=True)
    r2 = jnp.sum(hot2 * (run + tot1 + before2), axis=-1, keepdims=True)
    run = run + tot1 + tot2
    run_ref[...] = run
    cnt_ref[...] = run

    ids = jnp.where(lane_i == 0, e1, jnp.where(lane_i == 1, e2, jnp.where(lane_i == 2, r1,
                                                                        jnp.where(lane_i == 3, r2, 0.0))))
    id_ref[...] = ids.astype(I32)
    wt_ref[...] = jnp.where(lane_i == 0, w1, jnp.where(lane_i == 1, w2, 0.0))


def _router(h, gain, mods, rw, rb, *, n_rows, n_lat, seq, sh_idx, sc_idx, tm=1024):
    d = D_MODEL

    def mod_map(idx):
        return lambda i: (_mod_row(i * tm, n_lat, seq), 0, idx)

    return pl.pallas_call(
        _router_kernel,
        out_shape=(jax.ShapeDtypeStruct((n_rows, LANES), I32),
                   jax.ShapeDtypeStruct((n_rows, LANES), F32),
                   jax.ShapeDtypeStruct((1, LANES), F32)),
        grid=(n_rows // tm,),
        in_specs=[
            pl.BlockSpec((tm, d), lambda i: (i, 0)),
            pl.BlockSpec((1, d), lambda i: (0, 0)),
            pl.BlockSpec((None, 1, d), mod_map(sh_idx)),
            pl.BlockSpec((None, 1, d), mod_map(sc_idx)),
            pl.BlockSpec((d, LANES), lambda i: (0, 0)),
            pl.BlockSpec((1, LANES), lambda i: (0, 0)),
            pl.BlockSpec((tm, tm), lambda i: (0, 0)),
        ],
        out_specs=(pl.BlockSpec((tm, LANES), lambda i: (i, 0)),
                   pl.BlockSpec((tm, LANES), lambda i: (i, 0)),
                   pl.BlockSpec((1, LANES), lambda i: (0, 0))),
        scratch_shapes=[pltpu.VMEM((1, LANES), F32)],
        compiler_params=_params(("arbitrary",)),
        name="router",
    )(h, gain.reshape(1, d), mods, mods, rw, rb, jnp.tril(jnp.ones((tm, tm), BF16), -1))


def _dispatch_kernel(pos_ref, h_ref, g_ref, sh_ref, sc_ref, xs_hbm, f0_ref, f1_ref, sem):
    i = pl.program_id(0)
    tm = h_ref.shape[0]

    def run(f_ref):
        def wait_block_copies():
            for _ in range(2):
                pltpu.make_async_copy(f_ref, xs_hbm.at[pl.ds(0, tm), :], sem).wait()

        f_ref[...] = _norm_mod(h_ref[...], g_ref[...], sh_ref[...], sc_ref[...])

        @pl.when(i > 0)
        def _():
            wait_block_copies()

        base = i * (2 * tm)
        for r in range(tm):
            for k in range(2):
                dst = pos_ref[base + 2 * r + k]
                pltpu.make_async_copy(f_ref.at[pl.ds(r, 1), :], xs_hbm.at[pl.ds(dst, 1), :], sem).start()

        @pl.when(i == pl.num_programs(0) - 1)
        def _():
            wait_block_copies()

    @pl.when(i % 2 == 0)
    def _():
        run(f0_ref)

    @pl.when(i % 2 == 1)
    def _():
        run(f1_ref)


def _dispatch(h, gain, mods, pos, *, n_rows, n_lat, seq, sh_idx, sc_idx, tm=DISPATCH_ROWS):
    d = D_MODEL

    def mod_map(idx):
        return lambda i, pos_ref: (_mod_row(i * tm, n_lat, seq), 0, idx)

    grid_spec = pltpu.PrefetchScalarGridSpec(
        num_scalar_prefetch=1,
        grid=(n_rows // tm,),
        in_specs=[
            pl.BlockSpec((tm, d), lambda i, pos_ref: (i, 0)),
            pl.BlockSpec((1, d), lambda i, pos_ref: (0, 0)),
            pl.BlockSpec((None, 1, d), mod_map(sh_idx)),
            pl.BlockSpec((None, 1, d), mod_map(sc_idx)),
        ],
        out_specs=pl.BlockSpec(memory_space=pl.ANY),
        scratch_shapes=[pltpu.VMEM((tm, d), F32), pltpu.VMEM((tm, d), F32), pltpu.SemaphoreType.DMA],
    )
    return pl.pallas_call(
        _dispatch_kernel,
        out_shape=jax.ShapeDtypeStruct((2 * n_rows, d), F32),
        grid_spec=grid_spec,
        compiler_params=_params(("arbitrary",)),
        name="dispatch",
    )(pos, h, gain.reshape(1, d), mods, mods)


def _moe_plan(ids, cnt, n_tok):
    n_items = N_EXPERTS + (2 * n_tok) // MOE_ITEM_CAP
    experts = jnp.arange(N_EXPERTS, dtype=I32)
    counts = cnt[0, :N_EXPERTS].astype(I32)
    starts = jnp.cumsum(counts) - counts
    e_slot, rank = ids[:, 0:2], ids[:, 2:4]
    start_of_slot = jnp.sum(jnp.where(e_slot[:, :, None] == experts, starts, 0), axis=-1)
    pos = (start_of_slot + rank).reshape(-1).astype(I32)
    pos = jnp.concatenate([pos, jnp.zeros((2 * COMBINE_ROWS,), I32)])
    n_sb = (counts + MOE_ITEM_CAP - 1) // MOE_ITEM_CAP
    cum = jnp.cumsum(n_sb)
    total = cum[-1]
    it = jnp.arange(n_items, dtype=I32)
    e_i = jnp.minimum(jnp.sum((it[:, None] >= cum[None, :]).astype(I32), axis=1), N_EXPERTS - 1)
    sb = it - (cum[e_i] - n_sb[e_i])
    valid = it < total
    last_e = e_i[jnp.maximum(total - 1, 0)]
    item_e = jnp.where(valid, e_i, last_e).astype(I32)
    item_start = jnp.where(valid, starts[e_i] + sb * MOE_ITEM_CAP, 0).astype(I32)
    item_n = jnp.where(valid, jnp.minimum(MOE_ITEM_CAP, counts[e_i] - sb * MOE_ITEM_CAP), 0).astype(I32)
    front, back = jnp.zeros((MOE_ITEM_SHIFT,), I32), jnp.zeros((1,), I32)
    pad_items = lambda a: jnp.concatenate([front, a, back])
    return (pad_items(item_e), pad_items(item_start), pad_items(item_n)), pos


def _moe_kernel(ie_ref, ist_ref, in_ref,
                xs_hbm, wg_ref, wu_ref, wd_ref, ys_hbm,
                xf_ref, xb_ref, y_ref, gsem, ssem):
    i, j = pl.program_id(0), pl.program_id(1)
    sh = MOE_ITEM_SHIFT
    n_prev, n, n_next = in_ref[i + sh - 1], in_ref[i + sh], in_ref[i + sh + 1]
    start_prev, start, start_next = ist_ref[i + sh - 1], ist_ref[i + sh], ist_ref[i + sh + 1]
    par = i % 2

    def rows_in(hrow, vrow, size):
        return pltpu.make_async_copy(xs_hbm.at[pl.ds(hrow, size), :], xf_ref.at[pl.ds(vrow, size), :], gsem)

    def rows_out(hrow, vrow, size):
        return pltpu.make_async_copy(y_ref.at[par, pl.ds(vrow, size), :], ys_hbm.at[pl.ds(hrow, size), :], ssem)

    def row_groups(make, base, count, act):
        shift = base % SUBLANES
        head = jnp.minimum((SUBLANES - shift) % SUBLANES, count)
        body = ((count - head) // SUBLANES) * SUBLANES
        for t in range(SUBLANES - 1):
            @pl.when(t < head)
            def _():
                getattr(make(base + t, shift + t, 1), act)()

        for b in reversed(range(SUBLANES.bit_length() - 1, MOE_ITEM_ROWS.bit_length())):
            @pl.when(((body >> b) & 1) == 1)
            def _():
                off = head + ((body >> (b + 1)) << (b + 1))
                getattr(make(pl.multiple_of(base + off, SUBLANES), pl.multiple_of(shift + off, SUBLANES), 1 << b),
                        act)()

        done = head + body
        for t in range(SUBLANES - 1):
            @pl.when(t < count - done)
            def _():
                getattr(make(base + done + t, shift + done + t, 1), act)()

    @pl.when(jnp.logical_and(i == 0, j == 0))
    def _():
        xf_ref[...] = jnp.zeros_like(xf_ref)
        y_ref[...] = jnp.zeros_like(y_ref)

    @pl.when(j == 0)
    def _():
        @pl.when(i == 0)
        def _():
            row_groups(rows_in, start, n, "start")

        @pl.when(n > 0)
        def _():
            row_groups(rows_in, start, n, "wait")
            xb_ref[...] = xf_ref[...].astype(BF16)
            row_groups(rows_in, start_next, n_next, "start")

    def compute(m):
        x = xb_ref[0:m, :]
        g = jnp.dot(x, wg_ref[...].astype(BF16), preferred_element_type=F32)
        u = jnp.dot(x, wu_ref[...].astype(BF16), preferred_element_type=F32)
        hmid = (g * _sigmoid(g) * u).astype(BF16)
        y = jnp.dot(hmid, wd_ref[...].astype(BF16), preferred_element_type=F32)
        acc = y_ref.at[par]
        acc[0:m, :] = y + jnp.where(j == 0, 0.0, acc[0:m, :])

    used = jnp.where(n > 0, start % SUBLANES + n, 0)
    n_pad = (used + MOE_ROW_PAD - 1) // MOE_ROW_PAD
    for k in range(1, MOE_ITEM_ROWS // MOE_ROW_PAD + 1):
        @pl.when(n_pad == k)
        def _():
            compute(k * MOE_ROW_PAD)

    @pl.when(j == pl.num_programs(1) - 1)
    def _():
        row_groups(rows_out, start_prev, n_prev, "wait")

        row_groups(rows_out, start, n, "start")

        @pl.when(i == pl.num_programs(0) - 1)
        def _():
            row_groups(rows_out, start, n, "wait")


def _moe(xs, items, w_gu, w_down, layer):
    item_e, item_start, item_n = items
    d, hid, hk = D_MODEL, EXPERT_HIDDEN, MOE_HIDDEN_BLOCK
    nj = hid // hk
    sh = MOE_ITEM_SHIFT
    n_items = item_e.shape[0] - sh - 1

    def chunk(j, nn, i):
        return jnp.where(nn[i + sh] > 0, j, nj - 1)

    grid_spec = pltpu.PrefetchScalarGridSpec(
        num_scalar_prefetch=3,
        grid=(n_items, nj),
        in_specs=[
            pl.BlockSpec(memory_space=pl.ANY),
            pl.BlockSpec((None, None, d, hk), lambda i, j, ie, ist, nn: (layer, ie[i + sh], 0, chunk(j, nn, i))),
            pl.BlockSpec((None, None, d, hk), lambda i, j, ie, ist, nn: (layer, ie[i + sh], 0, nj + chunk(j, nn, i))),
            pl.BlockSpec((None, None, hk, d), lambda i, j, ie, ist, nn: (layer, ie[i + sh], chunk(j, nn, i), 0)),
        ],
        out_specs=pl.BlockSpec(memory_space=pl.ANY),
        scratch_shapes=[
            pltpu.VMEM((MOE_ITEM_ROWS, d), F32),
            pltpu.VMEM((MOE_ITEM_ROWS, d), BF16),
            pltpu.VMEM((2, MOE_ITEM_ROWS, d), F32),
            pltpu.SemaphoreType.DMA,
            pltpu.SemaphoreType.DMA,
        ],
    )
    return pl.pallas_call(
        _moe_kernel,
        out_shape=jax.ShapeDtypeStruct(xs.shape, F32),
        grid_spec=grid_spec,
        compiler_params=_params(("arbitrary", "arbitrary")),
        name="moe_experts",
    )(item_e, item_start, item_n, xs, w_gu, w_gu, w_down)


def _combine_kernel(pos_ref, h_ref, wt_ref, g2_ref, fg_ref, ys_hbm, o_ref, ybuf, sem, *, final):
    i = pl.program_id(0)
    tm = h_ref.shape[0]
    par = i % 2

    def fetch(step, half):
        base = step * (2 * tm)
        for r in range(tm):
            for k in range(2):
                src = pos_ref[base + 2 * r + k]
                pltpu.make_async_copy(ys_hbm.at[pl.ds(src, 1), :], ybuf.at[half, k, pl.ds(r, 1), :],
                                      sem.at[half]).start()

    @pl.when(i == 0)
    def _():
        fetch(0, 0)

    @pl.when(i < pl.num_programs(0) - 1)
    def _():
        fetch(i + 1, 1 - par)

    for k in range(2):
        pltpu.make_async_copy(ys_hbm.at[pl.ds(0, tm), :], ybuf.at[par, k], sem.at[par]).wait()
    wt = wt_ref[...]
    moe = wt[:, 0:1] * ybuf[par, 0] + wt[:, 1:2] * ybuf[par, 1]
    h = h_ref[...] + g2_ref[...] * moe
    if final:
        ms = jnp.mean(h * h, axis=-1, keepdims=True)
        h = h * lax.rsqrt(ms + NORM_EPS) * fg_ref[...]
    o_ref[...] = h


def _combine(h, ys, pos, wts, mods, final_g, *, n_rows, n_lat, seq, gate_idx, final, tm=COMBINE_ROWS):
    d = D_MODEL
    kernel = functools.partial(_combine_kernel, final=final)
    grid_spec = pltpu.PrefetchScalarGridSpec(
        num_scalar_prefetch=1,
        grid=(n_rows // tm,),
        in_specs=[
            pl.BlockSpec((tm, d), lambda i, pos_ref: (i, 0)),
            pl.BlockSpec((tm, LANES), lambda i, pos_ref: (i, 0)),
            pl.BlockSpec((None, 1, d), lambda i, pos_ref: (_mod_row(i * tm, n_lat, seq), 0, gate_idx)),
            pl.BlockSpec((1, d), lambda i, pos_ref: (0, 0)),
            pl.BlockSpec(memory_space=pl.ANY),
        ],
        out_specs=pl.BlockSpec((tm, d), lambda i, pos_ref: (i, 0)),
        scratch_shapes=[pltpu.VMEM((2, 2, tm, d), F32), pltpu.SemaphoreType.DMA((2,))],
    )
    return pl.pallas_call(
        kernel,
        out_shape=jax.ShapeDtypeStruct((n_rows, d), F32),
        grid_spec=grid_spec,
        compiler_params=_params(("arbitrary",)),
        name="combine",
    )(pos, h, wts, mods, final_g.reshape(1, d), ys)


def kernel(x, c, ctx, c_ctx, ada_w, ada_b, norm1_g, w_in, b_gate, diff_lambda, subln_g, w_attn_out, sc_conv_w,
           w_sc_out, cf_dw_w, cf_dw_b, cf_ln_g, cf_ln_b, w_cf_out, w_mix, norm2_g, router_g_w, router_g_b,
           router_e_w, router_e_b, exp_w_gu, exp_w_down, final_g):
    batch, seq, d = x.shape
    n_ctx = ctx.shape[1]
    depth = ada_w.shape[0]
    n_lat = batch * seq
    n_all = n_lat + batch * n_ctx
    assert d == D_MODEL and batch == MOD_ROWS // 2 and w_in.shape[2] == C_TOT

    cond = jnp.concatenate([c, c_ctx[None, :], jnp.zeros((MOD_ROWS - batch - 1, d), F32)], axis=0)
    mods_all = _adaln(cond, ada_w, ada_b)
    cos, sin = _rope_tables(seq)
    rows = (x.reshape(n_lat, d), ctx.reshape(batch * n_ctx, d))
    geo = dict(n_lat=n_lat, seq=seq)

    for layer in range(depth):
        last = layer == depth - 1
        lam_init = 0.8 - 0.6 * math.exp(-0.3 * layer)
        mods = mods_all[layer].reshape(MOD_ROWS, 1, 6 * d)
        n_rows = n_lat if last else n_all

        p = _in_proj(rows, norm1_g[layer], mods, w_in, layer, row0=0, n_rows=n_rows, col0=0, n_cols=C_TOT,
                     sh_idx=0, sc_idx=1, tm=1024, **geo)
        if last:
            kv_ctx = _in_proj(rows, norm1_g[layer], mods, w_in, layer, row0=n_lat, n_rows=batch * n_ctx, col0=OFF_K,
                              n_cols=OFF_SC - OFF_K, sh_idx=0, sc_idx=1, tm=n_ctx, **geo)
            kc_blk, vc_blk, ctx_rb0 = 0, QK_W // HEAD_W, 0
        else:
            kv_ctx = p
            kc_blk, vc_blk, ctx_rb0 = OFF_K // HEAD_W, OFF_V // HEAD_W, n_lat // n_ctx
        attn = _attn_lat(p, kv_ctx, kc_blk, vc_blk, ctx_rb0, cos, sin, diff_lambda[layer], subln_g[layer],
                         batch=batch, seq=seq, n_ctx=n_ctx, lam_init=lam_init)
        if not last:
            attn = (attn, _attn_ctx(p, n_lat // n_ctx, diff_lambda[layer], subln_g[layer],
                                    batch=batch, n_ctx=n_ctx, lam_init=lam_init))
        else:
            attn = (attn, attn)
        sc, cf = _convs(p, sc_conv_w[layer], cf_dw_w[layer], cf_dw_b[layer], cf_ln_g[layer], cf_ln_b[layer],
                        n_rows=n_rows, **geo)
        h_mix = _merge(attn, sc, cf, p, b_gate[layer].reshape(1, 3 * d),
                       w_attn_out[layer].astype(BF16), w_sc_out[layer].astype(BF16),
                       w_cf_out[layer].astype(BF16), w_mix[layer].astype(BF16), rows, mods,
                       n_rows=n_rows, gate_idx=2, **geo)

        rw = jnp.concatenate([router_g_w[layer], router_e_w[layer],
                              jnp.zeros((d, LANES - N_GROUPS - N_EXPERTS), F32)], axis=1)
        rb = jnp.concatenate([router_g_b[layer], router_e_b[layer],
                              jnp.zeros((LANES - N_GROUPS - N_EXPERTS,), F32)]).reshape(1, LANES)
        ids, wts, cnt = _router(h_mix, norm2_g[layer], mods, rw, rb, n_rows=n_rows, sh_idx=3, sc_idx=4, **geo)
        items, pos = _moe_plan(ids, cnt, n_rows)
        xs = _dispatch(h_mix, norm2_g[layer], mods, pos, n_rows=n_rows, sh_idx=3, sc_idx=4, **geo)
        ys = _moe(xs, items, exp_w_gu, exp_w_down, layer)
        h = _combine(h_mix, ys, pos, wts, mods, final_g, n_rows=n_rows, gate_idx=5, final=last, **geo)
        rows = (h, h)

    return h.reshape(batch, seq, d)
```

```python
import functools
import math

import jax
import jax.numpy as jnp
from jax import lax
from jax.experimental import pallas as pl
from jax.experimental.pallas import tpu as pltpu

F32 = jnp.float32
BF16 = jnp.bfloat16
I32 = jnp.int32

D_MODEL = 2048
GRID_W = 64
NORM_EPS = 1e-6
N_HEADS = 8
HEAD_DIM = 64
HEAD_W = 2 * HEAD_DIM
QK_W = N_HEADS * HEAD_W
ATTN_W = N_HEADS * HEAD_W
ROPE_BASE = 10000.0
ROPE_FREQS = HEAD_DIM // 4
SUBLN_EPS = 1e-5
QK_SCALE = HEAD_DIM ** -0.5 * math.log2(math.e)
SC_W = D_MODEL // 4
CF_W = D_MODEL // 4
SC_TAPS = 3
CF_TAPS = 31
CF_LN_EPS = 1e-5
OFF_Q = 0
OFF_K = OFF_Q + QK_W
OFF_V = OFF_K + QK_W
OFF_SC = OFF_V + ATTN_W
OFF_CF = OFF_SC + 3 * SC_W
OFF_GATE = OFF_CF + 2 * CF_W
C_TOT = OFF_GATE + 3 * D_MODEL
N_GROUPS = 4
EXPERTS_PER_GROUP = 8
N_EXPERTS = N_GROUPS * EXPERTS_PER_GROUP
EXPERT_HIDDEN = D_MODEL // 2

LANES = 128
SUBLANES = 8
MOD_ROWS = 8
MERGE_GATE_TILES = 4
CONV_HALO = 16
CONV_ROWS = 256
CONV_CHUNK = 32
MOE_ITEM_ROWS = 1024
MOE_ITEM_CAP = MOE_ITEM_ROWS - 8
MOE_ROW_PAD = 128
MOE_HIDDEN_BLOCK = 512
MOE_ITEM_SHIFT = 1
DISPATCH_ROWS = 512
COMBINE_ROWS = 256
VMEM_LIMIT = 56 * 1024 * 1024
MOE_VMEM_LIMIT = 60 * 1024 * 1024


def _params(sem, vmem=VMEM_LIMIT):
    return pltpu.CompilerParams(dimension_semantics=sem, vmem_limit_bytes=vmem)


def _sigmoid(x):
    return 1.0 / (1.0 + jnp.exp(-x))


def _adaln_kernel(s_ref, w_ref, b_ref, o_ref):
    s = s_ref[...]
    s = s * _sigmoid(s)
    s_hi = s.astype(BF16)
    w = w_ref[...]
    w_hi = w.astype(BF16)
    w_lo = (w - w_hi.astype(F32)).astype(BF16)
    lhs = jnp.concatenate([s_hi.astype(F32), s - s_hi.astype(F32)], axis=0).astype(BF16)
    r = jnp.dot(lhs, w_hi, preferred_element_type=F32)
    r2 = jnp.dot(s_hi, w_lo, preferred_element_type=F32)
    o_ref[...] = r[:MOD_ROWS] + r[MOD_ROWS:] + r2 + b_ref[...]


def _adaln(cond, ada_w, ada_b):
    n_layers, d, n = ada_w.shape
    tn = 512
    return pl.pallas_call(
        _adaln_kernel,
        out_shape=jax.ShapeDtypeStruct((n_layers, MOD_ROWS, n), F32),
        grid=(n_layers, n // tn),
        in_specs=[
            pl.BlockSpec((MOD_ROWS, d), lambda l, j: (0, 0)),
            pl.BlockSpec((None, d, tn), lambda l, j: (l, 0, j)),
            pl.BlockSpec((None, 1, tn), lambda l, j: (l, 0, j)),
        ],
        out_specs=pl.BlockSpec((None, MOD_ROWS, tn), lambda l, j: (l, 0, j)),
        compiler_params=_params(("arbitrary", "arbitrary")),
        name="adaln",
    )(cond, ada_w, ada_b.reshape(n_layers, 1, n))


def _mod_row(row0, n_lat, seq):
    return jnp.where(row0 < n_lat, row0 // seq, MOD_ROWS // 2)


def _in_proj_kernel(xa_ref, xb_ref, g_ref, sh_ref, sc_ref, w_ref, o_ref, u_ref, *, a_blocks):
    first_col = pl.program_id(1) == 0
    from_a = pl.program_id(0) < a_blocks

    def normed(x_ref):
        x = x_ref[...]
        ms = jnp.mean(x * x, axis=-1, keepdims=True)
        y = x * lax.rsqrt(ms + NORM_EPS) * g_ref[...]
        u_ref[...] = (y * (1.0 + sc_ref[...]) + sh_ref[...]).astype(BF16)

    @pl.when(jnp.logical_and(first_col, from_a))
    def _():
        normed(xa_ref)

    @pl.when(jnp.logical_and(first_col, jnp.logical_not(from_a)))
    def _():
        normed(xb_ref)

    o_ref[...] = jnp.dot(u_ref[...], w_ref[...].astype(BF16), preferred_element_type=F32).astype(o_ref.dtype)


def _in_proj(rows, gain, mods, w_in, layer, *, row0, n_rows, col0, n_cols, n_lat, seq, sh_idx, sc_idx, tm, tn=512):
    xa, xb = rows
    d = xa.shape[1]
    rb0, cb0 = row0 // tm, col0 // tn
    a_blocks = xa.shape[0] // tm - rb0

    def mod_map(idx):
        return lambda i, j: (_mod_row((i + rb0) * tm, n_lat, seq), 0, idx)

    return pl.pallas_call(
        functools.partial(_in_proj_kernel, a_blocks=a_blocks),
        out_shape=jax.ShapeDtypeStruct((n_rows, n_cols), BF16),
        grid=(n_rows // tm, n_cols // tn),
        in_specs=[
            pl.BlockSpec((tm, d), lambda i, j: (jnp.minimum(i, a_blocks - 1) + rb0, 0)),
            pl.BlockSpec((tm, d), lambda i, j: (jnp.maximum(i - a_blocks, 0), 0)),
            pl.BlockSpec((1, d), lambda i, j: (0, 0)),
            pl.BlockSpec((None, 1, d), mod_map(sh_idx)),
            pl.BlockSpec((None, 1, d), mod_map(sc_idx)),
            pl.BlockSpec((None, d, tn), lambda i, j: (layer, 0, j + cb0)),
        ],
        out_specs=pl.BlockSpec((tm, tn), lambda i, j: (i, j)),
        scratch_shapes=[pltpu.VMEM((tm, d), BF16)],
        compiler_params=_params(("arbitrary", "arbitrary")),
        name="in_proj",
    )(xa, xb, gain.reshape(1, d), mods, mods, w_in)


def _rope_tables(n_tokens):
    t = jnp.arange(n_tokens, dtype=I32)
    pos = jnp.stack([t // GRID_W, t % GRID_W], axis=-1).astype(F32)
    inv_freq = ROPE_BASE ** (-jnp.arange(ROPE_FREQS, dtype=F32) / ROPE_FREQS)
    ang = pos[:, :, None] * inv_freq
    cos, sin = jnp.cos(ang), jnp.sin(ang)
    c = jnp.stack([cos, cos], axis=2).reshape(n_tokens, HEAD_DIM)
    s = jnp.stack([-sin, sin], axis=2).reshape(n_tokens, HEAD_DIM)
    return jnp.tile(c, (1, 2)), jnp.tile(s, (1, 2))


def _rope(x, c, s):
    lane = lax.broadcasted_iota(I32, x.shape, 1)
    first_half = (lane % (2 * ROPE_FREQS)) < ROPE_FREQS
    partner = jnp.where(first_half, pltpu.roll(x, LANES - ROPE_FREQS, 1), pltpu.roll(x, ROPE_FREQS, 1))
    return x * c + partner * s


def _diff_lambda(lam_ref, lam_init):
    lv = lam_ref[...]
    a = jnp.sum(lv[0:1] * lv[1:2], axis=-1, keepdims=True)
    b = jnp.sum(lv[2:3] * lv[3:4], axis=-1, keepdims=True)
    return jnp.exp(a) - jnp.exp(b) + lam_init


def _attend(q, k_all, v_all, lam, subg, lam_init):
    tq = q.shape[0]
    lane = lax.broadcasted_iota(I32, q.shape, 1)
    q0 = jnp.where(lane < HEAD_DIM, q, 0.0).astype(BF16)
    q1 = jnp.where(lane >= HEAD_DIM, q, 0.0).astype(BF16)
    qq = jnp.concatenate([q0, q1], axis=0)
    s = lax.dot_general(qq, k_all, (((1,), (1,)), ((), ())), preferred_element_type=F32)
    m = jnp.max(s, axis=-1, keepdims=True)
    e = jnp.exp2(s - m)
    l = jnp.sum(e, axis=-1, keepdims=True)
    a = e[:tq] - e[tq:] * (lam * l[:tq] / l[tq:])
    o = jnp.dot(a.astype(BF16), v_all, preferred_element_type=F32) / l[:tq]
    ms = jnp.mean(o * o, axis=-1, keepdims=True)
    return o * lax.rsqrt(ms + SUBLN_EPS) * subg * (1.0 - lam_init)


def _attn_lat_kernel(q_ref, kl_ref, vl_ref, kc_ref, vc_ref, cq_ref, sq_ref, ck_ref, sk_ref, lam_ref, g_ref,
                     o_ref, k_all, v_aug, s0_ref, m0_ref, s1_ref, m1_ref, *, n_ctx, lam_init, n_blocks):
    i = pl.program_id(2)
    tq = q_ref.shape[0]

    @pl.when(i == 0)
    def _():
        k_all[0:n_ctx, :] = kc_ref[...]
        k_all[n_ctx:, :] = _rope(kl_ref[...].astype(F32), ck_ref[...], sk_ref[...]).astype(BF16)
        v_aug[0:n_ctx, 0:HEAD_W] = vc_ref[...]
        v_aug[n_ctx:, 0:HEAD_W] = vl_ref[...]
        v_aug[:, HEAD_W:2 * HEAD_W] = jnp.ones((k_all.shape[0], HEAD_W), BF16)

    lam = _diff_lambda(lam_ref, lam_init)

    def stage_a(sa_ref, ma_ref):
        q = _rope(q_ref[...].astype(F32), cq_ref[...], sq_ref[...]) * QK_SCALE
        lane = lax.broadcasted_iota(I32, q.shape, 1)
        q0 = jnp.where(lane < HEAD_DIM, q, 0.0).astype(BF16)
        q1 = jnp.where(lane >= HEAD_DIM, q, 0.0).astype(BF16)
        qq = jnp.concatenate([q0, q1], axis=0)
        s = lax.dot_general(qq, k_all[...], (((1,), (1,)), ((), ())), preferred_element_type=F32)
        sa_ref[...] = s
        ma_ref[...] = jnp.broadcast_to(jnp.max(s, axis=-1, keepdims=True), ma_ref.shape)

    def stage_b(sb_ref, mb_ref):
        mb = jnp.concatenate([mb_ref[...]] * (sb_ref.shape[1] // LANES), axis=1)
        e = jnp.exp2(sb_ref[...] - mb).astype(BF16)
        oa = jnp.dot(e, v_aug[...], preferred_element_type=F32)
        o = oa[:tq, 0:HEAD_W] / oa[:tq, HEAD_W:] - oa[tq:, 0:HEAD_W] * (lam / oa[tq:, HEAD_W:])
        ms = jnp.mean(o * o, axis=-1, keepdims=True)
        o_ref[...] = (o * lax.rsqrt(ms + SUBLN_EPS) * g_ref[...] * (1.0 - lam_init)).astype(o_ref.dtype)

    bufs = ((s0_ref, m0_ref), (s1_ref, m1_ref))

    @pl.when(i == 0)
    def _():
        stage_a(*bufs[0])

    for par in range(2):
        @pl.when(jnp.logical_and(jnp.logical_and(i > 0, i < n_blocks), i % 2 == par))
        def _():
            stage_a(*bufs[par])
            stage_b(*bufs[1 - par])

    @pl.when(i == n_blocks)
    def _():
        stage_b(*bufs[(n_blocks - 1) % 2])


def _attn_ctx_kernel(q_ref, k_ref, v_ref, lam_ref, g_ref, o_ref, *, lam_init):
    q = q_ref[...].astype(F32) * QK_SCALE
    lam = _diff_lambda(lam_ref, lam_init)
    o_ref[...] = _attend(q, k_ref[...], v_ref[...], lam, g_ref[...], lam_init).astype(o_ref.dtype)


def _attn_lat(p, kv_ctx, kc_blk, vc_blk, ctx_rb0, cos, sin, lam4, subg, *, batch, seq, n_ctx, lam_init, tq=512):
    nq = seq // tq
    n_keys = n_ctx + seq
    hb = lambda off: off // HEAD_W
    qblk = lambda i: jnp.minimum(i, nq - 1)
    kernel = functools.partial(_attn_lat_kernel, n_ctx=n_ctx, lam_init=lam_init, n_blocks=nq)
    return pl.pallas_call(
        kernel,
        out_shape=jax.ShapeDtypeStruct((batch * seq, ATTN_W), BF16),
        grid=(batch, N_HEADS, nq + 1),
        in_specs=[
            pl.BlockSpec((tq, HEAD_W), lambda b, h, i: (b * nq + qblk(i), hb(OFF_Q) + h)),
            pl.BlockSpec((seq, HEAD_W), lambda b, h, i: (b, hb(OFF_K) + h)),
            pl.BlockSpec((seq, HEAD_W), lambda b, h, i: (b, hb(OFF_V) + h)),
            pl.BlockSpec((n_ctx, HEAD_W), lambda b, h, i: (ctx_rb0 + b, kc_blk + h)),
            pl.BlockSpec((n_ctx, HEAD_W), lambda b, h, i: (ctx_rb0 + b, vc_blk + h)),
            pl.BlockSpec((tq, HEAD_W), lambda b, h, i: (qblk(i), 0)),
            pl.BlockSpec((tq, HEAD_W), lambda b, h, i: (qblk(i), 0)),
            pl.BlockSpec((seq, HEAD_W), lambda b, h, i: (0, 0)),
            pl.BlockSpec((seq, HEAD_W), lambda b, h, i: (0, 0)),
            pl.BlockSpec((4, HEAD_DIM), lambda b, h, i: (0, 0)),
            pl.BlockSpec((1, HEAD_W), lambda b, h, i: (0, 0)),
        ],
        out_specs=pl.BlockSpec((tq, HEAD_W), lambda b, h, i: (b * nq + jnp.maximum(i - 1, 0), h)),
        scratch_shapes=[pltpu.VMEM((n_keys, HEAD_W), BF16), pltpu.VMEM((n_keys, 2 * HEAD_W), BF16),
                        pltpu.VMEM((2 * tq, n_keys), F32), pltpu.VMEM((2 * tq, LANES), F32),
                        pltpu.VMEM((2 * tq, n_keys), F32), pltpu.VMEM((2 * tq, LANES), F32)],
        compiler_params=_params(("arbitrary", "arbitrary", "arbitrary")),
        name="attn_lat",
    )(p, p, p, kv_ctx, kv_ctx, cos, sin, cos, sin, lam4, subg.reshape(1, HEAD_W))


def _attn_ctx(p, ctx_rb0, lam4, subg, *, batch, n_ctx, lam_init):
    hb = lambda off: off // HEAD_W
    kernel = functools.partial(_attn_ctx_kernel, lam_init=lam_init)
    return pl.pallas_call(
        kernel,
        out_shape=jax.ShapeDtypeStruct((batch * n_ctx, ATTN_W), BF16),
        grid=(batch, N_HEADS),
        in_specs=[
            pl.BlockSpec((n_ctx, HEAD_W), lambda b, h: (ctx_rb0 + b, hb(OFF_Q) + h)),
            pl.BlockSpec((n_ctx, HEAD_W), lambda b, h: (ctx_rb0 + b, hb(OFF_K) + h)),
            pl.BlockSpec((n_ctx, HEAD_W), lambda b, h: (ctx_rb0 + b, hb(OFF_V) + h)),
            pl.BlockSpec((4, HEAD_DIM), lambda b, h: (0, 0)),
            pl.BlockSpec((1, HEAD_W), lambda b, h: (0, 0)),
        ],
        out_specs=pl.BlockSpec((n_ctx, HEAD_W), lambda b, h: (b, h)),
        compiler_params=_params(("arbitrary", "arbitrary")),
        name="attn_ctx",
    )(p, p, p, lam4, subg.reshape(1, HEAD_W))


def _conv_kernel(bg_ref, cg_ref, xi_ref, a_ref, g_ref,
                 cg_p, xi_p, a_p, g_p, cg_n, xi_n, a_n, g_n,
                 scw_ref, cfw_ref, cfb_ref, lng_ref, lnb_ref,
                 sco_ref, cfo_ref, pad_ref, *, lat_blocks, blocks_per_seq):
    i = pl.program_id(0)
    in_lat = i < lat_blocks
    pos = i % blocks_per_seq
    has_prev = jnp.logical_and(in_lat, pos != 0)
    has_next = jnp.logical_and(in_lat, pos != blocks_per_seq - 1)
    keep_prev = jnp.where(has_prev, 1.0, 0.0)
    keep_next = jnp.where(has_next, 1.0, 0.0)
    lo, hi = CONV_HALO, CONV_HALO + CONV_ROWS

    def fill(main, prev, nxt):
        pad_ref[0:lo, :] = prev * keep_prev
        pad_ref[lo:hi, :] = main
        pad_ref[hi:hi + CONV_HALO, :] = nxt * keep_next

    def f(ref):
        return ref[...].astype(F32)

    def glu(a, g):
        return a * _sigmoid(g)

    fill(f(cg_ref) * f(xi_ref), f(cg_p) * f(xi_p), f(cg_n) * f(xi_n))
    for c in range(CONV_ROWS // CONV_CHUNK):
        r0 = lo + c * CONV_CHUNK - SC_TAPS // 2
        acc = scw_ref[0:1, :] * pad_ref[r0:r0 + CONV_CHUNK, :]
        for k in range(1, SC_TAPS):
            acc = acc + scw_ref[k:k + 1, :] * pad_ref[r0 + k:r0 + k + CONV_CHUNK, :]
        rows = slice(c * CONV_CHUNK, (c + 1) * CONV_CHUNK)
        sco_ref[rows, :] = (bg_ref[rows, :].astype(F32) * acc).astype(sco_ref.dtype)

    fill(glu(f(a_ref), f(g_ref)), glu(f(a_p), f(g_p)), glu(f(a_n), f(g_n)))
    off = lo - CF_TAPS // 2
    win = CONV_CHUNK + SUBLANES
    for c in range(CONV_ROWS // CONV_CHUNK):
        base = c * CONV_CHUNK
        acc = None
        for b in range(SUBLANES):
            q = None
            for k in range(b, CF_TAPS, SUBLANES):
                term = cfw_ref[k:k + 1, :] * pad_ref[base + k - b:base + k - b + win, :]
                q = term if q is None else q + term
            part = q[off + b:off + b + CONV_CHUNK, :]
            acc = part if acc is None else acc + part
        z = acc + cfb_ref[...]
        mu = jnp.mean(z, axis=-1, keepdims=True)
        zc = z - mu
        var = jnp.mean(zc * zc, axis=-1, keepdims=True)
        y = zc * lax.rsqrt(var + CF_LN_EPS) * lng_ref[...] + lnb_ref[...]
        rows = slice(c * CONV_CHUNK, (c + 1) * CONV_CHUNK)
        cfo_ref[rows, :] = (y * _sigmoid(y)).astype(cfo_ref.dtype)


def _convs(p, sc_w, cf_w, cf_b, ln_g, ln_b, *, n_rows, n_lat, seq):
    nb = n_rows // CONV_ROWS
    halo_per_block = CONV_ROWS // CONV_HALO
    last_halo = n_rows // CONV_HALO - 1
    cb = lambda off: off // SC_W

    def main(off):
        return pl.BlockSpec((CONV_ROWS, SC_W), lambda i: (i, cb(off)))

    def prev(off):
        return pl.BlockSpec((CONV_HALO, SC_W), lambda i: (jnp.maximum(i * halo_per_block - 1, 0), cb(off)))

    def nxt(off):
        return pl.BlockSpec((CONV_HALO, SC_W), lambda i: (jnp.minimum((i + 1) * halo_per_block, last_halo), cb(off)))

    def vec(rows):
        return pl.BlockSpec((rows, SC_W), lambda i: (0, 0))

    o_bg, o_cg, o_xi, o_a, o_g = OFF_SC, OFF_SC + SC_W, OFF_SC + 2 * SC_W, OFF_CF, OFF_CF + CF_W
    kernel = functools.partial(_conv_kernel, lat_blocks=n_lat // CONV_ROWS, blocks_per_seq=seq // CONV_ROWS)
    return pl.pallas_call(
        kernel,
        out_shape=(jax.ShapeDtypeStruct((n_rows, SC_W), BF16), jax.ShapeDtypeStruct((n_rows, CF_W), BF16)),
        grid=(nb,),
        in_specs=[main(o_bg), main(o_cg), main(o_xi), main(o_a), main(o_g),
                  prev(o_cg), prev(o_xi), prev(o_a), prev(o_g),
                  nxt(o_cg), nxt(o_xi), nxt(o_a), nxt(o_g),
                  vec(SC_TAPS), vec(CF_TAPS), vec(1), vec(1), vec(1)],
        out_specs=(pl.BlockSpec((CONV_ROWS, SC_W), lambda i: (i, 0)),
                   pl.BlockSpec((CONV_ROWS, CF_W), lambda i: (i, 0))),
        scratch_shapes=[pltpu.VMEM((CONV_ROWS + 2 * CONV_HALO, SC_W), F32)],
        compiler_params=_params(("arbitrary",)),
        name="convs",
    )(p, p, p, p, p, p, p, p, p, p, p, p, p,
      sc_w, cf_w, cf_b.reshape(1, CF_W), ln_g.reshape(1, CF_W), ln_b.reshape(1, CF_W))


def _merge_kernel(*refs, a_blocks):
    n_gate = 3 * MERGE_GATE_TILES
    ata_ref, atb_ref, sc_ref, cf_ref = refs[:4]
    gate_refs = refs[4:4 + n_gate]
    bg_ref, wa_ref, wb_ref, wc_ref, wm_ref, ha_ref, hb_ref, g1_ref, o_ref = refs[4 + n_gate:]
    d = D_MODEL
    from_a = pl.program_id(0) < a_blocks
    attn = jnp.where(from_a, ata_ref[...], atb_ref[...])
    h = jnp.where(from_a, ha_ref[...], hb_ref[...])

    def gate(branch):
        tiles = gate_refs[branch * MERGE_GATE_TILES:(branch + 1) * MERGE_GATE_TILES]
        pre = jnp.concatenate([t[...] for t in tiles], axis=1).astype(F32)
        return _sigmoid(pre + bg_ref[:, branch * d:(branch + 1) * d])

    ya = jnp.dot(attn, wa_ref[...], preferred_element_type=F32)
    yb = jnp.dot(sc_ref[...], wb_ref[...], preferred_element_type=F32)
    yc = jnp.dot(cf_ref[...], wc_ref[...], preferred_element_type=F32)
    m = (gate(0) * ya + gate(1) * yb + gate(2) * yc).astype(BF16)
    o_ref[...] = h + g1_ref[...] * jnp.dot(m, wm_ref[...], preferred_element_type=F32)


def _merge(attn, sc, cf, p, b_gate, wa, wb, wc, wm, rows, mods, *, n_rows, n_lat, seq, gate_idx, tm=256):
    d = D_MODEL
    (ata, atb), (ha, hb) = attn, rows
    a_blocks = ha.shape[0] // tm
    first = lambda w: pl.BlockSpec((tm, w), lambda i: (jnp.minimum(i, a_blocks - 1), 0))
    rest = lambda w: pl.BlockSpec((tm, w), lambda i: (jnp.maximum(i - a_blocks, 0), 0))
    tg = d // MERGE_GATE_TILES
    gb0 = OFF_GATE // tg
    gates = [pl.BlockSpec((tm, tg), functools.partial(lambda i, c: (i, c), c=gb0 + t))
             for t in range(3 * MERGE_GATE_TILES)]
    whole = lambda a: pl.BlockSpec(a.shape, lambda i: (0, 0))
    return pl.pallas_call(
        functools.partial(_merge_kernel, a_blocks=a_blocks),
        out_shape=jax.ShapeDtypeStruct((n_rows, d), F32),
        grid=(n_rows // tm,),
        in_specs=[
            first(ATTN_W), rest(ATTN_W),
            pl.BlockSpec((tm, SC_W), lambda i: (i, 0)),
            pl.BlockSpec((tm, CF_W), lambda i: (i, 0)),
            *gates,
            whole(b_gate), whole(wa), whole(wb), whole(wc), whole(wm),
            first(d), rest(d),
            pl.BlockSpec((None, 1, d), lambda i: (_mod_row(i * tm, n_lat, seq), 0, gate_idx)),
        ],
        out_specs=pl.BlockSpec((tm, d), lambda i: (i, 0)),
        compiler_params=_params(("arbitrary",)),
        name="merge",
    )(ata, atb, sc, cf, *([p] * (3 * MERGE_GATE_TILES)), b_gate, wa, wb, wc, wm, ha, hb, mods)


def _split2(x):
    hi = x.astype(BF16)
    return hi, (x - hi.astype(F32)).astype(BF16)


def _norm_mod(x, g, sh, sc):
    ms = jnp.mean(x * x, axis=-1, keepdims=True)
    return x * lax.rsqrt(ms + NORM_EPS) * g * (1.0 + sc) + sh


def _router_kernel(h_ref, g_ref, sh_ref, sc_ref, rw_ref, rb_ref, tri_ref, id_ref, wt_ref, cnt_ref, run_ref):
    @pl.when(pl.program_id(0) == 0)
    def _():
        run_ref[...] = jnp.zeros_like(run_ref)

    f = _norm_mod(h_ref[...], g_ref[...], sh_ref[...], sc_ref[...])

    f_hi, f_lo = _split2(f)
    w_hi, w_lo = _split2(rw_ref[...])
    dot = lambda a, b: jnp.dot(a, b, preferred_element_type=F32)
    both = dot(f_hi, jnp.concatenate([w_hi, w_lo], axis=1))
    logits = (dot(f_lo, w_hi) + both[:, LANES:] + both[:, :LANES]) + rb_ref[...]

    lane_i = lax.broadcasted_iota(I32, logits.shape, 1)
    lane = lane_i.astype(F32)
    neg = -jnp.inf
    big = float(LANES)
    lg = jnp.where(lane_i < N_GROUPS, logits, neg)
    mg = jnp.max(lg, axis=-1, keepdims=True)
    grp = jnp.min(jnp.where(lg == mg, lane, big), axis=-1, keepdims=True)
    p_grp = 1.0 / jnp.sum(jnp.exp(lg - mg), axis=-1, keepdims=True)

    e_lane = lane - N_GROUPS
    in_grp = jnp.logical_and(e_lane >= grp * EXPERTS_PER_GROUP, e_lane < (grp + 1.0) * EXPERTS_PER_GROUP)
    le = jnp.where(in_grp, logits, neg)
    m1 = jnp.max(le, axis=-1, keepdims=True)
    i1 = jnp.min(jnp.where(le == m1, lane, big), axis=-1, keepdims=True)
    le2 = jnp.where(lane == i1, neg, le)
    m2 = jnp.max(le2, axis=-1, keepdims=True)
    i2 = jnp.min(jnp.where(le2 == m2, lane, big), axis=-1, keepdims=True)
    e2 = jnp.exp(m2 - m1)
    w1 = p_grp / (1.0 + e2)
    w2 = p_grp * e2 / (1.0 + e2)
    e1, e2 = i1 - N_GROUPS, i2 - N_GROUPS
    hot1 = jnp.where(lane == e1, 1.0, 0.0)
    hot2 = jnp.where(lane == e2, 1.0, 0.0)
    tri = tri_ref[...]
    before1 = jnp.dot(tri, hot1.astype(BF16), preferred_element_type=F32)
    before2 = jnp.dot(tri, hot2.astype(BF16), preferred_element_type=F32)
    tot1 = jnp.sum(hot1, axis=0, keepdims=True)
    tot2 = jnp.sum(hot2, axis=0, keepdims=True)
    run = run_ref[...]
    r1 = jnp.sum(hot1 * (run + before1), axis=-1, keepdims=True)
    r2 = jnp.sum(hot2 * (run + tot1 + before2), axis=-1, keepdims=True)
    run = run + tot1 + tot2
    run_ref[...] = run
    cnt_ref[...] = run

    ids = jnp.where(lane_i == 0, e1, jnp.where(lane_i == 1, e2, jnp.where(lane_i == 2, r1,
                                                                        jnp.where(lane_i == 3, r2, 0.0))))
    id_ref[...] = ids.astype(I32)
    wt_ref[...] = jnp.where(lane_i == 0, w1, jnp.where(lane_i == 1, w2, 0.0))


def _router(h, gain, mods, rw, rb, *, n_rows, n_lat, seq, sh_idx, sc_idx, tm=1024):
    d = D_MODEL

    def mod_map(idx):
        return lambda i: (_mod_row(i * tm, n_lat, seq), 0, idx)

    return pl.pallas_call(
        _router_kernel,
        out_shape=(jax.ShapeDtypeStruct((n_rows, LANES), I32),
                   jax.ShapeDtypeStruct((n_rows, LANES), F32),
                   jax.ShapeDtypeStruct((1, LANES), F32)),
        grid=(n_rows // tm,),
        in_specs=[
            pl.BlockSpec((tm, d), lambda i: (i, 0)),
            pl.BlockSpec((1, d), lambda i: (0, 0)),
            pl.BlockSpec((None, 1, d), mod_map(sh_idx)),
            pl.BlockSpec((None, 1, d), mod_map(sc_idx)),
            pl.BlockSpec((d, LANES), lambda i: (0, 0)),
            pl.BlockSpec((1, LANES), lambda i: (0, 0)),
            pl.BlockSpec((tm, tm), lambda i: (0, 0)),
        ],
        out_specs=(pl.BlockSpec((tm, LANES), lambda i: (i, 0)),
                   pl.BlockSpec((tm, LANES), lambda i: (i, 0)),
                   pl.BlockSpec((1, LANES), lambda i: (0, 0))),
        scratch_shapes=[pltpu.VMEM((1, LANES), F32)],
        compiler_params=_params(("arbitrary",)),
        name="router",
    )(h, gain.reshape(1, d), mods, mods, rw, rb, jnp.tril(jnp.ones((tm, tm), BF16), -1))


def _dispatch_kernel(pos_ref, h_ref, g_ref, sh_ref, sc_ref, xs_hbm, f0_ref, f1_ref, sem):
    i = pl.program_id(0)
    tm = h_ref.shape[0]

    def run(f_ref):
        def wait_block_copies():
            for _ in range(2):
                pltpu.make_async_copy(f_ref, xs_hbm.at[pl.ds(0, tm), :], sem).wait()

        f_ref[...] = _norm_mod(h_ref[...], g_ref[...], sh_ref[...], sc_ref[...])

        @pl.when(i > 0)
        def _():
            wait_block_copies()

        base = i * (2 * tm)
        for r in range(tm):
            for k in range(2):
                dst = pos_ref[base + 2 * r + k]
                pltpu.make_async_copy(f_ref.at[pl.ds(r, 1), :], xs_hbm.at[pl.ds(dst, 1), :], sem).start()

        @pl.when(i == pl.num_programs(0) - 1)
        def _():
            wait_block_copies()

    @pl.when(i % 2 == 0)
    def _():
        run(f0_ref)

    @pl.when(i % 2 == 1)
    def _():
        run(f1_ref)


def _dispatch(h, gain, mods, pos, *, n_rows, n_lat, seq, sh_idx, sc_idx, tm=DISPATCH_ROWS):
    d = D_MODEL

    def mod_map(idx):
        return lambda i, pos_ref: (_mod_row(i * tm, n_lat, seq), 0, idx)

    grid_spec = pltpu.PrefetchScalarGridSpec(
        num_scalar_prefetch=1,
        grid=(n_rows // tm,),
        in_specs=[
            pl.BlockSpec((tm, d), lambda i, pos_ref: (i, 0)),
            pl.BlockSpec((1, d), lambda i, pos_ref: (0, 0)),
            pl.BlockSpec((None, 1, d), mod_map(sh_idx)),
            pl.BlockSpec((None, 1, d), mod_map(sc_idx)),
        ],
        out_specs=pl.BlockSpec(memory_space=pl.ANY),
        scratch_shapes=[pltpu.VMEM((tm, d), F32), pltpu.VMEM((tm, d), F32), pltpu.SemaphoreType.DMA],
    )
    return pl.pallas_call(
        _dispatch_kernel,
        out_shape=jax.ShapeDtypeStruct((2 * n_rows, d), F32),
        grid_spec=grid_spec,
        compiler_params=_params(("arbitrary",)),
        name="dispatch",
    )(pos, h, gain.reshape(1, d), mods, mods)


def _moe_plan(ids, cnt, n_tok):
    n_items = N_EXPERTS + (2 * n_tok) // MOE_ITEM_CAP
    experts = jnp.arange(N_EXPERTS, dtype=I32)
    counts = cnt[0, :N_EXPERTS].astype(I32)
    starts = jnp.cumsum(counts) - counts
    e_slot, rank = ids[:, 0:2], ids[:, 2:4]
    start_of_slot = jnp.sum(jnp.where(e_slot[:, :, None] == experts, starts, 0), axis=-1)
    pos = (start_of_slot + rank).reshape(-1).astype(I32)
    pos = jnp.concatenate([pos, jnp.zeros((2 * COMBINE_ROWS,), I32)])
    n_sb = (counts + MOE_ITEM_CAP - 1) // MOE_ITEM_CAP
    cum = jnp.cumsum(n_sb)
    total = cum[-1]
    it = jnp.arange(n_items, dtype=I32)
    e_i = jnp.minimum(jnp.sum((it[:, None] >= cum[None, :]).astype(I32), axis=1), N_EXPERTS - 1)
    sb = it - (cum[e_i] - n_sb[e_i])
    valid = it < total
    last_e = e_i[jnp.maximum(total - 1, 0)]
    item_e = jnp.where(valid, e_i, last_e).astype(I32)
    item_start = jnp.where(valid, starts[e_i] + sb * MOE_ITEM_CAP, 0).astype(I32)
    item_n = jnp.where(valid, jnp.minimum(MOE_ITEM_CAP, counts[e_i] - sb * MOE_ITEM_CAP), 0).astype(I32)
    front, back = jnp.zeros((MOE_ITEM_SHIFT,), I32), jnp.zeros((1,), I32)
    pad_items = lambda a: jnp.concatenate([front, a, back])
    return (pad_items(item_e), pad_items(item_start), pad_items(item_n)), pos


def _moe_kernel(ie_ref, ist_ref, in_ref,
                xs_hbm, wg_ref, wu_ref, wd_ref, ys_hbm,
                xf_ref, xb_ref, y_ref, gsem, ssem):
    i, j = pl.program_id(0), pl.program_id(1)
    sh = MOE_ITEM_SHIFT
    n_prev, n, n_next = in_ref[i + sh - 1], in_ref[i + sh], in_ref[i + sh + 1]
    start_prev, start, start_next = ist_ref[i + sh - 1], ist_ref[i + sh], ist_ref[i + sh + 1]
    par = i % 2

    def rows_in(hrow, vrow, size):
        return pltpu.make_async_copy(xs_hbm.at[pl.ds(hrow, size), :], xf_ref.at[pl.ds(vrow, size), :], gsem)

    def rows_out(hrow, vrow, size):
        return pltpu.make_async_copy(y_ref.at[par, pl.ds(vrow, size), :], ys_hbm.at[pl.ds(hrow, size), :], ssem)

    def row_groups(make, base, count, act):
        shift = base % SUBLANES
        head = jnp.minimum((SUBLANES - shift) % SUBLANES, count)
        body = ((count - head) // SUBLANES) * SUBLANES
        for t in range(SUBLANES - 1):
            @pl.when(t < head)
            def _():
                getattr(make(base + t, shift + t, 1), act)()

        for b in reversed(range(SUBLANES.bit_length() - 1, MOE_ITEM_ROWS.bit_length())):
            @pl.when(((body >> b) & 1) == 1)
            def _():
                off = head + ((body >> (b + 1)) << (b + 1))
                getattr(make(pl.multiple_of(base + off, SUBLANES), pl.multiple_of(shift + off, SUBLANES), 1 << b),
                        act)()

        done = head + body
        for t in range(SUBLANES - 1):
            @pl.when(t < count - done)
            def _():
                getattr(make(base + done + t, shift + done + t, 1), act)()

    @pl.when(jnp.logical_and(i == 0, j == 0))
    def _():
        xf_ref[...] = jnp.zeros_like(xf_ref)
        y_ref[...] = jnp.zeros_like(y_ref)

    @pl.when(j == 0)
    def _():
        @pl.when(i == 0)
        def _():
            row_groups(rows_in, start, n, "start")

        @pl.when(n > 0)
        def _():
            row_groups(rows_in, start, n, "wait")
            xb_ref[...] = xf_ref[...].astype(BF16)
            row_groups(rows_in, start_next, n_next, "start")

    def compute(m):
        x = xb_ref[0:m, :]
        g = jnp.dot(x, wg_ref[...].astype(BF16), preferred_element_type=F32)
        u = jnp.dot(x, wu_ref[...].astype(BF16), preferred_element_type=F32)
        hmid = (g * _sigmoid(g) * u).astype(BF16)
        y = jnp.dot(hmid, wd_ref[...].astype(BF16), preferred_element_type=F32)
        acc = y_ref.at[par]
        acc[0:m, :] = y + jnp.where(j == 0, 0.0, acc[0:m, :])

    used = jnp.where(n > 0, start % SUBLANES + n, 0)
    n_pad = (used + MOE_ROW_PAD - 1) // MOE_ROW_PAD
    for k in range(1, MOE_ITEM_ROWS // MOE_ROW_PAD + 1):
        @pl.when(n_pad == k)
        def _():
            compute(k * MOE_ROW_PAD)

    @pl.when(j == pl.num_programs(1) - 1)
    def _():
        row_groups(rows_out, start_prev, n_prev, "wait")

        row_groups(rows_out, start, n, "start")

        @pl.when(i == pl.num_programs(0) - 1)
        def _():
            row_groups(rows_out, start, n, "wait")


def _moe(xs, items, w_gu, w_down, layer):
    item_e, item_start, item_n = items
    d, hid, hk = D_MODEL, EXPERT_HIDDEN, MOE_HIDDEN_BLOCK
    nj = hid // hk
    sh = MOE_ITEM_SHIFT
    n_items = item_e.shape[0] - sh - 1

    def chunk(j, nn, i):
        return jnp.where(nn[i + sh] > 0, j, nj - 1)

    grid_spec = pltpu.PrefetchScalarGridSpec(
        num_scalar_prefetch=3,
        grid=(n_items, nj),
        in_specs=[
            pl.BlockSpec(memory_space=pl.ANY),
            pl.BlockSpec((None, None, d, hk), lambda i, j, ie, ist, nn: (layer, ie[i + sh], 0, chunk(j, nn, i))),
            pl.BlockSpec((None, None, d, hk), lambda i, j, ie, ist, nn: (layer, ie[i + sh], 0, nj + chunk(j, nn, i))),
            pl.BlockSpec((None, None, hk, d), lambda i, j, ie, ist, nn: (layer, ie[i + sh], chunk(j, nn, i), 0)),
        ],
        out_specs=pl.BlockSpec(memory_space=pl.ANY),
        scratch_shapes=[
            pltpu.VMEM((MOE_ITEM_ROWS, d), F32),
            pltpu.VMEM((MOE_ITEM_ROWS, d), BF16),
            pltpu.VMEM((2, MOE_ITEM_ROWS, d), F32),
            pltpu.SemaphoreType.DMA,
            pltpu.SemaphoreType.DMA,
        ],
    )
    return pl.pallas_call(
        _moe_kernel,
        out_shape=jax.ShapeDtypeStruct(xs.shape, F32),
        grid_spec=grid_spec,
        compiler_params=_params(("arbitrary", "arbitrary"), MOE_VMEM_LIMIT),
        name="moe_experts",
    )(item_e, item_start, item_n, xs, w_gu, w_gu, w_down)


def _combine_kernel(pos_ref, h_ref, wt_ref, g2_ref, fg_ref, ys_hbm, o_ref, ybuf, sem, *, final):
    i = pl.program_id(0)
    tm = h_ref.shape[0]
    par = i % 2

    def fetch(step, half):
        base = step * (2 * tm)
        for r in range(tm):
            for k in range(2):
                src = pos_ref[base + 2 * r + k]
                pltpu.make_async_copy(ys_hbm.at[pl.ds(src, 1), :], ybuf.at[half, k, pl.ds(r, 1), :],
                                      sem.at[half]).start()

    @pl.when(i == 0)
    def _():
        fetch(0, 0)

    @pl.when(i < pl.num_programs(0) - 1)
    def _():
        fetch(i + 1, 1 - par)

    for k in range(2):
        pltpu.make_async_copy(ys_hbm.at[pl.ds(0, tm), :], ybuf.at[par, k], sem.at[par]).wait()
    wt = wt_ref[...]
    moe = wt[:, 0:1] * ybuf[par, 0] + wt[:, 1:2] * ybuf[par, 1]
    h = h_ref[...] + g2_ref[...] * moe
    if final:
        ms = jnp.mean(h * h, axis=-1, keepdims=True)
        h = h * lax.rsqrt(ms + NORM_EPS) * fg_ref[...]
    o_ref[...] = h


def _combine(h, ys, pos, wts, mods, final_g, *, n_rows, n_lat, seq, gate_idx, final, tm=COMBINE_ROWS):
    d = D_MODEL
    kernel = functools.partial(_combine_kernel, final=final)
    grid_spec = pltpu.PrefetchScalarGridSpec(
        num_scalar_prefetch=1,
        grid=(n_rows // tm,),
        in_specs=[
            pl.BlockSpec((tm, d), lambda i, pos_ref: (i, 0)),
            pl.BlockSpec((tm, LANES), lambda i, pos_ref: (i, 0)),
            pl.BlockSpec((None, 1, d), lambda i, pos_ref: (_mod_row(i * tm, n_lat, seq), 0, gate_idx)),
            pl.BlockSpec((1, d), lambda i, pos_ref: (0, 0)),
            pl.BlockSpec(memory_space=pl.ANY),
        ],
        out_specs=pl.BlockSpec((tm, d), lambda i, pos_ref: (i, 0)),
        scratch_shapes=[pltpu.VMEM((2, 2, tm, d), F32), pltpu.SemaphoreType.DMA((2,))],
    )
    return pl.pallas_call(
        kernel,
        out_shape=jax.ShapeDtypeStruct((n_rows, d), F32),
        grid_spec=grid_spec,
        compiler_params=_params(("arbitrary",)),
        name="combine",
    )(pos, h, wts, mods, final_g.reshape(1, d), ys)


def kernel(x, c, ctx, c_ctx, ada_w, ada_b, norm1_g, w_in, b_gate, diff_lambda, subln_g, w_attn_out, sc_conv_w,
           w_sc_out, cf_dw_w, cf_dw_b, cf_ln_g, cf_ln_b, w_cf_out, w_mix, norm2_g, router_g_w, router_g_b,
           router_e_w, router_e_b, exp_w_gu, exp_w_down, final_g):
    batch, seq, d = x.shape
    n_ctx = ctx.shape[1]
    depth = ada_w.shape[0]
    n_lat = batch * seq
    n_all = n_lat + batch * n_ctx
    assert d == D_MODEL and batch == MOD_ROWS // 2 and w_in.shape[2] == C_TOT

    cond = jnp.concatenate([c, c_ctx[None, :], jnp.zeros((MOD_ROWS - batch - 1, d), F32)], axis=0)
    mods_all = _adaln(cond, ada_w, ada_b)
    cos, sin = _rope_tables(seq)
    rows = (x.reshape(n_lat, d), ctx.reshape(batch * n_ctx, d))
    geo = dict(n_lat=n_lat, seq=seq)

    for layer in range(depth):
        last = layer == depth - 1
        lam_init = 0.8 - 0.6 * math.exp(-0.3 * layer)
        mods = mods_all[layer].reshape(MOD_ROWS, 1, 6 * d)
        n_rows = n_lat if last else n_all

        p = _in_proj(rows, norm1_g[layer], mods, w_in, layer, row0=0, n_rows=n_rows, col0=0, n_cols=C_TOT,
                     sh_idx=0, sc_idx=1, tm=1024, **geo)
        if last:
            kv_ctx = _in_proj(rows, norm1_g[layer], mods, w_in, layer, row0=n_lat, n_rows=batch * n_ctx, col0=OFF_K,
                              n_cols=OFF_SC - OFF_K, sh_idx=0, sc_idx=1, tm=n_ctx, **geo)
            kc_blk, vc_blk, ctx_rb0 = 0, QK_W // HEAD_W, 0
        else:
            kv_ctx = p
            kc_blk, vc_blk, ctx_rb0 = OFF_K // HEAD_W, OFF_V // HEAD_W, n_lat // n_ctx
        attn = _attn_lat(p, kv_ctx, kc_blk, vc_blk, ctx_rb0, cos, sin, diff_lambda[layer], subln_g[layer],
                         batch=batch, seq=seq, n_ctx=n_ctx, lam_init=lam_init)
        if not last:
            attn = (attn, _attn_ctx(p, n_lat // n_ctx, diff_lambda[layer], subln_g[layer],
                                    batch=batch, n_ctx=n_ctx, lam_init=lam_init))
        else:
            attn = (attn, attn)
        sc, cf = _convs(p, sc_conv_w[layer], cf_dw_w[layer], cf_dw_b[layer], cf_ln_g[layer], cf_ln_b[layer],
                        n_rows=n_rows, **geo)
        h_mix = _merge(attn, sc, cf, p, b_gate[layer].reshape(1, 3 * d),
                       w_attn_out[layer].astype(BF16), w_sc_out[layer].astype(BF16),
                       w_cf_out[layer].astype(BF16), w_mix[layer].astype(BF16), rows, mods,
                       n_rows=n_rows, gate_idx=2, **geo)

        rw = jnp.concatenate([router_g_w[layer], router_e_w[layer],
                              jnp.zeros((d, LANES - N_GROUPS - N_EXPERTS), F32)], axis=1)
        rb = jnp.concatenate([router_g_b[layer], router_e_b[layer],
                              jnp.zeros((LANES - N_GROUPS - N_EXPERTS,), F32)]).reshape(1, LANES)
        ids, wts, cnt = _router(h_mix, norm2_g[layer], mods, rw, rb, n_rows=n_rows, sh_idx=3, sc_idx=4, **geo)
        items, pos = _moe_plan(ids, cnt, n_rows)
        xs = _dispatch(h_mix, norm2_g[layer], mods, pos, n_rows=n_rows, sh_idx=3, sc_idx=4, **geo)
        ys = _moe(xs, items, exp_w_gu, exp_w_down, layer)
        h = _combine(h_mix, ys, pos, wts, mods, final_g, n_rows=n_rows, gate_idx=5, final=last, **geo)
        rows = (h, h)

    return h.reshape(batch, seq, d)
```

```python
import functools
import math

import jax
import jax.numpy as jnp
from jax import lax
from jax.experimental import pallas as pl
from jax.experimental.pallas import tpu as pltpu

F32 = jnp.float32
BF16 = jnp.bfloat16
I32 = jnp.int32

D_MODEL = 2048
GRID_W = 64
NORM_EPS = 1e-6
N_HEADS = 8
HEAD_DIM = 64
HEAD_W = 2 * HEAD_DIM
QK_W = N_HEADS * HEAD_W
ATTN_W = N_HEADS * HEAD_W
ROPE_BASE = 10000.0
ROPE_FREQS = HEAD_DIM // 4
SUBLN_EPS = 1e-5
QK_SCALE = HEAD_DIM ** -0.5 * math.log2(math.e)
SC_W = D_MODEL // 4
CF_W = D_MODEL // 4
SC_TAPS = 3
CF_TAPS = 31
CF_LN_EPS = 1e-5
OFF_Q = 0
OFF_K = OFF_Q + QK_W
OFF_V = OFF_K + QK_W
OFF_SC = OFF_V + ATTN_W
OFF_CF = OFF_SC + 3 * SC_W
OFF_GATE = OFF_CF + 2 * CF_W
C_TOT = OFF_GATE + 3 * D_MODEL
N_GROUPS = 4
EXPERTS_PER_GROUP = 8
N_EXPERTS = N_GROUPS * EXPERTS_PER_GROUP
EXPERT_HIDDEN = D_MODEL // 2

LANES = 128
SUBLANES = 8
MOD_ROWS = 8
MERGE_GATE_TILES = 4
CONV_HALO = 16
CONV_ROWS = 256
CONV_CHUNK = 32
MOE_ITEM_ROWS = 1024
MOE_ITEM_CAP = MOE_ITEM_ROWS - 8
MOE_ROW_PAD = 128
MOE_HIDDEN_BLOCK = 512
MOE_ITEM_SHIFT = 1
DISPATCH_ROWS = 512
COMBINE_ROWS = 256
VMEM_LIMIT = 56 * 1024 * 1024
MOE_VMEM_LIMIT = 60 * 1024 * 1024


def _params(sem, vmem=VMEM_LIMIT):
    return pltpu.CompilerParams(dimension_semantics=sem, vmem_limit_bytes=vmem)


def _sigmoid(x):
    return 1.0 / (1.0 + jnp.exp(-x))


def _adaln_kernel(s_ref, w_ref, b_ref, o_ref):
    s = s_ref[...]
    s = s * _sigmoid(s)
    s_hi = s.astype(BF16)
    w = w_ref[...]
    w_hi = w.astype(BF16)
    w_lo = (w - w_hi.astype(F32)).astype(BF16)
    lhs = jnp.concatenate([s_hi.astype(F32), s - s_hi.astype(F32)], axis=0).astype(BF16)
    r = jnp.dot(lhs, w_hi, preferred_element_type=F32)
    r2 = jnp.dot(s_hi, w_lo, preferred_element_type=F32)
    o_ref[...] = r[:MOD_ROWS] + r[MOD_ROWS:] + r2 + b_ref[...]


def _adaln(cond, ada_w, ada_b):
    n_layers, d, n = ada_w.shape
    tn = 512
    return pl.pallas_call(
        _adaln_kernel,
        out_shape=jax.ShapeDtypeStruct((n_layers, MOD_ROWS, n), F32),
        grid=(n_layers, n // tn),
        in_specs=[
            pl.BlockSpec((MOD_ROWS, d), lambda l, j: (0, 0)),
            pl.BlockSpec((None, d, tn), lambda l, j: (l, 0, j)),
            pl.BlockSpec((None, 1, tn), lambda l, j: (l, 0, j)),
        ],
        out_specs=pl.BlockSpec((None, MOD_ROWS, tn), lambda l, j: (l, 0, j)),
        compiler_params=_params(("arbitrary", "arbitrary")),
        name="adaln",
    )(cond, ada_w, ada_b.reshape(n_layers, 1, n))


def _mod_row(row0, n_lat, seq):
    return jnp.where(row0 < n_lat, row0 // seq, MOD_ROWS // 2)


def _in_proj_kernel(xa_ref, xb_ref, g_ref, sh_ref, sc_ref, w_ref, o_ref, u_ref, *, a_blocks):
    first_col = pl.program_id(1) == 0
    from_a = pl.program_id(0) < a_blocks

    def normed(x_ref):
        x = x_ref[...]
        ms = jnp.mean(x * x, axis=-1, keepdims=True)
        y = x * lax.rsqrt(ms + NORM_EPS) * g_ref[...]
        u_ref[...] = (y * (1.0 + sc_ref[...]) + sh_ref[...]).astype(BF16)

    @pl.when(jnp.logical_and(first_col, from_a))
    def _():
        normed(xa_ref)

    @pl.when(jnp.logical_and(first_col, jnp.logical_not(from_a)))
    def _():
        normed(xb_ref)

    o_ref[...] = jnp.dot(u_ref[...], w_ref[...].astype(BF16), preferred_element_type=F32).astype(o_ref.dtype)


def _in_proj(rows, gain, mods, w_in, layer, *, row0, n_rows, col0, n_cols, n_lat, seq, sh_idx, sc_idx, tm, tn=512):
    xa, xb = rows
    d = xa.shape[1]
    rb0, cb0 = row0 // tm, col0 // tn
    a_blocks = xa.shape[0] // tm - rb0

    def mod_map(idx):
        return lambda i, j: (_mod_row((i + rb0) * tm, n_lat, seq), 0, idx)

    return pl.pallas_call(
        functools.partial(_in_proj_kernel, a_blocks=a_blocks),
        out_shape=jax.ShapeDtypeStruct((n_rows, n_cols), BF16),
        grid=(n_rows // tm, n_cols // tn),
        in_specs=[
            pl.BlockSpec((tm, d), lambda i, j: (jnp.minimum(i, a_blocks - 1) + rb0, 0)),
            pl.BlockSpec((tm, d), lambda i, j: (jnp.maximum(i - a_blocks, 0), 0)),
            pl.BlockSpec((1, d), lambda i, j: (0, 0)),
            pl.BlockSpec((None, 1, d), mod_map(sh_idx)),
            pl.BlockSpec((None, 1, d), mod_map(sc_idx)),
            pl.BlockSpec((None, d, tn), lambda i, j: (layer, 0, j + cb0)),
        ],
        out_specs=pl.BlockSpec((tm, tn), lambda i, j: (i, j)),
        scratch_shapes=[pltpu.VMEM((tm, d), BF16)],
        compiler_params=_params(("arbitrary", "arbitrary")),
        name="in_proj",
    )(xa, xb, gain.reshape(1, d), mods, mods, w_in)


def _rope_tables(n_tokens):
    t = jnp.arange(n_tokens, dtype=I32)
    pos = jnp.stack([t // GRID_W, t % GRID_W], axis=-1).astype(F32)
    inv_freq = ROPE_BASE ** (-jnp.arange(ROPE_FREQS, dtype=F32) / ROPE_FREQS)
    ang = pos[:, :, None] * inv_freq
    cos, sin = jnp.cos(ang), jnp.sin(ang)
    c = jnp.stack([cos, cos], axis=2).reshape(n_tokens, HEAD_DIM)
    s = jnp.stack([-sin, sin], axis=2).reshape(n_tokens, HEAD_DIM)
    return jnp.tile(c, (1, 2)), jnp.tile(s, (1, 2))


def _rope(x, c, s):
    lane = lax.broadcasted_iota(I32, x.shape, 1)
    first_half = (lane % (2 * ROPE_FREQS)) < ROPE_FREQS
    partner = jnp.where(first_half, pltpu.roll(x, LANES - ROPE_FREQS, 1), pltpu.roll(x, ROPE_FREQS, 1))
    return x * c + partner * s


def _diff_lambda(lam_ref, lam_init):
    lv = lam_ref[...]
    a = jnp.sum(lv[0:1] * lv[1:2], axis=-1, keepdims=True)
    b = jnp.sum(lv[2:3] * lv[3:4], axis=-1, keepdims=True)
    return jnp.exp(a) - jnp.exp(b) + lam_init


def _attend(q, k_all, v_all, lam, subg, lam_init):
    tq = q.shape[0]
    lane = lax.broadcasted_iota(I32, q.shape, 1)
    q0 = jnp.where(lane < HEAD_DIM, q, 0.0).astype(BF16)
    q1 = jnp.where(lane >= HEAD_DIM, q, 0.0).astype(BF16)
    qq = jnp.concatenate([q0, q1], axis=0)
    s = lax.dot_general(qq, k_all, (((1,), (1,)), ((), ())), preferred_element_type=F32)
    m = jnp.max(s, axis=-1, keepdims=True)
    e = jnp.exp2(s - m)
    l = jnp.sum(e, axis=-1, keepdims=True)
    a = e[:tq] - e[tq:] * (lam * l[:tq] / l[tq:])
    o = jnp.dot(a.astype(BF16), v_all, preferred_element_type=F32) / l[:tq]
    ms = jnp.mean(o * o, axis=-1, keepdims=True)
    return o * lax.rsqrt(ms + SUBLN_EPS) * subg * (1.0 - lam_init)


def _attn_lat_kernel(q_ref, kl_ref, vl_ref, kc_ref, vc_ref, cq_ref, sq_ref, ck_ref, sk_ref, lam_ref, g_ref,
                     o_ref, k_all, v_aug, s0_ref, m0_ref, s1_ref, m1_ref, *, n_ctx, lam_init, n_blocks):
    i = pl.program_id(2)
    tq = q_ref.shape[0]

    @pl.when(i == 0)
    def _():
        k_all[0:n_ctx, :] = kc_ref[...]
        k_all[n_ctx:, :] = _rope(kl_ref[...].astype(F32), ck_ref[...], sk_ref[...]).astype(BF16)
        v_aug[0:n_ctx, 0:HEAD_W] = vc_ref[...]
        v_aug[n_ctx:, 0:HEAD_W] = vl_ref[...]
        v_aug[:, HEAD_W:2 * HEAD_W] = jnp.ones((k_all.shape[0], HEAD_W), BF16)

    lam = _diff_lambda(lam_ref, lam_init)

    def stage_a(sa_ref, ma_ref):
        q = _rope(q_ref[...].astype(F32), cq_ref[...], sq_ref[...]) * QK_SCALE
        lane = lax.broadcasted_iota(I32, q.shape, 1)
        q0 = jnp.where(lane < HEAD_DIM, q, 0.0).astype(BF16)
        q1 = jnp.where(lane >= HEAD_DIM, q, 0.0).astype(BF16)
        qq = jnp.concatenate([q0, q1], axis=0)
        s = lax.dot_general(qq, k_all[...], (((1,), (1,)), ((), ())), preferred_element_type=F32)
        sa_ref[...] = s
        ma_ref[...] = jnp.broadcast_to(jnp.max(s, axis=-1, keepdims=True), ma_ref.shape)

    def stage_b(sb_ref, mb_ref):
        mb = jnp.concatenate([mb_ref[...]] * (sb_ref.shape[1] // LANES), axis=1)
        e = jnp.exp2(sb_ref[...] - mb).astype(BF16)
        oa = jnp.dot(e, v_aug[...], preferred_element_type=F32)
        o = oa[:tq, 0:HEAD_W] / oa[:tq, HEAD_W:] - oa[tq:, 0:HEAD_W] * (lam / oa[tq:, HEAD_W:])
        ms = jnp.mean(o * o, axis=-1, keepdims=True)
        o_ref[...] = (o * lax.rsqrt(ms + SUBLN_EPS) * g_ref[...] * (1.0 - lam_init)).astype(o_ref.dtype)

    bufs = ((s0_ref, m0_ref), (s1_ref, m1_ref))

    @pl.when(i == 0)
    def _():
        stage_a(*bufs[0])

    for par in range(2):
        @pl.when(jnp.logical_and(jnp.logical_and(i > 0, i < n_blocks), i % 2 == par))
        def _():
            stage_a(*bufs[par])
            stage_b(*bufs[1 - par])

    @pl.when(i == n_blocks)
    def _():
        stage_b(*bufs[(n_blocks - 1) % 2])


def _attn_ctx_kernel(q_ref, k_ref, v_ref, lam_ref, g_ref, o_ref, *, lam_init):
    q = q_ref[...].astype(F32) * QK_SCALE
    lam = _diff_lambda(lam_ref, lam_init)
    o_ref[...] = _attend(q, k_ref[...], v_ref[...], lam, g_ref[...], lam_init).astype(o_ref.dtype)


def _attn_lat(p, kv_ctx, kc_blk, vc_blk, ctx_rb0, cos, sin, lam4, subg, *, batch, seq, n_ctx, lam_init, tq=512):
    nq = seq // tq
    n_keys = n_ctx + seq
    hb = lambda off: off // HEAD_W
    qblk = lambda i: jnp.minimum(i, nq - 1)
    kernel = functools.partial(_attn_lat_kernel, n_ctx=n_ctx, lam_init=lam_init, n_blocks=nq)
    return pl.pallas_call(
        kernel,
        out_shape=jax.ShapeDtypeStruct((batch * seq, ATTN_W), BF16),
        grid=(batch, N_HEADS, nq + 1),
        in_specs=[
            pl.BlockSpec((tq, HEAD_W), lambda b, h, i: (b * nq + qblk(i), hb(OFF_Q) + h)),
            pl.BlockSpec((seq, HEAD_W), lambda b, h, i: (b, hb(OFF_K) + h)),
            pl.BlockSpec((seq, HEAD_W), lambda b, h, i: (b, hb(OFF_V) + h)),
            pl.BlockSpec((n_ctx, HEAD_W), lambda b, h, i: (ctx_rb0 + b, kc_blk + h)),
            pl.BlockSpec((n_ctx, HEAD_W), lambda b, h, i: (ctx_rb0 + b, vc_blk + h)),
            pl.BlockSpec((tq, HEAD_W), lambda b, h, i: (qblk(i), 0)),
            pl.BlockSpec((tq, HEAD_W), lambda b, h, i: (qblk(i), 0)),
            pl.BlockSpec((seq, HEAD_W), lambda b, h, i: (0, 0)),
            pl.BlockSpec((seq, HEAD_W), lambda b, h, i: (0, 0)),
            pl.BlockSpec((4, HEAD_DIM), lambda b, h, i: (0, 0)),
            pl.BlockSpec((1, HEAD_W), lambda b, h, i: (0, 0)),
        ],
        out_specs=pl.BlockSpec((tq, HEAD_W), lambda b, h, i: (b * nq + jnp.maximum(i - 1, 0), h)),
        scratch_shapes=[pltpu.VMEM((n_keys, HEAD_W), BF16), pltpu.VMEM((n_keys, 2 * HEAD_W), BF16),
                        pltpu.VMEM((2 * tq, n_keys), F32), pltpu.VMEM((2 * tq, LANES), F32),
                        pltpu.VMEM((2 * tq, n_keys), F32), pltpu.VMEM((2 * tq, LANES), F32)],
        compiler_params=_params(("arbitrary", "arbitrary", "arbitrary")),
        name="attn_lat",
    )(p, p, p, kv_ctx, kv_ctx, cos, sin, cos, sin, lam4, subg.reshape(1, HEAD_W))


def _attn_ctx(p, ctx_rb0, lam4, subg, *, batch, n_ctx, lam_init):
    hb = lambda off: off // HEAD_W
    kernel = functools.partial(_attn_ctx_kernel, lam_init=lam_init)
    return pl.pallas_call(
        kernel,
        out_shape=jax.ShapeDtypeStruct((batch * n_ctx, ATTN_W), BF16),
        grid=(batch, N_HEADS),
        in_specs=[
            pl.BlockSpec((n_ctx, HEAD_W), lambda b, h: (ctx_rb0 + b, hb(OFF_Q) + h)),
            pl.BlockSpec((n_ctx, HEAD_W), lambda b, h: (ctx_rb0 + b, hb(OFF_K) + h)),
            pl.BlockSpec((n_ctx, HEAD_W), lambda b, h: (ctx_rb0 + b, hb(OFF_V) + h)),
            pl.BlockSpec((4, HEAD_DIM), lambda b, h: (0, 0)),
            pl.BlockSpec((1, HEAD_W), lambda b, h: (0, 0)),
        ],
        out_specs=pl.BlockSpec((n_ctx, HEAD_W), lambda b, h: (b, h)),
        compiler_params=_params(("arbitrary", "arbitrary")),
        name="attn_ctx",
    )(p, p, p, lam4, subg.reshape(1, HEAD_W))


def _conv_kernel(bg_ref, cg_ref, xi_ref, a_ref, g_ref,
                 cg_p, xi_p, a_p, g_p, cg_n, xi_n, a_n, g_n,
                 scw_ref, cfw_ref, cfb_ref, lng_ref, lnb_ref,
                 sco_ref, cfo_ref, pad_ref, *, lat_blocks, blocks_per_seq):
    i = pl.program_id(0)
    in_lat = i < lat_blocks
    pos = i % blocks_per_seq
    has_prev = jnp.logical_and(in_lat, pos != 0)
    has_next = jnp.logical_and(in_lat, pos != blocks_per_seq - 1)
    keep_prev = jnp.where(has_prev, 1.0, 0.0)
    keep_next = jnp.where(has_next, 1.0, 0.0)
    lo, hi = CONV_HALO, CONV_HALO + CONV_ROWS

    def fill(main, prev, nxt):
        pad_ref[0:lo, :] = prev * keep_prev
        pad_ref[lo:hi, :] = main
        pad_ref[hi:hi + CONV_HALO, :] = nxt * keep_next

    def f(ref):
        return ref[...].astype(F32)

    def glu(a, g):
        return a * _sigmoid(g)

    fill(f(cg_ref) * f(xi_ref), f(cg_p) * f(xi_p), f(cg_n) * f(xi_n))
    for c in range(CONV_ROWS // CONV_CHUNK):
        r0 = lo + c * CONV_CHUNK - SC_TAPS // 2
        acc = scw_ref[0:1, :] * pad_ref[r0:r0 + CONV_CHUNK, :]
        for k in range(1, SC_TAPS):
            acc = acc + scw_ref[k:k + 1, :] * pad_ref[r0 + k:r0 + k + CONV_CHUNK, :]
        rows = slice(c * CONV_CHUNK, (c + 1) * CONV_CHUNK)
        sco_ref[rows, :] = (bg_ref[rows, :].astype(F32) * acc).astype(sco_ref.dtype)

    fill(glu(f(a_ref), f(g_ref)), glu(f(a_p), f(g_p)), glu(f(a_n), f(g_n)))
    off = lo - CF_TAPS // 2
    win = CONV_CHUNK + SUBLANES
    for c in range(CONV_ROWS // CONV_CHUNK):
        base = c * CONV_CHUNK
        acc = None
        for b in range(SUBLANES):
            q = None
            for k in range(b, CF_TAPS, SUBLANES):
                term = cfw_ref[k:k + 1, :] * pad_ref[base + k - b:base + k - b + win, :]
                q = term if q is None else q + term
            part = q[off + b:off + b + CONV_CHUNK, :]
            acc = part if acc is None else acc + part
        z = acc + cfb_ref[...]
        mu = jnp.mean(z, axis=-1, keepdims=True)
        zc = z - mu
        var = jnp.mean(zc * zc, axis=-1, keepdims=True)
        y = zc * lax.rsqrt(var + CF_LN_EPS) * lng_ref[...] + lnb_ref[...]
        rows = slice(c * CONV_CHUNK, (c + 1) * CONV_CHUNK)
        cfo_ref[rows, :] = (y * _sigmoid(y)).astype(cfo_ref.dtype)


def _convs(p, sc_w, cf_w, cf_b, ln_g, ln_b, *, n_rows, n_lat, seq):
    nb = n_rows // CONV_ROWS
    halo_per_block = CONV_ROWS // CONV_HALO
    last_halo = n_rows // CONV_HALO - 1
    cb = lambda off: off // SC_W

    def main(off):
        return pl.BlockSpec((CONV_ROWS, SC_W), lambda i: (i, cb(off)))

    def prev(off):
        return pl.BlockSpec((CONV_HALO, SC_W), lambda i: (jnp.maximum(i * halo_per_block - 1, 0), cb(off)))

    def nxt(off):
        return pl.BlockSpec((CONV_HALO, SC_W), lambda i: (jnp.minimum((i + 1) * halo_per_block, last_halo), cb(off)))

    def vec(rows):
        return pl.BlockSpec((rows, SC_W), lambda i: (0, 0))

    o_bg, o_cg, o_xi, o_a, o_g = OFF_SC, OFF_SC + SC_W, OFF_SC + 2 * SC_W, OFF_CF, OFF_CF + CF_W
    kernel = functools.partial(_conv_kernel, lat_blocks=n_lat // CONV_ROWS, blocks_per_seq=seq // CONV_ROWS)
    return pl.pallas_call(
        kernel,
        out_shape=(jax.ShapeDtypeStruct((n_rows, SC_W), BF16), jax.ShapeDtypeStruct((n_rows, CF_W), BF16)),
        grid=(nb,),
        in_specs=[main(o_bg), main(o_cg), main(o_xi), main(o_a), main(o_g),
                  prev(o_cg), prev(o_xi), prev(o_a), prev(o_g),
                  nxt(o_cg), nxt(o_xi), nxt(o_a), nxt(o_g),
                  vec(SC_TAPS), vec(CF_TAPS), vec(1), vec(1), vec(1)],
        out_specs=(pl.BlockSpec((CONV_ROWS, SC_W), lambda i: (i, 0)),
                   pl.BlockSpec((CONV_ROWS, CF_W), lambda i: (i, 0))),
        scratch_shapes=[pltpu.VMEM((CONV_ROWS + 2 * CONV_HALO, SC_W), F32)],
        compiler_params=_params(("arbitrary",)),
        name="convs",
    )(p, p, p, p, p, p, p, p, p, p, p, p, p,
      sc_w, cf_w, cf_b.reshape(1, CF_W), ln_g.reshape(1, CF_W), ln_b.reshape(1, CF_W))


def _merge_kernel(*refs, a_blocks):
    n_gate = 3 * MERGE_GATE_TILES
    ata_ref, atb_ref, sc_ref, cf_ref = refs[:4]
    gate_refs = refs[4:4 + n_gate]
    bg_ref, wa_ref, wb_ref, wc_ref, wm_ref, ha_ref, hb_ref, g1_ref, o_ref = refs[4 + n_gate:]
    d = D_MODEL
    from_a = pl.program_id(0) < a_blocks
    attn = jnp.where(from_a, ata_ref[...], atb_ref[...])
    h = jnp.where(from_a, ha_ref[...], hb_ref[...])

    def gate(branch):
        tiles = gate_refs[branch * MERGE_GATE_TILES:(branch + 1) * MERGE_GATE_TILES]
        pre = jnp.concatenate([t[...] for t in tiles], axis=1).astype(F32)
        return _sigmoid(pre + bg_ref[:, branch * d:(branch + 1) * d])

    ya = jnp.dot(attn, wa_ref[...], preferred_element_type=F32)
    yb = jnp.dot(sc_ref[...], wb_ref[...], preferred_element_type=F32)
    yc = jnp.dot(cf_ref[...], wc_ref[...], preferred_element_type=F32)
    m = (gate(0) * ya + gate(1) * yb + gate(2) * yc).astype(BF16)
    o_ref[...] = h + g1_ref[...] * jnp.dot(m, wm_ref[...], preferred_element_type=F32)


def _merge(attn, sc, cf, p, b_gate, wa, wb, wc, wm, rows, mods, *, n_rows, n_lat, seq, gate_idx, tm=256):
    d = D_MODEL
    (ata, atb), (ha, hb) = attn, rows
    a_blocks = ha.shape[0] // tm
    first = lambda w: pl.BlockSpec((tm, w), lambda i: (jnp.minimum(i, a_blocks - 1), 0))
    rest = lambda w: pl.BlockSpec((tm, w), lambda i: (jnp.maximum(i - a_blocks, 0), 0))
    tg = d // MERGE_GATE_TILES
    gb0 = OFF_GATE // tg
    gates = [pl.BlockSpec((tm, tg), functools.partial(lambda i, c: (i, c), c=gb0 + t))
             for t in range(3 * MERGE_GATE_TILES)]
    whole = lambda a: pl.BlockSpec(a.shape, lambda i: (0, 0))
    return pl.pallas_call(
        functools.partial(_merge_kernel, a_blocks=a_blocks),
        out_shape=jax.ShapeDtypeStruct((n_rows, d), F32),
        grid=(n_rows // tm,),
        in_specs=[
            first(ATTN_W), rest(ATTN_W),
            pl.BlockSpec((tm, SC_W), lambda i: (i, 0)),
            pl.BlockSpec((tm, CF_W), lambda i: (i, 0)),
            *gates,
            whole(b_gate), whole(wa), whole(wb), whole(wc), whole(wm),
            first(d), rest(d),
            pl.BlockSpec((None, 1, d), lambda i: (_mod_row(i * tm, n_lat, seq), 0, gate_idx)),
        ],
        out_specs=pl.BlockSpec((tm, d), lambda i: (i, 0)),
        compiler_params=_params(("arbitrary",)),
        name="merge",
    )(ata, atb, sc, cf, *([p] * (3 * MERGE_GATE_TILES)), b_gate, wa, wb, wc, wm, ha, hb, mods)


def _split2(x):
    hi = x.astype(BF16)
    return hi, (x - hi.astype(F32)).astype(BF16)


def _norm_mod(x, g, sh, sc):
    ms = jnp.mean(x * x, axis=-1, keepdims=True)
    return x * lax.rsqrt(ms + NORM_EPS) * g * (1.0 + sc) + sh


def _router_kernel(h_ref, g_ref, sh_ref, sc_ref, rw_ref, rb_ref, tri_ref, id_ref, wt_ref, cnt_ref, run_ref):
    @pl.when(pl.program_id(0) == 0)
    def _():
        run_ref[...] = jnp.zeros_like(run_ref)

    f = _norm_mod(h_ref[...], g_ref[...], sh_ref[...], sc_ref[...])

    f_hi, f_lo = _split2(f)
    w_hi, w_lo = _split2(rw_ref[...])
    dot = lambda a, b: jnp.dot(a, b, preferred_element_type=F32)
    both = dot(f_hi, jnp.concatenate([w_hi, w_lo], axis=1))
    logits = (dot(f_lo, w_hi) + both[:, LANES:] + both[:, :LANES]) + rb_ref[...]

    lane_i = lax.broadcasted_iota(I32, logits.shape, 1)
    lane = lane_i.astype(F32)
    neg = -jnp.inf
    big = float(LANES)
    lg = jnp.where(lane_i < N_GROUPS, logits, neg)
    mg = jnp.max(lg, axis=-1, keepdims=True)
    grp = jnp.min(jnp.where(lg == mg, lane, big), axis=-1, keepdims=True)
    p_grp = 1.0 / jnp.sum(jnp.exp(lg - mg), axis=-1, keepdims=True)

    e_lane = lane - N_GROUPS
    in_grp = jnp.logical_and(e_lane >= grp * EXPERTS_PER_GROUP, e_lane < (grp + 1.0) * EXPERTS_PER_GROUP)
    le = jnp.where(in_grp, logits, neg)
    m1 = jnp.max(le, axis=-1, keepdims=True)
    i1 = jnp.min(jnp.where(le == m1, lane, big), axis=-1, keepdims=True)
    le2 = jnp.where(lane == i1, neg, le)
    m2 = jnp.max(le2, axis=-1, keepdims=True)
    i2 = jnp.min(jnp.where(le2 == m2, lane, big), axis=-1, keepdims=True)
    e2 = jnp.exp(m2 - m1)
    w1 = p_grp / (1.0 + e2)
    w2 = p_grp * e2 / (1.0 + e2)
    e1, e2 = i1 - N_GROUPS, i2 - N_GROUPS
    hot1 = jnp.where(lane == e1, 1.0, 0.0)
    hot2 = jnp.where(lane == e2, 1.0, 0.0)
    tri = tri_ref[...]
    before1 = jnp.dot(tri, hot1.astype(BF16), preferred_element_type=F32)
    before2 = jnp.dot(tri, hot2.astype(BF16), preferred_element_type=F32)
    tot1 = jnp.sum(hot1, axis=0, keepdims=True)
    tot2 = jnp.sum(hot2, axis=0, keepdims=True)
    run = run_ref[...]
    r1 = jnp.sum(hot1 * (run + before1), axis=-1, keepdims=True)
    r2 = jnp.sum(hot2 * (run + tot1 + before2), axis=-1, keepdims=True)
    run = run + tot1 + tot2
    run_ref[...] = run
    cnt_ref[...] = run

    ids = jnp.where(lane_i == 0, e1, jnp.where(lane_i == 1, e2, jnp.where(lane_i == 2, r1,
                                                                        jnp.where(lane_i == 3, r2, 0.0))))
    id_ref[...] = ids.astype(I32)
    wt_ref[...] = jnp.where(lane_i == 0, w1, jnp.where(lane_i == 1, w2, 0.0))


def _router(h, gain, mods, rw, rb, *, n_rows, n_lat, seq, sh_idx, sc_idx, tm=1024):
    d = D_MODEL

    def mod_map(idx):
        return lambda i: (_mod_row(i * tm, n_lat, seq), 0, idx)

    return pl.pallas_call(
        _router_kernel,
        out_shape=(jax.ShapeDtypeStruct((n_rows, LANES), I32),
                   jax.ShapeDtypeStruct((n_rows, LANES), F32),
                   jax.ShapeDtypeStruct((1, LANES), F32)),
        grid=(n_rows // tm,),
        in_specs=[
            pl.BlockSpec((tm, d), lambda i: (i, 0)),
            pl.BlockSpec((1, d), lambda i: (0, 0)),
            pl.BlockSpec((None, 1, d), mod_map(sh_idx)),
            pl.BlockSpec((None, 1, d), mod_map(sc_idx)),
            pl.BlockSpec((d, LANES), lambda i: (0, 0)),
            pl.BlockSpec((1, LANES), lambda i: (0, 0)),
            pl.BlockSpec((tm, tm), lambda i: (0, 0)),
        ],
        out_specs=(pl.BlockSpec((tm, LANES), lambda i: (i, 0)),
                   pl.BlockSpec((tm, LANES), lambda i: (i, 0)),
                   pl.BlockSpec((1, LANES), lambda i: (0, 0))),
        scratch_shapes=[pltpu.VMEM((1, LANES), F32)],
        compiler_params=_params(("arbitrary",)),
        name="router",
    )(h, gain.reshape(1, d), mods, mods, rw, rb, jnp.tril(jnp.ones((tm, tm), BF16), -1))


def _dispatch_kernel(pos_ref, h_ref, g_ref, sh_ref, sc_ref, xs_hbm, f0_ref, f1_ref, sem):
    i = pl.program_id(0)
    tm = h_ref.shape[0]

    def run(f_ref):
        def wait_block_copies():
            for _ in range(2):
                pltpu.make_async_copy(f_ref, xs_hbm.at[pl.ds(0, tm), :], sem).wait()

        f_ref[...] = _norm_mod(h_ref[...], g_ref[...], sh_ref[...], sc_ref[...])

        @pl.when(i > 0)
        def _():
            wait_block_copies()

        base = i * (2 * tm)
        for r in range(tm):
            for k in range(2):
                dst = pos_ref[base + 2 * r + k]
                pltpu.make_async_copy(f_ref.at[pl.ds(r, 1), :], xs_hbm.at[pl.ds(dst, 1), :], sem).start(priority=k)

        @pl.when(i == pl.num_programs(0) - 1)
        def _():
            wait_block_copies()

    @pl.when(i % 2 == 0)
    def _():
        run(f0_ref)

    @pl.when(i % 2 == 1)
    def _():
        run(f1_ref)


def _dispatch(h, gain, mods, pos, *, n_rows, n_lat, seq, sh_idx, sc_idx, tm=DISPATCH_ROWS):
    d = D_MODEL

    def mod_map(idx):
        return lambda i, pos_ref: (_mod_row(i * tm, n_lat, seq), 0, idx)

    grid_spec = pltpu.PrefetchScalarGridSpec(
        num_scalar_prefetch=1,
        grid=(n_rows // tm,),
        in_specs=[
            pl.BlockSpec((tm, d), lambda i, pos_ref: (i, 0)),
            pl.BlockSpec((1, d), lambda i, pos_ref: (0, 0)),
            pl.BlockSpec((None, 1, d), mod_map(sh_idx)),
            pl.BlockSpec((None, 1, d), mod_map(sc_idx)),
        ],
        out_specs=pl.BlockSpec(memory_space=pl.ANY),
        scratch_shapes=[pltpu.VMEM((tm, d), F32), pltpu.VMEM((tm, d), F32), pltpu.SemaphoreType.DMA],
    )
    return pl.pallas_call(
        _dispatch_kernel,
        out_shape=jax.ShapeDtypeStruct((2 * n_rows, d), F32),
        grid_spec=grid_spec,
        compiler_params=_params(("arbitrary",)),
        name="dispatch",
    )(pos, h, gain.reshape(1, d), mods, mods)


def _moe_plan(ids, cnt, n_tok):
    n_items = N_EXPERTS + (2 * n_tok) // MOE_ITEM_CAP
    experts = jnp.arange(N_EXPERTS, dtype=I32)
    counts = cnt[0, :N_EXPERTS].astype(I32)
    starts = jnp.cumsum(counts) - counts
    e_slot, rank = ids[:, 0:2], ids[:, 2:4]
    start_of_slot = jnp.sum(jnp.where(e_slot[:, :, None] == experts, starts, 0), axis=-1)
    pos = (start_of_slot + rank).reshape(-1).astype(I32)
    pos = jnp.concatenate([pos, jnp.zeros((2 * COMBINE_ROWS,), I32)])
    n_sb = (counts + MOE_ITEM_CAP - 1) // MOE_ITEM_CAP
    cum = jnp.cumsum(n_sb)
    total = cum[-1]
    it = jnp.arange(n_items, dtype=I32)
    e_i = jnp.minimum(jnp.sum((it[:, None] >= cum[None, :]).astype(I32), axis=1), N_EXPERTS - 1)
    sb = it - (cum[e_i] - n_sb[e_i])
    valid = it < total
    last_e = e_i[jnp.maximum(total - 1, 0)]
    item_e = jnp.where(valid, e_i, last_e).astype(I32)
    item_start = jnp.where(valid, starts[e_i] + sb * MOE_ITEM_CAP, 0).astype(I32)
    item_n = jnp.where(valid, jnp.minimum(MOE_ITEM_CAP, counts[e_i] - sb * MOE_ITEM_CAP), 0).astype(I32)
    front, back = jnp.zeros((MOE_ITEM_SHIFT,), I32), jnp.zeros((1,), I32)
    pad_items = lambda a: jnp.concatenate([front, a, back])
    return (pad_items(item_e), pad_items(item_start), pad_items(item_n)), pos


def _moe_kernel(ie_ref, ist_ref, in_ref,
                xs_hbm, wg_ref, wu_ref, wd_ref, ys_hbm,
                xf_ref, xb_ref, y_ref, gsem, ssem):
    i, j = pl.program_id(0), pl.program_id(1)
    sh = MOE_ITEM_SHIFT
    n_prev, n, n_next = in_ref[i + sh - 1], in_ref[i + sh], in_ref[i + sh + 1]
    start_prev, start, start_next = ist_ref[i + sh - 1], ist_ref[i + sh], ist_ref[i + sh + 1]
    par = i % 2

    def rows_in(hrow, vrow, size):
        return pltpu.make_async_copy(xs_hbm.at[pl.ds(hrow, size), :], xf_ref.at[pl.ds(vrow, size), :], gsem)

    def rows_out(hrow, vrow, size):
        return pltpu.make_async_copy(y_ref.at[par, pl.ds(vrow, size), :], ys_hbm.at[pl.ds(hrow, size), :], ssem)

    def row_groups(make, base, count, act):
        shift = base % SUBLANES
        head = jnp.minimum((SUBLANES - shift) % SUBLANES, count)
        body = ((count - head) // SUBLANES) * SUBLANES
        for t in range(SUBLANES - 1):
            @pl.when(t < head)
            def _():
                getattr(make(base + t, shift + t, 1), act)()

        for b in reversed(range(SUBLANES.bit_length() - 1, MOE_ITEM_ROWS.bit_length())):
            @pl.when(((body >> b) & 1) == 1)
            def _():
                off = head + ((body >> (b + 1)) << (b + 1))
                getattr(make(pl.multiple_of(base + off, SUBLANES), pl.multiple_of(shift + off, SUBLANES), 1 << b),
                        act)()

        done = head + body
        for t in range(SUBLANES - 1):
            @pl.when(t < count - done)
            def _():
                getattr(make(base + done + t, shift + done + t, 1), act)()

    @pl.when(jnp.logical_and(i == 0, j == 0))
    def _():
        xf_ref[...] = jnp.zeros_like(xf_ref)
        y_ref[...] = jnp.zeros_like(y_ref)

    @pl.when(j == 0)
    def _():
        @pl.when(i == 0)
        def _():
            row_groups(rows_in, start, n, "start")

        @pl.when(n > 0)
        def _():
            row_groups(rows_in, start, n, "wait")
            xb_ref[...] = xf_ref[...].astype(BF16)
            row_groups(rows_in, start_next, n_next, "start")

    def compute(m):
        x = xb_ref[0:m, :]
        g = jnp.dot(x, wg_ref[...].astype(BF16), preferred_element_type=F32)
        u = jnp.dot(x, wu_ref[...].astype(BF16), preferred_element_type=F32)
        hmid = (g * _sigmoid(g) * u).astype(BF16)
        y = jnp.dot(hmid, wd_ref[...].astype(BF16), preferred_element_type=F32)
        acc = y_ref.at[par]
        acc[0:m, :] = y + jnp.where(j == 0, 0.0, acc[0:m, :])

    used = jnp.where(n > 0, start % SUBLANES + n, 0)
    n_pad = (used + MOE_ROW_PAD - 1) // MOE_ROW_PAD
    for k in range(1, MOE_ITEM_ROWS // MOE_ROW_PAD + 1):
        @pl.when(n_pad == k)
        def _():
            compute(k * MOE_ROW_PAD)

    @pl.when(j == pl.num_programs(1) - 1)
    def _():
        row_groups(rows_out, start_prev, n_prev, "wait")

        row_groups(rows_out, start, n, "start")

        @pl.when(i == pl.num_programs(0) - 1)
        def _():
            row_groups(rows_out, start, n, "wait")


def _moe(xs, items, w_gu, w_down, layer):
    item_e, item_start, item_n = items
    d, hid, hk = D_MODEL, EXPERT_HIDDEN, MOE_HIDDEN_BLOCK
    nj = hid // hk
    sh = MOE_ITEM_SHIFT
    n_items = item_e.shape[0] - sh - 1

    def chunk(j, nn, i):
        return jnp.where(nn[i + sh] > 0, j, nj - 1)

    grid_spec = pltpu.PrefetchScalarGridSpec(
        num_scalar_prefetch=3,
        grid=(n_items, nj),
        in_specs=[
            pl.BlockSpec(memory_space=pl.ANY),
            pl.BlockSpec((None, None, d, hk), lambda i, j, ie, ist, nn: (layer, ie[i + sh], 0, chunk(j, nn, i))),
            pl.BlockSpec((None, None, d, hk), lambda i, j, ie, ist, nn: (layer, ie[i + sh], 0, nj + chunk(j, nn, i))),
            pl.BlockSpec((None, None, hk, d), lambda i, j, ie, ist, nn: (layer, ie[i + sh], chunk(j, nn, i), 0)),
        ],
        out_specs=pl.BlockSpec(memory_space=pl.ANY),
        scratch_shapes=[
            pltpu.VMEM((MOE_ITEM_ROWS, d), F32),
            pltpu.VMEM((MOE_ITEM_ROWS, d), BF16),
            pltpu.VMEM((2, MOE_ITEM_ROWS, d), F32),
            pltpu.SemaphoreType.DMA,
            pltpu.SemaphoreType.DMA,
        ],
    )
    return pl.pallas_call(
        _moe_kernel,
        out_shape=jax.ShapeDtypeStruct(xs.shape, F32),
        grid_spec=grid_spec,
        compiler_params=_params(("arbitrary", "arbitrary"), MOE_VMEM_LIMIT),
        name="moe_experts",
    )(item_e, item_start, item_n, xs, w_gu, w_gu, w_down)


def _combine_kernel(pos_ref, h_ref, wt_ref, g2_ref, fg_ref, ys_hbm, o_ref, ybuf, sem, *, final):
    i = pl.program_id(0)
    tm = h_ref.shape[0]
    par = i % 2

    def fetch(step, half):
        base = step * (2 * tm)
        for r in range(tm):
            for k in range(2):
                src = pos_ref[base + 2 * r + k]
                pltpu.make_async_copy(ys_hbm.at[pl.ds(src, 1), :], ybuf.at[half, k, pl.ds(r, 1), :],
                                      sem.at[half]).start(priority=k)

    @pl.when(i == 0)
    def _():
        fetch(0, 0)

    @pl.when(i < pl.num_programs(0) - 1)
    def _():
        fetch(i + 1, 1 - par)

    for k in range(2):
        pltpu.make_async_copy(ys_hbm.at[pl.ds(0, tm), :], ybuf.at[par, k], sem.at[par]).wait()
    wt = wt_ref[...]
    moe = wt[:, 0:1] * ybuf[par, 0] + wt[:, 1:2] * ybuf[par, 1]
    h = h_ref[...] + g2_ref[...] * moe
    if final:
        ms = jnp.mean(h * h, axis=-1, keepdims=True)
        h = h * lax.rsqrt(ms + NORM_EPS) * fg_ref[...]
    o_ref[...] = h


def _combine(h, ys, pos, wts, mods, final_g, *, n_rows, n_lat, seq, gate_idx, final, tm=COMBINE_ROWS):
    d = D_MODEL
    kernel = functools.partial(_combine_kernel, final=final)
    grid_spec = pltpu.PrefetchScalarGridSpec(
        num_scalar_prefetch=1,
        grid=(n_rows // tm,),
        in_specs=[
            pl.BlockSpec((tm, d), lambda i, pos_ref: (i, 0)),
            pl.BlockSpec((tm, LANES), lambda i, pos_ref: (i, 0)),
            pl.BlockSpec((None, 1, d), lambda i, pos_ref: (_mod_row(i * tm, n_lat, seq), 0, gate_idx)),
            pl.BlockSpec((1, d), lambda i, pos_ref: (0, 0)),
            pl.BlockSpec(memory_space=pl.ANY),
        ],
        out_specs=pl.BlockSpec((tm, d), lambda i, pos_ref: (i, 0)),
        scratch_shapes=[pltpu.VMEM((2, 2, tm, d), F32), pltpu.SemaphoreType.DMA((2,))],
    )
    return pl.pallas_call(
        kernel,
        out_shape=jax.ShapeDtypeStruct((n_rows, d), F32),
        grid_spec=grid_spec,
        compiler_params=_params(("arbitrary",)),
        name="combine",
    )(pos, h, wts, mods, final_g.reshape(1, d), ys)


def kernel(x, c, ctx, c_ctx, ada_w, ada_b, norm1_g, w_in, b_gate, diff_lambda, subln_g, w_attn_out, sc_conv_w,
           w_sc_out, cf_dw_w, cf_dw_b, cf_ln_g, cf_ln_b, w_cf_out, w_mix, norm2_g, router_g_w, router_g_b,
           router_e_w, router_e_b, exp_w_gu, exp_w_down, final_g):
    batch, seq, d = x.shape
    n_ctx = ctx.shape[1]
    depth = ada_w.shape[0]
    n_lat = batch * seq
    n_all = n_lat + batch * n_ctx
    assert d == D_MODEL and batch == MOD_ROWS // 2 and w_in.shape[2] == C_TOT

    cond = jnp.concatenate([c, c_ctx[None, :], jnp.zeros((MOD_ROWS - batch - 1, d), F32)], axis=0)
    mods_all = _adaln(cond, ada_w, ada_b)
    cos, sin = _rope_tables(seq)
    rows = (x.reshape(n_lat, d), ctx.reshape(batch * n_ctx, d))
    geo = dict(n_lat=n_lat, seq=seq)

    for layer in range(depth):
        last = layer == depth - 1
        lam_init = 0.8 - 0.6 * math.exp(-0.3 * layer)
        mods = mods_all[layer].reshape(MOD_ROWS, 1, 6 * d)
        n_rows = n_lat if last else n_all

        p = _in_proj(rows, norm1_g[layer], mods, w_in, layer, row0=0, n_rows=n_rows, col0=0, n_cols=C_TOT,
                     sh_idx=0, sc_idx=1, tm=1024, **geo)
        if last:
            kv_ctx = _in_proj(rows, norm1_g[layer], mods, w_in, layer, row0=n_lat, n_rows=batch * n_ctx, col0=OFF_K,
                              n_cols=OFF_SC - OFF_K, sh_idx=0, sc_idx=1, tm=n_ctx, **geo)
            kc_blk, vc_blk, ctx_rb0 = 0, QK_W // HEAD_W, 0
        else:
            kv_ctx = p
            kc_blk, vc_blk, ctx_rb0 = OFF_K // HEAD_W, OFF_V // HEAD_W, n_lat // n_ctx
        attn = _attn_lat(p, kv_ctx, kc_blk, vc_blk, ctx_rb0, cos, sin, diff_lambda[layer], subln_g[layer],
                         batch=batch, seq=seq, n_ctx=n_ctx, lam_init=lam_init)
        if not last:
            attn = (attn, _attn_ctx(p, n_lat // n_ctx, diff_lambda[layer], subln_g[layer],
                                    batch=batch, n_ctx=n_ctx, lam_init=lam_init))
        else:
            attn = (attn, attn)
        sc, cf = _convs(p, sc_conv_w[layer], cf_dw_w[layer], cf_dw_b[layer], cf_ln_g[layer], cf_ln_b[layer],
                        n_rows=n_rows, **geo)
        h_mix = _merge(attn, sc, cf, p, b_gate[layer].reshape(1, 3 * d),
                       w_attn_out[layer].astype(BF16), w_sc_out[layer].astype(BF16),
                       w_cf_out[layer].astype(BF16), w_mix[layer].astype(BF16), rows, mods,
                       n_rows=n_rows, gate_idx=2, **geo)

        rw = jnp.concatenate([router_g_w[layer], router_e_w[layer],
                              jnp.zeros((d, LANES - N_GROUPS - N_EXPERTS), F32)], axis=1)
        rb = jnp.concatenate([router_g_b[layer], router_e_b[layer],
                              jnp.zeros((LANES - N_GROUPS - N_EXPERTS,), F32)]).reshape(1, LANES)
        ids, wts, cnt = _router(h_mix, norm2_g[layer], mods, rw, rb, n_rows=n_rows, sh_idx=3, sc_idx=4, **geo)
        items, pos = _moe_plan(ids, cnt, n_rows)
        xs = _dispatch(h_mix, norm2_g[layer], mods, pos, n_rows=n_rows, sh_idx=3, sc_idx=4, **geo)
        ys = _moe(xs, items, exp_w_gu, exp_w_down, layer)
        h = _combine(h_mix, ys, pos, wts, mods, final_g, n_rows=n_rows, gate_idx=5, final=last, **geo)
        rows = (h, h)

    return h.reshape(batch, seq, d)
```

```python
import functools
import math

import jax
import jax.numpy as jnp
from jax import lax
from jax.experimental import pallas as pl
from jax.experimental.pallas import tpu as pltpu

F32 = jnp.float32
BF16 = jnp.bfloat16
I32 = jnp.int32

D_MODEL = 2048
GRID_W = 64
NORM_EPS = 1e-6
N_HEADS = 8
HEAD_DIM = 64
HEAD_W = 2 * HEAD_DIM
QK_W = N_HEADS * HEAD_W
ATTN_W = N_HEADS * HEAD_W
ROPE_BASE = 10000.0
ROPE_FREQS = HEAD_DIM // 4
SUBLN_EPS = 1e-5
QK_SCALE = HEAD_DIM ** -0.5 * math.log2(math.e)
SC_W = D_MODEL // 4
CF_W = D_MODEL // 4
SC_TAPS = 3
CF_TAPS = 31
CF_LN_EPS = 1e-5
OFF_Q = 0
OFF_K = OFF_Q + QK_W
OFF_V = OFF_K + QK_W
OFF_SC = OFF_V + ATTN_W
OFF_CF = OFF_SC + 3 * SC_W
OFF_GATE = OFF_CF + 2 * CF_W
C_TOT = OFF_GATE + 3 * D_MODEL
N_GROUPS = 4
EXPERTS_PER_GROUP = 8
N_EXPERTS = N_GROUPS * EXPERTS_PER_GROUP
EXPERT_HIDDEN = D_MODEL // 2

LANES = 128
SUBLANES = 8
MOD_ROWS = 8
MERGE_GATE_TILES = 4
CONV_HALO = 16
CONV_ROWS = 256
CONV_CHUNK = 32
MOE_ITEM_ROWS = 1024
MOE_ITEM_CAP = MOE_ITEM_ROWS - 8
MOE_ROW_PAD = 128
MOE_HIDDEN_BLOCK = 512
MOE_ITEM_SHIFT = 1
DISPATCH_ROWS = 512
COMBINE_ROWS = 256
VMEM_LIMIT = 56 * 1024 * 1024
MOE_VMEM_LIMIT = 60 * 1024 * 1024


def _params(sem, vmem=VMEM_LIMIT):
    return pltpu.CompilerParams(dimension_semantics=sem, vmem_limit_bytes=vmem)


def _sigmoid(x):
    return 1.0 / (1.0 + jnp.exp(-x))


def _adaln_kernel(s_ref, w_ref, b_ref, o_ref):
    s = s_ref[...]
    s = s * _sigmoid(s)
    s_hi = s.astype(BF16)
    w = w_ref[...]
    w_hi = w.astype(BF16)
    w_lo = (w - w_hi.astype(F32)).astype(BF16)
    lhs = jnp.concatenate([s_hi.astype(F32), s - s_hi.astype(F32)], axis=0).astype(BF16)
    r = jnp.dot(lhs, w_hi, preferred_element_type=F32)
    r2 = jnp.dot(s_hi, w_lo, preferred_element_type=F32)
    o_ref[...] = r[:MOD_ROWS] + r[MOD_ROWS:] + r2 + b_ref[...]


def _adaln(cond, ada_w, ada_b):
    n_layers, d, n = ada_w.shape
    tn = 1024
    return pl.pallas_call(
        _adaln_kernel,
        out_shape=jax.ShapeDtypeStruct((n_layers, MOD_ROWS, n), F32),
        grid=(n_layers, n // tn),
        in_specs=[
            pl.BlockSpec((MOD_ROWS, d), lambda l, j: (0, 0)),
            pl.BlockSpec((None, d, tn), lambda l, j: (l, 0, j)),
            pl.BlockSpec((None, 1, tn), lambda l, j: (l, 0, j)),
        ],
        out_specs=pl.BlockSpec((None, MOD_ROWS, tn), lambda l, j: (l, 0, j)),
        compiler_params=_params(("arbitrary", "arbitrary")),
        name="adaln",
    )(cond, ada_w, ada_b.reshape(n_layers, 1, n))


def _mod_row(row0, n_lat, seq):
    return jnp.where(row0 < n_lat, row0 // seq, MOD_ROWS // 2)


def _in_proj_kernel(xa_ref, xb_ref, g_ref, sh_ref, sc_ref, w_ref, o_ref, u_ref, *, a_blocks):
    first_col = pl.program_id(1) == 0
    from_a = pl.program_id(0) < a_blocks

    def normed(x_ref):
        x = x_ref[...]
        ms = jnp.mean(x * x, axis=-1, keepdims=True)
        y = x * lax.rsqrt(ms + NORM_EPS) * g_ref[...]
        u_ref[...] = (y * (1.0 + sc_ref[...]) + sh_ref[...]).astype(BF16)

    @pl.when(jnp.logical_and(first_col, from_a))
    def _():
        normed(xa_ref)

    @pl.when(jnp.logical_and(first_col, jnp.logical_not(from_a)))
    def _():
        normed(xb_ref)

    o_ref[...] = jnp.dot(u_ref[...], w_ref[...].astype(BF16), preferred_element_type=F32).astype(o_ref.dtype)


def _in_proj(rows, gain, mods, w_in, layer, *, row0, n_rows, col0, n_cols, n_lat, seq, sh_idx, sc_idx, tm, tn=512):
    xa, xb = rows
    d = xa.shape[1]
    rb0, cb0 = row0 // tm, col0 // tn
    a_blocks = xa.shape[0] // tm - rb0

    def mod_map(idx):
        return lambda i, j: (_mod_row((i + rb0) * tm, n_lat, seq), 0, idx)

    return pl.pallas_call(
        functools.partial(_in_proj_kernel, a_blocks=a_blocks),
        out_shape=jax.ShapeDtypeStruct((n_rows, n_cols), BF16),
        grid=(n_rows // tm, n_cols // tn),
        in_specs=[
            pl.BlockSpec((tm, d), lambda i, j: (jnp.minimum(i, a_blocks - 1) + rb0, 0)),
            pl.BlockSpec((tm, d), lambda i, j: (jnp.maximum(i - a_blocks, 0), 0)),
            pl.BlockSpec((1, d), lambda i, j: (0, 0)),
            pl.BlockSpec((None, 1, d), mod_map(sh_idx)),
            pl.BlockSpec((None, 1, d), mod_map(sc_idx)),
            pl.BlockSpec((None, d, tn), lambda i, j: (layer, 0, j + cb0)),
        ],
        out_specs=pl.BlockSpec((tm, tn), lambda i, j: (i, j)),
        scratch_shapes=[pltpu.VMEM((tm, d), BF16)],
        compiler_params=_params(("arbitrary", "arbitrary")),
        name="in_proj",
    )(xa, xb, gain.reshape(1, d), mods, mods, w_in)


def _rope_tables(n_tokens):
    t = jnp.arange(n_tokens, dtype=I32)
    pos = jnp.stack([t // GRID_W, t % GRID_W], axis=-1).astype(F32)
    inv_freq = ROPE_BASE ** (-jnp.arange(ROPE_FREQS, dtype=F32) / ROPE_FREQS)
    ang = pos[:, :, None] * inv_freq
    cos, sin = jnp.cos(ang), jnp.sin(ang)
    c = jnp.stack([cos, cos], axis=2).reshape(n_tokens, HEAD_DIM)
    s = jnp.stack([-sin, sin], axis=2).reshape(n_tokens, HEAD_DIM)
    return jnp.tile(c, (1, 2)), jnp.tile(s, (1, 2))


def _rope(x, c, s):
    lane = lax.broadcasted_iota(I32, x.shape, 1)
    first_half = (lane % (2 * ROPE_FREQS)) < ROPE_FREQS
    partner = jnp.where(first_half, pltpu.roll(x, LANES - ROPE_FREQS, 1), pltpu.roll(x, ROPE_FREQS, 1))
    return x * c + partner * s


def _diff_lambda(lam_ref, lam_init):
    lv = lam_ref[...]
    a = jnp.sum(lv[0:1] * lv[1:2], axis=-1, keepdims=True)
    b = jnp.sum(lv[2:3] * lv[3:4], axis=-1, keepdims=True)
    return jnp.exp(a) - jnp.exp(b) + lam_init


def _attend(q, k_all, v_all, lam, subg, lam_init):
    tq = q.shape[0]
    lane = lax.broadcasted_iota(I32, q.shape, 1)
    q0 = jnp.where(lane < HEAD_DIM, q, 0.0).astype(BF16)
    q1 = jnp.where(lane >= HEAD_DIM, q, 0.0).astype(BF16)
    qq = jnp.concatenate([q0, q1], axis=0)
    s = lax.dot_general(qq, k_all, (((1,), (1,)), ((), ())), preferred_element_type=F32)
    m = jnp.max(s, axis=-1, keepdims=True)
    e = jnp.exp2(s - m)
    l = jnp.sum(e, axis=-1, keepdims=True)
    a = e[:tq] - e[tq:] * (lam * l[:tq] / l[tq:])
    o = jnp.dot(a.astype(BF16), v_all, preferred_element_type=F32) / l[:tq]
    ms = jnp.mean(o * o, axis=-1, keepdims=True)
    return o * lax.rsqrt(ms + SUBLN_EPS) * subg * (1.0 - lam_init)


def _attn_lat_kernel(q_ref, kl_ref, vl_ref, kc_ref, vc_ref, cq_ref, sq_ref, ck_ref, sk_ref, lam_ref, g_ref,
                     o_ref, k_all, v_aug, s0_ref, m0_ref, s1_ref, m1_ref, *, n_ctx, lam_init, n_blocks):
    i = pl.program_id(2)
    tq = q_ref.shape[0]

    @pl.when(i == 0)
    def _():
        k_all[0:n_ctx, :] = kc_ref[...]
        k_all[n_ctx:, :] = _rope(kl_ref[...].astype(F32), ck_ref[...], sk_ref[...]).astype(BF16)
        v_aug[0:n_ctx, 0:HEAD_W] = vc_ref[...]
        v_aug[n_ctx:, 0:HEAD_W] = vl_ref[...]
        v_aug[:, HEAD_W:2 * HEAD_W] = jnp.ones((k_all.shape[0], HEAD_W), BF16)

    lam = _diff_lambda(lam_ref, lam_init)

    def stage_a(sa_ref, ma_ref):
        q = _rope(q_ref[...].astype(F32), cq_ref[...], sq_ref[...]) * QK_SCALE
        lane = lax.broadcasted_iota(I32, q.shape, 1)
        q0 = jnp.where(lane < HEAD_DIM, q, 0.0).astype(BF16)
        q1 = jnp.where(lane >= HEAD_DIM, q, 0.0).astype(BF16)
        qq = jnp.concatenate([q0, q1], axis=0)
        s = lax.dot_general(qq, k_all[...], (((1,), (1,)), ((), ())), preferred_element_type=F32)
        sa_ref[...] = s
        ma_ref[...] = jnp.broadcast_to(jnp.max(s, axis=-1, keepdims=True), ma_ref.shape)

    def stage_b(sb_ref, mb_ref):
        mb = jnp.concatenate([mb_ref[...]] * (sb_ref.shape[1] // LANES), axis=1)
        e = jnp.exp2(sb_ref[...] - mb).astype(BF16)
        oa = jnp.dot(e, v_aug[...], preferred_element_type=F32)
        o = oa[:tq, 0:HEAD_W] / oa[:tq, HEAD_W:] - oa[tq:, 0:HEAD_W] * (lam / oa[tq:, HEAD_W:])
        ms = jnp.mean(o * o, axis=-1, keepdims=True)
        o_ref[...] = (o * lax.rsqrt(ms + SUBLN_EPS) * g_ref[...] * (1.0 - lam_init)).astype(o_ref.dtype)

    bufs = ((s0_ref, m0_ref), (s1_ref, m1_ref))

    @pl.when(i == 0)
    def _():
        stage_a(*bufs[0])

    for par in range(2):
        @pl.when(jnp.logical_and(jnp.logical_and(i > 0, i < n_blocks), i % 2 == par))
        def _():
            stage_a(*bufs[par])
            stage_b(*bufs[1 - par])

    @pl.when(i == n_blocks)
    def _():
        stage_b(*bufs[(n_blocks - 1) % 2])


def _attn_ctx_kernel(q_ref, k_ref, v_ref, lam_ref, g_ref, o_ref, *, lam_init):
    q = q_ref[...].astype(F32) * QK_SCALE
    lam = _diff_lambda(lam_ref, lam_init)
    o_ref[...] = _attend(q, k_ref[...], v_ref[...], lam, g_ref[...], lam_init).astype(o_ref.dtype)


def _attn_lat(p, kv_ctx, kc_blk, vc_blk, ctx_rb0, cos, sin, lam4, subg, *, batch, seq, n_ctx, lam_init, tq=512):
    nq = seq // tq
    n_keys = n_ctx + seq
    hb = lambda off: off // HEAD_W
    qblk = lambda i: jnp.minimum(i, nq - 1)
    kernel = functools.partial(_attn_lat_kernel, n_ctx=n_ctx, lam_init=lam_init, n_blocks=nq)
    return pl.pallas_call(
        kernel,
        out_shape=jax.ShapeDtypeStruct((batch * seq, ATTN_W), BF16),
        grid=(batch, N_HEADS, nq + 1),
        in_specs=[
            pl.BlockSpec((tq, HEAD_W), lambda b, h, i: (b * nq + qblk(i), hb(OFF_Q) + h)),
            pl.BlockSpec((seq, HEAD_W), lambda b, h, i: (b, hb(OFF_K) + h)),
            pl.BlockSpec((seq, HEAD_W), lambda b, h, i: (b, hb(OFF_V) + h)),
            pl.BlockSpec((n_ctx, HEAD_W), lambda b, h, i: (ctx_rb0 + b, kc_blk + h)),
            pl.BlockSpec((n_ctx, HEAD_W), lambda b, h, i: (ctx_rb0 + b, vc_blk + h)),
            pl.BlockSpec((tq, HEAD_W), lambda b, h, i: (qblk(i), 0)),
            pl.BlockSpec((tq, HEAD_W), lambda b, h, i: (qblk(i), 0)),
            pl.BlockSpec((seq, HEAD_W), lambda b, h, i: (0, 0)),
            pl.BlockSpec((seq, HEAD_W), lambda b, h, i: (0, 0)),
            pl.BlockSpec((4, HEAD_DIM), lambda b, h, i: (0, 0)),
            pl.BlockSpec((1, HEAD_W), lambda b, h, i: (0, 0)),
        ],
        out_specs=pl.BlockSpec((tq, HEAD_W), lambda b, h, i: (b * nq + jnp.maximum(i - 1, 0), h)),
        scratch_shapes=[pltpu.VMEM((n_keys, HEAD_W), BF16), pltpu.VMEM((n_keys, 2 * HEAD_W), BF16),
                        pltpu.VMEM((2 * tq, n_keys), F32), pltpu.VMEM((2 * tq, LANES), F32),
                        pltpu.VMEM((2 * tq, n_keys), F32), pltpu.VMEM((2 * tq, LANES), F32)],
        compiler_params=_params(("arbitrary", "arbitrary", "arbitrary")),
        name="attn_lat",
    )(p, p, p, kv_ctx, kv_ctx, cos, sin, cos, sin, lam4, subg.reshape(1, HEAD_W))


def _attn_ctx(p, ctx_rb0, lam4, subg, *, batch, n_ctx, lam_init):
    hb = lambda off: off // HEAD_W
    kernel = functools.partial(_attn_ctx_kernel, lam_init=lam_init)
    return pl.pallas_call(
        kernel,
        out_shape=jax.ShapeDtypeStruct((batch * n_ctx, ATTN_W), BF16),
        grid=(batch, N_HEADS),
        in_specs=[
            pl.BlockSpec((n_ctx, HEAD_W), lambda b, h: (ctx_rb0 + b, hb(OFF_Q) + h)),
            pl.BlockSpec((n_ctx, HEAD_W), lambda b, h: (ctx_rb0 + b, hb(OFF_K) + h)),
            pl.BlockSpec((n_ctx, HEAD_W), lambda b, h: (ctx_rb0 + b, hb(OFF_V) + h)),
            pl.BlockSpec((4, HEAD_DIM), lambda b, h: (0, 0)),
            pl.BlockSpec((1, HEAD_W), lambda b, h: (0, 0)),
        ],
        out_specs=pl.BlockSpec((n_ctx, HEAD_W), lambda b, h: (b, h)),
        compiler_params=_params(("arbitrary", "arbitrary")),
        name="attn_ctx",
    )(p, p, p, lam4, subg.reshape(1, HEAD_W))


def _conv_kernel(bg_ref, cg_ref, xi_ref, a_ref, g_ref,
                 cg_p, xi_p, a_p, g_p, cg_n, xi_n, a_n, g_n,
                 scw_ref, cfw_ref, cfb_ref, lng_ref, lnb_ref,
                 sco_ref, cfo_ref, pad_ref, *, lat_blocks, blocks_per_seq):
    i = pl.program_id(0)
    in_lat = i < lat_blocks
    pos = i % blocks_per_seq
    has_prev = jnp.logical_and(in_lat, pos != 0)
    has_next = jnp.logical_and(in_lat, pos != blocks_per_seq - 1)
    keep_prev = jnp.where(has_prev, 1.0, 0.0)
    keep_next = jnp.where(has_next, 1.0, 0.0)
    lo, hi = CONV_HALO, CONV_HALO + CONV_ROWS

    def fill(main, prev, nxt):
        pad_ref[0:lo, :] = prev * keep_prev
        pad_ref[lo:hi, :] = main
        pad_ref[hi:hi + CONV_HALO, :] = nxt * keep_next

    def f(ref):
        return ref[...].astype(F32)

    def glu(a, g):
        return a * _sigmoid(g)

    fill(f(cg_ref) * f(xi_ref), f(cg_p) * f(xi_p), f(cg_n) * f(xi_n))
    for c in range(CONV_ROWS // CONV_CHUNK):
        r0 = lo + c * CONV_CHUNK - SC_TAPS // 2
        acc = scw_ref[0:1, :] * pad_ref[r0:r0 + CONV_CHUNK, :]
        for k in range(1, SC_TAPS):
            acc = acc + scw_ref[k:k + 1, :] * pad_ref[r0 + k:r0 + k + CONV_CHUNK, :]
        rows = slice(c * CONV_CHUNK, (c + 1) * CONV_CHUNK)
        sco_ref[rows, :] = (bg_ref[rows, :].astype(F32) * acc).astype(sco_ref.dtype)

    fill(glu(f(a_ref), f(g_ref)), glu(f(a_p), f(g_p)), glu(f(a_n), f(g_n)))
    off = lo - CF_TAPS // 2
    win = CONV_CHUNK + SUBLANES
    for c in range(CONV_ROWS // CONV_CHUNK):
        base = c * CONV_CHUNK
        acc = None
        for b in range(SUBLANES):
            q = None
            for k in range(b, CF_TAPS, SUBLANES):
                term = cfw_ref[k:k + 1, :] * pad_ref[base + k - b:base + k - b + win, :]
                q = term if q is None else q + term
            part = q[off + b:off + b + CONV_CHUNK, :]
            acc = part if acc is None else acc + part
        z = acc + cfb_ref[...]
        mu = jnp.mean(z, axis=-1, keepdims=True)
        zc = z - mu
        var = jnp.mean(zc * zc, axis=-1, keepdims=True)
        y = zc * lax.rsqrt(var + CF_LN_EPS) * lng_ref[...] + lnb_ref[...]
        rows = slice(c * CONV_CHUNK, (c + 1) * CONV_CHUNK)
        cfo_ref[rows, :] = (y * _sigmoid(y)).astype(cfo_ref.dtype)


def _convs(p, sc_w, cf_w, cf_b, ln_g, ln_b, *, n_rows, n_lat, seq):
    nb = n_rows // CONV_ROWS
    halo_per_block = CONV_ROWS // CONV_HALO
    last_halo = n_rows // CONV_HALO - 1
    cb = lambda off: off // SC_W

    def main(off):
        return pl.BlockSpec((CONV_ROWS, SC_W), lambda i: (i, cb(off)))

    def prev(off):
        return pl.BlockSpec((CONV_HALO, SC_W), lambda i: (jnp.maximum(i * halo_per_block - 1, 0), cb(off)))

    def nxt(off):
        return pl.BlockSpec((CONV_HALO, SC_W), lambda i: (jnp.minimum((i + 1) * halo_per_block, last_halo), cb(off)))

    def vec(rows):
        return pl.BlockSpec((rows, SC_W), lambda i: (0, 0))

    o_bg, o_cg, o_xi, o_a, o_g = OFF_SC, OFF_SC + SC_W, OFF_SC + 2 * SC_W, OFF_CF, OFF_CF + CF_W
    kernel = functools.partial(_conv_kernel, lat_blocks=n_lat // CONV_ROWS, blocks_per_seq=seq // CONV_ROWS)
    return pl.pallas_call(
        kernel,
        out_shape=(jax.ShapeDtypeStruct((n_rows, SC_W), BF16), jax.ShapeDtypeStruct((n_rows, CF_W), BF16)),
        grid=(nb,),
        in_specs=[main(o_bg), main(o_cg), main(o_xi), main(o_a), main(o_g),
                  prev(o_cg), prev(o_xi), prev(o_a), prev(o_g),
                  nxt(o_cg), nxt(o_xi), nxt(o_a), nxt(o_g),
                  vec(SC_TAPS), vec(CF_TAPS), vec(1), vec(1), vec(1)],
        out_specs=(pl.BlockSpec((CONV_ROWS, SC_W), lambda i: (i, 0)),
                   pl.BlockSpec((CONV_ROWS, CF_W), lambda i: (i, 0))),
        scratch_shapes=[pltpu.VMEM((CONV_ROWS + 2 * CONV_HALO, SC_W), F32)],
        compiler_params=_params(("arbitrary",)),
        name="convs",
    )(p, p, p, p, p, p, p, p, p, p, p, p, p,
      sc_w, cf_w, cf_b.reshape(1, CF_W), ln_g.reshape(1, CF_W), ln_b.reshape(1, CF_W))


def _merge_kernel(*refs, a_blocks):
    n_gate = 3 * MERGE_GATE_TILES
    ata_ref, atb_ref, sc_ref, cf_ref = refs[:4]
    gate_refs = refs[4:4 + n_gate]
    bg_ref, wa_ref, wb_ref, wc_ref, wm_ref, ha_ref, hb_ref, g1_ref, o_ref = refs[4 + n_gate:]
    d = D_MODEL
    from_a = pl.program_id(0) < a_blocks
    attn = jnp.where(from_a, ata_ref[...], atb_ref[...])
    h = jnp.where(from_a, ha_ref[...], hb_ref[...])

    def gate(branch):
        tiles = gate_refs[branch * MERGE_GATE_TILES:(branch + 1) * MERGE_GATE_TILES]
        pre = jnp.concatenate([t[...] for t in tiles], axis=1).astype(F32)
        return _sigmoid(pre + bg_ref[:, branch * d:(branch + 1) * d])

    ya = jnp.dot(attn, wa_ref[...], preferred_element_type=F32)
    yb = jnp.dot(sc_ref[...], wb_ref[...], preferred_element_type=F32)
    yc = jnp.dot(cf_ref[...], wc_ref[...], preferred_element_type=F32)
    m = (gate(0) * ya + gate(1) * yb + gate(2) * yc).astype(BF16)
    o_ref[...] = h + g1_ref[...] * jnp.dot(m, wm_ref[...], preferred_element_type=F32)


def _merge(attn, sc, cf, p, b_gate, wa, wb, wc, wm, rows, mods, *, n_rows, n_lat, seq, gate_idx, tm=256):
    d = D_MODEL
    (ata, atb), (ha, hb) = attn, rows
    a_blocks = ha.shape[0] // tm
    first = lambda w: pl.BlockSpec((tm, w), lambda i: (jnp.minimum(i, a_blocks - 1), 0))
    rest = lambda w: pl.BlockSpec((tm, w), lambda i: (jnp.maximum(i - a_blocks, 0), 0))
    tg = d // MERGE_GATE_TILES
    gb0 = OFF_GATE // tg
    gates = [pl.BlockSpec((tm, tg), functools.partial(lambda i, c: (i, c), c=gb0 + t))
             for t in range(3 * MERGE_GATE_TILES)]
    whole = lambda a: pl.BlockSpec(a.shape, lambda i: (0, 0))
    return pl.pallas_call(
        functools.partial(_merge_kernel, a_blocks=a_blocks),
        out_shape=jax.ShapeDtypeStruct((n_rows, d), F32),
        grid=(n_rows // tm,),
        in_specs=[
            first(ATTN_W), rest(ATTN_W),
            pl.BlockSpec((tm, SC_W), lambda i: (i, 0)),
            pl.BlockSpec((tm, CF_W), lambda i: (i, 0)),
            *gates,
            whole(b_gate), whole(wa), whole(wb), whole(wc), whole(wm),
            first(d), rest(d),
            pl.BlockSpec((None, 1, d), lambda i: (_mod_row(i * tm, n_lat, seq), 0, gate_idx)),
        ],
        out_specs=pl.BlockSpec((tm, d), lambda i: (i, 0)),
        compiler_params=_params(("arbitrary",)),
        name="merge",
    )(ata, atb, sc, cf, *([p] * (3 * MERGE_GATE_TILES)), b_gate, wa, wb, wc, wm, ha, hb, mods)


def _split2(x):
    hi = x.astype(BF16)
    return hi, (x - hi.astype(F32)).astype(BF16)


def _norm_mod(x, g, sh, sc):
    ms = jnp.mean(x * x, axis=-1, keepdims=True)
    return x * lax.rsqrt(ms + NORM_EPS) * g * (1.0 + sc) + sh


def _router_kernel(h_ref, g_ref, sh_ref, sc_ref, rw_ref, rb_ref, tri_ref, id_ref, wt_ref, cnt_ref, run_ref):
    @pl.when(pl.program_id(0) == 0)
    def _():
        run_ref[...] = jnp.zeros_like(run_ref)

    f = _norm_mod(h_ref[...], g_ref[...], sh_ref[...], sc_ref[...])

    f_hi, f_lo = _split2(f)
    w_hi, w_lo = _split2(rw_ref[...])
    dot = lambda a, b: jnp.dot(a, b, preferred_element_type=F32)
    both = dot(f_hi, jnp.concatenate([w_hi, w_lo], axis=1))
    logits = (dot(f_lo, w_hi) + both[:, LANES:] + both[:, :LANES]) + rb_ref[...]

    lane_i = lax.broadcasted_iota(I32, logits.shape, 1)
    lane = lane_i.astype(F32)
    neg = -jnp.inf
    big = float(LANES)
    lg = jnp.where(lane_i < N_GROUPS, logits, neg)
    mg = jnp.max(lg, axis=-1, keepdims=True)
    grp = jnp.min(jnp.where(lg == mg, lane, big), axis=-1, keepdims=True)
    p_grp = 1.0 / jnp.sum(jnp.exp(lg - mg), axis=-1, keepdims=True)

    e_lane = lane - N_GROUPS
    in_grp = jnp.logical_and(e_lane >= grp * EXPERTS_PER_GROUP, e_lane < (grp + 1.0) * EXPERTS_PER_GROUP)
    le = jnp.where(in_grp, logits, neg)
    m1 = jnp.max(le, axis=-1, keepdims=True)
    i1 = jnp.min(jnp.where(le == m1, lane, big), axis=-1, keepdims=True)
    le2 = jnp.where(lane == i1, neg, le)
    m2 = jnp.max(le2, axis=-1, keepdims=True)
    i2 = jnp.min(jnp.where(le2 == m2, lane, big), axis=-1, keepdims=True)
    e2 = jnp.exp(m2 - m1)
    w1 = p_grp / (1.0 + e2)
    w2 = p_grp * e2 / (1.0 + e2)
    e1, e2 = i1 - N_GROUPS, i2 - N_GROUPS
    hot1 = jnp.where(lane == e1, 1.0, 0.0)
    hot2 = jnp.where(lane == e2, 1.0, 0.0)
    tri = tri_ref[...]
    before1 = jnp.dot(tri, hot1.astype(BF16), preferred_element_type=F32)
    before2 = jnp.dot(tri, hot2.astype(BF16), preferred_element_type=F32)
    tot1 = jnp.sum(hot1, axis=0, keepdims=True)
    tot2 = jnp.sum(hot2, axis=0, keepdims=True)
    run = run_ref[...]
    r1 = jnp.sum(hot1 * (run + before1), axis=-1, keepdims=True)
    r2 = jnp.sum(hot2 * (run + tot1 + before2), axis=-1, keepdims=True)
    run = run + tot1 + tot2
    run_ref[...] = run
    cnt_ref[...] = run

    ids = jnp.where(lane_i == 0, e1, jnp.where(lane_i == 1, e2, jnp.where(lane_i == 2, r1,
                                                                        jnp.where(lane_i == 3, r2, 0.0))))
    id_ref[...] = ids.astype(I32)
    wt_ref[...] = jnp.where(lane_i == 0, w1, jnp.where(lane_i == 1, w2, 0.0))


def _router(h, gain, mods, rw, rb, *, n_rows, n_lat, seq, sh_idx, sc_idx, tm=1024):
    d = D_MODEL

    def mod_map(idx):
        return lambda i: (_mod_row(i * tm, n_lat, seq), 0, idx)

    return pl.pallas_call(
        _router_kernel,
        out_shape=(jax.ShapeDtypeStruct((n_rows, LANES), I32),
                   jax.ShapeDtypeStruct((n_rows, LANES), F32),
                   jax.ShapeDtypeStruct((1, LANES), F32)),
        grid=(n_rows // tm,),
        in_specs=[
            pl.BlockSpec((tm, d), lambda i: (i, 0)),
            pl.BlockSpec((1, d), lambda i: (0, 0)),
            pl.BlockSpec((None, 1, d), mod_map(sh_idx)),
            pl.BlockSpec((None, 1, d), mod_map(sc_idx)),
            pl.BlockSpec((d, LANES), lambda i: (0, 0)),
            pl.BlockSpec((1, LANES), lambda i: (0, 0)),
            pl.BlockSpec((tm, tm), lambda i: (0, 0)),
        ],
        out_specs=(pl.BlockSpec((tm, LANES), lambda i: (i, 0)),
                   pl.BlockSpec((tm, LANES), lambda i: (i, 0)),
                   pl.BlockSpec((1, LANES), lambda i: (0, 0))),
        scratch_shapes=[pltpu.VMEM((1, LANES), F32)],
        compiler_params=_params(("arbitrary",)),
        name="router",
    )(h, gain.reshape(1, d), mods, mods, rw, rb, jnp.tril(jnp.ones((tm, tm), BF16), -1))


def _dispatch_kernel(pos_ref, h_ref, g_ref, sh_ref, sc_ref, xs_hbm, f0_ref, f1_ref, sem):
    i = pl.program_id(0)
    tm = h_ref.shape[0]

    def run(f_ref):
        def wait_block_copies():
            for _ in range(2):
                pltpu.make_async_copy(f_ref, xs_hbm.at[pl.ds(0, tm), :], sem).wait()

        f_ref[...] = _norm_mod(h_ref[...], g_ref[...], sh_ref[...], sc_ref[...])

        @pl.when(i > 0)
        def _():
            wait_block_copies()

        base = i * (2 * tm)
        for r in range(tm):
            for k in range(2):
                dst = pos_ref[base + 2 * r + k]
                pltpu.make_async_copy(f_ref.at[pl.ds(r, 1), :], xs_hbm.at[pl.ds(dst, 1), :], sem).start(priority=k)

        @pl.when(i == pl.num_programs(0) - 1)
        def _():
            wait_block_copies()

    @pl.when(i % 2 == 0)
    def _():
        run(f0_ref)

    @pl.when(i % 2 == 1)
    def _():
        run(f1_ref)


def _dispatch(h, gain, mods, pos, *, n_rows, n_lat, seq, sh_idx, sc_idx, tm=DISPATCH_ROWS):
    d = D_MODEL

    def mod_map(idx):
        return lambda i, pos_ref: (_mod_row(i * tm, n_lat, seq), 0, idx)

    grid_spec = pltpu.PrefetchScalarGridSpec(
        num_scalar_prefetch=1,
        grid=(n_rows // tm,),
        in_specs=[
            pl.BlockSpec((tm, d), lambda i, pos_ref: (i, 0)),
            pl.BlockSpec((1, d), lambda i, pos_ref: (0, 0)),
            pl.BlockSpec((None, 1, d), mod_map(sh_idx)),
            pl.BlockSpec((None, 1, d), mod_map(sc_idx)),
        ],
        out_specs=pl.BlockSpec(memory_space=pl.ANY),
        scratch_shapes=[pltpu.VMEM((tm, d), F32), pltpu.VMEM((tm, d), F32), pltpu.SemaphoreType.DMA],
    )
    return pl.pallas_call(
        _dispatch_kernel,
        out_shape=jax.ShapeDtypeStruct((2 * n_rows, d), F32),
        grid_spec=grid_spec,
        compiler_params=_params(("arbitrary",)),
        name="dispatch",
    )(pos, h, gain.reshape(1, d), mods, mods)


def _moe_plan(ids, cnt, n_tok):
    n_items = N_EXPERTS + (2 * n_tok) // MOE_ITEM_CAP
    experts = jnp.arange(N_EXPERTS, dtype=I32)
    counts = cnt[0, :N_EXPERTS].astype(I32)
    starts = jnp.cumsum(counts) - counts
    e_slot, rank = ids[:, 0:2], ids[:, 2:4]
    start_of_slot = jnp.sum(jnp.where(e_slot[:, :, None] == experts, starts, 0), axis=-1)
    pos = (start_of_slot + rank).reshape(-1).astype(I32)
    pos = jnp.concatenate([pos, jnp.zeros((2 * COMBINE_ROWS,), I32)])
    n_sb = (counts + MOE_ITEM_CAP - 1) // MOE_ITEM_CAP
    cum = jnp.cumsum(n_sb)
    total = cum[-1]
    it = jnp.arange(n_items, dtype=I32)
    e_i = jnp.minimum(jnp.sum((it[:, None] >= cum[None, :]).astype(I32), axis=1), N_EXPERTS - 1)
    sb = it - (cum[e_i] - n_sb[e_i])
    valid = it < total
    last_e = e_i[jnp.maximum(total - 1, 0)]
    item_e = jnp.where(valid, e_i, last_e).astype(I32)
    item_start = jnp.where(valid, starts[e_i] + sb * MOE_ITEM_CAP, 0).astype(I32)
    item_n = jnp.where(valid, jnp.minimum(MOE_ITEM_CAP, counts[e_i] - sb * MOE_ITEM_CAP), 0).astype(I32)
    front, back = jnp.zeros((MOE_ITEM_SHIFT,), I32), jnp.zeros((1,), I32)
    pad_items = lambda a: jnp.concatenate([front, a, back])
    return (pad_items(item_e), pad_items(item_start), pad_items(item_n)), pos


def _moe_kernel(ie_ref, ist_ref, in_ref,
                xs_hbm, wg_ref, wu_ref, wd_ref, ys_hbm,
                xf_ref, xb_ref, y_ref, gsem, ssem):
    i, j = pl.program_id(0), pl.program_id(1)
    sh = MOE_ITEM_SHIFT
    n_prev, n, n_next = in_ref[i + sh - 1], in_ref[i + sh], in_ref[i + sh + 1]
    start_prev, start, start_next = ist_ref[i + sh - 1], ist_ref[i + sh], ist_ref[i + sh + 1]
    par = i % 2

    def rows_in(hrow, vrow, size):
        return pltpu.make_async_copy(xs_hbm.at[pl.ds(hrow, size), :], xf_ref.at[pl.ds(vrow, size), :], gsem)

    def rows_out(hrow, vrow, size):
        return pltpu.make_async_copy(y_ref.at[par, pl.ds(vrow, size), :], ys_hbm.at[pl.ds(hrow, size), :], ssem)

    def row_groups(make, base, count, act):
        shift = base % SUBLANES
        head = jnp.minimum((SUBLANES - shift) % SUBLANES, count)
        body = ((count - head) // SUBLANES) * SUBLANES
        for t in range(SUBLANES - 1):
            @pl.when(t < head)
            def _():
                getattr(make(base + t, shift + t, 1), act)()

        for b in reversed(range(SUBLANES.bit_length() - 1, MOE_ITEM_ROWS.bit_length())):
            @pl.when(((body >> b) & 1) == 1)
            def _():
                off = head + ((body >> (b + 1)) << (b + 1))
                getattr(make(pl.multiple_of(base + off, SUBLANES), pl.multiple_of(shift + off, SUBLANES), 1 << b),
                        act)()

        done = head + body
        for t in range(SUBLANES - 1):
            @pl.when(t < count - done)
            def _():
                getattr(make(base + done + t, shift + done + t, 1), act)()

    @pl.when(jnp.logical_and(i == 0, j == 0))
    def _():
        xf_ref[...] = jnp.zeros_like(xf_ref)
        y_ref[...] = jnp.zeros_like(y_ref)

    @pl.when(j == 0)
    def _():
        @pl.when(i == 0)
        def _():
            row_groups(rows_in, start, n, "start")

        @pl.when(n > 0)
        def _():
            row_groups(rows_in, start, n, "wait")
            xb_ref[...] = xf_ref[...].astype(BF16)
            row_groups(rows_in, start_next, n_next, "start")

    def compute(m):
        x = xb_ref[0:m, :]
        g = jnp.dot(x, wg_ref[...].astype(BF16), preferred_element_type=F32)
        u = jnp.dot(x, wu_ref[...].astype(BF16), preferred_element_type=F32)
        hmid = (g * _sigmoid(g) * u).astype(BF16)
        y = jnp.dot(hmid, wd_ref[...].astype(BF16), preferred_element_type=F32)
        acc = y_ref.at[par]
        acc[0:m, :] = y + jnp.where(j == 0, 0.0, acc[0:m, :])

    used = jnp.where(n > 0, start % SUBLANES + n, 0)
    n_pad = (used + MOE_ROW_PAD - 1) // MOE_ROW_PAD
    for k in range(1, MOE_ITEM_ROWS // MOE_ROW_PAD + 1):
        @pl.when(n_pad == k)
        def _():
            compute(k * MOE_ROW_PAD)

    @pl.when(j == pl.num_programs(1) - 1)
    def _():
        row_groups(rows_out, start_prev, n_prev, "wait")

        row_groups(rows_out, start, n, "start")

        @pl.when(i == pl.num_programs(0) - 1)
        def _():
            row_groups(rows_out, start, n, "wait")


def _moe(xs, items, w_gu, w_down, layer):
    item_e, item_start, item_n = items
    d, hid, hk = D_MODEL, EXPERT_HIDDEN, MOE_HIDDEN_BLOCK
    nj = hid // hk
    sh = MOE_ITEM_SHIFT
    n_items = item_e.shape[0] - sh - 1

    def chunk(j, nn, i):
        return jnp.where(nn[i + sh] > 0, j, nj - 1)

    grid_spec = pltpu.PrefetchScalarGridSpec(
        num_scalar_prefetch=3,
        grid=(n_items, nj),
        in_specs=[
            pl.BlockSpec(memory_space=pl.ANY),
            pl.BlockSpec((None, None, d, hk), lambda i, j, ie, ist, nn: (layer, ie[i + sh], 0, chunk(j, nn, i))),
            pl.BlockSpec((None, None, d, hk), lambda i, j, ie, ist, nn: (layer, ie[i + sh], 0, nj + chunk(j, nn, i))),
            pl.BlockSpec((None, None, hk, d), lambda i, j, ie, ist, nn: (layer, ie[i + sh], chunk(j, nn, i), 0)),
        ],
        out_specs=pl.BlockSpec(memory_space=pl.ANY),
        scratch_shapes=[
            pltpu.VMEM((MOE_ITEM_ROWS, d), F32),
            pltpu.VMEM((MOE_ITEM_ROWS, d), BF16),
            pltpu.VMEM((2, MOE_ITEM_ROWS, d), F32),
            pltpu.SemaphoreType.DMA,
            pltpu.SemaphoreType.DMA,
        ],
    )
    return pl.pallas_call(
        _moe_kernel,
        out_shape=jax.ShapeDtypeStruct(xs.shape, F32),
        grid_spec=grid_spec,
        compiler_params=_params(("arbitrary", "arbitrary"), MOE_VMEM_LIMIT),
        name="moe_experts",
    )(item_e, item_start, item_n, xs, w_gu, w_gu, w_down)


def _combine_kernel(pos_ref, h_ref, wt_ref, g2_ref, fg_ref, ys_hbm, o_ref, ybuf, sem, *, final):
    i = pl.program_id(0)
    tm = h_ref.shape[0]
    par = i % 2

    def fetch(step, half):
        base = step * (2 * tm)
        for r in range(tm):
            for k in range(2):
                src = pos_ref[base + 2 * r + k]
                pltpu.make_async_copy(ys_hbm.at[pl.ds(src, 1), :], ybuf.at[half, k, pl.ds(r, 1), :],
                                      sem.at[half]).start(priority=k)

    @pl.when(i == 0)
    def _():
        fetch(0, 0)

    @pl.when(i < pl.num_programs(0) - 1)
    def _():
        fetch(i + 1, 1 - par)

    for k in range(2):
        pltpu.make_async_copy(ys_hbm.at[pl.ds(0, tm), :], ybuf.at[par, k], sem.at[par]).wait()
    wt = wt_ref[...]
    moe = wt[:, 0:1] * ybuf[par, 0] + wt[:, 1:2] * ybuf[par, 1]
    h = h_ref[...] + g2_ref[...] * moe
    if final:
        ms = jnp.mean(h * h, axis=-1, keepdims=True)
        h = h * lax.rsqrt(ms + NORM_EPS) * fg_ref[...]
    o_ref[...] = h


def _combine(h, ys, pos, wts, mods, final_g, *, n_rows, n_lat, seq, gate_idx, final, tm=COMBINE_ROWS):
    d = D_MODEL
    kernel = functools.partial(_combine_kernel, final=final)
    grid_spec = pltpu.PrefetchScalarGridSpec(
        num_scalar_prefetch=1,
        grid=(n_rows // tm,),
        in_specs=[
            pl.BlockSpec((tm, d), lambda i, pos_ref: (i, 0)),
            pl.BlockSpec((tm, LANES), lambda i, pos_ref: (i, 0)),
            pl.BlockSpec((None, 1, d), lambda i, pos_ref: (_mod_row(i * tm, n_lat, seq), 0, gate_idx)),
            pl.BlockSpec((1, d), lambda i, pos_ref: (0, 0)),
            pl.BlockSpec(memory_space=pl.ANY),
        ],
        out_specs=pl.BlockSpec((tm, d), lambda i, pos_ref: (i, 0)),
        scratch_shapes=[pltpu.VMEM((2, 2, tm, d), F32), pltpu.SemaphoreType.DMA((2,))],
    )
    return pl.pallas_call(
        kernel,
        out_shape=jax.ShapeDtypeStruct((n_rows, d), F32),
        grid_spec=grid_spec,
        compiler_params=_params(("arbitrary",)),
        name="combine",
    )(pos, h, wts, mods, final_g.reshape(1, d), ys)


def kernel(x, c, ctx, c_ctx, ada_w, ada_b, norm1_g, w_in, b_gate, diff_lambda, subln_g, w_attn_out, sc_conv_w,
           w_sc_out, cf_dw_w, cf_dw_b, cf_ln_g, cf_ln_b, w_cf_out, w_mix, norm2_g, router_g_w, router_g_b,
           router_e_w, router_e_b, exp_w_gu, exp_w_down, final_g):
    batch, seq, d = x.shape
    n_ctx = ctx.shape[1]
    depth = ada_w.shape[0]
    n_lat = batch * seq
    n_all = n_lat + batch * n_ctx
    assert d == D_MODEL and batch == MOD_ROWS // 2 and w_in.shape[2] == C_TOT

    cond = jnp.concatenate([c, c_ctx[None, :], jnp.zeros((MOD_ROWS - batch - 1, d), F32)], axis=0)
    mods_all = _adaln(cond, ada_w, ada_b)
    cos, sin = _rope_tables(seq)
    rows = (x.reshape(n_lat, d), ctx.reshape(batch * n_ctx, d))
    geo = dict(n_lat=n_lat, seq=seq)

    for layer in range(depth):
        last = layer == depth - 1
        lam_init = 0.8 - 0.6 * math.exp(-0.3 * layer)
        mods = mods_all[layer].reshape(MOD_ROWS, 1, 6 * d)
        n_rows = n_lat if last else n_all

        p = _in_proj(rows, norm1_g[layer], mods, w_in, layer, row0=0, n_rows=n_rows, col0=0, n_cols=C_TOT,
                     sh_idx=0, sc_idx=1, tm=1024, **geo)
        if last:
            kv_ctx = _in_proj(rows, norm1_g[layer], mods, w_in, layer, row0=n_lat, n_rows=batch * n_ctx, col0=OFF_K,
                              n_cols=OFF_SC - OFF_K, sh_idx=0, sc_idx=1, tm=n_ctx, **geo)
            kc_blk, vc_blk, ctx_rb0 = 0, QK_W // HEAD_W, 0
        else:
            kv_ctx = p
            kc_blk, vc_blk, ctx_rb0 = OFF_K // HEAD_W, OFF_V // HEAD_W, n_lat // n_ctx
        attn = _attn_lat(p, kv_ctx, kc_blk, vc_blk, ctx_rb0, cos, sin, diff_lambda[layer], subln_g[layer],
                         batch=batch, seq=seq, n_ctx=n_ctx, lam_init=lam_init)
        if not last:
            attn = (attn, _attn_ctx(p, n_lat // n_ctx, diff_lambda[layer], subln_g[layer],
                                    batch=batch, n_ctx=n_ctx, lam_init=lam_init))
        else:
            attn = (attn, attn)
        sc, cf = _convs(p, sc_conv_w[layer], cf_dw_w[layer], cf_dw_b[layer], cf_ln_g[layer], cf_ln_b[layer],
                        n_rows=n_rows, **geo)
        h_mix = _merge(attn, sc, cf, p, b_gate[layer].reshape(1, 3 * d),
                       w_attn_out[layer].astype(BF16), w_sc_out[layer].astype(BF16),
                       w_cf_out[layer].astype(BF16), w_mix[layer].astype(BF16), rows, mods,
                       n_rows=n_rows, gate_idx=2, **geo)

        rw = jnp.concatenate([router_g_w[layer], router_e_w[layer],
                              jnp.zeros((d, LANES - N_GROUPS - N_EXPERTS), F32)], axis=1)
        rb = jnp.concatenate([router_g_b[layer], router_e_b[layer],
                              jnp.zeros((LANES - N_GROUPS - N_EXPERTS,), F32)]).reshape(1, LANES)
        ids, wts, cnt = _router(h_mix, norm2_g[layer], mods, rw, rb, n_rows=n_rows, sh_idx=3, sc_idx=4, **geo)
        items, pos = _moe_plan(ids, cnt, n_rows)
        xs = _dispatch(h_mix, norm2_g[layer], mods, pos, n_rows=n_rows, sh_idx=3, sc_idx=4, **geo)
        ys = _moe(xs, items, exp_w_gu, exp_w_down, layer)
        h = _combine(h_mix, ys, pos, wts, mods, final_g, n_rows=n_rows, gate_idx=5, final=last, **geo)
        rows = (h, h)

    return h.reshape(batch, seq, d)
```

```python
import functools
import math

import jax
import jax.numpy as jnp
from jax import lax
from jax.experimental import pallas as pl
from jax.experimental.pallas import tpu as pltpu

F32 = jnp.float32
BF16 = jnp.bfloat16
I32 = jnp.int32

D_MODEL = 2048
GRID_W = 64
NORM_EPS = 1e-6
N_HEADS = 8
HEAD_DIM = 64
HEAD_W = 2 * HEAD_DIM
QK_W = N_HEADS * HEAD_W
ATTN_W = N_HEADS * HEAD_W
ROPE_BASE = 10000.0
ROPE_FREQS = HEAD_DIM // 4
SUBLN_EPS = 1e-5
QK_SCALE = HEAD_DIM ** -0.5 * math.log2(math.e)
SC_W = D_MODEL // 4
CF_W = D_MODEL // 4
SC_TAPS = 3
CF_TAPS = 31
CF_LN_EPS = 1e-5
OFF_Q = 0
OFF_K = OFF_Q + QK_W
OFF_V = OFF_K + QK_W
OFF_SC = OFF_V + ATTN_W
OFF_CF = OFF_SC + 3 * SC_W
OFF_GATE = OFF_CF + 2 * CF_W
C_TOT = OFF_GATE + 3 * D_MODEL
N_GROUPS = 4
EXPERTS_PER_GROUP = 8
N_EXPERTS = N_GROUPS * EXPERTS_PER_GROUP
EXPERT_HIDDEN = D_MODEL // 2

LANES = 128
SUBLANES = 8
MOD_ROWS = 8
MERGE_GATE_TILES = 4
CONV_HALO = 16
CONV_ROWS = 256
CONV_CHUNK = 32
MOE_ITEM_ROWS = 1024
MOE_ITEM_CAP = MOE_ITEM_ROWS - 8
MOE_ROW_PAD = 128
MOE_HIDDEN_BLOCK = 512
MOE_ITEM_SHIFT = 1
DISPATCH_ROWS = 512
COMBINE_ROWS = 256
VMEM_LIMIT = 56 * 1024 * 1024
MOE_VMEM_LIMIT = 60 * 1024 * 1024


def _params(sem, vmem=VMEM_LIMIT):
    return pltpu.CompilerParams(dimension_semantics=sem, vmem_limit_bytes=vmem)


def _sigmoid(x):
    return 1.0 / (1.0 + jnp.exp(-x))


def _adaln_kernel(s_ref, w_ref, b_ref, o_ref):
    s = s_ref[...]
    s = s * _sigmoid(s)
    s_hi = s.astype(BF16)
    w = w_ref[...]
    w_hi = w.astype(BF16)
    w_lo = (w - w_hi.astype(F32)).astype(BF16)
    lhs = jnp.concatenate([s_hi.astype(F32), s - s_hi.astype(F32)], axis=0).astype(BF16)
    r = jnp.dot(lhs, w_hi, preferred_element_type=F32)
    r2 = jnp.dot(s_hi, w_lo, preferred_element_type=F32)
    o_ref[...] = r[:MOD_ROWS] + r[MOD_ROWS:] + r2 + b_ref[...]


def _adaln(cond, ada_w, ada_b):
    n_layers, d, n = ada_w.shape
    tn = 1024
    return pl.pallas_call(
        _adaln_kernel,
        out_shape=jax.ShapeDtypeStruct((n_layers, MOD_ROWS, n), F32),
        grid=(n_layers, n // tn),
        in_specs=[
            pl.BlockSpec((MOD_ROWS, d), lambda l, j: (0, 0)),
            pl.BlockSpec((None, d, tn), lambda l, j: (l, 0, j)),
            pl.BlockSpec((None, 1, tn), lambda l, j: (l, 0, j)),
        ],
        out_specs=pl.BlockSpec((None, MOD_ROWS, tn), lambda l, j: (l, 0, j)),
        compiler_params=_params(("arbitrary", "arbitrary")),
        name="adaln",
    )(cond, ada_w, ada_b.reshape(n_layers, 1, n))


def _mod_row(row0, n_lat, seq):
    return jnp.where(row0 < n_lat, row0 // seq, MOD_ROWS // 2)


def _in_proj_kernel(xa_ref, xb_ref, g_ref, sh_ref, sc_ref, w_ref, o_ref, u_ref, *, a_blocks):
    first_col = pl.program_id(1) == 0
    from_a = pl.program_id(0) < a_blocks

    def normed(x_ref):
        x = x_ref[...]
        ms = jnp.mean(x * x, axis=-1, keepdims=True)
        y = x * lax.rsqrt(ms + NORM_EPS) * g_ref[...]
        u_ref[...] = (y * (1.0 + sc_ref[...]) + sh_ref[...]).astype(BF16)

    @pl.when(jnp.logical_and(first_col, from_a))
    def _():
        normed(xa_ref)

    @pl.when(jnp.logical_and(first_col, jnp.logical_not(from_a)))
    def _():
        normed(xb_ref)

    o_ref[...] = jnp.dot(u_ref[...], w_ref[...].astype(BF16), preferred_element_type=F32).astype(o_ref.dtype)


def _in_proj(rows, gain, mods, w_in, layer, *, row0, n_rows, col0, n_cols, n_lat, seq, sh_idx, sc_idx, tm, tn=512):
    xa, xb = rows
    d = xa.shape[1]
    rb0, cb0 = row0 // tm, col0 // tn
    a_blocks = xa.shape[0] // tm - rb0

    def mod_map(idx):
        return lambda i, j: (_mod_row((i + rb0) * tm, n_lat, seq), 0, idx)

    return pl.pallas_call(
        functools.partial(_in_proj_kernel, a_blocks=a_blocks),
        out_shape=jax.ShapeDtypeStruct((n_rows, n_cols), BF16),
        grid=(n_rows // tm, n_cols // tn),
        in_specs=[
            pl.BlockSpec((tm, d), lambda i, j: (jnp.minimum(i, a_blocks - 1) + rb0, 0)),
            pl.BlockSpec((tm, d), lambda i, j: (jnp.maximum(i - a_blocks, 0), 0)),
            pl.BlockSpec((1, d), lambda i, j: (0, 0)),
            pl.BlockSpec((None, 1, d), mod_map(sh_idx)),
            pl.BlockSpec((None, 1, d), mod_map(sc_idx)),
            pl.BlockSpec((None, d, tn), lambda i, j: (layer, 0, j + cb0)),
        ],
        out_specs=pl.BlockSpec((tm, tn), lambda i, j: (i, j)),
        scratch_shapes=[pltpu.VMEM((tm, d), BF16)],
        compiler_params=_params(("arbitrary", "arbitrary")),
        name="in_proj",
    )(xa, xb, gain.reshape(1, d), mods, mods, w_in)


def _rope_tables(n_tokens):
    t = jnp.arange(n_tokens, dtype=I32)
    pos = jnp.stack([t // GRID_W, t % GRID_W], axis=-1).astype(F32)
    inv_freq = ROPE_BASE ** (-jnp.arange(ROPE_FREQS, dtype=F32) / ROPE_FREQS)
    ang = pos[:, :, None] * inv_freq
    cos, sin = jnp.cos(ang), jnp.sin(ang)
    c = jnp.stack([cos, cos], axis=2).reshape(n_tokens, HEAD_DIM)
    s = jnp.stack([-sin, sin], axis=2).reshape(n_tokens, HEAD_DIM)
    return jnp.tile(c, (1, 2)), jnp.tile(s, (1, 2))


def _rope(x, c, s):
    lane = lax.broadcasted_iota(I32, x.shape, 1)
    first_half = (lane % (2 * ROPE_FREQS)) < ROPE_FREQS
    partner = jnp.where(first_half, pltpu.roll(x, LANES - ROPE_FREQS, 1), pltpu.roll(x, ROPE_FREQS, 1))
    return x * c + partner * s


def _diff_lambda(lam_ref, lam_init):
    lv = lam_ref[...]
    a = jnp.sum(lv[0:1] * lv[1:2], axis=-1, keepdims=True)
    b = jnp.sum(lv[2:3] * lv[3:4], axis=-1, keepdims=True)
    return jnp.exp(a) - jnp.exp(b) + lam_init


def _attend(q, k_all, v_all, lam, subg, lam_init):
    tq = q.shape[0]
    lane = lax.broadcasted_iota(I32, q.shape, 1)
    q0 = jnp.where(lane < HEAD_DIM, q, 0.0).astype(BF16)
    q1 = jnp.where(lane >= HEAD_DIM, q, 0.0).astype(BF16)
    qq = jnp.concatenate([q0, q1], axis=0)
    s = lax.dot_general(qq, k_all, (((1,), (1,)), ((), ())), preferred_element_type=F32)
    m = jnp.max(s, axis=-1, keepdims=True)
    e = jnp.exp2(s - m)
    l = jnp.sum(e, axis=-1, keepdims=True)
    a = e[:tq] - e[tq:] * (lam * l[:tq] / l[tq:])
    o = jnp.dot(a.astype(BF16), v_all, preferred_element_type=F32) / l[:tq]
    ms = jnp.mean(o * o, axis=-1, keepdims=True)
    return o * lax.rsqrt(ms + SUBLN_EPS) * subg * (1.0 - lam_init)


def _attn_lat_kernel(q_ref, kl_ref, vl_ref, kc_ref, vc_ref, cq_ref, sq_ref, ck_ref, sk_ref, lam_ref, g_ref,
                     o_ref, k_all, v_aug, s0_ref, m0_ref, s1_ref, m1_ref, *, n_ctx, lam_init, n_blocks):
    i = pl.program_id(2)
    tq = q_ref.shape[0]

    @pl.when(i == 0)
    def _():
        k_all[0:n_ctx, :] = kc_ref[...]
        k_all[n_ctx:, :] = _rope(kl_ref[...].astype(F32), ck_ref[...], sk_ref[...]).astype(BF16)
        v_aug[0:n_ctx, 0:HEAD_W] = vc_ref[...]
        v_aug[n_ctx:, 0:HEAD_W] = vl_ref[...]
        v_aug[:, HEAD_W:2 * HEAD_W] = jnp.ones((k_all.shape[0], HEAD_W), BF16)

    lam = _diff_lambda(lam_ref, lam_init)

    def stage_a(sa_ref, ma_ref):
        q = _rope(q_ref[...].astype(F32), cq_ref[...], sq_ref[...]) * QK_SCALE
        lane = lax.broadcasted_iota(I32, q.shape, 1)
        q0 = jnp.where(lane < HEAD_DIM, q, 0.0).astype(BF16)
        q1 = jnp.where(lane >= HEAD_DIM, q, 0.0).astype(BF16)
        qq = jnp.concatenate([q0, q1], axis=0)
        s = lax.dot_general(qq, k_all[...], (((1,), (1,)), ((), ())), preferred_element_type=F32)
        sa_ref[...] = s
        ma_ref[...] = jnp.broadcast_to(jnp.max(s, axis=-1, keepdims=True), ma_ref.shape)

    def stage_b(sb_ref, mb_ref):
        mb = jnp.concatenate([mb_ref[...]] * (sb_ref.shape[1] // LANES), axis=1)
        e = jnp.exp2(sb_ref[...] - mb).astype(BF16)
        oa = jnp.dot(e, v_aug[...], preferred_element_type=F32)
        o = oa[:tq, 0:HEAD_W] / oa[:tq, HEAD_W:] - oa[tq:, 0:HEAD_W] * (lam / oa[tq:, HEAD_W:])
        ms = jnp.mean(o * o, axis=-1, keepdims=True)
        o_ref[...] = (o * lax.rsqrt(ms + SUBLN_EPS) * g_ref[...] * (1.0 - lam_init)).astype(o_ref.dtype)

    bufs = ((s0_ref, m0_ref), (s1_ref, m1_ref))

    @pl.when(i == 0)
    def _():
        stage_a(*bufs[0])

    for par in range(2):
        @pl.when(jnp.logical_and(jnp.logical_and(i > 0, i < n_blocks), i % 2 == par))
        def _():
            stage_a(*bufs[par])
            stage_b(*bufs[1 - par])

    @pl.when(i == n_blocks)
    def _():
        stage_b(*bufs[(n_blocks - 1) % 2])


def _attn_ctx_kernel(q_ref, k_ref, v_ref, lam_ref, g_ref, o_ref, *, lam_init):
    q = q_ref[...].astype(F32) * QK_SCALE
    lam = _diff_lambda(lam_ref, lam_init)
    o_ref[...] = _attend(q, k_ref[...], v_ref[...], lam, g_ref[...], lam_init).astype(o_ref.dtype)


def _attn_lat(p, kv_ctx, kc_blk, vc_blk, ctx_rb0, cos, sin, lam4, subg, *, batch, seq, n_ctx, lam_init, tq=512):
    nq = seq // tq
    n_keys = n_ctx + seq
    hb = lambda off: off // HEAD_W
    qblk = lambda i: jnp.minimum(i, nq - 1)
    kernel = functools.partial(_attn_lat_kernel, n_ctx=n_ctx, lam_init=lam_init, n_blocks=nq)
    return pl.pallas_call(
        kernel,
        out_shape=jax.ShapeDtypeStruct((batch * seq, ATTN_W), BF16),
        grid=(batch, N_HEADS, nq + 1),
        in_specs=[
            pl.BlockSpec((tq, HEAD_W), lambda b, h, i: (b * nq + qblk(i), hb(OFF_Q) + h)),
            pl.BlockSpec((seq, HEAD_W), lambda b, h, i: (b, hb(OFF_K) + h)),
            pl.BlockSpec((seq, HEAD_W), lambda b, h, i: (b, hb(OFF_V) + h)),
            pl.BlockSpec((n_ctx, HEAD_W), lambda b, h, i: (ctx_rb0 + b, kc_blk + h)),
            pl.BlockSpec((n_ctx, HEAD_W), lambda b, h, i: (ctx_rb0 + b, vc_blk + h)),
            pl.BlockSpec((tq, HEAD_W), lambda b, h, i: (qblk(i), 0)),
            pl.BlockSpec((tq, HEAD_W), lambda b, h, i: (qblk(i), 0)),
            pl.BlockSpec((seq, HEAD_W), lambda b, h, i: (0, 0)),
            pl.BlockSpec((seq, HEAD_W), lambda b, h, i: (0, 0)),
            pl.BlockSpec((4, HEAD_DIM), lambda b, h, i: (0, 0)),
            pl.BlockSpec((1, HEAD_W), lambda b, h, i: (0, 0)),
        ],
        out_specs=pl.BlockSpec((tq, HEAD_W), lambda b, h, i: (b * nq + jnp.maximum(i - 1, 0), h)),
        scratch_shapes=[pltpu.VMEM((n_keys, HEAD_W), BF16), pltpu.VMEM((n_keys, 2 * HEAD_W), BF16),
                        pltpu.VMEM((2 * tq, n_keys), F32), pltpu.VMEM((2 * tq, LANES), F32),
                        pltpu.VMEM((2 * tq, n_keys), F32), pltpu.VMEM((2 * tq, LANES), F32)],
        compiler_params=_params(("arbitrary", "arbitrary", "arbitrary")),
        name="attn_lat",
    )(p, p, p, kv_ctx, kv_ctx, cos, sin, cos, sin, lam4, subg.reshape(1, HEAD_W))


def _attn_ctx(p, ctx_rb0, lam4, subg, *, batch, n_ctx, lam_init):
    hb = lambda off: off // HEAD_W
    kernel = functools.partial(_attn_ctx_kernel, lam_init=lam_init)
    return pl.pallas_call(
        kernel,
        out_shape=jax.ShapeDtypeStruct((batch * n_ctx, ATTN_W), BF16),
        grid=(batch, N_HEADS),
        in_specs=[
            pl.BlockSpec((n_ctx, HEAD_W), lambda b, h: (ctx_rb0 + b, hb(OFF_Q) + h)),
            pl.BlockSpec((n_ctx, HEAD_W), lambda b, h: (ctx_rb0 + b, hb(OFF_K) + h)),
            pl.BlockSpec((n_ctx, HEAD_W), lambda b, h: (ctx_rb0 + b, hb(OFF_V) + h)),
            pl.BlockSpec((4, HEAD_DIM), lambda b, h: (0, 0)),
            pl.BlockSpec((1, HEAD_W), lambda b, h: (0, 0)),
        ],
        out_specs=pl.BlockSpec((n_ctx, HEAD_W), lambda b, h: (b, h)),
        compiler_params=_params(("arbitrary", "arbitrary")),
        name="attn_ctx",
    )(p, p, p, lam4, subg.reshape(1, HEAD_W))


def _conv_kernel(bg_ref, cg_ref, xi_ref, a_ref, g_ref,
                 cg_p, xi_p, a_p, g_p, cg_n, xi_n, a_n, g_n,
                 scw_ref, cfw_ref, cfb_ref, lng_ref, lnb_ref,
                 sco_ref, cfo_ref, pad_ref, *, lat_blocks, blocks_per_seq):
    i = pl.program_id(0)
    in_lat = i < lat_blocks
    pos = i % blocks_per_seq
    has_prev = jnp.logical_and(in_lat, pos != 0)
    has_next = jnp.logical_and(in_lat, pos != blocks_per_seq - 1)
    keep_prev = jnp.where(has_prev, 1.0, 0.0)
    keep_next = jnp.where(has_next, 1.0, 0.0)
    lo, hi = CONV_HALO, CONV_HALO + CONV_ROWS

    def fill(main, prev, nxt):
        pad_ref[0:lo, :] = prev * keep_prev
        pad_ref[lo:hi, :] = main
        pad_ref[hi:hi + CONV_HALO, :] = nxt * keep_next

    def f(ref):
        return ref[...].astype(F32)

    def glu(a, g):
        return a * _sigmoid(g)

    fill(f(cg_ref) * f(xi_ref), f(cg_p) * f(xi_p), f(cg_n) * f(xi_n))
    for c in range(CONV_ROWS // CONV_CHUNK):
        r0 = lo + c * CONV_CHUNK - SC_TAPS // 2
        acc = scw_ref[0:1, :] * pad_ref[r0:r0 + CONV_CHUNK, :]
        for k in range(1, SC_TAPS):
            acc = acc + scw_ref[k:k + 1, :] * pad_ref[r0 + k:r0 + k + CONV_CHUNK, :]
        rows = slice(c * CONV_CHUNK, (c + 1) * CONV_CHUNK)
        sco_ref[rows, :] = (bg_ref[rows, :].astype(F32) * acc).astype(sco_ref.dtype)

    fill(glu(f(a_ref), f(g_ref)), glu(f(a_p), f(g_p)), glu(f(a_n), f(g_n)))
    off = lo - CF_TAPS // 2
    win = CONV_CHUNK + SUBLANES
    for c in range(CONV_ROWS // CONV_CHUNK):
        base = c * CONV_CHUNK
        acc = None
        for b in range(SUBLANES):
            q = None
            for k in range(b, CF_TAPS, SUBLANES):
                term = cfw_ref[k:k + 1, :] * pad_ref[base + k - b:base + k - b + win, :]
                q = term if q is None else q + term
            part = q[off + b:off + b + CONV_CHUNK, :]
            acc = part if acc is None else acc + part
        z = acc + cfb_ref[...]
        mu = jnp.mean(z, axis=-1, keepdims=True)
        zc = z - mu
        var = jnp.mean(zc * zc, axis=-1, keepdims=True)
        y = zc * lax.rsqrt(var + CF_LN_EPS) * lng_ref[...] + lnb_ref[...]
        rows = slice(c * CONV_CHUNK, (c + 1) * CONV_CHUNK)
        cfo_ref[rows, :] = (y * _sigmoid(y)).astype(cfo_ref.dtype)


def _convs(p, sc_w, cf_w, cf_b, ln_g, ln_b, *, n_rows, n_lat, seq):
    nb = n_rows // CONV_ROWS
    halo_per_block = CONV_ROWS // CONV_HALO
    last_halo = n_rows // CONV_HALO - 1
    cb = lambda off: off // SC_W

    def main(off):
        return pl.BlockSpec((CONV_ROWS, SC_W), lambda i: (i, cb(off)))

    def prev(off):
        return pl.BlockSpec((CONV_HALO, SC_W), lambda i: (jnp.maximum(i * halo_per_block - 1, 0), cb(off)))

    def nxt(off):
        return pl.BlockSpec((CONV_HALO, SC_W), lambda i: (jnp.minimum((i + 1) * halo_per_block, last_halo), cb(off)))

    def vec(rows):
        return pl.BlockSpec((rows, SC_W), lambda i: (0, 0))

    o_bg, o_cg, o_xi, o_a, o_g = OFF_SC, OFF_SC + SC_W, OFF_SC + 2 * SC_W, OFF_CF, OFF_CF + CF_W
    kernel = functools.partial(_conv_kernel, lat_blocks=n_lat // CONV_ROWS, blocks_per_seq=seq // CONV_ROWS)
    return pl.pallas_call(
        kernel,
        out_shape=(jax.ShapeDtypeStruct((n_rows, SC_W), BF16), jax.ShapeDtypeStruct((n_rows, CF_W), BF16)),
        grid=(nb,),
        in_specs=[main(o_bg), main(o_cg), main(o_xi), main(o_a), main(o_g),
                  prev(o_cg), prev(o_xi), prev(o_a), prev(o_g),
                  nxt(o_cg), nxt(o_xi), nxt(o_a), nxt(o_g),
                  vec(SC_TAPS), vec(CF_TAPS), vec(1), vec(1), vec(1)],
        out_specs=(pl.BlockSpec((CONV_ROWS, SC_W), lambda i: (i, 0)),
                   pl.BlockSpec((CONV_ROWS, CF_W), lambda i: (i, 0))),
        scratch_shapes=[pltpu.VMEM((CONV_ROWS + 2 * CONV_HALO, SC_W), F32)],
        compiler_params=_params(("arbitrary",)),
        name="convs",
    )(p, p, p, p, p, p, p, p, p, p, p, p, p,
      sc_w, cf_w, cf_b.reshape(1, CF_W), ln_g.reshape(1, CF_W), ln_b.reshape(1, CF_W))


def _merge_kernel(*refs, a_blocks):
    n_gate = 3 * MERGE_GATE_TILES
    ata_ref, atb_ref, sc_ref, cf_ref = refs[:4]
    gate_refs = refs[4:4 + n_gate]
    bg_ref, wa_ref, wb_ref, wc_ref, wm_ref, ha_ref, hb_ref, g1_ref, o_ref = refs[4 + n_gate:]
    d = D_MODEL
    from_a = pl.program_id(0) < a_blocks
    attn = jnp.where(from_a, ata_ref[...], atb_ref[...])
    h = jnp.where(from_a, ha_ref[...], hb_ref[...])

    def gate(branch):
        tiles = gate_refs[branch * MERGE_GATE_TILES:(branch + 1) * MERGE_GATE_TILES]
        pre = jnp.concatenate([t[...] for t in tiles], axis=1).astype(F32)
        return _sigmoid(pre + bg_ref[:, branch * d:(branch + 1) * d])

    ya = jnp.dot(attn, wa_ref[...], preferred_element_type=F32)
    yb = jnp.dot(sc_ref[...], wb_ref[...], preferred_element_type=F32)
    yc = jnp.dot(cf_ref[...], wc_ref[...], preferred_element_type=F32)
    m = (gate(0) * ya + gate(1) * yb + gate(2) * yc).astype(BF16)
    o_ref[...] = h + g1_ref[...] * jnp.dot(m, wm_ref[...], preferred_element_type=F32)


def _merge(attn, sc, cf, p, b_gate, wa, wb, wc, wm, rows, mods, *, n_rows, n_lat, seq, gate_idx, tm=256):
    d = D_MODEL
    (ata, atb), (ha, hb) = attn, rows
    a_blocks = ha.shape[0] // tm
    first = lambda w: pl.BlockSpec((tm, w), lambda i: (jnp.minimum(i, a_blocks - 1), 0))
    rest = lambda w: pl.BlockSpec((tm, w), lambda i: (jnp.maximum(i - a_blocks, 0), 0))
    tg = d // MERGE_GATE_TILES
    gb0 = OFF_GATE // tg
    gates = [pl.BlockSpec((tm, tg), functools.partial(lambda i, c: (i, c), c=gb0 + t))
             for t in range(3 * MERGE_GATE_TILES)]
    whole = lambda a: pl.BlockSpec(a.shape, lambda i: (0, 0), pipeline_mode=pl.Buffered(1))
    return pl.pallas_call(
        functools.partial(_merge_kernel, a_blocks=a_blocks),
        out_shape=jax.ShapeDtypeStruct((n_rows, d), F32),
        grid=(n_rows // tm,),
        in_specs=[
            first(ATTN_W), rest(ATTN_W),
            pl.BlockSpec((tm, SC_W), lambda i: (i, 0)),
            pl.BlockSpec((tm, CF_W), lambda i: (i, 0)),
            *gates,
            whole(b_gate), whole(wa), whole(wb), whole(wc), whole(wm),
            first(d), rest(d),
            pl.BlockSpec((None, 1, d), lambda i: (_mod_row(i * tm, n_lat, seq), 0, gate_idx)),
        ],
        out_specs=pl.BlockSpec((tm, d), lambda i: (i, 0)),
        compiler_params=_params(("arbitrary",)),
        name="merge",
    )(ata, atb, sc, cf, *([p] * (3 * MERGE_GATE_TILES)), b_gate, wa, wb, wc, wm, ha, hb, mods)


def _split2(x):
    hi = x.astype(BF16)
    return hi, (x - hi.astype(F32)).astype(BF16)


def _norm_mod(x, g, sh, sc):
    ms = jnp.mean(x * x, axis=-1, keepdims=True)
    return x * lax.rsqrt(ms + NORM_EPS) * g * (1.0 + sc) + sh


def _router_kernel(h_ref, g_ref, sh_ref, sc_ref, rw_ref, rb_ref, tri_ref, id_ref, wt_ref, cnt_ref, run_ref):
    @pl.when(pl.program_id(0) == 0)
    def _():
        run_ref[...] = jnp.zeros_like(run_ref)

    f = _norm_mod(h_ref[...], g_ref[...], sh_ref[...], sc_ref[...])

    f_hi, f_lo = _split2(f)
    w_hi, w_lo = _split2(rw_ref[...])
    dot = lambda a, b: jnp.dot(a, b, preferred_element_type=F32)
    both = dot(f_hi, jnp.concatenate([w_hi, w_lo], axis=1))
    logits = (dot(f_lo, w_hi) + both[:, LANES:] + both[:, :LANES]) + rb_ref[...]

    lane_i = lax.broadcasted_iota(I32, logits.shape, 1)
    lane = lane_i.astype(F32)
    neg = -jnp.inf
    big = float(LANES)
    lg = jnp.where(lane_i < N_GROUPS, logits, neg)
    mg = jnp.max(lg, axis=-1, keepdims=True)
    grp = jnp.min(jnp.where(lg == mg, lane, big), axis=-1, keepdims=True)
    p_grp = 1.0 / jnp.sum(jnp.exp(lg - mg), axis=-1, keepdims=True)

    e_lane = lane - N_GROUPS
    in_grp = jnp.logical_and(e_lane >= grp * EXPERTS_PER_GROUP, e_lane < (grp + 1.0) * EXPERTS_PER_GROUP)
    le = jnp.where(in_grp, logits, neg)
    m1 = jnp.max(le, axis=-1, keepdims=True)
    i1 = jnp.min(jnp.where(le == m1, lane, big), axis=-1, keepdims=True)
    le2 = jnp.where(lane == i1, neg, le)
    m2 = jnp.max(le2, axis=-1, keepdims=True)
    i2 = jnp.min(jnp.where(le2 == m2, lane, big), axis=-1, keepdims=True)
    e2 = jnp.exp(m2 - m1)
    w1 = p_grp / (1.0 + e2)
    w2 = p_grp * e2 / (1.0 + e2)
    e1, e2 = i1 - N_GROUPS, i2 - N_GROUPS
    hot1 = jnp.where(lane == e1, 1.0, 0.0)
    hot2 = jnp.where(lane == e2, 1.0, 0.0)
    tri = tri_ref[...]
    before1 = jnp.dot(tri, hot1.astype(BF16), preferred_element_type=F32)
    before2 = jnp.dot(tri, hot2.astype(BF16), preferred_element_type=F32)
    tot1 = jnp.sum(hot1, axis=0, keepdims=True)
    tot2 = jnp.sum(hot2, axis=0, keepdims=True)
    run = run_ref[...]
    r1 = jnp.sum(hot1 * (run + before1), axis=-1, keepdims=True)
    r2 = jnp.sum(hot2 * (run + tot1 + before2), axis=-1, keepdims=True)
    run = run + tot1 + tot2
    run_ref[...] = run
    cnt_ref[...] = run

    ids = jnp.where(lane_i == 0, e1, jnp.where(lane_i == 1, e2, jnp.where(lane_i == 2, r1,
                                                                        jnp.where(lane_i == 3, r2, 0.0))))
    id_ref[...] = ids.astype(I32)
    wt_ref[...] = jnp.where(lane_i == 0, w1, jnp.where(lane_i == 1, w2, 0.0))


def _router(h, gain, mods, rw, rb, *, n_rows, n_lat, seq, sh_idx, sc_idx, tm=1024):
    d = D_MODEL

    def mod_map(idx):
        return lambda i: (_mod_row(i * tm, n_lat, seq), 0, idx)

    return pl.pallas_call(
        _router_kernel,
        out_shape=(jax.ShapeDtypeStruct((n_rows, LANES), I32),
                   jax.ShapeDtypeStruct((n_rows, LANES), F32),
                   jax.ShapeDtypeStruct((1, LANES), F32)),
        grid=(n_rows // tm,),
        in_specs=[
            pl.BlockSpec((tm, d), lambda i: (i, 0)),
            pl.BlockSpec((1, d), lambda i: (0, 0)),
            pl.BlockSpec((None, 1, d), mod_map(sh_idx)),
            pl.BlockSpec((None, 1, d), mod_map(sc_idx)),
            pl.BlockSpec((d, LANES), lambda i: (0, 0)),
            pl.BlockSpec((1, LANES), lambda i: (0, 0)),
            pl.BlockSpec((tm, tm), lambda i: (0, 0)),
        ],
        out_specs=(pl.BlockSpec((tm, LANES), lambda i: (i, 0)),
                   pl.BlockSpec((tm, LANES), lambda i: (i, 0)),
                   pl.BlockSpec((1, LANES), lambda i: (0, 0))),
        scratch_shapes=[pltpu.VMEM((1, LANES), F32)],
        compiler_params=_params(("arbitrary",)),
        name="router",
    )(h, gain.reshape(1, d), mods, mods, rw, rb, jnp.tril(jnp.ones((tm, tm), BF16), -1))


def _dispatch_kernel(pos_ref, h_ref, g_ref, sh_ref, sc_ref, xs_hbm, f0_ref, f1_ref, sem):
    i = pl.program_id(0)
    tm = h_ref.shape[0]

    def run(f_ref):
        def wait_block_copies():
            for _ in range(2):
                pltpu.make_async_copy(f_ref, xs_hbm.at[pl.ds(0, tm), :], sem).wait()

        f_ref[...] = _norm_mod(h_ref[...], g_ref[...], sh_ref[...], sc_ref[...])

        @pl.when(i > 0)
        def _():
            wait_block_copies()

        base = i * (2 * tm)
        for r in range(tm):
            for k in range(2):
                dst = pos_ref[base + 2 * r + k]
                pltpu.make_async_copy(f_ref.at[pl.ds(r, 1), :], xs_hbm.at[pl.ds(dst, 1), :], sem).start(priority=k)

        @pl.when(i == pl.num_programs(0) - 1)
        def _():
            wait_block_copies()

    @pl.when(i % 2 == 0)
    def _():
        run(f0_ref)

    @pl.when(i % 2 == 1)
    def _():
        run(f1_ref)


def _dispatch(h, gain, mods, pos, *, n_rows, n_lat, seq, sh_idx, sc_idx, tm=DISPATCH_ROWS):
    d = D_MODEL

    def mod_map(idx):
        return lambda i, pos_ref: (_mod_row(i * tm, n_lat, seq), 0, idx)

    grid_spec = pltpu.PrefetchScalarGridSpec(
        num_scalar_prefetch=1,
        grid=(n_rows // tm,),
        in_specs=[
            pl.BlockSpec((tm, d), lambda i, pos_ref: (i, 0)),
            pl.BlockSpec((1, d), lambda i, pos_ref: (0, 0)),
            pl.BlockSpec((None, 1, d), mod_map(sh_idx)),
            pl.BlockSpec((None, 1, d), mod_map(sc_idx)),
        ],
        out_specs=pl.BlockSpec(memory_space=pl.ANY),
        scratch_shapes=[pltpu.VMEM((tm, d), F32), pltpu.VMEM((tm, d), F32), pltpu.SemaphoreType.DMA],
    )
    return pl.pallas_call(
        _dispatch_kernel,
        out_shape=jax.ShapeDtypeStruct((2 * n_rows, d), F32),
        grid_spec=grid_spec,
        compiler_params=_params(("arbitrary",)),
        name="dispatch",
    )(pos, h, gain.reshape(1, d), mods, mods)


def _moe_plan(ids, cnt, n_tok):
    n_items = N_EXPERTS + (2 * n_tok) // MOE_ITEM_CAP
    experts = jnp.arange(N_EXPERTS, dtype=I32)
    counts = cnt[0, :N_EXPERTS].astype(I32)
    starts = jnp.cumsum(counts) - counts
    e_slot, rank = ids[:, 0:2], ids[:, 2:4]
    start_of_slot = jnp.sum(jnp.where(e_slot[:, :, None] == experts, starts, 0), axis=-1)
    pos = (start_of_slot + rank).reshape(-1).astype(I32)
    pos = jnp.concatenate([pos, jnp.zeros((2 * COMBINE_ROWS,), I32)])
    n_sb = (counts + MOE_ITEM_CAP - 1) // MOE_ITEM_CAP
    cum = jnp.cumsum(n_sb)
    total = cum[-1]
    it = jnp.arange(n_items, dtype=I32)
    e_i = jnp.minimum(jnp.sum((it[:, None] >= cum[None, :]).astype(I32), axis=1), N_EXPERTS - 1)
    sb = it - (cum[e_i] - n_sb[e_i])
    valid = it < total
    last_e = e_i[jnp.maximum(total - 1, 0)]
    item_e = jnp.where(valid, e_i, last_e).astype(I32)
    item_start = jnp.where(valid, starts[e_i] + sb * MOE_ITEM_CAP, 0).astype(I32)
    item_n = jnp.where(valid, jnp.minimum(MOE_ITEM_CAP, counts[e_i] - sb * MOE_ITEM_CAP), 0).astype(I32)
    front, back = jnp.zeros((MOE_ITEM_SHIFT,), I32), jnp.zeros((1,), I32)
    pad_items = lambda a: jnp.concatenate([front, a, back])
    return (pad_items(item_e), pad_items(item_start), pad_items(item_n)), pos


def _moe_kernel(ie_ref, ist_ref, in_ref,
                xs_hbm, wg_ref, wu_ref, wd_ref, ys_hbm,
                xf_ref, xb_ref, y_ref, gsem, ssem):
    i, j = pl.program_id(0), pl.program_id(1)
    sh = MOE_ITEM_SHIFT
    n_prev, n, n_next = in_ref[i + sh - 1], in_ref[i + sh], in_ref[i + sh + 1]
    start_prev, start, start_next = ist_ref[i + sh - 1], ist_ref[i + sh], ist_ref[i + sh + 1]
    par = i % 2

    def rows_in(hrow, vrow, size):
        return pltpu.make_async_copy(xs_hbm.at[pl.ds(hrow, size), :], xf_ref.at[pl.ds(vrow, size), :], gsem)

    def rows_out(hrow, vrow, size):
        return pltpu.make_async_copy(y_ref.at[par, pl.ds(vrow, size), :], ys_hbm.at[pl.ds(hrow, size), :], ssem)

    def row_groups(make, base, count, act):
        shift = base % SUBLANES
        head = jnp.minimum((SUBLANES - shift) % SUBLANES, count)
        body = ((count - head) // SUBLANES) * SUBLANES
        for t in range(SUBLANES - 1):
            @pl.when(t < head)
            def _():
                getattr(make(base + t, shift + t, 1), act)()

        for b in reversed(range(SUBLANES.bit_length() - 1, MOE_ITEM_ROWS.bit_length())):
            @pl.when(((body >> b) & 1) == 1)
            def _():
                off = head + ((body >> (b + 1)) << (b + 1))
                getattr(make(pl.multiple_of(base + off, SUBLANES), pl.multiple_of(shift + off, SUBLANES), 1 << b),
                        act)()

        done = head + body
        for t in range(SUBLANES - 1):
            @pl.when(t < count - done)
            def _():
                getattr(make(base + done + t, shift + done + t, 1), act)()

    @pl.when(jnp.logical_and(i == 0, j == 0))
    def _():
        xf_ref[...] = jnp.zeros_like(xf_ref)
        y_ref[...] = jnp.zeros_like(y_ref)

    @pl.when(j == 0)
    def _():
        @pl.when(i == 0)
        def _():
            row_groups(rows_in, start, n, "start")

        @pl.when(n > 0)
        def _():
            row_groups(rows_in, start, n, "wait")
            xb_ref[...] = xf_ref[...].astype(BF16)
            row_groups(rows_in, start_next, n_next, "start")

    def compute(m):
        x = xb_ref[0:m, :]
        g = jnp.dot(x, wg_ref[...].astype(BF16), preferred_element_type=F32)
        u = jnp.dot(x, wu_ref[...].astype(BF16), preferred_element_type=F32)
        hmid = (g * _sigmoid(g) * u).astype(BF16)
        y = jnp.dot(hmid, wd_ref[...].astype(BF16), preferred_element_type=F32)
        acc = y_ref.at[par]
        acc[0:m, :] = y + jnp.where(j == 0, 0.0, acc[0:m, :])

    used = jnp.where(n > 0, start % SUBLANES + n, 0)
    n_pad = (used + MOE_ROW_PAD - 1) // MOE_ROW_PAD
    for k in range(1, MOE_ITEM_ROWS // MOE_ROW_PAD + 1):
        @pl.when(n_pad == k)
        def _():
            compute(k * MOE_ROW_PAD)

    @pl.when(j == pl.num_programs(1) - 1)
    def _():
        row_groups(rows_out, start_prev, n_prev, "wait")

        row_groups(rows_out, start, n, "start")

        @pl.when(i == pl.num_programs(0) - 1)
        def _():
            row_groups(rows_out, start, n, "wait")


def _moe(xs, items, w_gu, w_down, layer):
    item_e, item_start, item_n = items
    d, hid, hk = D_MODEL, EXPERT_HIDDEN, MOE_HIDDEN_BLOCK
    nj = hid // hk
    sh = MOE_ITEM_SHIFT
    n_items = item_e.shape[0] - sh - 1

    def chunk(j, nn, i):
        return jnp.where(nn[i + sh] > 0, j, nj - 1)

    grid_spec = pltpu.PrefetchScalarGridSpec(
        num_scalar_prefetch=3,
        grid=(n_items, nj),
        in_specs=[
            pl.BlockSpec(memory_space=pl.ANY),
            pl.BlockSpec((None, None, d, hk), lambda i, j, ie, ist, nn: (layer, ie[i + sh], 0, chunk(j, nn, i))),
            pl.BlockSpec((None, None, d, hk), lambda i, j, ie, ist, nn: (layer, ie[i + sh], 0, nj + chunk(j, nn, i))),
            pl.BlockSpec((None, None, hk, d), lambda i, j, ie, ist, nn: (layer, ie[i + sh], chunk(j, nn, i), 0)),
        ],
        out_specs=pl.BlockSpec(memory_space=pl.ANY),
        scratch_shapes=[
            pltpu.VMEM((MOE_ITEM_ROWS, d), F32),
            pltpu.VMEM((MOE_ITEM_ROWS, d), BF16),
            pltpu.VMEM((2, MOE_ITEM_ROWS, d), F32),
            pltpu.SemaphoreType.DMA,
            pltpu.SemaphoreType.DMA,
        ],
    )
    return pl.pallas_call(
        _moe_kernel,
        out_shape=jax.ShapeDtypeStruct(xs.shape, F32),
        grid_spec=grid_spec,
        compiler_params=_params(("arbitrary", "arbitrary"), MOE_VMEM_LIMIT),
        name="moe_experts",
    )(item_e, item_start, item_n, xs, w_gu, w_gu, w_down)


def _combine_kernel(pos_ref, h_ref, wt_ref, g2_ref, fg_ref, ys_hbm, o_ref, ybuf, sem, *, final):
    i = pl.program_id(0)
    tm = h_ref.shape[0]
    par = i % 2

    def fetch(step, half):
        base = step * (2 * tm)
        for r in range(tm):
            for k in range(2):
                src = pos_ref[base + 2 * r + k]
                pltpu.make_async_copy(ys_hbm.at[pl.ds(src, 1), :], ybuf.at[half, k, pl.ds(r, 1), :],
                                      sem.at[half]).start(priority=k)

    @pl.when(i == 0)
    def _():
        fetch(0, 0)

    @pl.when(i < pl.num_programs(0) - 1)
    def _():
        fetch(i + 1, 1 - par)

    for k in range(2):
        pltpu.make_async_copy(ys_hbm.at[pl.ds(0, tm), :], ybuf.at[par, k], sem.at[par]).wait()
    wt = wt_ref[...]
    moe = wt[:, 0:1] * ybuf[par, 0] + wt[:, 1:2] * ybuf[par, 1]
    h = h_ref[...] + g2_ref[...] * moe
    if final:
        ms = jnp.mean(h * h, axis=-1, keepdims=True)
        h = h * lax.rsqrt(ms + NORM_EPS) * fg_ref[...]
    o_ref[...] = h


def _combine(h, ys, pos, wts, mods, final_g, *, n_rows, n_lat, seq, gate_idx, final, tm=COMBINE_ROWS):
    d = D_MODEL
    kernel = functools.partial(_combine_kernel, final=final)
    grid_spec = pltpu.PrefetchScalarGridSpec(
        num_scalar_prefetch=1,
        grid=(n_rows // tm,),
        in_specs=[
            pl.BlockSpec((tm, d), lambda i, pos_ref: (i, 0)),
            pl.BlockSpec((tm, LANES), lambda i, pos_ref: (i, 0)),
            pl.BlockSpec((None, 1, d), lambda i, pos_ref: (_mod_row(i * tm, n_lat, seq), 0, gate_idx)),
            pl.BlockSpec((1, d), lambda i, pos_ref: (0, 0)),
            pl.BlockSpec(memory_space=pl.ANY),
        ],
        out_specs=pl.BlockSpec((tm, d), lambda i, pos_ref: (i, 0)),
        scratch_shapes=[pltpu.VMEM((2, 2, tm, d), F32), pltpu.SemaphoreType.DMA((2,))],
    )
    return pl.pallas_call(
        kernel,
        out_shape=jax.ShapeDtypeStruct((n_rows, d), F32),
        grid_spec=grid_spec,
        compiler_params=_params(("arbitrary",)),
        name="combine",
    )(pos, h, wts, mods, final_g.reshape(1, d), ys)


def kernel(x, c, ctx, c_ctx, ada_w, ada_b, norm1_g, w_in, b_gate, diff_lambda, subln_g, w_attn_out, sc_conv_w,
           w_sc_out, cf_dw_w, cf_dw_b, cf_ln_g, cf_ln_b, w_cf_out, w_mix, norm2_g, router_g_w, router_g_b,
           router_e_w, router_e_b, exp_w_gu, exp_w_down, final_g):
    batch, seq, d = x.shape
    n_ctx = ctx.shape[1]
    depth = ada_w.shape[0]
    n_lat = batch * seq
    n_all = n_lat + batch * n_ctx
    assert d == D_MODEL and batch == MOD_ROWS // 2 and w_in.shape[2] == C_TOT

    cond = jnp.concatenate([c, c_ctx[None, :], jnp.zeros((MOD_ROWS - batch - 1, d), F32)], axis=0)
    mods_all = _adaln(cond, ada_w, ada_b)
    cos, sin = _rope_tables(seq)
    rows = (x.reshape(n_lat, d), ctx.reshape(batch * n_ctx, d))
    geo = dict(n_lat=n_lat, seq=seq)

    for layer in range(depth):
        last = layer == depth - 1
        lam_init = 0.8 - 0.6 * math.exp(-0.3 * layer)
        mods = mods_all[layer].reshape(MOD_ROWS, 1, 6 * d)
        n_rows = n_lat if last else n_all

        p = _in_proj(rows, norm1_g[layer], mods, w_in, layer, row0=0, n_rows=n_rows, col0=0, n_cols=C_TOT,
                     sh_idx=0, sc_idx=1, tm=1024, **geo)
        if last:
            kv_ctx = _in_proj(rows, norm1_g[layer], mods, w_in, layer, row0=n_lat, n_rows=batch * n_ctx, col0=OFF_K,
                              n_cols=OFF_SC - OFF_K, sh_idx=0, sc_idx=1, tm=n_ctx, **geo)
            kc_blk, vc_blk, ctx_rb0 = 0, QK_W // HEAD_W, 0
        else:
            kv_ctx = p
            kc_blk, vc_blk, ctx_rb0 = OFF_K // HEAD_W, OFF_V // HEAD_W, n_lat // n_ctx
        attn = _attn_lat(p, kv_ctx, kc_blk, vc_blk, ctx_rb0, cos, sin, diff_lambda[layer], subln_g[layer],
                         batch=batch, seq=seq, n_ctx=n_ctx, lam_init=lam_init)
        if not last:
            attn = (attn, _attn_ctx(p, n_lat // n_ctx, diff_lambda[layer], subln_g[layer],
                                    batch=batch, n_ctx=n_ctx, lam_init=lam_init))
        else:
            attn = (attn, attn)
        sc, cf = _convs(p, sc_conv_w[layer], cf_dw_w[layer], cf_dw_b[layer], cf_ln_g[layer], cf_ln_b[layer],
                        n_rows=n_rows, **geo)
        h_mix = _merge(attn, sc, cf, p, b_gate[layer].reshape(1, 3 * d),
                       w_attn_out[layer].astype(BF16), w_sc_out[layer].astype(BF16),
                       w_cf_out[layer].astype(BF16), w_mix[layer].astype(BF16), rows, mods,
                       n_rows=n_rows, gate_idx=2, **geo)

        rw = jnp.concatenate([router_g_w[layer], router_e_w[layer],
                              jnp.zeros((d, LANES - N_GROUPS - N_EXPERTS), F32)], axis=1)
        rb = jnp.concatenate([router_g_b[layer], router_e_b[layer],
                              jnp.zeros((LANES - N_GROUPS - N_EXPERTS,), F32)]).reshape(1, LANES)
        ids, wts, cnt = _router(h_mix, norm2_g[layer], mods, rw, rb, n_rows=n_rows, sh_idx=3, sc_idx=4, **geo)
        items, pos = _moe_plan(ids, cnt, n_rows)
        xs = _dispatch(h_mix, norm2_g[layer], mods, pos, n_rows=n_rows, sh_idx=3, sc_idx=4, **geo)
        ys = _moe(xs, items, exp_w_gu, exp_w_down, layer)
        h = _combine(h_mix, ys, pos, wts, mods, final_g, n_rows=n_rows, gate_idx=5, final=last, **geo)
        rows = (h, h)

    return h.reshape(batch, seq, d)
```
